```python
import jax, jax.numpy as jnp
from jax import lax
import numpy as np

D_MODEL = 1024
BATCH = 8
SEQ = 8192
DEPTH = 4

HEAD_DIM = 64
A_Q_HEADS = 8
A_KV_HEADS = 2
B_GROUP_CFG = ((128, 1), (512, 4), (2048, 16))
B_HEADS_PER_GROUP = 4
B_GROUPS = len(B_GROUP_CFG)
B_HEADS = B_GROUPS * B_HEADS_PER_GROUP
POOL_WINDOWS = (2, 4, 8, 16)
POOL_GROUPS = len(POOL_WINDOWS)
POOL_GROUP_WIDTH = 128
POOL_WIDTH = POOL_GROUPS * POOL_GROUP_WIDTH
N_BRANCHES = 3
D_FF = ((-(-8 * D_MODEL // 3) + 255) // 256) * 256
GRID_W = 64
ROPE_THETA = 10000.0
BLOCK = 128
EPS = 1e-6
NEG_INF = -1e30

A_Q_WIDTH = A_Q_HEADS * HEAD_DIM
A_KV_WIDTH = A_KV_HEADS * HEAD_DIM
B_WIDTH = B_HEADS * HEAD_DIM
B_OUT_WIDTH = B_HEADS_PER_GROUP * HEAD_DIM
GATE_WIDTH = N_BRANCHES * D_MODEL
IN_WIDTH = A_Q_WIDTH + 2 * A_KV_WIDTH + 3 * B_WIDTH + POOL_WIDTH + GATE_WIDTH
SPLIT_POINTS = (
    A_Q_WIDTH,
    A_Q_WIDTH + A_KV_WIDTH,
    A_Q_WIDTH + 2 * A_KV_WIDTH,
    A_Q_WIDTH + 2 * A_KV_WIDTH + B_WIDTH,
    A_Q_WIDTH + 2 * A_KV_WIDTH + 2 * B_WIDTH,
    A_Q_WIDTH + 2 * A_KV_WIDTH + 3 * B_WIDTH,
    A_Q_WIDTH + 2 * A_KV_WIDTH + 3 * B_WIDTH + POOL_WIDTH,
)
RESID_SCALE = (2 * DEPTH) ** -0.5

kernel_name = 'hybrid_gated_axial_dilated_pool_encoder'


def rms_norm(x, g):
    xf = x.astype(jnp.float32)
    y = xf * lax.rsqrt(jnp.mean(xf * xf, axis=-1, keepdims=True) + EPS)
    return (y * g.astype(jnp.float32)).astype(x.dtype)


def rope_angles(pos, dim):
    inv = ROPE_THETA ** (-jnp.arange(0, dim, 2, dtype=jnp.float32) / dim)
    return pos.astype(jnp.float32)[:, None] * inv[None, :]


def apply_rope(x, ang):
    xf = x.astype(jnp.float32)
    x1, x2 = jnp.split(xf, 2, axis=-1)
    c = jnp.cos(ang)[None, :, None, :]
    s = jnp.sin(ang)[None, :, None, :]
    return jnp.concatenate([x1 * c - x2 * s, x2 * c + x1 * s], axis=-1).astype(x.dtype)


def apply_axial_rope(x, ang_row, ang_col):
    xr, xc = jnp.split(x, 2, axis=-1)
    return jnp.concatenate([apply_rope(xr, ang_row), apply_rope(xc, ang_col)], axis=-1)


def dense_gqa_blocked(q, k, v):
    b, s, hq, dh = q.shape
    hkv = k.shape[2]
    g = hq // hkv
    nb = s // BLOCK
    scale = dh ** -0.5
    qb = q.reshape(b, nb, BLOCK, hkv, g, dh).transpose(1, 0, 2, 3, 4, 5)

    def one_block(q_blk):
        sc = jnp.einsum('bqkgd,bskd->bkgqs', q_blk, k, preferred_element_type=jnp.float32) * scale
        p = jax.nn.softmax(sc, axis=-1)
        return jnp.einsum('bkgqs,bskd->bqkgd', p.astype(v.dtype), v)

    o = lax.map(one_block, qb)
    return o.transpose(1, 0, 2, 3, 4, 5).reshape(b, s, hq * dh)


def dilated_window_attention(q, k, v, dilation, half_span):
    b, s, h, dh = q.shape
    L = s // dilation
    nb = -(-L // BLOCK)
    lp = nb * BLOCK
    bd = b * dilation

    def to_sub(t):
        t = t.reshape(b, L, dilation, h, dh).transpose(0, 2, 1, 3, 4).reshape(bd, L, h, dh)
        return jnp.pad(t, ((0, 0), (0, lp - L), (0, 0), (0, 0)))

    def band(t):
        tp = jnp.pad(t, ((0, 0), (BLOCK, BLOCK), (0, 0), (0, 0)))
        return jnp.concatenate(
            [tp[:, i * BLOCK:i * BLOCK + lp].reshape(bd, nb, BLOCK, h, dh) for i in range(3)], axis=2)

    qb = to_sub(q).reshape(bd, nb, BLOCK, h, dh)
    kb = band(to_sub(k))
    vb = band(to_sub(v))
    sc = jnp.einsum('znqhd,znkhd->znhqk', qb, kb, preferred_element_type=jnp.float32) * (dh ** -0.5)
    blk = jnp.arange(nb)[:, None, None] * BLOCK
    qpos = blk + jnp.arange(BLOCK)[None, :, None]
    kpos = blk - BLOCK + jnp.arange(3 * BLOCK)[None, None, :]
    valid = (jnp.abs(qpos - kpos) <= half_span) & (kpos >= 0) & (kpos < L)
    sc = jnp.where(valid[None, :, None], sc, NEG_INF)
    m = jnp.max(sc, axis=-1, keepdims=True)
    p = jnp.exp(sc - m)
    den = jnp.sum(p, axis=-1, keepdims=True)
    o = jnp.einsum('znhqk,znkhd->znqhd', (p / den).astype(v.dtype), vb)
    lse = (m + jnp.log(den))[..., 0]
    o = o.reshape(b, dilation, lp, h, dh)[:, :, :L].transpose(0, 2, 1, 3, 4).reshape(b, s, h, dh)
    lse = lse.transpose(0, 1, 3, 2).reshape(b, dilation, lp, h)[:, :, :L]
    lse = lse.transpose(0, 2, 1, 3).reshape(b, s, h)
    return o, lse


def multiscale_pool(u, lin, scale):
    b, s, _ = u.shape
    uf = u.astype(jnp.float32).reshape(b, s, POOL_GROUPS, POOL_GROUP_WIDTH)
    cs = jnp.pad(lax.cumsum(uf, axis=1), ((0, 0), (1, 0), (0, 0), (0, 0)))
    t = jnp.arange(s)
    outs = []
    for gi, w in enumerate(POOL_WINDOWS):
        lo = jnp.clip(t - w // 2, 0, s)
        hi = jnp.clip(t + w - w // 2, 0, s)
        mean = (cs[:, hi, gi] - cs[:, lo, gi]) / (hi - lo).astype(jnp.float32)[None, :, None]
        outs.append(mean - uf[:, :, gi])
    pooled = jnp.stack(outs, axis=2).astype(u.dtype)
    mixed = jnp.einsum('bsgc,gcd->bsgd', pooled, lin)
    return mixed.reshape(b, s, POOL_WIDTH) * scale


def _fwd_setup_inputs(seed: int = 0) -> dict:
    key = jax.random.key(seed)
    ks = jax.random.split(key, 19)
    f32 = jnp.float32

    def nrm(k, shape, fan_in):
        return jax.random.normal(k, shape, f32) * (fan_in ** -0.5)

    def gain(k, shape):
        return 1.0 + 0.02 * jax.random.normal(k, shape, f32)

    return {
        'x': jax.random.normal(ks[0], (BATCH, SEQ, D_MODEL), f32),
        'norm_mix': gain(ks[1], (DEPTH, D_MODEL)),
        'w_in': nrm(ks[2], (DEPTH, D_MODEL, IN_WIDTH), D_MODEL),
        'b_gate': 0.02 * jax.random.normal(ks[3], (DEPTH, GATE_WIDTH), f32),
        'qn_a': gain(ks[4], (DEPTH, HEAD_DIM)),
        'kn_a': gain(ks[5], (DEPTH, HEAD_DIM)),
        'qn_b': gain(ks[6], (DEPTH, HEAD_DIM)),
        'kn_b': gain(ks[7], (DEPTH, HEAD_DIM)),
        'pool_lin': nrm(ks[8], (DEPTH, POOL_GROUPS, POOL_GROUP_WIDTH, POOL_GROUP_WIDTH), POOL_GROUP_WIDTH),
        'pool_scale': gain(ks[9], (DEPTH, POOL_WIDTH)),
        'w_branch_a': nrm(ks[10], (DEPTH, A_Q_WIDTH, D_MODEL), A_Q_WIDTH),
        'w_branch_b': nrm(ks[11], (DEPTH, B_OUT_WIDTH, D_MODEL), B_OUT_WIDTH),
        'w_branch_c': nrm(ks[12], (DEPTH, POOL_WIDTH, D_MODEL), POOL_WIDTH),
        'w_out': nrm(ks[13], (DEPTH, D_MODEL, D_MODEL), D_MODEL) * RESID_SCALE,
        'norm_ffn': gain(ks[14], (DEPTH, D_MODEL)),
        'w_ffn_gate': nrm(ks[15], (DEPTH, D_MODEL, D_FF), D_MODEL),
        'w_ffn_up': nrm(ks[16], (DEPTH, D_MODEL, D_FF), D_MODEL),
        'w_ffn_down': nrm(ks[17], (DEPTH, D_FF, D_MODEL), D_FF) * RESID_SCALE,
    }


def _fwd_reference(x, norm_mix, w_in, b_gate, qn_a, kn_a, qn_b, kn_b, pool_lin, pool_scale,
              w_branch_a, w_branch_b, w_branch_c, w_out, norm_ffn, w_ffn_gate, w_ffn_up, w_ffn_down):
    b, s, _ = x.shape
    rows = s // GRID_W
    row_idx = jnp.repeat(jnp.arange(rows), GRID_W)
    col_idx = jnp.tile(jnp.arange(GRID_W), rows)
    ang_row = rope_angles(row_idx, HEAD_DIM // 2)
    ang_col = rope_angles(col_idx, HEAD_DIM // 2)
    ang_seq = rope_angles(jnp.arange(s), HEAD_DIM)

    for l in range(DEPTH):
        h = rms_norm(x, norm_mix[l])
        z = h @ w_in[l]
        qa, ka, va, qb, kb, vb, uc, gz = jnp.split(z, SPLIT_POINTS, axis=-1)

        qa = apply_axial_rope(rms_norm(qa.reshape(b, s, A_Q_HEADS, HEAD_DIM), qn_a[l]), ang_row, ang_col)
        ka = apply_axial_rope(rms_norm(ka.reshape(b, s, A_KV_HEADS, HEAD_DIM), kn_a[l]), ang_row, ang_col)
        va = va.reshape(b, s, A_KV_HEADS, HEAD_DIM)
        ya = dense_gqa_blocked(qa, ka, va) @ w_branch_a[l]

        qb = apply_rope(rms_norm(qb.reshape(b, s, B_HEADS, HEAD_DIM), qn_b[l]), ang_seq)
        kb = apply_rope(rms_norm(kb.reshape(b, s, B_HEADS, HEAD_DIM), kn_b[l]), ang_seq)
        qb = qb.reshape(b, s, B_GROUPS, B_HEADS_PER_GROUP, HEAD_DIM)
        kb = kb.reshape(b, s, B_GROUPS, B_HEADS_PER_GROUP, HEAD_DIM)
        vb = vb.reshape(b, s, B_GROUPS, B_HEADS_PER_GROUP, HEAD_DIM)
        o_list, lse_list = [], []
        for gi, (window, dil) in enumerate(B_GROUP_CFG):
            o_g, lse_g = dilated_window_attention(qb[:, :, gi], kb[:, :, gi], vb[:, :, gi], dil, window // (2 * dil))
            o_list.append(o_g)
            lse_list.append(lse_g)
        wts = jax.nn.softmax(jnp.stack(lse_list, axis=0), axis=0)
        ob = jnp.einsum('gbsh,gbshd->bshd', wts.astype(x.dtype), jnp.stack(o_list, axis=0))
        yb = ob.reshape(b, s, B_OUT_WIDTH) @ w_branch_b[l]

        yc = multiscale_pool(uc, pool_lin[l], pool_scale[l]) @ w_branch_c[l]

        gates = jax.nn.sigmoid(gz + b_gate[l]).reshape(b, s, N_BRANCHES, D_MODEL)
        merged = gates[:, :, 0] * ya + gates[:, :, 1] * yb + gates[:, :, 2] * yc
        x = x + merged @ w_out[l]

        h2 = rms_norm(x, norm_ffn[l])
        x = x + (jax.nn.silu(h2 @ w_ffn_gate[l]) * (h2 @ w_ffn_up[l])) @ w_ffn_down[l]
    return x


import jax as _jax
import jax.numpy as _jnp

TWIN_FORMAT = 'train_step'
FWD_PARAMS = ['x', 'norm_mix', 'w_in', 'b_gate', 'qn_a', 'kn_a', 'qn_b', 'kn_b', 'pool_lin', 'pool_scale', 'w_branch_a', 'w_branch_b', 'w_branch_c', 'w_out', 'norm_ffn', 'w_ffn_gate', 'w_ffn_up', 'w_ffn_down']
TWIN_WEIGHTS = ['norm_mix', 'w_in', 'b_gate', 'qn_a', 'kn_a', 'qn_b', 'kn_b', 'pool_lin', 'pool_scale', 'w_branch_a', 'w_branch_b', 'w_branch_c', 'w_out', 'norm_ffn', 'w_ffn_gate', 'w_ffn_up', 'w_ffn_down']
TWIN_DIFF_INPUT = 'x'
TWIN_INPUTS = ['x', 'norm_mix', 'w_in', 'b_gate', 'qn_a', 'kn_a', 'qn_b', 'kn_b', 'pool_lin', 'pool_scale', 'w_branch_a', 'w_branch_b', 'w_branch_c', 'w_out', 'norm_ffn', 'w_ffn_gate', 'w_ffn_up', 'w_ffn_down', 'loss_target', 'm_norm_mix', 'm_w_in', 'm_b_gate', 'm_qn_a', 'm_kn_a', 'm_qn_b', 'm_kn_b', 'm_pool_lin', 'm_pool_scale', 'm_w_branch_a', 'm_w_branch_b', 'm_w_branch_c', 'm_w_out', 'm_norm_ffn', 'm_w_ffn_gate', 'm_w_ffn_up', 'm_w_ffn_down', 'v_norm_mix', 'v_w_in', 'v_b_gate', 'v_qn_a', 'v_kn_a', 'v_qn_b', 'v_kn_b', 'v_pool_lin', 'v_pool_scale', 'v_w_branch_a', 'v_w_branch_b', 'v_w_branch_c', 'v_w_out', 'v_norm_ffn', 'v_w_ffn_gate', 'v_w_ffn_up', 'v_w_ffn_down']
TWIN_OUTPUTS = ['loss', 'grad_x', 'grad_norm_mix', 'grad_w_in', 'grad_b_gate', 'grad_qn_a', 'grad_kn_a', 'grad_qn_b', 'grad_kn_b', 'grad_pool_lin', 'grad_pool_scale', 'grad_w_branch_a', 'grad_w_branch_b', 'grad_w_branch_c', 'grad_w_out', 'grad_norm_ffn', 'grad_w_ffn_gate', 'grad_w_ffn_up', 'grad_w_ffn_down', 'delta_norm_mix', 'delta_w_in', 'delta_b_gate', 'delta_qn_a', 'delta_kn_a', 'delta_qn_b', 'delta_kn_b', 'delta_pool_lin', 'delta_pool_scale', 'delta_w_branch_a', 'delta_w_branch_b', 'delta_w_branch_c', 'delta_w_out', 'delta_norm_ffn', 'delta_w_ffn_gate', 'delta_w_ffn_up', 'delta_w_ffn_down', 'new_m_norm_mix', 'new_m_w_in', 'new_m_b_gate', 'new_m_qn_a', 'new_m_kn_a', 'new_m_qn_b', 'new_m_kn_b', 'new_m_pool_lin', 'new_m_pool_scale', 'new_m_w_branch_a', 'new_m_w_branch_b', 'new_m_w_branch_c', 'new_m_w_out', 'new_m_norm_ffn', 'new_m_w_ffn_gate', 'new_m_w_ffn_up', 'new_m_w_ffn_down', 'new_v_norm_mix', 'new_v_w_in', 'new_v_b_gate', 'new_v_qn_a', 'new_v_kn_a', 'new_v_qn_b', 'new_v_kn_b', 'new_v_pool_lin', 'new_v_pool_scale', 'new_v_w_branch_a', 'new_v_w_branch_b', 'new_v_w_branch_c', 'new_v_w_out', 'new_v_norm_ffn', 'new_v_w_ffn_gate', 'new_v_w_ffn_up', 'new_v_w_ffn_down']
TWIN_LEAF_KINDS = {'loss': 'loss', 'grad_x': 'grad_x', 'grad_norm_mix': 'grad_w', 'grad_w_in': 'grad_w', 'grad_b_gate': 'grad_w', 'grad_qn_a': 'grad_w', 'grad_kn_a': 'grad_w', 'grad_qn_b': 'grad_w', 'grad_kn_b': 'grad_w', 'grad_pool_lin': 'grad_w', 'grad_pool_scale': 'grad_w', 'grad_w_branch_a': 'grad_w', 'grad_w_branch_b': 'grad_w', 'grad_w_branch_c': 'grad_w', 'grad_w_out': 'grad_w', 'grad_norm_ffn': 'grad_w', 'grad_w_ffn_gate': 'grad_w', 'grad_w_ffn_up': 'grad_w', 'grad_w_ffn_down': 'grad_w', 'delta_norm_mix': 'delta_w', 'delta_w_in': 'delta_w', 'delta_b_gate': 'delta_w', 'delta_qn_a': 'delta_w', 'delta_kn_a': 'delta_w', 'delta_qn_b': 'delta_w', 'delta_kn_b': 'delta_w', 'delta_pool_lin': 'delta_w', 'delta_pool_scale': 'delta_w', 'delta_w_branch_a': 'delta_w', 'delta_w_branch_b': 'delta_w', 'delta_w_branch_c': 'delta_w', 'delta_w_out': 'delta_w', 'delta_norm_ffn': 'delta_w', 'delta_w_ffn_gate': 'delta_w', 'delta_w_ffn_up': 'delta_w', 'delta_w_ffn_down': 'delta_w', 'new_m_norm_mix': 'new_m', 'new_m_w_in': 'new_m', 'new_m_b_gate': 'new_m', 'new_m_qn_a': 'new_m', 'new_m_kn_a': 'new_m', 'new_m_qn_b': 'new_m', 'new_m_kn_b': 'new_m', 'new_m_pool_lin': 'new_m', 'new_m_pool_scale': 'new_m', 'new_m_w_branch_a': 'new_m', 'new_m_w_branch_b': 'new_m', 'new_m_w_branch_c': 'new_m', 'new_m_w_out': 'new_m', 'new_m_norm_ffn': 'new_m', 'new_m_w_ffn_gate': 'new_m', 'new_m_w_ffn_up': 'new_m', 'new_m_w_ffn_down': 'new_m', 'new_v_norm_mix': 'new_v', 'new_v_w_in': 'new_v', 'new_v_b_gate': 'new_v', 'new_v_qn_a': 'new_v', 'new_v_kn_a': 'new_v', 'new_v_qn_b': 'new_v', 'new_v_kn_b': 'new_v', 'new_v_pool_lin': 'new_v', 'new_v_pool_scale': 'new_v', 'new_v_w_branch_a': 'new_v', 'new_v_w_branch_b': 'new_v', 'new_v_w_branch_c': 'new_v', 'new_v_w_out': 'new_v', 'new_v_norm_ffn': 'new_v', 'new_v_w_ffn_gate': 'new_v', 'new_v_w_ffn_up': 'new_v', 'new_v_w_ffn_down': 'new_v'}


def _forward(args):
    return _fwd_reference(*[args[k] for k in FWD_PARAMS])


def _output_shape():
    def fwd():
        inp = _fwd_setup_inputs(0)
        return _fwd_reference(*[inp[k] for k in FWD_PARAMS])
    out = _jax.eval_shape(fwd)
    return out.shape, out.dtype

N_MICROBATCH = 1
ADAM_LR = 0.001
ADAM_B1 = 0.9
ADAM_B2 = 0.999
ADAM_EPS = 1e-08
ADAM_WD = 0.01
ADAM_STEP = 10
PER_EXAMPLE_BATCH_AXIS = {'x': 0, 'loss_target': 0}
SHARED_INPUTS = []
_WEIGHT_DTYPES = {'norm_mix': _jnp.float32, 'w_in': _jnp.float32, 'b_gate': _jnp.float32, 'qn_a': _jnp.float32, 'kn_a': _jnp.float32, 'qn_b': _jnp.float32, 'kn_b': _jnp.float32, 'pool_lin': _jnp.float32, 'pool_scale': _jnp.float32, 'w_branch_a': _jnp.float32, 'w_branch_b': _jnp.float32, 'w_branch_c': _jnp.float32, 'w_out': _jnp.float32, 'norm_ffn': _jnp.float32, 'w_ffn_gate': _jnp.float32, 'w_ffn_up': _jnp.float32, 'w_ffn_down': _jnp.float32}
MOMENT_SCALE = {'norm_mix': 1.982353e+00, 'w_in': 9.918093e-02, 'b_gate': 3.692004e-01, 'qn_a': 6.740044e-02, 'kn_a': 6.659404e-02, 'qn_b': 1.486365e-01, 'kn_b': 1.495281e-01, 'pool_lin': 4.727277e-01, 'pool_scale': 3.747833e+00, 'w_branch_a': 9.280854e-03, 'w_branch_b': 9.912136e-03, 'w_branch_c': 2.760664e-01, 'w_out': 7.278820e-01, 'norm_ffn': 6.183173e+00, 'w_ffn_gate': 5.968604e-02, 'w_ffn_up': 8.196926e-02, 'w_ffn_down': 3.819798e-01}


def _to_microbatches(a, axis):
    t = _jnp.moveaxis(a, axis, 0)
    t = t.reshape((N_MICROBATCH, t.shape[0] // N_MICROBATCH) + t.shape[1:])
    return _jnp.moveaxis(t, 1, axis + 1)


def setup_inputs(seed: int = 0) -> dict:
    inp = _fwd_setup_inputs(seed)
    key = _jax.random.fold_in(_jax.random.key(seed), 7919)
    shape, _ = _output_shape()
    out = dict(inp)
    out["loss_target"] = _jax.random.normal(_jax.random.fold_in(key, 0), shape, _jnp.float32)
    for i, name in enumerate(TWIN_WEIGHTS):
        w = inp[name].astype(_jnp.float32)
        if MOMENT_SCALE is None:
            s = _jnp.sqrt(_jnp.mean(_jnp.square(w)) + 1e-30)
        else:
            s = MOMENT_SCALE[name]
        km, kv = _jax.random.split(_jax.random.fold_in(key, i + 1))
        out[name] = w
        out["m_" + name] = s * _jax.random.normal(km, w.shape, _jnp.float32)
        out["v_" + name] = (s * s) * _jax.random.uniform(kv, w.shape, _jnp.float32, 0.5, 1.5)
    if N_MICROBATCH > 1:
        for name, axis in PER_EXAMPLE_BATCH_AXIS.items():
            out[name] = _to_microbatches(out[name], axis)
    return {'x': out['x'], 'norm_mix': out['norm_mix'], 'w_in': out['w_in'], 'b_gate': out['b_gate'], 'qn_a': out['qn_a'], 'kn_a': out['kn_a'], 'qn_b': out['qn_b'], 'kn_b': out['kn_b'], 'pool_lin': out['pool_lin'], 'pool_scale': out['pool_scale'], 'w_branch_a': out['w_branch_a'], 'w_branch_b': out['w_branch_b'], 'w_branch_c': out['w_branch_c'], 'w_out': out['w_out'], 'norm_ffn': out['norm_ffn'], 'w_ffn_gate': out['w_ffn_gate'], 'w_ffn_up': out['w_ffn_up'], 'w_ffn_down': out['w_ffn_down'], 'loss_target': out['loss_target'], 'm_norm_mix': out['m_norm_mix'], 'm_w_in': out['m_w_in'], 'm_b_gate': out['m_b_gate'], 'm_qn_a': out['m_qn_a'], 'm_kn_a': out['m_kn_a'], 'm_qn_b': out['m_qn_b'], 'm_kn_b': out['m_kn_b'], 'm_pool_lin': out['m_pool_lin'], 'm_pool_scale': out['m_pool_scale'], 'm_w_branch_a': out['m_w_branch_a'], 'm_w_branch_b': out['m_w_branch_b'], 'm_w_branch_c': out['m_w_branch_c'], 'm_w_out': out['m_w_out'], 'm_norm_ffn': out['m_norm_ffn'], 'm_w_ffn_gate': out['m_w_ffn_gate'], 'm_w_ffn_up': out['m_w_ffn_up'], 'm_w_ffn_down': out['m_w_ffn_down'], 'v_norm_mix': out['v_norm_mix'], 'v_w_in': out['v_w_in'], 'v_b_gate': out['v_b_gate'], 'v_qn_a': out['v_qn_a'], 'v_kn_a': out['v_kn_a'], 'v_qn_b': out['v_qn_b'], 'v_kn_b': out['v_kn_b'], 'v_pool_lin': out['v_pool_lin'], 'v_pool_scale': out['v_pool_scale'], 'v_w_branch_a': out['v_w_branch_a'], 'v_w_branch_b': out['v_w_branch_b'], 'v_w_branch_c': out['v_w_branch_c'], 'v_w_out': out['v_w_out'], 'v_norm_ffn': out['v_norm_ffn'], 'v_w_ffn_gate': out['v_w_ffn_gate'], 'v_w_ffn_up': out['v_w_ffn_up'], 'v_w_ffn_down': out['v_w_ffn_down']}


def _loss(weights, diff, rest, loss_target):
    with _jax.named_scope("forward"):
        args = {**rest, TWIN_DIFF_INPUT: diff, **{k: w.astype(_WEIGHT_DTYPES[k]) for k, w in weights.items()}}
        y = _forward(args)
    with _jax.named_scope("loss_head"):
        err = _jnp.square(y.astype(_jnp.float32) - loss_target)
        return 0.5 * _jnp.sum(_jnp.mean(err, axis=-1)) if err.ndim else 0.5 * err


def _adamw(w, g, m, v):
    m = ADAM_B1 * m + (1.0 - ADAM_B1) * g
    v = ADAM_B2 * v + (1.0 - ADAM_B2) * _jnp.square(g)
    m_hat = m / (1.0 - ADAM_B1 ** ADAM_STEP)
    v_hat = v / (1.0 - ADAM_B2 ** ADAM_STEP)
    delta = -ADAM_LR * (m_hat / (_jnp.sqrt(v_hat) + ADAM_EPS) + ADAM_WD * w)
    return delta, m, v


def reference(x, norm_mix, w_in, b_gate, qn_a, kn_a, qn_b, kn_b, pool_lin, pool_scale, w_branch_a, w_branch_b, w_branch_c, w_out, norm_ffn, w_ffn_gate, w_ffn_up, w_ffn_down, loss_target, m_norm_mix, m_w_in, m_b_gate, m_qn_a, m_kn_a, m_qn_b, m_kn_b, m_pool_lin, m_pool_scale, m_w_branch_a, m_w_branch_b, m_w_branch_c, m_w_out, m_norm_ffn, m_w_ffn_gate, m_w_ffn_up, m_w_ffn_down, v_norm_mix, v_w_in, v_b_gate, v_qn_a, v_kn_a, v_qn_b, v_kn_b, v_pool_lin, v_pool_scale, v_w_branch_a, v_w_branch_b, v_w_branch_c, v_w_out, v_norm_ffn, v_w_ffn_gate, v_w_ffn_up, v_w_ffn_down):
    given = dict(x=x, norm_mix=norm_mix, w_in=w_in, b_gate=b_gate, qn_a=qn_a, kn_a=kn_a, qn_b=qn_b, kn_b=kn_b, pool_lin=pool_lin, pool_scale=pool_scale, w_branch_a=w_branch_a, w_branch_b=w_branch_b, w_branch_c=w_branch_c, w_out=w_out, norm_ffn=norm_ffn, w_ffn_gate=w_ffn_gate, w_ffn_up=w_ffn_up, w_ffn_down=w_ffn_down, loss_target=loss_target, m_norm_mix=m_norm_mix, m_w_in=m_w_in, m_b_gate=m_b_gate, m_qn_a=m_qn_a, m_kn_a=m_kn_a, m_qn_b=m_qn_b, m_kn_b=m_kn_b, m_pool_lin=m_pool_lin, m_pool_scale=m_pool_scale, m_w_branch_a=m_w_branch_a, m_w_branch_b=m_w_branch_b, m_w_branch_c=m_w_branch_c, m_w_out=m_w_out, m_norm_ffn=m_norm_ffn, m_w_ffn_gate=m_w_ffn_gate, m_w_ffn_up=m_w_ffn_up, m_w_ffn_down=m_w_ffn_down, v_norm_mix=v_norm_mix, v_w_in=v_w_in, v_b_gate=v_b_gate, v_qn_a=v_qn_a, v_kn_a=v_kn_a, v_qn_b=v_qn_b, v_kn_b=v_kn_b, v_pool_lin=v_pool_lin, v_pool_scale=v_pool_scale, v_w_branch_a=v_w_branch_a, v_w_branch_b=v_w_branch_b, v_w_branch_c=v_w_branch_c, v_w_out=v_w_out, v_norm_ffn=v_norm_ffn, v_w_ffn_gate=v_w_ffn_gate, v_w_ffn_up=v_w_ffn_up, v_w_ffn_down=v_w_ffn_down)
    weights = {n: given[n] for n in TWIN_WEIGHTS}
    shared = {n: given[n] for n in SHARED_INPUTS}
    per_example = {n: given[n] for n in ['x']}
    grad_fn = _jax.value_and_grad(_loss, argnums=(0, 1))

    def one_microbatch(ex, loss_target):
        ex = dict(ex)
        diff = ex.pop(TWIN_DIFF_INPUT)
        return grad_fn(weights, diff, {**shared, **ex}, loss_target)

    if N_MICROBATCH == 1:
        loss, (grad_w, grad_x) = one_microbatch(per_example, given["loss_target"])
    else:
        def body(carry, xs):
            loss_sum, grad_sum = carry
            l_k, (gw_k, gx_k) = one_microbatch(xs[0], xs[1])
            with _jax.named_scope("update"):
                return (loss_sum + l_k, _jax.tree.map(_jnp.add, grad_sum, gw_k)), gx_k

        init = (_jnp.zeros((), _jnp.float32), _jax.tree.map(_jnp.zeros_like, weights))
        (loss, grad_w), grad_x = _jax.lax.scan(body, init, (per_example, given["loss_target"]))
    with _jax.named_scope("update"):
        delta_w, new_m, new_v = {}, {}, {}
        for n in TWIN_WEIGHTS:
            delta_w[n], new_m[n], new_v[n] = _adamw(weights[n], grad_w[n], given["m_" + n], given["v_" + n])
    return (loss, grad_x, *[grad_w[n] for n in TWIN_WEIGHTS], *[delta_w[n] for n in TWIN_WEIGHTS],
            *[new_m[n] for n in TWIN_WEIGHTS], *[new_v[n] for n in TWIN_WEIGHTS])
```

```python
import functools

import jax
import jax.numpy as jnp
from jax import lax
from jax.experimental import pallas as pl
from jax.experimental.pallas import tpu as pltpu

F32 = jnp.float32
BF16 = jnp.bfloat16

N_DEV = 8
D_MODEL = 1024
DEPTH = 4
HEAD_DIM = 64
LANES = 128
A_Q_WIDTH = 512
A_KV_WIDTH = 128
B_WIDTH = 768
B_GROUPS = 3
B_DILATIONS = (1, 4, 16)
B_HALF_SPAN = 64
B_OUT_WIDTH = 256
POOL_WIDTH = 512
POOL_HALF = (1, 2, 4, 8)
GATE_WIDTH = 3072
QKV_WIDTH = A_Q_WIDTH + 2 * A_KV_WIDTH + 3 * B_WIDTH
IN_WIDTH = QKV_WIDTH + POOL_WIDTH + GATE_WIDTH
D_FF = 2816
GRID_W = 64
ROPE_THETA = 10000.0
EPS = 1e-6
NEG_INF = -1e30
ATTN_SCALE = HEAD_DIM ** -0.5

ADAM_LR = 0.001
ADAM_B1 = 0.9
ADAM_B2 = 0.999
ADAM_EPS = 1e-08
ADAM_WD = 0.01
ADAM_STEP = 10

PACK_COLS = 1024
VMEM_LIMIT = 56 * 1024 * 1024

SHARDED = ('w_in', 'w_branch_a', 'w_branch_b', 'w_branch_c', 'w_out', 'w_ffn_gate', 'w_ffn_up', 'w_ffn_down')
SMALL = ('norm_mix', 'b_gate', 'qn_a', 'kn_a', 'qn_b', 'kn_b', 'pool_lin', 'pool_scale', 'norm_ffn')
WEIGHTS = ('norm_mix', 'w_in', 'b_gate', 'qn_a', 'kn_a', 'qn_b', 'kn_b', 'pool_lin', 'pool_scale',
           'w_branch_a', 'w_branch_b', 'w_branch_c', 'w_out', 'norm_ffn', 'w_ffn_gate', 'w_ffn_up', 'w_ffn_down')

NN = (((1,), (0,)), ((), ()))
NT = (((1,), (1,)), ((), ()))
TN = (((0,), (0,)), ((), ()))


def _params(vmem=None):
    return pltpu.CompilerParams(vmem_limit_bytes=VMEM_LIMIT if vmem is None else vmem)


def _lane_iota(n=LANES):
    return lax.broadcasted_iota(jnp.int32, (1, n), 1)


def _swap(x, sh, lane):
    n = x.shape[-1]
    down = pltpu.roll(x, sh, axis=1)
    up = pltpu.roll(x, n - sh, axis=1)
    return jnp.where((lane & sh) == 0, up, down)


def _head_sum(v, lane):
    sh = HEAD_DIM // 2
    while sh >= 1:
        v = v + _swap(v, sh, lane)
        sh //= 2
    return v


def _mm(a, b, *, dims, out_dtype, tm, tn, tk, name, res=None):
    if dims == 'nn':
        (m, k), n = a.shape, b.shape[1]
    elif dims == 'nt':
        (m, k), n = a.shape, b.shape[0]
    else:
        (k, m), n = a.shape, b.shape[1]
    tm, tn, tk = min(tm, m), min(tn, n), min(tk, k)
    assert m % tm == 0 and n % tn == 0 and k % tk == 0, (name, m, n, k, tm, tn, tk)
    nk = k // tk
    if dims == 'tn':
        a_spec = pl.BlockSpec((tk, tm), lambda i, j, kk: (kk, i))
    else:
        a_spec = pl.BlockSpec((tm, tk), lambda i, j, kk: (i, kk))
    if dims == 'nt':
        b_spec = pl.BlockSpec((tn, tk), lambda i, j, kk: (j, kk))
    else:
        b_spec = pl.BlockSpec((tk, tn), lambda i, j, kk: (kk, j))
    o_spec = pl.BlockSpec((tm, tn), lambda i, j, kk: (i, j))
    dn = {'nn': NN, 'nt': NT, 'tn': TN}[dims]
    has_res = res is not None

    def body(*refs):
        if has_res:
            a_ref, b_ref, r_ref, o_ref, acc_ref = refs
        else:
            a_ref, b_ref, o_ref, acc_ref = refs
        prod = lax.dot_general(a_ref[...].astype(BF16), b_ref[...].astype(BF16), dn,
                               preferred_element_type=F32)

        def finish(total):
            if has_res:
                total = total + r_ref[...]
            o_ref[...] = total.astype(out_dtype)

        if nk == 1:
            finish(prod)
        else:
            kk = pl.program_id(2)

            @pl.when(kk == 0)
            def _():
                acc_ref[...] = prod

            @pl.when(kk > 0)
            def _():
                acc_ref[...] += prod

            @pl.when(kk == nk - 1)
            def _():
                finish(acc_ref[...])

    in_specs = [a_spec, b_spec] + ([o_spec] if has_res else [])
    args = (a, b) + ((res,) if has_res else ())
    acc_shape = (tm, tn) if nk > 1 else (8, LANES)
    return pl.pallas_call(
        body, name=name, grid=(m // tm, n // tn, nk),
        in_specs=in_specs, out_specs=o_spec,
        out_shape=jax.ShapeDtypeStruct((m, n), out_dtype),
        scratch_shapes=[pltpu.VMEM(acc_shape, F32)],
        compiler_params=_params(),
    )(*args)


def _rms_fwd(x, g, *, name, tm=512):
    s, d = x.shape

    def body(x_ref, g_ref, h_ref):
        xv = x_ref[...]
        rstd = lax.rsqrt(jnp.mean(xv * xv, axis=-1, keepdims=True) + EPS)
        h_ref[...] = (xv * rstd * g_ref[...]).astype(BF16)

    return pl.pallas_call(
        body, name=name, grid=(s // tm,),
        in_specs=[pl.BlockSpec((tm, d), lambda i: (i, 0)), pl.BlockSpec((1, d), lambda i: (0, 0))],
        out_specs=pl.BlockSpec((tm, d), lambda i: (i, 0)),
        out_shape=jax.ShapeDtypeStruct((s, d), BF16),
        compiler_params=_params(),
    )(x, g)


def _rms_bwd(x, g, dh, dres, *, name, tm=512):
    s, d = x.shape

    def body(x_ref, g_ref, dh_ref, dres_ref, dx_ref, dg_ref):
        i = pl.program_id(0)
        xv = x_ref[...]
        rstd = lax.rsqrt(jnp.mean(xv * xv, axis=-1, keepdims=True) + EPS)
        xhat = xv * rstd
        dhv = dh_ref[...]
        dxhat = dhv * g_ref[...]
        proj = jnp.mean(dxhat * xhat, axis=-1, keepdims=True)
        dx_ref[...] = dres_ref[...] + rstd * (dxhat - xhat * proj)
        part = jnp.sum(dhv * xhat, axis=0, keepdims=True)

        @pl.when(i == 0)
        def _():
            dg_ref[...] = part

        @pl.when(i > 0)
        def _():
            dg_ref[...] += part

    row = pl.BlockSpec((tm, d), lambda i: (i, 0))
    vec = pl.BlockSpec((1, d), lambda i: (0, 0))
    return pl.pallas_call(
        body, name=name, grid=(s // tm,),
        in_specs=[row, vec, row, row], out_specs=[row, vec],
        out_shape=[jax.ShapeDtypeStruct((s, d), F32), jax.ShapeDtypeStruct((1, d), F32)],
        compiler_params=_params(),
    )(x, g, dh, dres)


N_QKV_BLOCKS = QKV_WIDTH // LANES
A_BLOCKS = (A_Q_WIDTH + 2 * A_KV_WIDTH) // LANES
V_A_BLOCK = A_BLOCKS - 1
V_B_FIRST = A_BLOCKS + 2 * (B_WIDTH // LANES)


def _qk_kind(j):
    return jnp.where(j < A_BLOCKS, 0, 1)


def _is_v_block(j):
    return (j == V_A_BLOCK) | (j >= V_B_FIRST)


def _rope_tables(s):
    def ang(pos, dim):
        inv = ROPE_THETA ** (-jnp.arange(0, dim, 2, dtype=F32) / dim)
        return pos.astype(F32)[:, None] * inv[None, :]
    t = jnp.arange(s)
    a_row = ang(t // GRID_W, HEAD_DIM // 2)
    a_col = ang(t % GRID_W, HEAD_DIM // 2)
    a_seq = ang(t, HEAD_DIM)
    cos_a = jnp.concatenate([jnp.cos(a_row)] * 2 + [jnp.cos(a_col)] * 2, axis=-1)
    sin_a = jnp.concatenate([-jnp.sin(a_row), jnp.sin(a_row), -jnp.sin(a_col), jnp.sin(a_col)], axis=-1)
    cos_b = jnp.concatenate([jnp.cos(a_seq)] * 2, axis=-1)
    sin_b = jnp.concatenate([-jnp.sin(a_seq), jnp.sin(a_seq)], axis=-1)
    cos = jnp.stack([jnp.tile(cos_a, (1, 2)), jnp.tile(cos_b, (1, 2))])
    sin = jnp.stack([jnp.tile(sin_a, (1, 2)), jnp.tile(sin_b, (1, 2))])
    return cos, sin


def _qk_gains(qn_a, kn_a, qn_b, kn_b):
    one = jnp.ones((HEAD_DIM,), F32)
    parts = [jnp.tile(qn_a, 8), jnp.tile(kn_a, 2), jnp.tile(one, 2),
             jnp.tile(qn_b, 12), jnp.tile(kn_b, 12), jnp.tile(one, 12)]
    return jnp.concatenate(parts)[None, :]


def _qkrope_fwd(z, gains, cos, sin, *, name, tm=512):
    s = z.shape[0]

    def body(z_ref, g_ref, c_ref, s_ref, o_ref):
        j = pl.program_id(0)
        lane = _lane_iota()
        xv = z_ref[...]

        def normed_rope(pair):
            ms = _head_sum(xv * xv, lane) * (1.0 / HEAD_DIM)
            n = xv * lax.rsqrt(ms + EPS) * g_ref[...]
            return n * c_ref[...] + _swap(n, pair, lane) * s_ref[...]

        @pl.when(_is_v_block(j))
        def _():
            o_ref[...] = xv.astype(BF16)

        @pl.when(jnp.logical_not(_is_v_block(j)) & (j < A_BLOCKS))
        def _():
            o_ref[...] = normed_rope(HEAD_DIM // 4).astype(BF16)

        @pl.when(jnp.logical_not(_is_v_block(j)) & (j >= A_BLOCKS))
        def _():
            o_ref[...] = normed_rope(HEAD_DIM // 2).astype(BF16)

    tab = pl.BlockSpec((None, tm, LANES), lambda j, i: (_qk_kind(j), i, 0))
    blk = pl.BlockSpec((tm, LANES), lambda j, i: (i, j))
    return pl.pallas_call(
        body, name=name, grid=(N_QKV_BLOCKS, s // tm),
        in_specs=[blk, pl.BlockSpec((1, LANES), lambda j, i: (0, j)), tab, tab],
        out_specs=blk,
        out_shape=jax.ShapeDtypeStruct((s, QKV_WIDTH), BF16),
        compiler_params=_params(),
    )(z, gains, cos, sin)


def _qkrope_bwd(z, gains, cos, sin, dqkv, *, name, tm=512):
    s = z.shape[0]

    def body(z_ref, g_ref, c_ref, s_ref, dy_ref, dz_ref, dg_ref):
        j = pl.program_id(0)
        i = pl.program_id(1)
        lane = _lane_iota()
        xv = z_ref[...]
        dy = dy_ref[...]

        @pl.when(i == 0)
        def _():
            dg_ref[...] = jnp.zeros_like(dg_ref)

        def back(pair):
            ms = _head_sum(xv * xv, lane) * (1.0 / HEAD_DIM)
            rstd = lax.rsqrt(ms + EPS)
            xhat = xv * rstd
            dn = dy * c_ref[...] + _swap(dy * s_ref[...], pair, lane)
            dg_ref[...] += jnp.sum(dn * xhat, axis=0, keepdims=True)
            dxhat = dn * g_ref[...]
            proj = _head_sum(dxhat * xhat, lane) * (1.0 / HEAD_DIM)
            dz_ref[...] = (rstd * (dxhat - xhat * proj)).astype(BF16)

        @pl.when(_is_v_block(j))
        def _():
            dz_ref[...] = dy.astype(BF16)

        @pl.when(jnp.logical_not(_is_v_block(j)) & (j < A_BLOCKS))
        def _():
            back(HEAD_DIM // 4)

        @pl.when(jnp.logical_not(_is_v_block(j)) & (j >= A_BLOCKS))
        def _():
            back(HEAD_DIM // 2)

    tab = pl.BlockSpec((None, tm, LANES), lambda j, i: (_qk_kind(j), i, 0))
    blk = pl.BlockSpec((tm, LANES), lambda j, i: (i, j))
    vec = pl.BlockSpec((1, LANES), lambda j, i: (0, j))
    return pl.pallas_call(
        body, name=name, grid=(N_QKV_BLOCKS, s // tm),
        in_specs=[blk, vec, tab, tab, blk],
        out_specs=[blk, vec],
        out_shape=[jax.ShapeDtypeStruct((s, QKV_WIDTH), BF16), jax.ShapeDtypeStruct((1, QKV_WIDTH), F32)],
        compiler_params=_params(),
    )(z, gains, cos, sin, dqkv)


def _dup_halves(kv):
    h0, h1 = kv[:, :HEAD_DIM], kv[:, HEAD_DIM:]
    return jnp.concatenate([h0, h0, h1, h1], axis=1)


def _attn_a_fwd(qkv, kdup, vdup, *, name, tq=256, tk=512):
    s = qkv.shape[0]
    tq, tk = min(tq, s), min(tk, s)
    n_chunks = s // tk

    def body(q_ref, k_ref, v_ref, o_ref, lse_ref):
        lane = _lane_iota()
        q = q_ref[...]
        outs, lses = [], []
        for e in range(2):
            mine = (lane >= HEAD_DIM) if e else (lane < HEAD_DIM)
            qm = jnp.where(mine, q, jnp.zeros_like(q)) * ATTN_SCALE

            def chunk(c, carry):
                m, l, acc = carry
                off = pl.multiple_of(c * tk, tk)
                kc = k_ref[pl.ds(off, tk), :]
                vc = v_ref[pl.ds(off, tk), :]
                sc = lax.dot_general(qm, kc, NT, preferred_element_type=F32)
                m_new = jnp.maximum(m, jnp.max(sc, axis=1, keepdims=True))
                alpha = jnp.exp(m - m_new)
                p = jnp.exp(sc - m_new)
                l = alpha * l + jnp.sum(p, axis=1, keepdims=True)
                acc = alpha * acc + lax.dot_general(p.astype(BF16), vc, NN, preferred_element_type=F32)
                return m_new, l, acc

            init = (jnp.full((tq, 1), NEG_INF, F32), jnp.zeros((tq, 1), F32), jnp.zeros((tq, LANES), F32))
            m, l, acc = lax.fori_loop(0, n_chunks, chunk, init)
            outs.append(acc / l)
            lses.append(m + jnp.log(l))
        low = lane < HEAD_DIM
        o_ref[...] = jnp.where(low, outs[0], outs[1]).astype(BF16)
        lse_ref[...] = jnp.where(low, lses[0], lses[1])

    q_spec = pl.BlockSpec((tq, LANES), lambda hb, qi: (qi, hb))
    kv_spec = pl.BlockSpec((s, LANES), lambda hb, qi: (0, hb // 2))
    return pl.pallas_call(
        body, name=name, grid=(A_Q_WIDTH // LANES, s // tq),
        in_specs=[q_spec, kv_spec, kv_spec], out_specs=[q_spec, q_spec],
        out_shape=[jax.ShapeDtypeStruct((s, A_Q_WIDTH), BF16), jax.ShapeDtypeStruct((s, A_Q_WIDTH), F32)],
        compiler_params=_params(),
    )(qkv, kdup, vdup)


def _attn_a_bwd(qkv, kdup, vdup, o, lse, do, *, name, tq=256, tk=512):
    s = qkv.shape[0]
    tq, tk = min(tq, s), min(tk, s)
    n_chunks = s // tk

    def body(q_ref, k_ref, v_ref, o_ref, lse_ref, do_ref, dq_ref, dk_ref, dv_ref):
        first = (pl.program_id(1) == 0) & (pl.program_id(2) == 0)

        @pl.when(first)
        def _():
            dk_ref[...] = jnp.zeros_like(dk_ref)
            dv_ref[...] = jnp.zeros_like(dv_ref)

        lane = _lane_iota()
        q = q_ref[...]
        dov = do_ref[...]
        prod = dov.astype(F32) * o_ref[...].astype(F32)
        lsev = lse_ref[...]
        dqs = []
        for e in range(2):
            mine = (lane >= HEAD_DIM) if e else (lane < HEAD_DIM)
            qs = jnp.where(mine, q, jnp.zeros_like(q)) * ATTN_SCALE
            dom = jnp.where(mine, dov, jnp.zeros_like(dov))
            delta = jnp.sum(jnp.where(mine, prod, 0.0), axis=1, keepdims=True)
            lse_e = lsev[:, e * HEAD_DIM:e * HEAD_DIM + 1]

            def chunk(c, dq_acc):
                off = pl.multiple_of(c * tk, tk)
                kc = k_ref[pl.ds(off, tk), :]
                vc = v_ref[pl.ds(off, tk), :]
                sc = lax.dot_general(qs, kc, NT, preferred_element_type=F32)
                p = jnp.exp(sc - lse_e)
                dp = lax.dot_general(dom, vc, NT, preferred_element_type=F32)
                ds = (p * (dp - delta)).astype(BF16)
                dv_ref[pl.ds(off, tk), :] += lax.dot_general(p.astype(BF16), dom, TN, preferred_element_type=F32)
                dk_ref[pl.ds(off, tk), :] += lax.dot_general(ds, qs, TN, preferred_element_type=F32)
                return dq_acc + lax.dot_general(ds, kc, NN, preferred_element_type=F32)

            dq_e = lax.fori_loop(0, n_chunks, chunk, jnp.zeros((tq, LANES), F32))
            dqs.append(dq_e * ATTN_SCALE)
        dq_ref[...] = jnp.where(lane < HEAD_DIM, dqs[0], dqs[1])

    q_spec = pl.BlockSpec((tq, LANES), lambda kvh, hb, qi: (qi, kvh * 2 + hb))
    kv_spec = pl.BlockSpec((s, LANES), lambda kvh, hb, qi: (0, kvh))
    return pl.pallas_call(
        body, name=name, grid=(2, 2, s // tq),
        in_specs=[q_spec, kv_spec, kv_spec, q_spec, q_spec, q_spec],
        out_specs=[q_spec, kv_spec, kv_spec],
        out_shape=[jax.ShapeDtypeStruct((s, A_Q_WIDTH), F32),
                   jax.ShapeDtypeStruct((s, 2 * LANES), F32), jax.ShapeDtypeStruct((s, 2 * LANES), F32)],
        compiler_params=_params(),
    )(qkv, kdup, vdup, o, lse, do)


BAND_Q = 128
B_Q_BLOCK0 = A_BLOCKS
B_K_BLOCK0 = A_BLOCKS + B_WIDTH // LANES
B_V_BLOCK0 = A_BLOCKS + 2 * (B_WIDTH // LANES)
TOKEN_BLOCKS = QKV_WIDTH // LANES


def _band_geometry(length):
    seg = min(length, 2048)
    win = min(2 * BAND_Q, length)
    return seg, win


def _band_window(qs, length, win):
    st = jnp.clip(qs - B_HALF_SPAN, 0, length - win)
    st = pl.multiple_of(st, B_HALF_SPAN)
    qpos = qs + lax.broadcasted_iota(jnp.int32, (BAND_Q, 1), 0)
    kpos = st + lax.broadcasted_iota(jnp.int32, (1, win), 1)
    return st, jnp.abs(qpos - kpos) <= B_HALF_SPAN


def _band_fwd(qkv_view, gi, dil, *, name):
    length = qkv_view.shape[0]
    seg, win = _band_geometry(length)
    n_sub = seg // BAND_Q

    def body(q_ref, k_ref, v_ref, o_ref, lse_ref):
        seg_i = pl.program_id(2)
        lane = _lane_iota()
        low = lane < HEAD_DIM

        def sub(i, carry):
            ql = pl.multiple_of(i * BAND_Q, BAND_Q)
            st, valid = _band_window(seg_i * seg + ql, length, win)
            q = q_ref[pl.ds(ql, BAND_Q), :]
            kw = k_ref[pl.ds(st, win), :]
            vw = v_ref[pl.ds(st, win), :]
            outs, lses = [], []
            for e in range(2):
                mine = (lane >= HEAD_DIM) if e else (lane < HEAD_DIM)
                qm = jnp.where(mine, q, jnp.zeros_like(q)) * ATTN_SCALE
                sc = lax.dot_general(qm, kw, NT, preferred_element_type=F32)
                sc = jnp.where(valid, sc, NEG_INF)
                m = jnp.max(sc, axis=1, keepdims=True)
                p = jnp.exp(sc - m)
                l = jnp.sum(p, axis=1, keepdims=True)
                outs.append(lax.dot_general(p.astype(BF16), vw, NN, preferred_element_type=F32) / l)
                lses.append(m + jnp.log(l))
            o_ref[pl.ds(ql, BAND_Q), :] = jnp.where(low, outs[0], outs[1]).astype(BF16)
            lse_ref[pl.ds(ql, BAND_Q), :] = jnp.where(low, lses[0], lses[1])
            return carry

        lax.fori_loop(0, n_sub, sub, 0)

    def col(base):
        return lambda r, hp, sg: (0, r * TOKEN_BLOCKS + base + gi * 2 + hp)

    q_spec = pl.BlockSpec((seg, LANES), lambda r, hp, sg: (sg, r * TOKEN_BLOCKS + B_Q_BLOCK0 + gi * 2 + hp))
    out_spec = pl.BlockSpec((seg, LANES), lambda r, hp, sg: (sg, r * 2 + hp))
    return pl.pallas_call(
        body, name=name, grid=(dil, 2, length // seg),
        in_specs=[q_spec, pl.BlockSpec((length, LANES), col(B_K_BLOCK0)), pl.BlockSpec((length, LANES), col(B_V_BLOCK0))],
        out_specs=[out_spec, out_spec],
        out_shape=[jax.ShapeDtypeStruct((length, dil * B_OUT_WIDTH), BF16),
                   jax.ShapeDtypeStruct((length, dil * B_OUT_WIDTH), F32)],
        compiler_params=_params(),
    )(qkv_view, qkv_view, qkv_view)


def _band_bwd(qkv_view, do, lse, dd, gi, dil, *, name):
    length = qkv_view.shape[0]
    seg, win = _band_geometry(length)
    n_sub = seg // BAND_Q

    def body(q_ref, k_ref, v_ref, do_ref, lse_ref, dd_ref, dq_ref, dk_ref, dv_ref):
        seg_i = pl.program_id(2)
        lane = _lane_iota()

        @pl.when(seg_i == 0)
        def _():
            dk_ref[...] = jnp.zeros_like(dk_ref)
            dv_ref[...] = jnp.zeros_like(dv_ref)

        def sub(i, carry):
            ql = pl.multiple_of(i * BAND_Q, BAND_Q)
            st, valid = _band_window(seg_i * seg + ql, length, win)
            q = q_ref[pl.ds(ql, BAND_Q), :]
            dov = do_ref[pl.ds(ql, BAND_Q), :]
            lsev = lse_ref[pl.ds(ql, BAND_Q), :]
            ddv = dd_ref[pl.ds(ql, BAND_Q), :]
            kw = k_ref[pl.ds(st, win), :]
            vw = v_ref[pl.ds(st, win), :]
            dq = jnp.zeros((BAND_Q, LANES), F32)
            for e in range(2):
                mine = (lane >= HEAD_DIM) if e else (lane < HEAD_DIM)
                qs = jnp.where(mine, q, jnp.zeros_like(q)) * ATTN_SCALE
                dom = jnp.where(mine, dov, jnp.zeros_like(dov))
                lse_e = lsev[:, e * HEAD_DIM:e * HEAD_DIM + 1]
                dd_e = ddv[:, e * HEAD_DIM:e * HEAD_DIM + 1]
                sc = lax.dot_general(qs, kw, NT, preferred_element_type=F32)
                p = jnp.exp(jnp.where(valid, sc, NEG_INF) - lse_e)
                dp = lax.dot_general(dom, vw, NT, preferred_element_type=F32)
                ds = (p * (dp - dd_e)).astype(BF16)
                dv_ref[pl.ds(st, win), :] += lax.dot_general(p.astype(BF16), dom, TN, preferred_element_type=F32)
                dk_ref[pl.ds(st, win), :] += lax.dot_general(ds, qs, TN, preferred_element_type=F32)
                dq_e = lax.dot_general(ds, kw, NN, preferred_element_type=F32) * ATTN_SCALE
                dq = dq + jnp.where(mine, dq_e, 0.0)
            dq_ref[pl.ds(ql, BAND_Q), :] = dq
            return carry

        lax.fori_loop(0, n_sub, sub, 0)

    def col(base):
        return lambda r, hp, sg: (0, r * TOKEN_BLOCKS + base + gi * 2 + hp)

    q_spec = pl.BlockSpec((seg, LANES), lambda r, hp, sg: (sg, r * TOKEN_BLOCKS + B_Q_BLOCK0 + gi * 2 + hp))
    seg_spec = pl.BlockSpec((seg, LANES), lambda r, hp, sg: (sg, r * 2 + hp))
    full_spec = pl.BlockSpec((length, LANES), lambda r, hp, sg: (0, r * 2 + hp))
    shp = jax.ShapeDtypeStruct((length, dil * B_OUT_WIDTH), F32)
    return pl.pallas_call(
        body, name=name, grid=(dil, 2, length // seg),
        in_specs=[q_spec, pl.BlockSpec((length, LANES), col(B_K_BLOCK0)), pl.BlockSpec((length, LANES), col(B_V_BLOCK0)),
                  seg_spec, seg_spec, seg_spec],
        out_specs=[seg_spec, full_spec, full_spec],
        out_shape=[shp, shp, shp],
        compiler_params=_params(),
    )(qkv_view, qkv_view, qkv_view, do, lse, dd)


def _merge_weights(lses):
    m = jnp.maximum(jnp.maximum(lses[0], lses[1]), lses[2])
    ex = [jnp.exp(v - m) for v in lses]
    tot = ex[0] + ex[1] + ex[2]
    return [v / tot for v in ex]


def _merge_fwd(os_, lses, *, name, tm=512):
    s = os_[0].shape[0]

    def body(o0, o1, o2, l0, l1, l2, ob_ref):
        w = _merge_weights([l0[...], l1[...], l2[...]])
        ob = w[0] * o0[...].astype(F32) + w[1] * o1[...].astype(F32) + w[2] * o2[...].astype(F32)
        ob_ref[...] = ob.astype(BF16)

    blk = pl.BlockSpec((tm, B_OUT_WIDTH), lambda i: (i, 0))
    return pl.pallas_call(
        body, name=name, grid=(s // tm,), in_specs=[blk] * 6, out_specs=blk,
        out_shape=jax.ShapeDtypeStruct((s, B_OUT_WIDTH), BF16),
        compiler_params=_params(),
    )(*os_, *lses)


def _merge_bwd(os_, lses, dob, *, name, tm=512):
    s = os_[0].shape[0]

    def body(o0, o1, o2, l0, l1, l2, dob_ref, d0, d1, d2, t0, t1, t2):
        lane = _lane_iota(B_OUT_WIDTH)
        w = _merge_weights([l0[...], l1[...], l2[...]])
        dv = dob_ref[...]
        ob = w[0] * o0[...].astype(F32) + w[1] * o1[...].astype(F32) + w[2] * o2[...].astype(F32)
        tot = _head_sum(dv * ob, lane)
        for wg, d_ref, t_ref in zip(w, (d0, d1, d2), (t0, t1, t2)):
            d_ref[...] = (wg * dv).astype(BF16)
            t_ref[...] = wg * tot

    blk = pl.BlockSpec((tm, B_OUT_WIDTH), lambda i: (i, 0))
    return pl.pallas_call(
        body, name=name, grid=(s // tm,), in_specs=[blk] * 7, out_specs=[blk] * 6,
        out_shape=[jax.ShapeDtypeStruct((s, B_OUT_WIDTH), BF16)] * 3 + [jax.ShapeDtypeStruct((s, B_OUT_WIDTH), F32)] * 3,
        compiler_params=_params(),
    )(*os_, *lses, dob)


HALO = 8
POOL_BLOCK0 = QKV_WIDTH // LANES


def _window_sum(ext, lo, hi, tm):
    rows = ext.shape[0]
    acc = None
    for j in range(lo, hi + 1):
        r = ext if j == 0 else pltpu.roll(ext, (-j) % rows, axis=0)
        acc = r if acc is None else acc + r
    return acc[HALO:HALO + tm]


def _pool_counts(t, half, s):
    return (jnp.minimum(t + half, s) - jnp.maximum(t - half, 0)).astype(F32)


def _halo_specs(tm, s, col0):
    per = tm // HALO
    last = s // HALO - 1
    prev = pl.BlockSpec((HALO, LANES), lambda g, i: (jnp.maximum(i * per - 1, 0), col0 + g))
    cur = pl.BlockSpec((tm, LANES), lambda g, i: (i, col0 + g))
    nxt = pl.BlockSpec((HALO, LANES), lambda g, i: (jnp.minimum((i + 1) * per, last), col0 + g))
    return prev, cur, nxt


def _extended(prev_ref, cur_ref, next_ref, i, n_tiles):
    prev = jnp.where(i > 0, prev_ref[...], 0.0)
    nxt = jnp.where(i < n_tiles - 1, next_ref[...], 0.0)
    return jnp.concatenate([prev, cur_ref[...], nxt], axis=0)


def _pool_fwd(z, lin, scale, *, name, tm=512):
    s = z.shape[0]
    tm = min(tm, s)
    n_tiles = s // tm

    def body(prev_ref, cur_ref, next_ref, lin_ref, sc_ref, pooled_ref, mixed_ref):
        g = pl.program_id(0)
        i = pl.program_id(1)
        ext = _extended(prev_ref, cur_ref, next_ref, i, n_tiles)
        t = i * tm + lax.broadcasted_iota(jnp.int32, (tm, 1), 0)
        for gi, half in enumerate(POOL_HALF):
            @pl.when(g == gi)
            def _(half=half):
                mean = _window_sum(ext, -half, half - 1, tm) / _pool_counts(t, half, s)
                pooled = (mean - cur_ref[...]).astype(BF16)
                pooled_ref[...] = pooled
                mixed = lax.dot_general(pooled, lin_ref[...].astype(BF16), NN, preferred_element_type=F32)
                mixed_ref[...] = (mixed * sc_ref[...]).astype(BF16)

    prev, cur, nxt = _halo_specs(tm, s, POOL_BLOCK0)
    out = pl.BlockSpec((tm, LANES), lambda g, i: (i, g))
    return pl.pallas_call(
        body, name=name, grid=(len(POOL_HALF), n_tiles),
        in_specs=[prev, cur, nxt, pl.BlockSpec((None, LANES, LANES), lambda g, i: (g, 0, 0)),
                  pl.BlockSpec((1, LANES), lambda g, i: (0, g))],
        out_specs=[out, out],
        out_shape=[jax.ShapeDtypeStruct((s, POOL_WIDTH), BF16)] * 2,
        compiler_params=_params(),
    )(z, z, z, lin, scale)


def _pool_bwd(dmixed, pooled, lin, scale, *, name, tm=512):
    s = dmixed.shape[0]
    tm = min(tm, s)
    n_tiles = s // tm

    def body(prev_ref, cur_ref, next_ref, pooled_ref, lin_ref, sc_ref, du_ref, dlin_ref, dsc_ref):
        g = pl.program_id(0)
        i = pl.program_id(1)

        @pl.when(i == 0)
        def _():
            dlin_ref[...] = jnp.zeros_like(dlin_ref)
            dsc_ref[...] = jnp.zeros_like(dsc_ref)

        linb = lin_ref[...].astype(BF16)
        ext = _extended(prev_ref, cur_ref, next_ref, i, n_tiles)
        dpl_ext = (ext * sc_ref[...]).astype(BF16)
        dpl_cur = (cur_ref[...] * sc_ref[...]).astype(BF16)
        dpooled_ext = lax.dot_general(dpl_ext, linb, NT, preferred_element_type=F32)
        t_ext = i * tm - HALO + lax.broadcasted_iota(jnp.int32, (tm + 2 * HALO, 1), 0)
        pooled = pooled_ref[...]
        mixed = lax.dot_general(pooled, linb, NN, preferred_element_type=F32)
        dsc_ref[...] += jnp.sum(cur_ref[...] * mixed, axis=0, keepdims=True)
        dlin_ref[...] += lax.dot_general(pooled, dpl_cur, TN, preferred_element_type=F32)
        for gi, half in enumerate(POOL_HALF):
            @pl.when(g == gi)
            def _(half=half):
                share = dpooled_ext / jnp.maximum(_pool_counts(t_ext, half, s), 1.0)
                du = _window_sum(share, -(half - 1), half, tm) - dpooled_ext[HALO:HALO + tm]
                du_ref[...] = du.astype(BF16)

    prev, cur, nxt = _halo_specs(tm, s, 0)
    out = pl.BlockSpec((tm, LANES), lambda g, i: (i, g))
    lin_spec = pl.BlockSpec((None, LANES, LANES), lambda g, i: (g, 0, 0))
    vec = pl.BlockSpec((1, LANES), lambda g, i: (0, g))
    return pl.pallas_call(
        body, name=name, grid=(len(POOL_HALF), n_tiles),
        in_specs=[prev, cur, nxt, out, lin_spec, vec],
        out_specs=[out, lin_spec, vec],
        out_shape=[jax.ShapeDtypeStruct((s, POOL_WIDTH), BF16),
                   jax.ShapeDtypeStruct((len(POOL_HALF), LANES, LANES), F32),
                   jax.ShapeDtypeStruct((1, POOL_WIDTH), F32)],
        compiler_params=_params(),
    )(dmixed, dmixed, dmixed, pooled, lin, scale)


GATE_TILE = 512
GATE_BLOCK0 = (QKV_WIDTH + POOL_WIDTH) // GATE_TILE
GATE_BLOCKS_PER_BRANCH = D_MODEL // GATE_TILE


def _sigmoid(v):
    return 1.0 / (1.0 + jnp.exp(-v))


def _gate_specs(tm):
    def zspec(br):
        return pl.BlockSpec((tm, GATE_TILE), lambda jj, i: (i, GATE_BLOCK0 + GATE_BLOCKS_PER_BRANCH * br + jj))

    def bspec(br):
        return pl.BlockSpec((1, GATE_TILE), lambda jj, i: (0, GATE_BLOCKS_PER_BRANCH * br + jj))

    row = pl.BlockSpec((tm, GATE_TILE), lambda jj, i: (i, jj))
    vec = pl.BlockSpec((1, GATE_TILE), lambda jj, i: (0, jj))
    return [zspec(0), zspec(1), zspec(2)], [bspec(0), bspec(1), bspec(2)], row, vec


def _gate_fwd(z, b_gate, ya, yb, yc, *, name, tm=512):
    s = z.shape[0]

    def body(z0, z1, z2, b0, b1, b2, ya_ref, yb_ref, yc_ref, out_ref):
        acc = _sigmoid(z0[...] + b0[...]) * ya_ref[...]
        acc = acc + _sigmoid(z1[...] + b1[...]) * yb_ref[...]
        acc = acc + _sigmoid(z2[...] + b2[...]) * yc_ref[...]
        out_ref[...] = acc.astype(BF16)

    zs, bs, row, _ = _gate_specs(tm)
    return pl.pallas_call(
        body, name=name, grid=(GATE_BLOCKS_PER_BRANCH, s // tm),
        in_specs=zs + bs + [row] * 3, out_specs=row,
        out_shape=jax.ShapeDtypeStruct((s, D_MODEL), BF16),
        compiler_params=_params(),
    )(z, z, z, b_gate, b_gate, b_gate, ya, yb, yc)


def _gate_bwd(z, b_gate, ya, yb, yc, dmerged, *, name, tm=512):
    s = z.shape[0]

    def body(z0, z1, z2, b0, b1, b2, ya_ref, yb_ref, yc_ref, dm_ref,
             dya_ref, dyb_ref, dyc_ref, dg0, dg1, dg2, db0, db1, db2):
        i = pl.program_id(1)
        dm = dm_ref[...]
        for z_ref, b_ref, y_ref, dy_ref, dg_ref, db_ref in (
                (z0, b0, ya_ref, dya_ref, dg0, db0), (z1, b1, yb_ref, dyb_ref, dg1, db1),
                (z2, b2, yc_ref, dyc_ref, dg2, db2)):
            gate = _sigmoid(z_ref[...] + b_ref[...])
            dy_ref[...] = (gate * dm).astype(BF16)
            dpre = dm * y_ref[...] * gate * (1.0 - gate)
            dg_ref[...] = dpre.astype(BF16)
            part = jnp.sum(dpre, axis=0, keepdims=True)

            @pl.when(i == 0)
            def _(db_ref=db_ref, part=part):
                db_ref[...] = part

            @pl.when(i > 0)
            def _(db_ref=db_ref, part=part):
                db_ref[...] += part

    zs, bs, row, vec = _gate_specs(tm)
    big = jax.ShapeDtypeStruct((s, D_MODEL), BF16)
    small = jax.ShapeDtypeStruct((1, D_MODEL), F32)
    return pl.pallas_call(
        body, name=name, grid=(GATE_BLOCKS_PER_BRANCH, s // tm),
        in_specs=zs + bs + [row] * 4, out_specs=[row] * 6 + [vec] * 3,
        out_shape=[big] * 6 + [small] * 3,
        compiler_params=_params(),
    )(z, z, z, b_gate, b_gate, b_gate, ya, yb, yc, dmerged)


def _swiglu_fwd(a, b, *, name, tm=512, tn=256):
    s, f = a.shape

    def body(a_ref, b_ref, o_ref):
        av = a_ref[...]
        o_ref[...] = (av * _sigmoid(av) * b_ref[...]).astype(BF16)

    blk = pl.BlockSpec((tm, tn), lambda i, j: (i, j))
    return pl.pallas_call(
        body, name=name, grid=(s // tm, f // tn), in_specs=[blk, blk], out_specs=blk,
        out_shape=jax.ShapeDtypeStruct((s, f), BF16), compiler_params=_params(),
    )(a, b)


def _swiglu_bwd(a, b, df, *, name, tm=512, tn=256):
    s, f = a.shape

    def body(a_ref, b_ref, df_ref, da_ref, db_ref):
        av = a_ref[...]
        dfv = df_ref[...]
        sg = _sigmoid(av)
        silu = av * sg
        da_ref[...] = (dfv * b_ref[...] * (sg + silu * (1.0 - sg))).astype(BF16)
        db_ref[...] = (dfv * silu).astype(BF16)

    blk = pl.BlockSpec((tm, tn), lambda i, j: (i, j))
    out = jax.ShapeDtypeStruct((s, f), BF16)
    return pl.pallas_call(
        body, name=name, grid=(s // tm, f // tn), in_specs=[blk] * 3, out_specs=[blk] * 2,
        out_shape=[out, out], compiler_params=_params(),
    )(a, b, df)


def _loss_head(y, target, *, name, tm=512):
    s, d = y.shape

    def body(y_ref, t_ref, part_ref, dy_ref):
        i = pl.program_id(0)
        err = y_ref[...] - t_ref[...]
        dy_ref[...] = err * (1.0 / d)
        part = jnp.sum(err * err, axis=0, keepdims=True) * (0.5 / d)

        @pl.when(i == 0)
        def _():
            part_ref[...] = part

        @pl.when(i > 0)
        def _():
            part_ref[...] += part

    row = pl.BlockSpec((tm, d), lambda i: (i, 0))
    vec = pl.BlockSpec((1, d), lambda i: (0, 0))
    return pl.pallas_call(
        body, name=name, grid=(s // tm,), in_specs=[row, row], out_specs=[vec, row],
        out_shape=[jax.ShapeDtypeStruct((1, d), F32), jax.ShapeDtypeStruct((s, d), F32)],
        compiler_params=_params(),
    )(y, target)


def _mesh_place():
    x, y, c = lax.axis_index('x'), lax.axis_index('y'), lax.axis_index('c')
    return x, y, c, 4 * x + 2 * y + c


def _peer(x, y, c, k):
    return (x ^ ((k >> 2) & 1), y ^ ((k >> 1) & 1), c ^ (k & 1))


def _exchange(buf, *, gather, name):
    shape = buf.shape[-2:]

    def body(in_ref, out_ref, send_sems, recv_sems, local_sem):
        x, y, c, me = _mesh_place()

        def src(slot):
            return in_ref if gather else in_ref.at[slot]

        def remote(k):
            return pltpu.make_async_remote_copy(
                src_ref=src(me ^ k), dst_ref=out_ref.at[me],
                send_sem=send_sems.at[k - 1], recv_sem=recv_sems.at[k - 1],
                device_id=_peer(x, y, c, k), device_id_type=pl.DeviceIdType.MESH)

        def arrival(k):
            return pltpu.make_async_remote_copy(
                src_ref=src(me ^ k), dst_ref=out_ref.at[me ^ k],
                send_sem=send_sems.at[k - 1], recv_sem=recv_sems.at[k - 1],
                device_id=_peer(x, y, c, k), device_id_type=pl.DeviceIdType.MESH)

        mine = pltpu.make_async_copy(src(me), out_ref.at[me], local_sem)
        mine.start()
        sends = [remote(k) for k in range(1, N_DEV)]
        for cp in sends:
            cp.start()
        for k in range(1, N_DEV):
            arrival(k).wait_recv()
        for cp in sends:
            cp.wait_send()
        mine.wait()

    return pl.pallas_call(
        body, name=name,
        in_specs=[pl.BlockSpec(memory_space=pl.ANY)], out_specs=pl.BlockSpec(memory_space=pl.ANY),
        out_shape=jax.ShapeDtypeStruct((N_DEV,) + shape, buf.dtype),
        scratch_shapes=[pltpu.SemaphoreType.DMA((N_DEV - 1,)), pltpu.SemaphoreType.DMA((N_DEV - 1,)),
                        pltpu.SemaphoreType.DMA],
    )(buf)


def _adamw(parts, w, m, v, *, name, tr):
    rows, cols = w.shape
    assert rows % tr == 0
    bias1 = 1.0 - ADAM_B1 ** ADAM_STEP
    bias2 = 1.0 - ADAM_B2 ** ADAM_STEP

    def body(p_ref, w_ref, m_ref, v_ref, g_ref, d_ref, nm_ref, nv_ref):
        g = p_ref[0]
        for j in range(1, N_DEV):
            g = g + p_ref[j]
        nm = ADAM_B1 * m_ref[...] + (1.0 - ADAM_B1) * g
        nv = ADAM_B2 * v_ref[...] + (1.0 - ADAM_B2) * (g * g)
        g_ref[...] = g
        nm_ref[...] = nm
        nv_ref[...] = nv
        d_ref[...] = -ADAM_LR * ((nm / bias1) / (jnp.sqrt(nv / bias2) + ADAM_EPS) + ADAM_WD * w_ref[...])

    blk = pl.BlockSpec((tr, cols), lambda i: (i, 0))
    out = jax.ShapeDtypeStruct((rows, cols), F32)
    return pl.pallas_call(
        body, name=name, grid=(rows // tr,),
        in_specs=[pl.BlockSpec((N_DEV, tr, cols), lambda i: (0, i, 0)), blk, blk, blk],
        out_specs=[blk] * 4, out_shape=[out] * 4, compiler_params=_params(),
    )(parts, w, m, v)


def _col_blocks(full):
    r, n = full.shape
    return full.reshape(r, N_DEV, n // N_DEV).transpose(1, 0, 2)


def _row_blocks(full):
    r, n = full.shape
    return full.reshape(N_DEV, r // N_DEV, n)


def _from_col_blocks(blocks):
    j, r, c = blocks.shape
    return blocks.transpose(1, 0, 2).reshape(r, j * c)


def _from_row_blocks(blocks):
    j, r, c = blocks.shape
    return blocks.reshape(j * r, c)


SHARD_ROWWISE = {'w_out', 'w_ffn_down'}


def _pack(shards):
    return jnp.concatenate([shards[n].reshape(-1, PACK_COLS) for n in SHARDED], axis=0)


def _unpack(packed, shapes):
    out, off = {}, 0
    lead = packed.shape[:-2]
    for n in SHARDED:
        size = shapes[n][0] * shapes[n][1] // PACK_COLS
        out[n] = packed[..., off:off + size, :].reshape(lead + tuple(shapes[n]))
        off += size
    return out


def _pack_small(vals):
    flat = jnp.concatenate([vals[n].reshape(-1) for n in SMALL])
    return flat.reshape(-1, PACK_COLS)


def _unpack_small(packed, shapes):
    flat = packed.reshape(-1)
    out, off = {}, 0
    for n in SMALL:
        size = 1
        for dim in shapes[n]:
            size *= dim
        out[n] = flat[off:off + size].reshape(shapes[n])
        off += size
    return out


def _band_views(qkv, s):
    return [qkv.reshape(s // d, d * QKV_WIDTH) for d in B_DILATIONS]


def _layer_fwd(x, p, tables, tag):
    s = x.shape[0]
    cos, sin = tables
    h = _rms_fwd(x, p['norm_mix'], name=f'rms_mix_fwd')
    z = _mm(h, p['w_in'], dims='nn', out_dtype=F32, tm=1024, tn=512, tk=1024, name='mm_in_fwd')
    gains = _qk_gains(p['qn_a'], p['kn_a'], p['qn_b'], p['kn_b'])
    qkv = _qkrope_fwd(z, gains, cos, sin, name='qkrope_fwd')

    kdup = _dup_halves(qkv[:, A_Q_WIDTH:A_Q_WIDTH + A_KV_WIDTH])
    vdup = _dup_halves(qkv[:, A_Q_WIDTH + A_KV_WIDTH:A_Q_WIDTH + 2 * A_KV_WIDTH])
    oa, lse_a = _attn_a_fwd(qkv, kdup, vdup, name='attn_a_fwd')
    ya = _mm(oa, p['w_branch_a'], dims='nn', out_dtype=F32, tm=1024, tn=512, tk=512, name='mm_branch_a_fwd')

    views = _band_views(qkv, s)
    o_g, lse_g = [], []
    for gi, d in enumerate(B_DILATIONS):
        o, lse = _band_fwd(views[gi], gi, d, name=f'band_fwd_d{d}')
        o_g.append(o.reshape(s, B_OUT_WIDTH))
        lse_g.append(lse.reshape(s, B_OUT_WIDTH))
    ob = _merge_fwd(o_g, lse_g, name='merge_fwd')
    yb = _mm(ob, p['w_branch_b'], dims='nn', out_dtype=F32, tm=1024, tn=512, tk=256, name='mm_branch_b_fwd')

    pooled, mixed = _pool_fwd(z, p['pool_lin'], p['pool_scale'], name='pool_fwd')
    yc = _mm(mixed, p['w_branch_c'], dims='nn', out_dtype=F32, tm=1024, tn=512, tk=512, name='mm_branch_c_fwd')

    merged = _gate_fwd(z, p['b_gate'], ya, yb, yc, name='gate_fwd')
    x_mid = _mm(merged, p['w_out'], dims='nn', out_dtype=F32, tm=1024, tn=512, tk=1024, res=x, name='mm_out_fwd')

    h2 = _rms_fwd(x_mid, p['norm_ffn'], name='rms_ffn_fwd')
    fa = _mm(h2, p['w_ffn_gate'], dims='nn', out_dtype=F32, tm=1024, tn=256, tk=1024, name='mm_ffn_gate_fwd')
    fb = _mm(h2, p['w_ffn_up'], dims='nn', out_dtype=F32, tm=1024, tn=256, tk=1024, name='mm_ffn_up_fwd')
    f = _swiglu_fwd(fa, fb, name='swiglu_fwd')
    x_out = _mm(f, p['w_ffn_down'], dims='nn', out_dtype=F32, tm=1024, tn=512, tk=2816, res=x_mid, name='mm_ffn_down_fwd')

    saved = dict(x=x, h=h, z=z, gains=gains, qkv=qkv, kdup=kdup, vdup=vdup, oa=oa, lse_a=lse_a, o_g=o_g, lse_g=lse_g,
                 ob=ob, pooled=pooled, mixed=mixed, ya=ya, yb=yb, yc=yc, merged=merged, x_mid=x_mid, h2=h2,
                 fa=fa, fb=fb, f=f)
    return x_out, saved


def _fold_heads(v, heads):
    return v.reshape(heads, HEAD_DIM).sum(axis=0)


def _layer_bwd(dx, p, sv, tables):
    s = dx.shape[0]
    cos, sin = tables
    g = {}

    df = _mm(dx, p['w_ffn_down'], dims='nt', out_dtype=F32, tm=512, tn=256, tk=1024, name='mm_ffn_down_dx')
    g['w_ffn_down'] = _mm(sv['f'], dx, dims='tn', out_dtype=F32, tm=256, tn=1024, tk=512, name='mm_ffn_down_dw')
    da, db = _swiglu_bwd(sv['fa'], sv['fb'], df, name='swiglu_bwd')
    dh2 = _mm(da, p['w_ffn_gate'], dims='nt', out_dtype=F32, tm=512, tn=512, tk=2816, name='mm_ffn_gate_dx')
    dh2 = _mm(db, p['w_ffn_up'], dims='nt', out_dtype=F32, tm=512, tn=512, tk=2816, res=dh2, name='mm_ffn_up_dx')
    g['w_ffn_gate'] = _mm(sv['h2'], da, dims='tn', out_dtype=F32, tm=1024, tn=256, tk=512, name='mm_ffn_gate_dw')
    g['w_ffn_up'] = _mm(sv['h2'], db, dims='tn', out_dtype=F32, tm=1024, tn=256, tk=512, name='mm_ffn_up_dw')
    dx_mid, g['norm_ffn'] = _rms_bwd(sv['x_mid'], p['norm_ffn'], dh2, dx, name='rms_ffn_bwd')

    dmerged = _mm(dx_mid, p['w_out'], dims='nt', out_dtype=F32, tm=512, tn=512, tk=1024, name='mm_out_dx')
    g['w_out'] = _mm(sv['merged'], dx_mid, dims='tn', out_dtype=F32, tm=1024, tn=512, tk=512, name='mm_out_dw')
    dya, dyb, dyc, dg0, dg1, dg2, db0, db1, db2 = _gate_bwd(
        sv['z'], p['b_gate'], sv['ya'], sv['yb'], sv['yc'], dmerged, name='gate_bwd')
    g['b_gate'] = jnp.concatenate([db0, db1, db2], axis=1)

    doa = _mm(dya, p['w_branch_a'], dims='nt', out_dtype=BF16, tm=512, tn=512, tk=1024, name='mm_branch_a_dx')
    g['w_branch_a'] = _mm(sv['oa'], dya, dims='tn', out_dtype=F32, tm=512, tn=512, tk=512, name='mm_branch_a_dw')
    dqa, dkdup, dvdup = _attn_a_bwd(sv['qkv'], sv['kdup'], sv['vdup'], sv['oa'], sv['lse_a'], doa, name='attn_a_bwd')

    def fold(dup):
        return jnp.concatenate([dup[:, 0:64] + dup[:, 64:128], dup[:, 128:192] + dup[:, 192:256]], axis=1)

    dka, dva = fold(dkdup), fold(dvdup)

    dob = _mm(dyb, p['w_branch_b'], dims='nt', out_dtype=F32, tm=512, tn=256, tk=1024, name='mm_branch_b_dx')
    g['w_branch_b'] = _mm(sv['ob'], dyb, dims='tn', out_dtype=F32, tm=256, tn=512, tk=512, name='mm_branch_b_dw')
    merged_b = _merge_bwd(sv['o_g'], sv['lse_g'], dob, name='merge_bwd')
    do_g, dd_g = merged_b[:3], merged_b[3:]
    views = _band_views(sv['qkv'], s)
    dq_parts, dk_parts, dv_parts = [], [], []
    for gi, d in enumerate(B_DILATIONS):
        ln = s // d
        dq, dk, dv = _band_bwd(views[gi], do_g[gi].reshape(ln, d * B_OUT_WIDTH),
                               sv['lse_g'][gi].reshape(ln, d * B_OUT_WIDTH), dd_g[gi].reshape(ln, d * B_OUT_WIDTH),
                               gi, d, name=f'band_bwd_d{d}')
        dq_parts.append(dq.reshape(s, B_OUT_WIDTH))
        dk_parts.append(dk.reshape(s, B_OUT_WIDTH))
        dv_parts.append(dv.reshape(s, B_OUT_WIDTH))

    dmixed = _mm(dyc, p['w_branch_c'], dims='nt', out_dtype=F32, tm=512, tn=512, tk=1024, name='mm_branch_c_dx')
    g['w_branch_c'] = _mm(sv['mixed'], dyc, dims='tn', out_dtype=F32, tm=512, tn=512, tk=512, name='mm_branch_c_dw')
    du, g['pool_lin'], g['pool_scale'] = _pool_bwd(dmixed, sv['pooled'], p['pool_lin'], p['pool_scale'], name='pool_bwd')

    dqkv = jnp.concatenate([dqa, dka, dva] + dq_parts + dk_parts + dv_parts, axis=1)
    dz_qkv, dgains = _qkrope_bwd(sv['z'], sv['gains'], cos, sin, dqkv, name='qkrope_bwd')
    dgains = dgains[0]
    g['qn_a'] = _fold_heads(dgains[0:512], 8)
    g['kn_a'] = _fold_heads(dgains[512:640], 2)
    g['qn_b'] = _fold_heads(dgains[768:1536], 12)
    g['kn_b'] = _fold_heads(dgains[1536:2304], 12)

    dz = jnp.concatenate([dz_qkv, du, dg0, dg1, dg2], axis=1)
    dh = _mm(dz, p['w_in'], dims='nt', out_dtype=F32, tm=512, tn=512, tk=1664, name='mm_in_dx')
    g['w_in'] = _mm(sv['h'], dz, dims='tn', out_dtype=F32, tm=1024, tn=512, tk=512, name='mm_in_dw')
    dx_in, g['norm_mix'] = _rms_bwd(sv['x'], p['norm_mix'], dh, dx_mid, name='rms_mix_bwd')
    return dx_in, g


def _local_step(x, target, layers):
    s = x.shape[0]
    tables = _rope_tables(s)
    saved = []
    for l, p in enumerate(layers):
        x, sv = _layer_fwd(x, p, tables, l)
        saved.append(sv)
    part, dx = _loss_head(x, target, name='loss_head')
    loss = jnp.sum(part)
    grads = [None] * len(layers)
    for l in reversed(range(len(layers))):
        dx, grads[l] = _layer_bwd(dx, layers[l], saved[l], tables)
    return loss, dx, grads


def _small_views(vals, l):
    return {
        'norm_mix': vals['norm_mix'][l][None, :], 'b_gate': vals['b_gate'][l][None, :],
        'qn_a': vals['qn_a'][l], 'kn_a': vals['kn_a'][l], 'qn_b': vals['qn_b'][l], 'kn_b': vals['kn_b'][l],
        'pool_lin': vals['pool_lin'][l], 'pool_scale': vals['pool_scale'][l][None, :],
        'norm_ffn': vals['norm_ffn'][l][None, :],
    }


def kernel(x, norm_mix, w_in, b_gate, qn_a, kn_a, qn_b, kn_b, pool_lin, pool_scale, w_branch_a, w_branch_b, w_branch_c, w_out, norm_ffn, w_ffn_gate, w_ffn_up, w_ffn_down, loss_target, m_norm_mix, m_w_in, m_b_gate, m_qn_a, m_kn_a, m_qn_b, m_kn_b, m_pool_lin, m_pool_scale, m_w_branch_a, m_w_branch_b, m_w_branch_c, m_w_out, m_norm_ffn, m_w_ffn_gate, m_w_ffn_up, m_w_ffn_down, v_norm_mix, v_w_in, v_b_gate, v_qn_a, v_kn_a, v_qn_b, v_kn_b, v_pool_lin, v_pool_scale, v_w_branch_a, v_w_branch_b, v_w_branch_c, v_w_out, v_norm_ffn, v_w_ffn_gate, v_w_ffn_up, v_w_ffn_down):
    w = dict(norm_mix=norm_mix, w_in=w_in, b_gate=b_gate, qn_a=qn_a, kn_a=kn_a, qn_b=qn_b, kn_b=kn_b,
             pool_lin=pool_lin, pool_scale=pool_scale, w_branch_a=w_branch_a, w_branch_b=w_branch_b,
             w_branch_c=w_branch_c, w_out=w_out, norm_ffn=norm_ffn, w_ffn_gate=w_ffn_gate, w_ffn_up=w_ffn_up,
             w_ffn_down=w_ffn_down)
    m = dict(norm_mix=m_norm_mix, w_in=m_w_in, b_gate=m_b_gate, qn_a=m_qn_a, kn_a=m_kn_a, qn_b=m_qn_b, kn_b=m_kn_b,
             pool_lin=m_pool_lin, pool_scale=m_pool_scale, w_branch_a=m_w_branch_a, w_branch_b=m_w_branch_b,
             w_branch_c=m_w_branch_c, w_out=m_w_out, norm_ffn=m_norm_ffn, w_ffn_gate=m_w_ffn_gate,
             w_ffn_up=m_w_ffn_up, w_ffn_down=m_w_ffn_down)
    v = dict(norm_mix=v_norm_mix, w_in=v_w_in, b_gate=v_b_gate, qn_a=v_qn_a, kn_a=v_kn_a, qn_b=v_qn_b, kn_b=v_kn_b,
             pool_lin=v_pool_lin, pool_scale=v_pool_scale, w_branch_a=v_w_branch_a, w_branch_b=v_w_branch_b,
             w_branch_c=v_w_branch_c, w_out=v_w_out, norm_ffn=v_norm_ffn, w_ffn_gate=v_w_ffn_gate,
             w_ffn_up=v_w_ffn_up, w_ffn_down=v_w_ffn_down)
    depth = w_in.shape[0]
    shard_shapes = {n: w[n].shape[1:] for n in SHARDED}
    small_shapes = {n: w[n].shape for n in SMALL}

    layers = []
    for l in range(depth):
        packed = _pack({n: w[n][l].astype(BF16) for n in SHARDED})
        gathered = _exchange(packed, gather=True, name='gather_weights')
        blocks = _unpack(gathered, shard_shapes)
        full = {n: (_from_row_blocks(blocks[n]) if n in SHARD_ROWWISE else _from_col_blocks(blocks[n]))
                for n in SHARDED}
        full.update(_small_views(w, l))
        layers.append(full)

    loss, grad_x, grads = _local_step(x[0], loss_target[0], layers)
    loss = lax.psum(loss, ('x', 'y', 'c'))

    out_g, out_d, out_m, out_v = ({n: [] for n in SHARDED} for _ in range(4))
    for l in range(depth):
        blocks = {n: (_row_blocks(grads[l][n]) if n in SHARD_ROWWISE else _col_blocks(grads[l][n])) for n in SHARDED}
        to_send = jnp.concatenate([blocks[n].reshape(N_DEV, -1, PACK_COLS) for n in SHARDED], axis=1)
        parts = _exchange(to_send, gather=False, name='scatter_grads')
        res = _adamw(parts, _pack({n: w[n][l] for n in SHARDED}), _pack({n: m[n][l] for n in SHARDED}),
                     _pack({n: v[n][l] for n in SHARDED}), name='adamw_sharded', tr=128)
        for dst, packed in zip((out_g, out_d, out_m, out_v), res):
            un = _unpack(packed, shard_shapes)
            for n in SHARDED:
                dst[n].append(un[n])
    new = {}
    for n in SHARDED:
        new[n] = tuple(jnp.stack(dst[n]) for dst in (out_g, out_d, out_m, out_v))

    small_grad = {n: jnp.stack([grads[l][n].reshape(small_shapes[n][1:]) for l in range(depth)]) for n in SMALL}
    parts = _exchange(_pack_small(small_grad), gather=True, name='gather_small_grads')
    res = _adamw(parts, _pack_small({n: w[n] for n in SMALL}), _pack_small({n: m[n] for n in SMALL}),
                 _pack_small({n: v[n] for n in SMALL}), name='adamw_small', tr=parts.shape[1])
    unpacked = [_unpack_small(r, small_shapes) for r in res]
    for n in SMALL:
        new[n] = tuple(u[n] for u in unpacked)

    outs = [loss, grad_x[None]]
    for idx in range(4):
        outs.extend(new[n][idx] for n in WEIGHTS)
    return tuple(outs)
```

```python
import functools

import jax
import jax.numpy as jnp
from jax import lax
from jax.experimental import pallas as pl
from jax.experimental.pallas import tpu as pltpu

F32 = jnp.float32
BF16 = jnp.bfloat16

N_DEV = 8
D_MODEL = 1024
DEPTH = 4
HEAD_DIM = 64
LANES = 128
A_Q_WIDTH = 512
A_KV_WIDTH = 128
B_WIDTH = 768
B_GROUPS = 3
B_DILATIONS = (1, 4, 16)
B_HALF_SPAN = 64
B_OUT_WIDTH = 256
POOL_WIDTH = 512
POOL_HALF = (1, 2, 4, 8)
GATE_WIDTH = 3072
QKV_WIDTH = A_Q_WIDTH + 2 * A_KV_WIDTH + 3 * B_WIDTH
IN_WIDTH = QKV_WIDTH + POOL_WIDTH + GATE_WIDTH
D_FF = 2816
GRID_W = 64
ROPE_THETA = 10000.0
EPS = 1e-6
NEG_INF = -1e30
ATTN_SCALE = HEAD_DIM ** -0.5

ADAM_LR = 0.001
ADAM_B1 = 0.9
ADAM_B2 = 0.999
ADAM_EPS = 1e-08
ADAM_WD = 0.01
ADAM_STEP = 10

PACK_COLS = 1024
VMEM_LIMIT = 56 * 1024 * 1024

SHARDED = ('w_in', 'w_branch_a', 'w_branch_b', 'w_branch_c', 'w_out', 'w_ffn_gate', 'w_ffn_up', 'w_ffn_down')
SMALL = ('norm_mix', 'b_gate', 'qn_a', 'kn_a', 'qn_b', 'kn_b', 'pool_lin', 'pool_scale', 'norm_ffn')
WEIGHTS = ('norm_mix', 'w_in', 'b_gate', 'qn_a', 'kn_a', 'qn_b', 'kn_b', 'pool_lin', 'pool_scale',
           'w_branch_a', 'w_branch_b', 'w_branch_c', 'w_out', 'norm_ffn', 'w_ffn_gate', 'w_ffn_up', 'w_ffn_down')

NN = (((1,), (0,)), ((), ()))
NT = (((1,), (1,)), ((), ()))
TN = (((0,), (0,)), ((), ()))


def _params(vmem=None):
    return pltpu.CompilerParams(vmem_limit_bytes=VMEM_LIMIT if vmem is None else vmem)


def _lane_iota(n=LANES):
    return lax.broadcasted_iota(jnp.int32, (1, n), 1)


def _swap(x, sh, lane):
    n = x.shape[-1]
    down = pltpu.roll(x, sh, axis=1)
    up = pltpu.roll(x, n - sh, axis=1)
    return jnp.where((lane & sh) == 0, up, down)


def _head_sum(v):
    w = v.shape[-1]
    r = lax.broadcasted_iota(jnp.int32, (w, w), 0) // HEAD_DIM
    c = lax.broadcasted_iota(jnp.int32, (w, w), 1) // HEAD_DIM
    ones = (r == c).astype(BF16)
    hi = v.astype(BF16)
    lo = (v - hi.astype(F32)).astype(BF16)
    return (lax.dot_general(hi, ones, NN, preferred_element_type=F32)
            + lax.dot_general(lo, ones, NN, preferred_element_type=F32))


def _mm(a, b, *, dims, out_dtype, tm, tn, tk, name, res=None):
    if dims == 'nn':
        (m, k), n = a.shape, b.shape[1]
    elif dims == 'nt':
        (m, k), n = a.shape, b.shape[0]
    else:
        (k, m), n = a.shape, b.shape[1]
    tm, tn, tk = min(tm, m), min(tn, n), min(tk, k)
    assert m % tm == 0 and n % tn == 0 and k % tk == 0, (name, m, n, k, tm, tn, tk)
    nk = k // tk
    if dims == 'tn':
        a_spec = pl.BlockSpec((tk, tm), lambda i, j, kk: (kk, i))
    else:
        a_spec = pl.BlockSpec((tm, tk), lambda i, j, kk: (i, kk))
    if dims == 'nt':
        b_spec = pl.BlockSpec((tn, tk), lambda i, j, kk: (j, kk))
    else:
        b_spec = pl.BlockSpec((tk, tn), lambda i, j, kk: (kk, j))
    o_spec = pl.BlockSpec((tm, tn), lambda i, j, kk: (i, j))
    dn = {'nn': NN, 'nt': NT, 'tn': TN}[dims]
    has_res = res is not None

    def body(*refs):
        if has_res:
            a_ref, b_ref, r_ref, o_ref, acc_ref = refs
        else:
            a_ref, b_ref, o_ref, acc_ref = refs
        prod = lax.dot_general(a_ref[...].astype(BF16), b_ref[...].astype(BF16), dn,
                               preferred_element_type=F32)

        def finish(total):
            if has_res:
                total = total + r_ref[...]
            o_ref[...] = total.astype(out_dtype)

        if nk == 1:
            finish(prod)
        else:
            kk = pl.program_id(2)

            @pl.when(kk == 0)
            def _():
                acc_ref[...] = prod

            @pl.when(kk > 0)
            def _():
                acc_ref[...] += prod

            @pl.when(kk == nk - 1)
            def _():
                finish(acc_ref[...])

    in_specs = [a_spec, b_spec] + ([o_spec] if has_res else [])
    args = (a, b) + ((res,) if has_res else ())
    acc_shape = (tm, tn) if nk > 1 else (8, LANES)
    return pl.pallas_call(
        body, name=name, grid=(m // tm, n // tn, nk),
        in_specs=in_specs, out_specs=o_spec,
        out_shape=jax.ShapeDtypeStruct((m, n), out_dtype),
        scratch_shapes=[pltpu.VMEM(acc_shape, F32)],
        compiler_params=_params(),
    )(*args)


def _rms_fwd(x, g, *, name, tm=512):
    s, d = x.shape

    def body(x_ref, g_ref, h_ref):
        xv = x_ref[...]
        rstd = lax.rsqrt(jnp.mean(xv * xv, axis=-1, keepdims=True) + EPS)
        h_ref[...] = (xv * rstd * g_ref[...]).astype(BF16)

    return pl.pallas_call(
        body, name=name, grid=(s // tm,),
        in_specs=[pl.BlockSpec((tm, d), lambda i: (i, 0)), pl.BlockSpec((1, d), lambda i: (0, 0))],
        out_specs=pl.BlockSpec((tm, d), lambda i: (i, 0)),
        out_shape=jax.ShapeDtypeStruct((s, d), BF16),
        compiler_params=_params(),
    )(x, g)


def _rms_bwd(x, g, dh, dres, *, name, tm=512):
    s, d = x.shape

    def body(x_ref, g_ref, dh_ref, dres_ref, dx_ref, dg_ref):
        i = pl.program_id(0)
        xv = x_ref[...]
        rstd = lax.rsqrt(jnp.mean(xv * xv, axis=-1, keepdims=True) + EPS)
        xhat = xv * rstd
        dhv = dh_ref[...]
        dxhat = dhv * g_ref[...]
        proj = jnp.mean(dxhat * xhat, axis=-1, keepdims=True)
        dx_ref[...] = dres_ref[...] + rstd * (dxhat - xhat * proj)
        part = jnp.sum(dhv * xhat, axis=0, keepdims=True)

        @pl.when(i == 0)
        def _():
            dg_ref[...] = part

        @pl.when(i > 0)
        def _():
            dg_ref[...] += part

    row = pl.BlockSpec((tm, d), lambda i: (i, 0))
    vec = pl.BlockSpec((1, d), lambda i: (0, 0))
    return pl.pallas_call(
        body, name=name, grid=(s // tm,),
        in_specs=[row, vec, row, row], out_specs=[row, vec],
        out_shape=[jax.ShapeDtypeStruct((s, d), F32), jax.ShapeDtypeStruct((1, d), F32)],
        compiler_params=_params(),
    )(x, g, dh, dres)


N_QKV_BLOCKS = QKV_WIDTH // LANES
A_BLOCKS = (A_Q_WIDTH + 2 * A_KV_WIDTH) // LANES
V_A_BLOCK = A_BLOCKS - 1
V_B_FIRST = A_BLOCKS + 2 * (B_WIDTH // LANES)


def _qk_kind(j):
    return jnp.where(j < A_BLOCKS, 0, 1)


def _is_v_block(j):
    return (j == V_A_BLOCK) | (j >= V_B_FIRST)


def _rope_tables(s):
    def ang(pos, dim):
        inv = ROPE_THETA ** (-jnp.arange(0, dim, 2, dtype=F32) / dim)
        return pos.astype(F32)[:, None] * inv[None, :]
    t = jnp.arange(s)
    a_row = ang(t // GRID_W, HEAD_DIM // 2)
    a_col = ang(t % GRID_W, HEAD_DIM // 2)
    a_seq = ang(t, HEAD_DIM)
    cos_a = jnp.concatenate([jnp.cos(a_row)] * 2 + [jnp.cos(a_col)] * 2, axis=-1)
    sin_a = jnp.concatenate([-jnp.sin(a_row), jnp.sin(a_row), -jnp.sin(a_col), jnp.sin(a_col)], axis=-1)
    cos_b = jnp.concatenate([jnp.cos(a_seq)] * 2, axis=-1)
    sin_b = jnp.concatenate([-jnp.sin(a_seq), jnp.sin(a_seq)], axis=-1)
    cos = jnp.stack([jnp.tile(cos_a, (1, 2)), jnp.tile(cos_b, (1, 2))])
    sin = jnp.stack([jnp.tile(sin_a, (1, 2)), jnp.tile(sin_b, (1, 2))])
    return cos, sin


def _qk_gains(qn_a, kn_a, qn_b, kn_b):
    one = jnp.ones((HEAD_DIM,), F32)
    parts = [jnp.tile(qn_a, 8), jnp.tile(kn_a, 2), jnp.tile(one, 2),
             jnp.tile(qn_b, 12), jnp.tile(kn_b, 12), jnp.tile(one, 12)]
    return jnp.concatenate(parts)[None, :]


def _qkrope_fwd(z, gains, cos, sin, *, name, tm=512):
    s = z.shape[0]

    def body(z_ref, g_ref, c_ref, s_ref, o_ref):
        j = pl.program_id(0)
        lane = _lane_iota()
        xv = z_ref[...]

        def normed_rope(pair):
            ms = _head_sum(xv * xv) * (1.0 / HEAD_DIM)
            n = xv * lax.rsqrt(ms + EPS) * g_ref[...]
            return n * c_ref[...] + _swap(n, pair, lane) * s_ref[...]

        @pl.when(_is_v_block(j))
        def _():
            o_ref[...] = xv.astype(BF16)

        @pl.when(jnp.logical_not(_is_v_block(j)) & (j < A_BLOCKS))
        def _():
            o_ref[...] = normed_rope(HEAD_DIM // 4).astype(BF16)

        @pl.when(jnp.logical_not(_is_v_block(j)) & (j >= A_BLOCKS))
        def _():
            o_ref[...] = normed_rope(HEAD_DIM // 2).astype(BF16)

    tab = pl.BlockSpec((None, tm, LANES), lambda j, i: (_qk_kind(j), i, 0))
    blk = pl.BlockSpec((tm, LANES), lambda j, i: (i, j))
    return pl.pallas_call(
        body, name=name, grid=(N_QKV_BLOCKS, s // tm),
        in_specs=[blk, pl.BlockSpec((1, LANES), lambda j, i: (0, j)), tab, tab],
        out_specs=blk,
        out_shape=jax.ShapeDtypeStruct((s, QKV_WIDTH), BF16),
        compiler_params=_params(),
    )(z, gains, cos, sin)


def _qkrope_bwd(z, gains, cos, sin, dqkv, *, name, tm=512):
    s = z.shape[0]

    def body(z_ref, g_ref, c_ref, s_ref, dy_ref, dz_ref, dg_ref):
        j = pl.program_id(0)
        i = pl.program_id(1)
        lane = _lane_iota()
        xv = z_ref[...]
        dy = dy_ref[...]

        @pl.when(i == 0)
        def _():
            dg_ref[...] = jnp.zeros_like(dg_ref)

        def back(pair):
            ms = _head_sum(xv * xv) * (1.0 / HEAD_DIM)
            rstd = lax.rsqrt(ms + EPS)
            xhat = xv * rstd
            dn = dy * c_ref[...] + _swap(dy * s_ref[...], pair, lane)
            dg_ref[...] += jnp.sum(dn * xhat, axis=0, keepdims=True)
            dxhat = dn * g_ref[...]
            proj = _head_sum(dxhat * xhat) * (1.0 / HEAD_DIM)
            dz_ref[...] = (rstd * (dxhat - xhat * proj)).astype(BF16)

        @pl.when(_is_v_block(j))
        def _():
            dz_ref[...] = dy.astype(BF16)

        @pl.when(jnp.logical_not(_is_v_block(j)) & (j < A_BLOCKS))
        def _():
            back(HEAD_DIM // 4)

        @pl.when(jnp.logical_not(_is_v_block(j)) & (j >= A_BLOCKS))
        def _():
            back(HEAD_DIM // 2)

    tab = pl.BlockSpec((None, tm, LANES), lambda j, i: (_qk_kind(j), i, 0))
    blk = pl.BlockSpec((tm, LANES), lambda j, i: (i, j))
    vec = pl.BlockSpec((1, LANES), lambda j, i: (0, j))
    return pl.pallas_call(
        body, name=name, grid=(N_QKV_BLOCKS, s // tm),
        in_specs=[blk, vec, tab, tab, blk],
        out_specs=[blk, vec],
        out_shape=[jax.ShapeDtypeStruct((s, QKV_WIDTH), BF16), jax.ShapeDtypeStruct((1, QKV_WIDTH), F32)],
        compiler_params=_params(),
    )(z, gains, cos, sin, dqkv)


def _dup_halves(kv):
    h0, h1 = kv[:, :HEAD_DIM], kv[:, HEAD_DIM:]
    return jnp.concatenate([h0, h0, h1, h1], axis=1)


def _with_ones(kv):
    h0, h1 = kv[:, :HEAD_DIM], kv[:, HEAD_DIM:]
    one = jnp.ones_like(h0)
    return jnp.concatenate([h0, one, one, h0, h1, one, one, h1], axis=1)


def _attn_a_fwd(qkv, kdup, vones, *, name, tq=256, tk=512, unroll=2):
    s = qkv.shape[0]
    tq, tk = min(tq, s), min(tk, s)
    n_chunks = s // tk
    assert n_chunks % unroll == 0

    def body(q_ref, k_ref, v_ref, o_ref, lse_ref):
        lane = _lane_iota()
        low = lane < HEAD_DIM
        q = q_ref[...]
        zero = jnp.zeros_like(q)
        qm = [jnp.where(low, q, zero) * ATTN_SCALE, jnp.where(low, zero, q) * ATTN_SCALE]

        def chunks(c, carry):
            state = list(carry)
            scs = []
            for u in range(unroll):
                off = pl.multiple_of((c * unroll + u) * tk, tk)
                kc = k_ref[pl.ds(off, tk), :]
                scs.append([lax.dot_general(qm[e], kc, NT, preferred_element_type=F32) for e in range(2)])
            for u in range(unroll):
                off = pl.multiple_of((c * unroll + u) * tk, tk)
                for e in range(2):
                    m, acc = state[2 * e], state[2 * e + 1]
                    ve = v_ref[pl.ds(off, tk), e * LANES:(e + 1) * LANES]
                    m_new = jnp.maximum(m, jnp.max(scs[u][e], axis=1, keepdims=True))
                    alpha = jnp.exp(m - m_new)
                    p = jnp.exp(scs[u][e] - m_new).astype(BF16)
                    state[2 * e] = m_new
                    state[2 * e + 1] = alpha * acc + lax.dot_general(p, ve, NN, preferred_element_type=F32)
            return tuple(state)

        m_init = jnp.full((tq, 1), NEG_INF, F32)
        a_init = jnp.zeros((tq, LANES), F32)
        m0, a0, m1, a1 = lax.fori_loop(0, n_chunks // unroll, chunks, (m_init, a_init, m_init, a_init))
        l0 = pltpu.roll(a0, HEAD_DIM, axis=1)
        l1 = pltpu.roll(a1, HEAD_DIM, axis=1)
        o_ref[...] = jnp.where(low, a0 / l0, a1 / l1).astype(BF16)
        lse_ref[...] = jnp.where(low, m0 + jnp.log(l0), m1 + jnp.log(l1))

    q_spec = pl.BlockSpec((tq, LANES), lambda hb, qi: (qi, hb))
    k_spec = pl.BlockSpec((s, LANES), lambda hb, qi: (0, hb // 2))
    v_spec = pl.BlockSpec((s, 2 * LANES), lambda hb, qi: (0, hb // 2))
    return pl.pallas_call(
        body, name=name, grid=(A_Q_WIDTH // LANES, s // tq),
        in_specs=[q_spec, k_spec, v_spec], out_specs=[q_spec, q_spec],
        out_shape=[jax.ShapeDtypeStruct((s, A_Q_WIDTH), BF16), jax.ShapeDtypeStruct((s, A_Q_WIDTH), F32)],
        compiler_params=_params(),
    )(qkv, kdup, vones)


def _attn_a_bwd(qkv, kdup, vdup, o, lse, do, *, name, tq=256, tk=512, unroll=2):
    s = qkv.shape[0]
    tq, tk = min(tq, s), min(tk, s)
    n_chunks = s // tk
    assert n_chunks % unroll == 0

    def body(q_ref, k_ref, v_ref, o_ref, lse_ref, do_ref, dq_ref, dk_ref, dv_ref):
        first = (pl.program_id(1) == 0) & (pl.program_id(2) == 0)

        @pl.when(first)
        def _():
            dk_ref[...] = jnp.zeros_like(dk_ref)
            dv_ref[...] = jnp.zeros_like(dv_ref)

        lane = _lane_iota()
        low = lane < HEAD_DIM
        q = q_ref[...]
        dov = do_ref[...]
        zero = jnp.zeros_like(q)
        prod = dov.astype(F32) * o_ref[...].astype(F32)
        lsev = lse_ref[...]
        qs = [jnp.where(low, q, zero) * ATTN_SCALE, jnp.where(low, zero, q) * ATTN_SCALE]
        dom = [jnp.where(low, dov, zero), jnp.where(low, zero, dov)]
        delta = [jnp.sum(jnp.where(low, prod, 0.0), axis=1, keepdims=True),
                 jnp.sum(jnp.where(low, 0.0, prod), axis=1, keepdims=True)]
        lse = [lsev[:, 0:1], lsev[:, HEAD_DIM:HEAD_DIM + 1]]
        qs_both = jnp.concatenate(qs, axis=0)
        dom_both = jnp.concatenate(dom, axis=0)

        def chunks(c, carry):
            dqs = list(carry)
            for u in range(unroll):
                off = pl.multiple_of((c * unroll + u) * tk, tk)
                kc = k_ref[pl.ds(off, tk), :]
                vc = v_ref[pl.ds(off, tk), :]
                ps, dss = [], []
                for e in range(2):
                    sc = lax.dot_general(qs[e], kc, NT, preferred_element_type=F32)
                    p = jnp.exp(sc - lse[e])
                    dp = lax.dot_general(dom[e], vc, NT, preferred_element_type=F32)
                    ds = (p * (dp - delta[e])).astype(BF16)
                    ps.append(p.astype(BF16))
                    dss.append(ds)
                    dqs[e] = dqs[e] + lax.dot_general(ds, kc, NN, preferred_element_type=F32)
                dv_ref[pl.ds(off, tk), :] += lax.dot_general(jnp.concatenate(ps, axis=0), dom_both, TN,
                                                             preferred_element_type=F32)
                dk_ref[pl.ds(off, tk), :] += lax.dot_general(jnp.concatenate(dss, axis=0), qs_both, TN,
                                                             preferred_element_type=F32)
            return tuple(dqs)

        dq_init = jnp.zeros((tq, LANES), F32)
        dq0, dq1 = lax.fori_loop(0, n_chunks // unroll, chunks, (dq_init, dq_init))
        dq_ref[...] = jnp.where(low, dq0, dq1) * ATTN_SCALE

    q_spec = pl.BlockSpec((tq, LANES), lambda kvh, hb, qi: (qi, kvh * 2 + hb))
    kv_spec = pl.BlockSpec((s, LANES), lambda kvh, hb, qi: (0, kvh))
    return pl.pallas_call(
        body, name=name, grid=(2, 2, s // tq),
        in_specs=[q_spec, kv_spec, kv_spec, q_spec, q_spec, q_spec],
        out_specs=[q_spec, kv_spec, kv_spec],
        out_shape=[jax.ShapeDtypeStruct((s, A_Q_WIDTH), F32),
                   jax.ShapeDtypeStruct((s, 2 * LANES), F32), jax.ShapeDtypeStruct((s, 2 * LANES), F32)],
        compiler_params=_params(),
    )(qkv, kdup, vdup, o, lse, do)


BAND_Q = 128
B_Q_BLOCK0 = A_BLOCKS
B_K_BLOCK0 = A_BLOCKS + B_WIDTH // LANES
B_V_BLOCK0 = A_BLOCKS + 2 * (B_WIDTH // LANES)
TOKEN_BLOCKS = QKV_WIDTH // LANES


def _band_geometry(length):
    seg = min(length, 2048)
    win = min(2 * BAND_Q, length)
    return seg, win


def _band_window(qs, length, win):
    st = jnp.clip(qs - B_HALF_SPAN, 0, length - win)
    st = pl.multiple_of(st, B_HALF_SPAN)
    qpos = qs + lax.broadcasted_iota(jnp.int32, (BAND_Q, 1), 0)
    kpos = st + lax.broadcasted_iota(jnp.int32, (1, win), 1)
    return st, jnp.abs(qpos - kpos) <= B_HALF_SPAN


def _band_fwd(qkv_view, gi, dil, *, name):
    length = qkv_view.shape[0]
    seg, win = _band_geometry(length)
    n_sub = seg // BAND_Q

    def body(q_ref, k_ref, v_ref, o_ref, lse_ref):
        seg_i = pl.program_id(2)
        lane = _lane_iota()
        low = lane < HEAD_DIM

        def sub(i, carry):
            ql = pl.multiple_of(i * BAND_Q, BAND_Q)
            st, valid = _band_window(seg_i * seg + ql, length, win)
            q = q_ref[pl.ds(ql, BAND_Q), :]
            kw = k_ref[pl.ds(st, win), :]
            vw = v_ref[pl.ds(st, win), :]
            outs, lses = [], []
            for e in range(2):
                mine = (lane >= HEAD_DIM) if e else (lane < HEAD_DIM)
                qm = jnp.where(mine, q, jnp.zeros_like(q)) * ATTN_SCALE
                sc = lax.dot_general(qm, kw, NT, preferred_element_type=F32)
                sc = jnp.where(valid, sc, NEG_INF)
                m = jnp.max(sc, axis=1, keepdims=True)
                p = jnp.exp(sc - m)
                l = jnp.sum(p, axis=1, keepdims=True)
                outs.append(lax.dot_general(p.astype(BF16), vw, NN, preferred_element_type=F32) / l)
                lses.append(m + jnp.log(l))
            o_ref[pl.ds(ql, BAND_Q), :] = jnp.where(low, outs[0], outs[1]).astype(BF16)
            lse_ref[pl.ds(ql, BAND_Q), :] = jnp.where(low, lses[0], lses[1])
            return carry

        lax.fori_loop(0, n_sub, sub, 0)

    def col(base):
        return lambda r, hp, sg: (0, r * TOKEN_BLOCKS + base + gi * 2 + hp)

    q_spec = pl.BlockSpec((seg, LANES), lambda r, hp, sg: (sg, r * TOKEN_BLOCKS + B_Q_BLOCK0 + gi * 2 + hp))
    out_spec = pl.BlockSpec((seg, LANES), lambda r, hp, sg: (sg, r * 2 + hp))
    return pl.pallas_call(
        body, name=name, grid=(dil, 2, length // seg),
        in_specs=[q_spec, pl.BlockSpec((length, LANES), col(B_K_BLOCK0)), pl.BlockSpec((length, LANES), col(B_V_BLOCK0))],
        out_specs=[out_spec, out_spec],
        out_shape=[jax.ShapeDtypeStruct((length, dil * B_OUT_WIDTH), BF16),
                   jax.ShapeDtypeStruct((length, dil * B_OUT_WIDTH), F32)],
        compiler_params=_params(),
    )(qkv_view, qkv_view, qkv_view)


def _band_bwd(qkv_view, do, lse, dd, gi, dil, *, name):
    length = qkv_view.shape[0]
    seg, win = _band_geometry(length)
    n_sub = seg // BAND_Q

    def body(q_ref, k_ref, v_ref, do_ref, lse_ref, dd_ref, dq_ref, dk_ref, dv_ref):
        seg_i = pl.program_id(2)
        lane = _lane_iota()

        @pl.when(seg_i == 0)
        def _():
            dk_ref[...] = jnp.zeros_like(dk_ref)
            dv_ref[...] = jnp.zeros_like(dv_ref)

        def sub(i, carry):
            ql = pl.multiple_of(i * BAND_Q, BAND_Q)
            st, valid = _band_window(seg_i * seg + ql, length, win)
            q = q_ref[pl.ds(ql, BAND_Q), :]
            dov = do_ref[pl.ds(ql, BAND_Q), :]
            lsev = lse_ref[pl.ds(ql, BAND_Q), :]
            ddv = dd_ref[pl.ds(ql, BAND_Q), :]
            kw = k_ref[pl.ds(st, win), :]
            vw = v_ref[pl.ds(st, win), :]
            dq = jnp.zeros((BAND_Q, LANES), F32)
            for e in range(2):
                mine = (lane >= HEAD_DIM) if e else (lane < HEAD_DIM)
                qs = jnp.where(mine, q, jnp.zeros_like(q)) * ATTN_SCALE
                dom = jnp.where(mine, dov, jnp.zeros_like(dov))
                lse_e = lsev[:, e * HEAD_DIM:e * HEAD_DIM + 1]
                dd_e = ddv[:, e * HEAD_DIM:e * HEAD_DIM + 1]
                sc = lax.dot_general(qs, kw, NT, preferred_element_type=F32)
                p = jnp.exp(jnp.where(valid, sc, NEG_INF) - lse_e)
                dp = lax.dot_general(dom, vw, NT, preferred_element_type=F32)
                ds = (p * (dp - dd_e)).astype(BF16)
                dv_ref[pl.ds(st, win), :] += lax.dot_general(p.astype(BF16), dom, TN, preferred_element_type=F32)
                dk_ref[pl.ds(st, win), :] += lax.dot_general(ds, qs, TN, preferred_element_type=F32)
                dq_e = lax.dot_general(ds, kw, NN, preferred_element_type=F32) * ATTN_SCALE
                dq = dq + jnp.where(mine, dq_e, 0.0)
            dq_ref[pl.ds(ql, BAND_Q), :] = dq
            return carry

        lax.fori_loop(0, n_sub, sub, 0)

    def col(base):
        return lambda r, hp, sg: (0, r * TOKEN_BLOCKS + base + gi * 2 + hp)

    q_spec = pl.BlockSpec((seg, LANES), lambda r, hp, sg: (sg, r * TOKEN_BLOCKS + B_Q_BLOCK0 + gi * 2 + hp))
    seg_spec = pl.BlockSpec((seg, LANES), lambda r, hp, sg: (sg, r * 2 + hp))
    full_spec = pl.BlockSpec((length, LANES), lambda r, hp, sg: (0, r * 2 + hp))
    shp = jax.ShapeDtypeStruct((length, dil * B_OUT_WIDTH), F32)
    return pl.pallas_call(
        body, name=name, grid=(dil, 2, length // seg),
        in_specs=[q_spec, pl.BlockSpec((length, LANES), col(B_K_BLOCK0)), pl.BlockSpec((length, LANES), col(B_V_BLOCK0)),
                  seg_spec, seg_spec, seg_spec],
        out_specs=[seg_spec, full_spec, full_spec],
        out_shape=[shp, shp, shp],
        compiler_params=_params(),
    )(qkv_view, qkv_view, qkv_view, do, lse, dd)


def _merge_weights(lses):
    m = jnp.maximum(jnp.maximum(lses[0], lses[1]), lses[2])
    ex = [jnp.exp(v - m) for v in lses]
    tot = ex[0] + ex[1] + ex[2]
    return [v / tot for v in ex]


def _merge_fwd(os_, lses, *, name, tm=512):
    s = os_[0].shape[0]

    def body(o0, o1, o2, l0, l1, l2, ob_ref):
        w = _merge_weights([l0[...], l1[...], l2[...]])
        ob = w[0] * o0[...].astype(F32) + w[1] * o1[...].astype(F32) + w[2] * o2[...].astype(F32)
        ob_ref[...] = ob.astype(BF16)

    blk = pl.BlockSpec((tm, B_OUT_WIDTH), lambda i: (i, 0))
    return pl.pallas_call(
        body, name=name, grid=(s // tm,), in_specs=[blk] * 6, out_specs=blk,
        out_shape=jax.ShapeDtypeStruct((s, B_OUT_WIDTH), BF16),
        compiler_params=_params(),
    )(*os_, *lses)


def _merge_bwd(os_, lses, dob, *, name, tm=512):
    s = os_[0].shape[0]

    def body(o0, o1, o2, l0, l1, l2, dob_ref, d0, d1, d2, t0, t1, t2):
        w = _merge_weights([l0[...], l1[...], l2[...]])
        dv = dob_ref[...]
        ob = w[0] * o0[...].astype(F32) + w[1] * o1[...].astype(F32) + w[2] * o2[...].astype(F32)
        tot = _head_sum(dv * ob)
        for wg, d_ref, t_ref in zip(w, (d0, d1, d2), (t0, t1, t2)):
            d_ref[...] = (wg * dv).astype(BF16)
            t_ref[...] = wg * tot

    blk = pl.BlockSpec((tm, B_OUT_WIDTH), lambda i: (i, 0))
    return pl.pallas_call(
        body, name=name, grid=(s // tm,), in_specs=[blk] * 7, out_specs=[blk] * 6,
        out_shape=[jax.ShapeDtypeStruct((s, B_OUT_WIDTH), BF16)] * 3 + [jax.ShapeDtypeStruct((s, B_OUT_WIDTH), F32)] * 3,
        compiler_params=_params(),
    )(*os_, *lses, dob)


HALO = 8
POOL_BLOCK0 = QKV_WIDTH // LANES


def _window_sum(ext, lo, hi, tm):
    rows = ext.shape[0]
    acc = None
    for j in range(lo, hi + 1):
        r = ext if j == 0 else pltpu.roll(ext, (-j) % rows, axis=0)
        acc = r if acc is None else acc + r
    return acc[HALO:HALO + tm]


def _pool_counts(t, half, s):
    return (jnp.minimum(t + half, s) - jnp.maximum(t - half, 0)).astype(F32)


def _halo_specs(tm, s, col0):
    per = tm // HALO
    last = s // HALO - 1
    prev = pl.BlockSpec((HALO, LANES), lambda g, i: (jnp.maximum(i * per - 1, 0), col0 + g))
    cur = pl.BlockSpec((tm, LANES), lambda g, i: (i, col0 + g))
    nxt = pl.BlockSpec((HALO, LANES), lambda g, i: (jnp.minimum((i + 1) * per, last), col0 + g))
    return prev, cur, nxt


def _extended(prev_ref, cur_ref, next_ref, i, n_tiles):
    prev = jnp.where(i > 0, prev_ref[...], 0.0)
    nxt = jnp.where(i < n_tiles - 1, next_ref[...], 0.0)
    return jnp.concatenate([prev, cur_ref[...], nxt], axis=0)


def _pool_fwd(z, lin, scale, *, name, tm=512):
    s = z.shape[0]
    tm = min(tm, s)
    n_tiles = s // tm

    def body(prev_ref, cur_ref, next_ref, lin_ref, sc_ref, pooled_ref, mixed_ref):
        g = pl.program_id(0)
        i = pl.program_id(1)
        ext = _extended(prev_ref, cur_ref, next_ref, i, n_tiles)
        t = i * tm + lax.broadcasted_iota(jnp.int32, (tm, 1), 0)
        for gi, half in enumerate(POOL_HALF):
            @pl.when(g == gi)
            def _(half=half):
                mean = _window_sum(ext, -half, half - 1, tm) / _pool_counts(t, half, s)
                pooled = (mean - cur_ref[...]).astype(BF16)
                pooled_ref[...] = pooled
                mixed = lax.dot_general(pooled, lin_ref[...].astype(BF16), NN, preferred_element_type=F32)
                mixed_ref[...] = (mixed * sc_ref[...]).astype(BF16)

    prev, cur, nxt = _halo_specs(tm, s, POOL_BLOCK0)
    out = pl.BlockSpec((tm, LANES), lambda g, i: (i, g))
    return pl.pallas_call(
        body, name=name, grid=(len(POOL_HALF), n_tiles),
        in_specs=[prev, cur, nxt, pl.BlockSpec((None, LANES, LANES), lambda g, i: (g, 0, 0)),
                  pl.BlockSpec((1, LANES), lambda g, i: (0, g))],
        out_specs=[out, out],
        out_shape=[jax.ShapeDtypeStruct((s, POOL_WIDTH), BF16)] * 2,
        compiler_params=_params(),
    )(z, z, z, lin, scale)


def _pool_bwd(dmixed, pooled, lin, scale, *, name, tm=512):
    s = dmixed.shape[0]
    tm = min(tm, s)
    n_tiles = s // tm

    def body(prev_ref, cur_ref, next_ref, pooled_ref, lin_ref, sc_ref, du_ref, dlin_ref, dsc_ref):
        g = pl.program_id(0)
        i = pl.program_id(1)

        @pl.when(i == 0)
        def _():
            dlin_ref[...] = jnp.zeros_like(dlin_ref)
            dsc_ref[...] = jnp.zeros_like(dsc_ref)

        linb = lin_ref[...].astype(BF16)
        ext = _extended(prev_ref, cur_ref, next_ref, i, n_tiles)
        dpl_ext = (ext * sc_ref[...]).astype(BF16)
        dpl_cur = (cur_ref[...] * sc_ref[...]).astype(BF16)
        dpooled_ext = lax.dot_general(dpl_ext, linb, NT, preferred_element_type=F32)
        t_ext = i * tm - HALO + lax.broadcasted_iota(jnp.int32, (tm + 2 * HALO, 1), 0)
        pooled = pooled_ref[...]
        mixed = lax.dot_general(pooled, linb, NN, preferred_element_type=F32)
        dsc_ref[...] += jnp.sum(cur_ref[...] * mixed, axis=0, keepdims=True)
        dlin_ref[...] += lax.dot_general(pooled, dpl_cur, TN, preferred_element_type=F32)
        for gi, half in enumerate(POOL_HALF):
            @pl.when(g == gi)
            def _(half=half):
                share = dpooled_ext / jnp.maximum(_pool_counts(t_ext, half, s), 1.0)
                du = _window_sum(share, -(half - 1), half, tm) - dpooled_ext[HALO:HALO + tm]
                du_ref[...] = du.astype(BF16)

    prev, cur, nxt = _halo_specs(tm, s, 0)
    out = pl.BlockSpec((tm, LANES), lambda g, i: (i, g))
    lin_spec = pl.BlockSpec((None, LANES, LANES), lambda g, i: (g, 0, 0))
    vec = pl.BlockSpec((1, LANES), lambda g, i: (0, g))
    return pl.pallas_call(
        body, name=name, grid=(len(POOL_HALF), n_tiles),
        in_specs=[prev, cur, nxt, out, lin_spec, vec],
        out_specs=[out, lin_spec, vec],
        out_shape=[jax.ShapeDtypeStruct((s, POOL_WIDTH), BF16),
                   jax.ShapeDtypeStruct((len(POOL_HALF), LANES, LANES), F32),
                   jax.ShapeDtypeStruct((1, POOL_WIDTH), F32)],
        compiler_params=_params(),
    )(dmixed, dmixed, dmixed, pooled, lin, scale)


GATE_TILE = 512
GATE_BLOCK0 = (QKV_WIDTH + POOL_WIDTH) // GATE_TILE
GATE_BLOCKS_PER_BRANCH = D_MODEL // GATE_TILE


def _sigmoid(v):
    return 1.0 / (1.0 + jnp.exp(-v))


def _gate_specs(tm):
    def zspec(br):
        return pl.BlockSpec((tm, GATE_TILE), lambda jj, i: (i, GATE_BLOCK0 + GATE_BLOCKS_PER_BRANCH * br + jj))

    def bspec(br):
        return pl.BlockSpec((1, GATE_TILE), lambda jj, i: (0, GATE_BLOCKS_PER_BRANCH * br + jj))

    row = pl.BlockSpec((tm, GATE_TILE), lambda jj, i: (i, jj))
    vec = pl.BlockSpec((1, GATE_TILE), lambda jj, i: (0, jj))
    return [zspec(0), zspec(1), zspec(2)], [bspec(0), bspec(1), bspec(2)], row, vec


def _gate_fwd(z, b_gate, ya, yb, yc, *, name, tm=512):
    s = z.shape[0]

    def body(z0, z1, z2, b0, b1, b2, ya_ref, yb_ref, yc_ref, out_ref):
        acc = _sigmoid(z0[...] + b0[...]) * ya_ref[...]
        acc = acc + _sigmoid(z1[...] + b1[...]) * yb_ref[...]
        acc = acc + _sigmoid(z2[...] + b2[...]) * yc_ref[...]
        out_ref[...] = acc.astype(BF16)

    zs, bs, row, _ = _gate_specs(tm)
    return pl.pallas_call(
        body, name=name, grid=(GATE_BLOCKS_PER_BRANCH, s // tm),
        in_specs=zs + bs + [row] * 3, out_specs=row,
        out_shape=jax.ShapeDtypeStruct((s, D_MODEL), BF16),
        compiler_params=_params(),
    )(z, z, z, b_gate, b_gate, b_gate, ya, yb, yc)


def _gate_bwd(z, b_gate, ya, yb, yc, dmerged, *, name, tm=512):
    s = z.shape[0]

    def body(z0, z1, z2, b0, b1, b2, ya_ref, yb_ref, yc_ref, dm_ref,
             dya_ref, dyb_ref, dyc_ref, dg0, dg1, dg2, db0, db1, db2):
        i = pl.program_id(1)
        dm = dm_ref[...]
        for z_ref, b_ref, y_ref, dy_ref, dg_ref, db_ref in (
                (z0, b0, ya_ref, dya_ref, dg0, db0), (z1, b1, yb_ref, dyb_ref, dg1, db1),
                (z2, b2, yc_ref, dyc_ref, dg2, db2)):
            gate = _sigmoid(z_ref[...] + b_ref[...])
            dy_ref[...] = (gate * dm).astype(BF16)
            dpre = dm * y_ref[...] * gate * (1.0 - gate)
            dg_ref[...] = dpre.astype(BF16)
            part = jnp.sum(dpre, axis=0, keepdims=True)

            @pl.when(i == 0)
            def _(db_ref=db_ref, part=part):
                db_ref[...] = part

            @pl.when(i > 0)
            def _(db_ref=db_ref, part=part):
                db_ref[...] += part

    zs, bs, row, vec = _gate_specs(tm)
    big = jax.ShapeDtypeStruct((s, D_MODEL), BF16)
    small = jax.ShapeDtypeStruct((1, D_MODEL), F32)
    return pl.pallas_call(
        body, name=name, grid=(GATE_BLOCKS_PER_BRANCH, s // tm),
        in_specs=zs + bs + [row] * 4, out_specs=[row] * 6 + [vec] * 3,
        out_shape=[big] * 6 + [small] * 3,
        compiler_params=_params(),
    )(z, z, z, b_gate, b_gate, b_gate, ya, yb, yc, dmerged)


def _swiglu_fwd(a, b, *, name, tm=512, tn=256):
    s, f = a.shape

    def body(a_ref, b_ref, o_ref):
        av = a_ref[...]
        o_ref[...] = (av * _sigmoid(av) * b_ref[...]).astype(BF16)

    blk = pl.BlockSpec((tm, tn), lambda i, j: (i, j))
    return pl.pallas_call(
        body, name=name, grid=(s // tm, f // tn), in_specs=[blk, blk], out_specs=blk,
        out_shape=jax.ShapeDtypeStruct((s, f), BF16), compiler_params=_params(),
    )(a, b)


def _swiglu_bwd(a, b, df, *, name, tm=512, tn=256):
    s, f = a.shape

    def body(a_ref, b_ref, df_ref, da_ref, db_ref):
        av = a_ref[...]
        dfv = df_ref[...]
        sg = _sigmoid(av)
        silu = av * sg
        da_ref[...] = (dfv * b_ref[...] * (sg + silu * (1.0 - sg))).astype(BF16)
        db_ref[...] = (dfv * silu).astype(BF16)

    blk = pl.BlockSpec((tm, tn), lambda i, j: (i, j))
    out = jax.ShapeDtypeStruct((s, f), BF16)
    return pl.pallas_call(
        body, name=name, grid=(s // tm, f // tn), in_specs=[blk] * 3, out_specs=[blk] * 2,
        out_shape=[out, out], compiler_params=_params(),
    )(a, b, df)


def _loss_head(y, target, *, name, tm=512):
    s, d = y.shape

    def body(y_ref, t_ref, part_ref, dy_ref):
        i = pl.program_id(0)
        err = y_ref[...] - t_ref[...]
        dy_ref[...] = err * (1.0 / d)
        part = jnp.sum(err * err, axis=0, keepdims=True) * (0.5 / d)

        @pl.when(i == 0)
        def _():
            part_ref[...] = part

        @pl.when(i > 0)
        def _():
            part_ref[...] += part

    row = pl.BlockSpec((tm, d), lambda i: (i, 0))
    vec = pl.BlockSpec((1, d), lambda i: (0, 0))
    return pl.pallas_call(
        body, name=name, grid=(s // tm,), in_specs=[row, row], out_specs=[vec, row],
        out_shape=[jax.ShapeDtypeStruct((1, d), F32), jax.ShapeDtypeStruct((s, d), F32)],
        compiler_params=_params(),
    )(y, target)


def _mesh_place():
    x, y, c = lax.axis_index('x'), lax.axis_index('y'), lax.axis_index('c')
    return x, y, c, 4 * x + 2 * y + c


def _peer(x, y, c, k):
    return (x ^ ((k >> 2) & 1), y ^ ((k >> 1) & 1), c ^ (k & 1))


def _exchange(buf, *, gather, name):
    shape = buf.shape[-2:]

    def body(in_ref, out_ref, send_sems, recv_sems, local_sem):
        x, y, c, me = _mesh_place()

        def src(slot):
            return in_ref if gather else in_ref.at[slot]

        def remote(k):
            return pltpu.make_async_remote_copy(
                src_ref=src(me ^ k), dst_ref=out_ref.at[me],
                send_sem=send_sems.at[k - 1], recv_sem=recv_sems.at[k - 1],
                device_id=_peer(x, y, c, k), device_id_type=pl.DeviceIdType.MESH)

        def arrival(k):
            return pltpu.make_async_remote_copy(
                src_ref=src(me ^ k), dst_ref=out_ref.at[me ^ k],
                send_sem=send_sems.at[k - 1], recv_sem=recv_sems.at[k - 1],
                device_id=_peer(x, y, c, k), device_id_type=pl.DeviceIdType.MESH)

        mine = pltpu.make_async_copy(src(me), out_ref.at[me], local_sem)
        mine.start()
        sends = [remote(k) for k in range(1, N_DEV)]
        for cp in sends:
            cp.start()
        for k in range(1, N_DEV):
            arrival(k).wait_recv()
        for cp in sends:
            cp.wait_send()
        mine.wait()

    return pl.pallas_call(
        body, name=name,
        in_specs=[pl.BlockSpec(memory_space=pl.ANY)], out_specs=pl.BlockSpec(memory_space=pl.ANY),
        out_shape=jax.ShapeDtypeStruct((N_DEV,) + shape, buf.dtype),
        scratch_shapes=[pltpu.SemaphoreType.DMA((N_DEV - 1,)), pltpu.SemaphoreType.DMA((N_DEV - 1,)),
                        pltpu.SemaphoreType.DMA],
    )(buf)


def _adamw(parts, w, m, v, *, name, tr):
    rows, cols = w.shape
    assert rows % tr == 0
    bias1 = 1.0 - ADAM_B1 ** ADAM_STEP
    bias2 = 1.0 - ADAM_B2 ** ADAM_STEP

    def body(p_ref, w_ref, m_ref, v_ref, g_ref, d_ref, nm_ref, nv_ref):
        g = p_ref[0].astype(F32)
        for j in range(1, N_DEV):
            g = g + p_ref[j].astype(F32)
        nm = ADAM_B1 * m_ref[...] + (1.0 - ADAM_B1) * g
        nv = ADAM_B2 * v_ref[...] + (1.0 - ADAM_B2) * (g * g)
        g_ref[...] = g
        nm_ref[...] = nm
        nv_ref[...] = nv
        d_ref[...] = -ADAM_LR * ((nm / bias1) / (jnp.sqrt(nv / bias2) + ADAM_EPS) + ADAM_WD * w_ref[...])

    blk = pl.BlockSpec((tr, cols), lambda i: (i, 0))
    out = jax.ShapeDtypeStruct((rows, cols), F32)
    return pl.pallas_call(
        body, name=name, grid=(rows // tr,),
        in_specs=[pl.BlockSpec((N_DEV, tr, cols), lambda i: (0, i, 0)), blk, blk, blk],
        out_specs=[blk] * 4, out_shape=[out] * 4, compiler_params=_params(),
    )(parts, w, m, v)


def _col_blocks(full):
    r, n = full.shape
    return full.reshape(r, N_DEV, n // N_DEV).transpose(1, 0, 2)


def _row_blocks(full):
    r, n = full.shape
    return full.reshape(N_DEV, r // N_DEV, n)


def _from_col_blocks(blocks):
    j, r, c = blocks.shape
    return blocks.transpose(1, 0, 2).reshape(r, j * c)


def _from_row_blocks(blocks):
    j, r, c = blocks.shape
    return blocks.reshape(j * r, c)


SHARD_ROWWISE = {'w_out', 'w_ffn_down'}


def _pack(shards):
    return jnp.concatenate([shards[n].reshape(-1, PACK_COLS) for n in SHARDED], axis=0)


def _unpack(packed, shapes):
    out, off = {}, 0
    lead = packed.shape[:-2]
    for n in SHARDED:
        size = shapes[n][0] * shapes[n][1] // PACK_COLS
        out[n] = packed[..., off:off + size, :].reshape(lead + tuple(shapes[n]))
        off += size
    return out


def _pack_small(vals):
    flat = jnp.concatenate([vals[n].reshape(-1) for n in SMALL])
    return flat.reshape(-1, PACK_COLS)


def _unpack_small(packed, shapes):
    flat = packed.reshape(-1)
    out, off = {}, 0
    for n in SMALL:
        size = 1
        for dim in shapes[n]:
            size *= dim
        out[n] = flat[off:off + size].reshape(shapes[n])
        off += size
    return out


def _band_views(qkv, s):
    return [qkv.reshape(s // d, d * QKV_WIDTH) for d in B_DILATIONS]


def _layer_fwd(x, p, tables, tag):
    s = x.shape[0]
    cos, sin = tables
    h = _rms_fwd(x, p['norm_mix'], name=f'rms_mix_fwd')
    z = _mm(h, p['w_in'], dims='nn', out_dtype=F32, tm=1024, tn=512, tk=1024, name='mm_in_fwd')
    gains = _qk_gains(p['qn_a'], p['kn_a'], p['qn_b'], p['kn_b'])
    qkv = _qkrope_fwd(z, gains, cos, sin, name='qkrope_fwd')

    kdup = _dup_halves(qkv[:, A_Q_WIDTH:A_Q_WIDTH + A_KV_WIDTH])
    va = qkv[:, A_Q_WIDTH + A_KV_WIDTH:A_Q_WIDTH + 2 * A_KV_WIDTH]
    vdup = _dup_halves(va)
    oa, lse_a = _attn_a_fwd(qkv, kdup, _with_ones(va), name='attn_a_fwd')
    ya = _mm(oa, p['w_branch_a'], dims='nn', out_dtype=F32, tm=1024, tn=512, tk=512, name='mm_branch_a_fwd')

    views = _band_views(qkv, s)
    o_g, lse_g = [], []
    for gi, d in enumerate(B_DILATIONS):
        o, lse = _band_fwd(views[gi], gi, d, name=f'band_fwd_d{d}')
        o_g.append(o.reshape(s, B_OUT_WIDTH))
        lse_g.append(lse.reshape(s, B_OUT_WIDTH))
    ob = _merge_fwd(o_g, lse_g, name='merge_fwd')
    yb = _mm(ob, p['w_branch_b'], dims='nn', out_dtype=F32, tm=1024, tn=512, tk=256, name='mm_branch_b_fwd')

    pooled, mixed = _pool_fwd(z, p['pool_lin'], p['pool_scale'], name='pool_fwd')
    yc = _mm(mixed, p['w_branch_c'], dims='nn', out_dtype=F32, tm=1024, tn=512, tk=512, name='mm_branch_c_fwd')

    merged = _gate_fwd(z, p['b_gate'], ya, yb, yc, name='gate_fwd')
    x_mid = _mm(merged, p['w_out'], dims='nn', out_dtype=F32, tm=1024, tn=512, tk=1024, res=x, name='mm_out_fwd')

    h2 = _rms_fwd(x_mid, p['norm_ffn'], name='rms_ffn_fwd')
    fa = _mm(h2, p['w_ffn_gate'], dims='nn', out_dtype=F32, tm=1024, tn=256, tk=1024, name='mm_ffn_gate_fwd')
    fb = _mm(h2, p['w_ffn_up'], dims='nn', out_dtype=F32, tm=1024, tn=256, tk=1024, name='mm_ffn_up_fwd')
    f = _swiglu_fwd(fa, fb, name='swiglu_fwd')
    x_out = _mm(f, p['w_ffn_down'], dims='nn', out_dtype=F32, tm=1024, tn=512, tk=2816, res=x_mid, name='mm_ffn_down_fwd')

    saved = dict(x=x, h=h, z=z, gains=gains, qkv=qkv, kdup=kdup, vdup=vdup, oa=oa, lse_a=lse_a, o_g=o_g, lse_g=lse_g,
                 ob=ob, pooled=pooled, mixed=mixed, ya=ya, yb=yb, yc=yc, merged=merged, x_mid=x_mid, h2=h2,
                 fa=fa, fb=fb, f=f)
    return x_out, saved


def _fold_heads(v, heads):
    return v.reshape(heads, HEAD_DIM).sum(axis=0)


def _layer_bwd(dx, p, sv, tables):
    s = dx.shape[0]
    cos, sin = tables
    g = {}

    df = _mm(dx, p['w_ffn_down'], dims='nt', out_dtype=F32, tm=512, tn=256, tk=1024, name='mm_ffn_down_dx')
    g['w_ffn_down'] = _mm(sv['f'], dx, dims='tn', out_dtype=BF16, tm=256, tn=1024, tk=512, name='mm_ffn_down_dw')
    da, db = _swiglu_bwd(sv['fa'], sv['fb'], df, name='swiglu_bwd')
    dh2 = _mm(da, p['w_ffn_gate'], dims='nt', out_dtype=F32, tm=512, tn=512, tk=2816, name='mm_ffn_gate_dx')
    dh2 = _mm(db, p['w_ffn_up'], dims='nt', out_dtype=F32, tm=512, tn=512, tk=2816, res=dh2, name='mm_ffn_up_dx')
    g['w_ffn_gate'] = _mm(sv['h2'], da, dims='tn', out_dtype=BF16, tm=1024, tn=256, tk=512, name='mm_ffn_gate_dw')
    g['w_ffn_up'] = _mm(sv['h2'], db, dims='tn', out_dtype=BF16, tm=1024, tn=256, tk=512, name='mm_ffn_up_dw')
    dx_mid, g['norm_ffn'] = _rms_bwd(sv['x_mid'], p['norm_ffn'], dh2, dx, name='rms_ffn_bwd')

    dmerged = _mm(dx_mid, p['w_out'], dims='nt', out_dtype=F32, tm=512, tn=512, tk=1024, name='mm_out_dx')
    g['w_out'] = _mm(sv['merged'], dx_mid, dims='tn', out_dtype=BF16, tm=1024, tn=512, tk=512, name='mm_out_dw')
    dya, dyb, dyc, dg0, dg1, dg2, db0, db1, db2 = _gate_bwd(
        sv['z'], p['b_gate'], sv['ya'], sv['yb'], sv['yc'], dmerged, name='gate_bwd')
    g['b_gate'] = jnp.concatenate([db0, db1, db2], axis=1)

    doa = _mm(dya, p['w_branch_a'], dims='nt', out_dtype=BF16, tm=512, tn=512, tk=1024, name='mm_branch_a_dx')
    g['w_branch_a'] = _mm(sv['oa'], dya, dims='tn', out_dtype=BF16, tm=512, tn=512, tk=512, name='mm_branch_a_dw')
    dqa, dkdup, dvdup = _attn_a_bwd(sv['qkv'], sv['kdup'], sv['vdup'], sv['oa'], sv['lse_a'], doa, name='attn_a_bwd')

    def fold(dup):
        return jnp.concatenate([dup[:, 0:64] + dup[:, 64:128], dup[:, 128:192] + dup[:, 192:256]], axis=1)

    dka, dva = fold(dkdup), fold(dvdup)

    dob = _mm(dyb, p['w_branch_b'], dims='nt', out_dtype=F32, tm=512, tn=256, tk=1024, name='mm_branch_b_dx')
    g['w_branch_b'] = _mm(sv['ob'], dyb, dims='tn', out_dtype=BF16, tm=256, tn=512, tk=512, name='mm_branch_b_dw')
    merged_b = _merge_bwd(sv['o_g'], sv['lse_g'], dob, name='merge_bwd')
    do_g, dd_g = merged_b[:3], merged_b[3:]
    views = _band_views(sv['qkv'], s)
    dq_parts, dk_parts, dv_parts = [], [], []
    for gi, d in enumerate(B_DILATIONS):
        ln = s // d
        dq, dk, dv = _band_bwd(views[gi], do_g[gi].reshape(ln, d * B_OUT_WIDTH),
                               sv['lse_g'][gi].reshape(ln, d * B_OUT_WIDTH), dd_g[gi].reshape(ln, d * B_OUT_WIDTH),
                               gi, d, name=f'band_bwd_d{d}')
        dq_parts.append(dq.reshape(s, B_OUT_WIDTH))
        dk_parts.append(dk.reshape(s, B_OUT_WIDTH))
        dv_parts.append(dv.reshape(s, B_OUT_WIDTH))

    dmixed = _mm(dyc, p['w_branch_c'], dims='nt', out_dtype=F32, tm=512, tn=512, tk=1024, name='mm_branch_c_dx')
    g['w_branch_c'] = _mm(sv['mixed'], dyc, dims='tn', out_dtype=BF16, tm=512, tn=512, tk=512, name='mm_branch_c_dw')
    du, g['pool_lin'], g['pool_scale'] = _pool_bwd(dmixed, sv['pooled'], p['pool_lin'], p['pool_scale'], name='pool_bwd')

    dqkv = jnp.concatenate([dqa, dka, dva] + dq_parts + dk_parts + dv_parts, axis=1)
    dz_qkv, dgains = _qkrope_bwd(sv['z'], sv['gains'], cos, sin, dqkv, name='qkrope_bwd')
    dgains = dgains[0]
    g['qn_a'] = _fold_heads(dgains[0:512], 8)
    g['kn_a'] = _fold_heads(dgains[512:640], 2)
    g['qn_b'] = _fold_heads(dgains[768:1536], 12)
    g['kn_b'] = _fold_heads(dgains[1536:2304], 12)

    dz = jnp.concatenate([dz_qkv, du, dg0, dg1, dg2], axis=1)
    dh = _mm(dz, p['w_in'], dims='nt', out_dtype=F32, tm=512, tn=512, tk=1664, name='mm_in_dx')
    g['w_in'] = _mm(sv['h'], dz, dims='tn', out_dtype=BF16, tm=1024, tn=512, tk=512, name='mm_in_dw')
    dx_in, g['norm_mix'] = _rms_bwd(sv['x'], p['norm_mix'], dh, dx_mid, name='rms_mix_bwd')
    return dx_in, g


def _local_step(x, target, layers):
    s = x.shape[0]
    tables = _rope_tables(s)
    saved = []
    for l, p in enumerate(layers):
        x, sv = _layer_fwd(x, p, tables, l)
        saved.append(sv)
    part, dx = _loss_head(x, target, name='loss_head')
    loss = jnp.sum(part)
    grads = [None] * len(layers)
    for l in reversed(range(len(layers))):
        dx, grads[l] = _layer_bwd(dx, layers[l], saved[l], tables)
    return loss, dx, grads


def _small_views(vals, l):
    return {
        'norm_mix': vals['norm_mix'][l][None, :], 'b_gate': vals['b_gate'][l][None, :],
        'qn_a': vals['qn_a'][l], 'kn_a': vals['kn_a'][l], 'qn_b': vals['qn_b'][l], 'kn_b': vals['kn_b'][l],
        'pool_lin': vals['pool_lin'][l], 'pool_scale': vals['pool_scale'][l][None, :],
        'norm_ffn': vals['norm_ffn'][l][None, :],
    }


def kernel(x, norm_mix, w_in, b_gate, qn_a, kn_a, qn_b, kn_b, pool_lin, pool_scale, w_branch_a, w_branch_b, w_branch_c, w_out, norm_ffn, w_ffn_gate, w_ffn_up, w_ffn_down, loss_target, m_norm_mix, m_w_in, m_b_gate, m_qn_a, m_kn_a, m_qn_b, m_kn_b, m_pool_lin, m_pool_scale, m_w_branch_a, m_w_branch_b, m_w_branch_c, m_w_out, m_norm_ffn, m_w_ffn_gate, m_w_ffn_up, m_w_ffn_down, v_norm_mix, v_w_in, v_b_gate, v_qn_a, v_kn_a, v_qn_b, v_kn_b, v_pool_lin, v_pool_scale, v_w_branch_a, v_w_branch_b, v_w_branch_c, v_w_out, v_norm_ffn, v_w_ffn_gate, v_w_ffn_up, v_w_ffn_down):
    w = dict(norm_mix=norm_mix, w_in=w_in, b_gate=b_gate, qn_a=qn_a, kn_a=kn_a, qn_b=qn_b, kn_b=kn_b,
             pool_lin=pool_lin, pool_scale=pool_scale, w_branch_a=w_branch_a, w_branch_b=w_branch_b,
             w_branch_c=w_branch_c, w_out=w_out, norm_ffn=norm_ffn, w_ffn_gate=w_ffn_gate, w_ffn_up=w_ffn_up,
             w_ffn_down=w_ffn_down)
    m = dict(norm_mix=m_norm_mix, w_in=m_w_in, b_gate=m_b_gate, qn_a=m_qn_a, kn_a=m_kn_a, qn_b=m_qn_b, kn_b=m_kn_b,
             pool_lin=m_pool_lin, pool_scale=m_pool_scale, w_branch_a=m_w_branch_a, w_branch_b=m_w_branch_b,
             w_branch_c=m_w_branch_c, w_out=m_w_out, norm_ffn=m_norm_ffn, w_ffn_gate=m_w_ffn_gate,
             w_ffn_up=m_w_ffn_up, w_ffn_down=m_w_ffn_down)
    v = dict(norm_mix=v_norm_mix, w_in=v_w_in, b_gate=v_b_gate, qn_a=v_qn_a, kn_a=v_kn_a, qn_b=v_qn_b, kn_b=v_kn_b,
             pool_lin=v_pool_lin, pool_scale=v_pool_scale, w_branch_a=v_w_branch_a, w_branch_b=v_w_branch_b,
             w_branch_c=v_w_branch_c, w_out=v_w_out, norm_ffn=v_norm_ffn, w_ffn_gate=v_w_ffn_gate,
             w_ffn_up=v_w_ffn_up, w_ffn_down=v_w_ffn_down)
    depth = w_in.shape[0]
    shard_shapes = {n: w[n].shape[1:] for n in SHARDED}
    small_shapes = {n: w[n].shape for n in SMALL}

    layers = []
    for l in range(depth):
        packed = _pack({n: w[n][l].astype(BF16) for n in SHARDED})
        gathered = _exchange(packed, gather=True, name='gather_weights')
        blocks = _unpack(gathered, shard_shapes)
        full = {n: (_from_row_blocks(blocks[n]) if n in SHARD_ROWWISE else _from_col_blocks(blocks[n]))
                for n in SHARDED}
        full.update(_small_views(w, l))
        layers.append(full)

    loss, grad_x, grads = _local_step(x[0], loss_target[0], layers)
    loss = lax.psum(loss, ('x', 'y', 'c'))

    out_g, out_d, out_m, out_v = ({n: [] for n in SHARDED} for _ in range(4))
    for l in range(depth):
        blocks = {n: (_row_blocks(grads[l][n]) if n in SHARD_ROWWISE else _col_blocks(grads[l][n])) for n in SHARDED}
        to_send = jnp.concatenate([blocks[n].reshape(N_DEV, -1, PACK_COLS) for n in SHARDED], axis=1)
        parts = _exchange(to_send, gather=False, name='scatter_grads')
        res = _adamw(parts, _pack({n: w[n][l] for n in SHARDED}), _pack({n: m[n][l] for n in SHARDED}),
                     _pack({n: v[n][l] for n in SHARDED}), name='adamw_sharded', tr=128)
        for dst, packed in zip((out_g, out_d, out_m, out_v), res):
            un = _unpack(packed, shard_shapes)
            for n in SHARDED:
                dst[n].append(un[n])
    new = {}
    for n in SHARDED:
        new[n] = tuple(jnp.stack(dst[n]) for dst in (out_g, out_d, out_m, out_v))

    small_grad = {n: jnp.stack([grads[l][n].reshape(small_shapes[n][1:]) for l in range(depth)]) for n in SMALL}
    parts = _exchange(_pack_small(small_grad), gather=True, name='gather_small_grads')
    res = _adamw(parts, _pack_small({n: w[n] for n in SMALL}), _pack_small({n: m[n] for n in SMALL}),
                 _pack_small({n: v[n] for n in SMALL}), name='adamw_small', tr=parts.shape[1])
    unpacked = [_unpack_small(r, small_shapes) for r in res]
    for n in SMALL:
        new[n] = tuple(u[n] for u in unpacked)

    outs = [loss, grad_x[None]]
    for idx in range(4):
        outs.extend(new[n][idx] for n in WEIGHTS)
    return tuple(outs)
```

```python
import functools

import jax
import jax.numpy as jnp
from jax import lax
from jax.experimental import pallas as pl
from jax.experimental.pallas import tpu as pltpu

F32 = jnp.float32
BF16 = jnp.bfloat16

N_DEV = 8
D_MODEL = 1024
DEPTH = 4
HEAD_DIM = 64
LANES = 128
A_Q_WIDTH = 512
A_KV_WIDTH = 128
B_WIDTH = 768
B_GROUPS = 3
B_DILATIONS = (1, 4, 16)
B_HALF_SPAN = 64
B_OUT_WIDTH = 256
POOL_WIDTH = 512
POOL_HALF = (1, 2, 4, 8)
GATE_WIDTH = 3072
QKV_WIDTH = A_Q_WIDTH + 2 * A_KV_WIDTH + 3 * B_WIDTH
IN_WIDTH = QKV_WIDTH + POOL_WIDTH + GATE_WIDTH
D_FF = 2816
GRID_W = 64
ROPE_THETA = 10000.0
EPS = 1e-6
NEG_INF = -1e30
ATTN_SCALE = HEAD_DIM ** -0.5

ADAM_LR = 0.001
ADAM_B1 = 0.9
ADAM_B2 = 0.999
ADAM_EPS = 1e-08
ADAM_WD = 0.01
ADAM_STEP = 10

PACK_COLS = 1024
VMEM_LIMIT = 56 * 1024 * 1024

SHARDED = ('w_in', 'w_branch_a', 'w_branch_b', 'w_branch_c', 'w_out', 'w_ffn_gate', 'w_ffn_up', 'w_ffn_down')
SMALL = ('norm_mix', 'b_gate', 'qn_a', 'kn_a', 'qn_b', 'kn_b', 'pool_lin', 'pool_scale', 'norm_ffn')
WEIGHTS = ('norm_mix', 'w_in', 'b_gate', 'qn_a', 'kn_a', 'qn_b', 'kn_b', 'pool_lin', 'pool_scale',
           'w_branch_a', 'w_branch_b', 'w_branch_c', 'w_out', 'norm_ffn', 'w_ffn_gate', 'w_ffn_up', 'w_ffn_down')

NN = (((1,), (0,)), ((), ()))
NT = (((1,), (1,)), ((), ()))
TN = (((0,), (0,)), ((), ()))


def _params(vmem=None):
    return pltpu.CompilerParams(vmem_limit_bytes=VMEM_LIMIT if vmem is None else vmem)


def _lane_iota(n=LANES):
    return lax.broadcasted_iota(jnp.int32, (1, n), 1)


def _swap(x, sh, lane):
    n = x.shape[-1]
    down = pltpu.roll(x, sh, axis=1)
    up = pltpu.roll(x, n - sh, axis=1)
    return jnp.where((lane & sh) == 0, up, down)


def _head_sum(v):
    w = v.shape[-1]
    r = lax.broadcasted_iota(jnp.int32, (w, w), 0) // HEAD_DIM
    c = lax.broadcasted_iota(jnp.int32, (w, w), 1) // HEAD_DIM
    ones = (r == c).astype(BF16)
    hi = v.astype(BF16)
    lo = (v - hi.astype(F32)).astype(BF16)
    return (lax.dot_general(hi, ones, NN, preferred_element_type=F32)
            + lax.dot_general(lo, ones, NN, preferred_element_type=F32))


def _mm(a, b, *, dims, out_dtype, tm, tn, tk, name, res=None):
    if dims == 'nn':
        (m, k), n = a.shape, b.shape[1]
    elif dims == 'nt':
        (m, k), n = a.shape, b.shape[0]
    else:
        (k, m), n = a.shape, b.shape[1]
    tm, tn, tk = min(tm, m), min(tn, n), min(tk, k)
    assert m % tm == 0 and n % tn == 0 and k % tk == 0, (name, m, n, k, tm, tn, tk)
    nk = k // tk
    if dims == 'tn':
        a_spec = pl.BlockSpec((tk, tm), lambda i, j, kk: (kk, i))
    else:
        a_spec = pl.BlockSpec((tm, tk), lambda i, j, kk: (i, kk))
    if dims == 'nt':
        b_spec = pl.BlockSpec((tn, tk), lambda i, j, kk: (j, kk))
    else:
        b_spec = pl.BlockSpec((tk, tn), lambda i, j, kk: (kk, j))
    o_spec = pl.BlockSpec((tm, tn), lambda i, j, kk: (i, j))
    dn = {'nn': NN, 'nt': NT, 'tn': TN}[dims]
    has_res = res is not None

    def body(*refs):
        if has_res:
            a_ref, b_ref, r_ref, o_ref, acc_ref = refs
        else:
            a_ref, b_ref, o_ref, acc_ref = refs
        prod = lax.dot_general(a_ref[...].astype(BF16), b_ref[...].astype(BF16), dn,
                               preferred_element_type=F32)

        def finish(total):
            if has_res:
                total = total + r_ref[...]
            o_ref[...] = total.astype(out_dtype)

        if nk == 1:
            finish(prod)
        else:
            kk = pl.program_id(2)

            @pl.when(kk == 0)
            def _():
                acc_ref[...] = prod

            @pl.when(kk > 0)
            def _():
                acc_ref[...] += prod

            @pl.when(kk == nk - 1)
            def _():
                finish(acc_ref[...])

    in_specs = [a_spec, b_spec] + ([o_spec] if has_res else [])
    args = (a, b) + ((res,) if has_res else ())
    acc_shape = (tm, tn) if nk > 1 else (8, LANES)
    return pl.pallas_call(
        body, name=name, grid=(m // tm, n // tn, nk),
        in_specs=in_specs, out_specs=o_spec,
        out_shape=jax.ShapeDtypeStruct((m, n), out_dtype),
        scratch_shapes=[pltpu.VMEM(acc_shape, F32)],
        compiler_params=_params(),
    )(*args)


def _rms_fwd(x, g, *, name, tm=512):
    s, d = x.shape

    def body(x_ref, g_ref, h_ref):
        xv = x_ref[...]
        rstd = lax.rsqrt(jnp.mean(xv * xv, axis=-1, keepdims=True) + EPS)
        h_ref[...] = (xv * rstd * g_ref[...]).astype(BF16)

    return pl.pallas_call(
        body, name=name, grid=(s // tm,),
        in_specs=[pl.BlockSpec((tm, d), lambda i: (i, 0)), pl.BlockSpec((1, d), lambda i: (0, 0))],
        out_specs=pl.BlockSpec((tm, d), lambda i: (i, 0)),
        out_shape=jax.ShapeDtypeStruct((s, d), BF16),
        compiler_params=_params(),
    )(x, g)


def _rms_bwd(x, g, dh, dres, *, name, tm=512):
    s, d = x.shape

    def body(x_ref, g_ref, dh_ref, dres_ref, dx_ref, dg_ref):
        i = pl.program_id(0)
        xv = x_ref[...]
        rstd = lax.rsqrt(jnp.mean(xv * xv, axis=-1, keepdims=True) + EPS)
        xhat = xv * rstd
        dhv = dh_ref[...]
        dxhat = dhv * g_ref[...]
        proj = jnp.mean(dxhat * xhat, axis=-1, keepdims=True)
        dx_ref[...] = dres_ref[...] + rstd * (dxhat - xhat * proj)
        part = jnp.sum(dhv * xhat, axis=0, keepdims=True)

        @pl.when(i == 0)
        def _():
            dg_ref[...] = part

        @pl.when(i > 0)
        def _():
            dg_ref[...] += part

    row = pl.BlockSpec((tm, d), lambda i: (i, 0))
    vec = pl.BlockSpec((1, d), lambda i: (0, 0))
    return pl.pallas_call(
        body, name=name, grid=(s // tm,),
        in_specs=[row, vec, row, row], out_specs=[row, vec],
        out_shape=[jax.ShapeDtypeStruct((s, d), F32), jax.ShapeDtypeStruct((1, d), F32)],
        compiler_params=_params(),
    )(x, g, dh, dres)


N_QKV_BLOCKS = QKV_WIDTH // LANES
A_BLOCKS = (A_Q_WIDTH + 2 * A_KV_WIDTH) // LANES
V_A_BLOCK = A_BLOCKS - 1
V_B_FIRST = A_BLOCKS + 2 * (B_WIDTH // LANES)


def _qk_kind(j):
    return jnp.where(j < A_BLOCKS, 0, 1)


def _is_v_block(j):
    return (j == V_A_BLOCK) | (j >= V_B_FIRST)


def _rope_tables(s):
    def ang(pos, dim):
        inv = ROPE_THETA ** (-jnp.arange(0, dim, 2, dtype=F32) / dim)
        return pos.astype(F32)[:, None] * inv[None, :]
    t = jnp.arange(s)
    a_row = ang(t // GRID_W, HEAD_DIM // 2)
    a_col = ang(t % GRID_W, HEAD_DIM // 2)
    a_seq = ang(t, HEAD_DIM)
    cos_a = jnp.concatenate([jnp.cos(a_row)] * 2 + [jnp.cos(a_col)] * 2, axis=-1)
    sin_a = jnp.concatenate([-jnp.sin(a_row), jnp.sin(a_row), -jnp.sin(a_col), jnp.sin(a_col)], axis=-1)
    cos_b = jnp.concatenate([jnp.cos(a_seq)] * 2, axis=-1)
    sin_b = jnp.concatenate([-jnp.sin(a_seq), jnp.sin(a_seq)], axis=-1)
    cos = jnp.stack([jnp.tile(cos_a, (1, 2)), jnp.tile(cos_b, (1, 2))])
    sin = jnp.stack([jnp.tile(sin_a, (1, 2)), jnp.tile(sin_b, (1, 2))])
    return cos, sin


def _qk_gains(qn_a, kn_a, qn_b, kn_b):
    one = jnp.ones((HEAD_DIM,), F32)
    parts = [jnp.tile(qn_a, 8), jnp.tile(kn_a, 2), jnp.tile(one, 2),
             jnp.tile(qn_b, 12), jnp.tile(kn_b, 12), jnp.tile(one, 12)]
    return jnp.concatenate(parts)[None, :]


def _qkrope_fwd(z, gains, cos, sin, *, name, tm=2048):
    s = z.shape[0]
    tm = min(tm, s)

    def body(z_ref, g_ref, c_ref, s_ref, o_ref):
        j = pl.program_id(0)
        lane = _lane_iota()
        xv = z_ref[...]

        def normed_rope(pair):
            ms = _head_sum(xv * xv) * (1.0 / HEAD_DIM)
            n = xv * lax.rsqrt(ms + EPS) * g_ref[...]
            return n * c_ref[...] + _swap(n, pair, lane) * s_ref[...]

        @pl.when(_is_v_block(j))
        def _():
            o_ref[...] = xv.astype(BF16)

        @pl.when(jnp.logical_not(_is_v_block(j)) & (j < A_BLOCKS))
        def _():
            o_ref[...] = normed_rope(HEAD_DIM // 4).astype(BF16)

        @pl.when(jnp.logical_not(_is_v_block(j)) & (j >= A_BLOCKS))
        def _():
            o_ref[...] = normed_rope(HEAD_DIM // 2).astype(BF16)

    tab = pl.BlockSpec((None, tm, LANES), lambda j, i: (_qk_kind(j), i, 0))
    blk = pl.BlockSpec((tm, LANES), lambda j, i: (i, j))
    return pl.pallas_call(
        body, name=name, grid=(N_QKV_BLOCKS, s // tm),
        in_specs=[blk, pl.BlockSpec((1, LANES), lambda j, i: (0, j)), tab, tab],
        out_specs=blk,
        out_shape=jax.ShapeDtypeStruct((s, QKV_WIDTH), BF16),
        compiler_params=_params(),
    )(z, gains, cos, sin)


def _qkrope_bwd(z, gains, cos, sin, dqkv, *, name, tm=2048):
    s = z.shape[0]
    tm = min(tm, s)

    def body(z_ref, g_ref, c_ref, s_ref, dy_ref, dz_ref, dg_ref):
        j = pl.program_id(0)
        i = pl.program_id(1)
        lane = _lane_iota()
        xv = z_ref[...]
        dy = dy_ref[...]

        @pl.when(i == 0)
        def _():
            dg_ref[...] = jnp.zeros_like(dg_ref)

        def back(pair):
            ms = _head_sum(xv * xv) * (1.0 / HEAD_DIM)
            rstd = lax.rsqrt(ms + EPS)
            xhat = xv * rstd
            dn = dy * c_ref[...] + _swap(dy * s_ref[...], pair, lane)
            dg_ref[...] += jnp.sum(dn * xhat, axis=0, keepdims=True)
            dxhat = dn * g_ref[...]
            proj = _head_sum(dxhat * xhat) * (1.0 / HEAD_DIM)
            dz_ref[...] = (rstd * (dxhat - xhat * proj)).astype(BF16)

        @pl.when(_is_v_block(j))
        def _():
            dz_ref[...] = dy.astype(BF16)

        @pl.when(jnp.logical_not(_is_v_block(j)) & (j < A_BLOCKS))
        def _():
            back(HEAD_DIM // 4)

        @pl.when(jnp.logical_not(_is_v_block(j)) & (j >= A_BLOCKS))
        def _():
            back(HEAD_DIM // 2)

    tab = pl.BlockSpec((None, tm, LANES), lambda j, i: (_qk_kind(j), i, 0))
    blk = pl.BlockSpec((tm, LANES), lambda j, i: (i, j))
    vec = pl.BlockSpec((1, LANES), lambda j, i: (0, j))
    return pl.pallas_call(
        body, name=name, grid=(N_QKV_BLOCKS, s // tm),
        in_specs=[blk, vec, tab, tab, blk],
        out_specs=[blk, vec],
        out_shape=[jax.ShapeDtypeStruct((s, QKV_WIDTH), BF16), jax.ShapeDtypeStruct((1, QKV_WIDTH), F32)],
        compiler_params=_params(),
    )(z, gains, cos, sin, dqkv)


def _dup_halves(kv):
    h0, h1 = kv[:, :HEAD_DIM], kv[:, HEAD_DIM:]
    return jnp.concatenate([h0, h0, h1, h1], axis=1)


def _with_ones(kv):
    h0, h1 = kv[:, :HEAD_DIM], kv[:, HEAD_DIM:]
    one = jnp.ones_like(h0)
    return jnp.concatenate([h0, one, one, h0, h1, one, one, h1], axis=1)


def _attn_a_fwd(qkv, kdup, vones, *, name, ride=None, tq=256, tk=512, unroll=2):
    s = qkv.shape[0]
    tq, tk = min(tq, s), min(tk, s)
    n_chunks = s // tk
    assert n_chunks % unroll == 0

    def body(q_ref, k_ref, v_ref, o_ref, lse_ref):
        lane = _lane_iota()
        low = lane < HEAD_DIM
        q = q_ref[...]
        zero = jnp.zeros_like(q)
        qm = [jnp.where(low, q, zero) * ATTN_SCALE, jnp.where(low, zero, q) * ATTN_SCALE]

        def chunks(c, carry):
            state = list(carry)
            scs = []
            for u in range(unroll):
                off = pl.multiple_of((c * unroll + u) * tk, tk)
                kc = k_ref[pl.ds(off, tk), :]
                scs.append([lax.dot_general(qm[e], kc, NT, preferred_element_type=F32) for e in range(2)])
            for u in range(unroll):
                off = pl.multiple_of((c * unroll + u) * tk, tk)
                for e in range(2):
                    m, acc = state[2 * e], state[2 * e + 1]
                    ve = v_ref[pl.ds(off, tk), e * LANES:(e + 1) * LANES]
                    m_new = jnp.maximum(m, jnp.max(scs[u][e], axis=1, keepdims=True))
                    alpha = jnp.exp(m - m_new)
                    p = jnp.exp(scs[u][e] - m_new).astype(BF16)
                    state[2 * e] = m_new
                    state[2 * e + 1] = alpha * acc + lax.dot_general(p, ve, NN, preferred_element_type=F32)
            return tuple(state)

        m_init = jnp.full((tq, 1), NEG_INF, F32)
        a_init = jnp.zeros((tq, LANES), F32)
        m0, a0, m1, a1 = lax.fori_loop(0, n_chunks // unroll, chunks, (m_init, a_init, m_init, a_init))
        l0 = pltpu.roll(a0, HEAD_DIM, axis=1)
        l1 = pltpu.roll(a1, HEAD_DIM, axis=1)
        o_ref[...] = jnp.where(low, a0 / l0, a1 / l1).astype(BF16)
        lse_ref[...] = jnp.where(low, m0 + jnp.log(l0), m1 + jnp.log(l1))

    q_spec = pl.BlockSpec((tq, LANES), lambda hb, qi: (qi, hb))
    k_spec = pl.BlockSpec((s, LANES), lambda hb, qi: (0, hb // 2))
    v_spec = pl.BlockSpec((s, 2 * LANES), lambda hb, qi: (0, hb // 2))
    grid = (A_Q_WIDTH // LANES, s // tq)
    body, extra = _carry_exchange(body, 3, 2, grid, ride)
    return pl.pallas_call(
        body, name=name, grid=grid,
        in_specs=[q_spec, k_spec, v_spec] + extra.in_specs, out_specs=[q_spec, q_spec] + extra.out_specs,
        out_shape=[jax.ShapeDtypeStruct((s, A_Q_WIDTH), BF16), jax.ShapeDtypeStruct((s, A_Q_WIDTH), F32)]
        + extra.out_shape,
        scratch_shapes=extra.scratch, compiler_params=_params(),
    )(qkv, kdup, vones, *extra.args)


def _attn_a_bwd(qkv, kdup, vdup, o, lse, do, *, name, ride=None, tq=256, tk=512, unroll=2):
    s = qkv.shape[0]
    tq, tk = min(tq, s), min(tk, s)
    n_chunks = s // tk
    assert n_chunks % unroll == 0

    def body(q_ref, k_ref, v_ref, o_ref, lse_ref, do_ref, dq_ref, dk_ref, dv_ref):
        first = (pl.program_id(1) == 0) & (pl.program_id(2) == 0)

        @pl.when(first)
        def _():
            dk_ref[...] = jnp.zeros_like(dk_ref)
            dv_ref[...] = jnp.zeros_like(dv_ref)

        lane = _lane_iota()
        low = lane < HEAD_DIM
        q = q_ref[...]
        dov = do_ref[...]
        zero = jnp.zeros_like(q)
        prod = dov.astype(F32) * o_ref[...].astype(F32)
        lsev = lse_ref[...]
        qs = [jnp.where(low, q, zero) * ATTN_SCALE, jnp.where(low, zero, q) * ATTN_SCALE]
        dom = [jnp.where(low, dov, zero), jnp.where(low, zero, dov)]
        delta = [jnp.sum(jnp.where(low, prod, 0.0), axis=1, keepdims=True),
                 jnp.sum(jnp.where(low, 0.0, prod), axis=1, keepdims=True)]
        lse = [lsev[:, 0:1], lsev[:, HEAD_DIM:HEAD_DIM + 1]]
        qs_both = jnp.concatenate(qs, axis=0)
        dom_both = jnp.concatenate(dom, axis=0)

        def chunks(c, carry):
            dqs = list(carry)
            for u in range(unroll):
                off = pl.multiple_of((c * unroll + u) * tk, tk)
                kc = k_ref[pl.ds(off, tk), :]
                vc = v_ref[pl.ds(off, tk), :]
                ps, dss = [], []
                for e in range(2):
                    sc = lax.dot_general(qs[e], kc, NT, preferred_element_type=F32)
                    p = jnp.exp(sc - lse[e])
                    dp = lax.dot_general(dom[e], vc, NT, preferred_element_type=F32)
                    ds = (p * (dp - delta[e])).astype(BF16)
                    ps.append(p.astype(BF16))
                    dss.append(ds)
                    dqs[e] = dqs[e] + lax.dot_general(ds, kc, NN, preferred_element_type=F32)
                dv_ref[pl.ds(off, tk), :] += lax.dot_general(jnp.concatenate(ps, axis=0), dom_both, TN,
                                                             preferred_element_type=F32)
                dk_ref[pl.ds(off, tk), :] += lax.dot_general(jnp.concatenate(dss, axis=0), qs_both, TN,
                                                             preferred_element_type=F32)
            return tuple(dqs)

        dq_init = jnp.zeros((tq, LANES), F32)
        dq0, dq1 = lax.fori_loop(0, n_chunks // unroll, chunks, (dq_init, dq_init))
        dq_ref[...] = jnp.where(low, dq0, dq1) * ATTN_SCALE

    q_spec = pl.BlockSpec((tq, LANES), lambda kvh, hb, qi: (qi, kvh * 2 + hb))
    kv_spec = pl.BlockSpec((s, LANES), lambda kvh, hb, qi: (0, kvh))
    grid = (2, 2, s // tq)
    body, extra = _carry_exchange(body, 6, 3, grid, ride)
    return pl.pallas_call(
        body, name=name, grid=grid,
        in_specs=[q_spec, kv_spec, kv_spec, q_spec, q_spec, q_spec] + extra.in_specs,
        out_specs=[q_spec, kv_spec, kv_spec] + extra.out_specs,
        out_shape=[jax.ShapeDtypeStruct((s, A_Q_WIDTH), F32),
                   jax.ShapeDtypeStruct((s, 2 * LANES), F32), jax.ShapeDtypeStruct((s, 2 * LANES), F32)]
        + extra.out_shape,
        scratch_shapes=extra.scratch, compiler_params=_params(),
    )(qkv, kdup, vdup, o, lse, do, *extra.args)


BAND_Q = 128
GROUP_QKV = 3 * B_OUT_WIDTH
TOKEN_BLOCKS = GROUP_QKV // LANES
B_Q_BLOCK0 = 0
B_K_BLOCK0 = B_OUT_WIDTH // LANES
B_V_BLOCK0 = 2 * (B_OUT_WIDTH // LANES)


def _band_geometry(length):
    seg = min(length, 2048)
    win = min(2 * BAND_Q, length)
    return seg, win


def _band_window(qs, length, win):
    st = jnp.clip(qs - B_HALF_SPAN, 0, length - win)
    st = pl.multiple_of(st, B_HALF_SPAN)
    qpos = qs + lax.broadcasted_iota(jnp.int32, (BAND_Q, 1), 0)
    kpos = st + lax.broadcasted_iota(jnp.int32, (1, win), 1)
    return st, jnp.abs(qpos - kpos) <= B_HALF_SPAN


def _band_fwd(qkv_view, gi, dil, *, name):
    length = qkv_view.shape[0]
    seg, win = _band_geometry(length)
    n_sub = seg // BAND_Q

    def body(q_ref, k_ref, v_ref, o_ref, lse_ref):
        seg_i = pl.program_id(2)
        lane = _lane_iota()
        low = lane < HEAD_DIM

        def sub(i, carry):
            ql = pl.multiple_of(i * BAND_Q, BAND_Q)
            st, valid = _band_window(seg_i * seg + ql, length, win)
            q = q_ref[pl.ds(ql, BAND_Q), :]
            kw = k_ref[pl.ds(st, win), :]
            vw = v_ref[pl.ds(st, win), :]
            outs, lses = [], []
            for e in range(2):
                mine = (lane >= HEAD_DIM) if e else (lane < HEAD_DIM)
                qm = jnp.where(mine, q, jnp.zeros_like(q)) * ATTN_SCALE
                sc = lax.dot_general(qm, kw, NT, preferred_element_type=F32)
                sc = jnp.where(valid, sc, NEG_INF)
                m = jnp.max(sc, axis=1, keepdims=True)
                p = jnp.exp(sc - m)
                l = jnp.sum(p, axis=1, keepdims=True)
                outs.append(lax.dot_general(p.astype(BF16), vw, NN, preferred_element_type=F32) / l)
                lses.append(m + jnp.log(l))
            o_ref[pl.ds(ql, BAND_Q), :] = jnp.where(low, outs[0], outs[1]).astype(BF16)
            lse_ref[pl.ds(ql, BAND_Q), :] = jnp.where(low, lses[0], lses[1])
            return carry

        lax.fori_loop(0, n_sub, sub, 0)

    def col(base):
        return lambda r, hp, sg: (0, r * TOKEN_BLOCKS + base + hp)

    q_spec = pl.BlockSpec((seg, LANES), lambda r, hp, sg: (sg, r * TOKEN_BLOCKS + B_Q_BLOCK0 + hp))
    out_spec = pl.BlockSpec((seg, LANES), lambda r, hp, sg: (sg, r * 2 + hp))
    return pl.pallas_call(
        body, name=name, grid=(dil, 2, length // seg),
        in_specs=[q_spec, pl.BlockSpec((length, LANES), col(B_K_BLOCK0)), pl.BlockSpec((length, LANES), col(B_V_BLOCK0))],
        out_specs=[out_spec, out_spec],
        out_shape=[jax.ShapeDtypeStruct((length, dil * B_OUT_WIDTH), BF16),
                   jax.ShapeDtypeStruct((length, dil * B_OUT_WIDTH), F32)],
        compiler_params=_params(),
    )(qkv_view, qkv_view, qkv_view)


def _band_bwd(qkv_view, do, lse, dd, gi, dil, *, name):
    length = qkv_view.shape[0]
    seg, win = _band_geometry(length)
    n_sub = seg // BAND_Q

    def body(q_ref, k_ref, v_ref, do_ref, lse_ref, dd_ref, dq_ref, dk_ref, dv_ref):
        seg_i = pl.program_id(2)
        lane = _lane_iota()

        @pl.when(seg_i == 0)
        def _():
            dk_ref[...] = jnp.zeros_like(dk_ref)
            dv_ref[...] = jnp.zeros_like(dv_ref)

        def sub(i, carry):
            ql = pl.multiple_of(i * BAND_Q, BAND_Q)
            st, valid = _band_window(seg_i * seg + ql, length, win)
            q = q_ref[pl.ds(ql, BAND_Q), :]
            dov = do_ref[pl.ds(ql, BAND_Q), :]
            lsev = lse_ref[pl.ds(ql, BAND_Q), :]
            ddv = dd_ref[pl.ds(ql, BAND_Q), :]
            kw = k_ref[pl.ds(st, win), :]
            vw = v_ref[pl.ds(st, win), :]
            dq = jnp.zeros((BAND_Q, LANES), F32)
            for e in range(2):
                mine = (lane >= HEAD_DIM) if e else (lane < HEAD_DIM)
                qs = jnp.where(mine, q, jnp.zeros_like(q)) * ATTN_SCALE
                dom = jnp.where(mine, dov, jnp.zeros_like(dov))
                lse_e = lsev[:, e * HEAD_DIM:e * HEAD_DIM + 1]
                dd_e = ddv[:, e * HEAD_DIM:e * HEAD_DIM + 1]
                sc = lax.dot_general(qs, kw, NT, preferred_element_type=F32)
                p = jnp.exp(jnp.where(valid, sc, NEG_INF) - lse_e)
                dp = lax.dot_general(dom, vw, NT, preferred_element_type=F32)
                ds = (p * (dp - dd_e)).astype(BF16)
                dv_ref[pl.ds(st, win), :] += lax.dot_general(p.astype(BF16), dom, TN, preferred_element_type=F32)
                dk_ref[pl.ds(st, win), :] += lax.dot_general(ds, qs, TN, preferred_element_type=F32)
                dq_e = lax.dot_general(ds, kw, NN, preferred_element_type=F32) * ATTN_SCALE
                dq = dq + jnp.where(mine, dq_e, 0.0)
            dq_ref[pl.ds(ql, BAND_Q), :] = dq
            return carry

        lax.fori_loop(0, n_sub, sub, 0)

    def col(base):
        return lambda r, hp, sg: (0, r * TOKEN_BLOCKS + base + hp)

    q_spec = pl.BlockSpec((seg, LANES), lambda r, hp, sg: (sg, r * TOKEN_BLOCKS + B_Q_BLOCK0 + hp))
    seg_spec = pl.BlockSpec((seg, LANES), lambda r, hp, sg: (sg, r * 2 + hp))
    full_spec = pl.BlockSpec((length, LANES), lambda r, hp, sg: (0, r * 2 + hp))
    shp = jax.ShapeDtypeStruct((length, dil * B_OUT_WIDTH), F32)
    return pl.pallas_call(
        body, name=name, grid=(dil, 2, length // seg),
        in_specs=[q_spec, pl.BlockSpec((length, LANES), col(B_K_BLOCK0)), pl.BlockSpec((length, LANES), col(B_V_BLOCK0)),
                  seg_spec, seg_spec, seg_spec],
        out_specs=[seg_spec, full_spec, full_spec],
        out_shape=[shp, shp, shp],
        compiler_params=_params(),
    )(qkv_view, qkv_view, qkv_view, do, lse, dd)


def _merge_weights(lses):
    m = jnp.maximum(jnp.maximum(lses[0], lses[1]), lses[2])
    ex = [jnp.exp(v - m) for v in lses]
    tot = ex[0] + ex[1] + ex[2]
    return [v / tot for v in ex]


def _merge_fwd(os_, lses, *, name, tm=512):
    s = os_[0].shape[0]

    def body(o0, o1, o2, l0, l1, l2, ob_ref):
        w = _merge_weights([l0[...], l1[...], l2[...]])
        ob = w[0] * o0[...].astype(F32) + w[1] * o1[...].astype(F32) + w[2] * o2[...].astype(F32)
        ob_ref[...] = ob.astype(BF16)

    blk = pl.BlockSpec((tm, B_OUT_WIDTH), lambda i: (i, 0))
    return pl.pallas_call(
        body, name=name, grid=(s // tm,), in_specs=[blk] * 6, out_specs=blk,
        out_shape=jax.ShapeDtypeStruct((s, B_OUT_WIDTH), BF16),
        compiler_params=_params(),
    )(*os_, *lses)


def _merge_bwd(os_, lses, dob, *, name, tm=512):
    s = os_[0].shape[0]

    def body(o0, o1, o2, l0, l1, l2, dob_ref, d0, d1, d2, t0, t1, t2):
        w = _merge_weights([l0[...], l1[...], l2[...]])
        dv = dob_ref[...]
        ob = w[0] * o0[...].astype(F32) + w[1] * o1[...].astype(F32) + w[2] * o2[...].astype(F32)
        tot = _head_sum(dv * ob)
        for wg, d_ref, t_ref in zip(w, (d0, d1, d2), (t0, t1, t2)):
            d_ref[...] = (wg * dv).astype(BF16)
            t_ref[...] = wg * tot

    blk = pl.BlockSpec((tm, B_OUT_WIDTH), lambda i: (i, 0))
    return pl.pallas_call(
        body, name=name, grid=(s // tm,), in_specs=[blk] * 7, out_specs=[blk] * 6,
        out_shape=[jax.ShapeDtypeStruct((s, B_OUT_WIDTH), BF16)] * 3 + [jax.ShapeDtypeStruct((s, B_OUT_WIDTH), F32)] * 3,
        compiler_params=_params(),
    )(*os_, *lses, dob)


HALO = 8
POOL_BLOCK0 = QKV_WIDTH // LANES


def _window_sum(ext, lo, hi, tm):
    rows = ext.shape[0]
    acc = None
    for j in range(lo, hi + 1):
        r = ext if j == 0 else pltpu.roll(ext, (-j) % rows, axis=0)
        acc = r if acc is None else acc + r
    return acc[HALO:HALO + tm]


def _pool_counts(t, half, s):
    return (jnp.minimum(t + half, s) - jnp.maximum(t - half, 0)).astype(F32)


def _halo_specs(tm, s, col0):
    per = tm // HALO
    last = s // HALO - 1
    prev = pl.BlockSpec((HALO, LANES), lambda g, i: (jnp.maximum(i * per - 1, 0), col0 + g))
    cur = pl.BlockSpec((tm, LANES), lambda g, i: (i, col0 + g))
    nxt = pl.BlockSpec((HALO, LANES), lambda g, i: (jnp.minimum((i + 1) * per, last), col0 + g))
    return prev, cur, nxt


def _extended(prev_ref, cur_ref, next_ref, i, n_tiles):
    prev = jnp.where(i > 0, prev_ref[...], 0.0)
    nxt = jnp.where(i < n_tiles - 1, next_ref[...], 0.0)
    return jnp.concatenate([prev, cur_ref[...], nxt], axis=0)


def _pool_fwd(z, lin, scale, *, name, tm=512):
    s = z.shape[0]
    tm = min(tm, s)
    n_tiles = s // tm

    def body(prev_ref, cur_ref, next_ref, lin_ref, sc_ref, pooled_ref, mixed_ref):
        g = pl.program_id(0)
        i = pl.program_id(1)
        ext = _extended(prev_ref, cur_ref, next_ref, i, n_tiles)
        t = i * tm + lax.broadcasted_iota(jnp.int32, (tm, 1), 0)
        for gi, half in enumerate(POOL_HALF):
            @pl.when(g == gi)
            def _(half=half):
                mean = _window_sum(ext, -half, half - 1, tm) / _pool_counts(t, half, s)
                pooled = (mean - cur_ref[...]).astype(BF16)
                pooled_ref[...] = pooled
                mixed = lax.dot_general(pooled, lin_ref[...].astype(BF16), NN, preferred_element_type=F32)
                mixed_ref[...] = (mixed * sc_ref[...]).astype(BF16)

    prev, cur, nxt = _halo_specs(tm, s, POOL_BLOCK0)
    out = pl.BlockSpec((tm, LANES), lambda g, i: (i, g))
    return pl.pallas_call(
        body, name=name, grid=(len(POOL_HALF), n_tiles),
        in_specs=[prev, cur, nxt, pl.BlockSpec((None, LANES, LANES), lambda g, i: (g, 0, 0)),
                  pl.BlockSpec((1, LANES), lambda g, i: (0, g))],
        out_specs=[out, out],
        out_shape=[jax.ShapeDtypeStruct((s, POOL_WIDTH), BF16)] * 2,
        compiler_params=_params(),
    )(z, z, z, lin, scale)


def _pool_bwd(dmixed, pooled, lin, scale, *, name, tm=512):
    s = dmixed.shape[0]
    tm = min(tm, s)
    n_tiles = s // tm

    def body(prev_ref, cur_ref, next_ref, pooled_ref, lin_ref, sc_ref, du_ref, dlin_ref, dsc_ref):
        g = pl.program_id(0)
        i = pl.program_id(1)

        @pl.when(i == 0)
        def _():
            dlin_ref[...] = jnp.zeros_like(dlin_ref)
            dsc_ref[...] = jnp.zeros_like(dsc_ref)

        linb = lin_ref[...].astype(BF16)
        ext = _extended(prev_ref, cur_ref, next_ref, i, n_tiles)
        dpl_ext = (ext * sc_ref[...]).astype(BF16)
        dpl_cur = (cur_ref[...] * sc_ref[...]).astype(BF16)
        dpooled_ext = lax.dot_general(dpl_ext, linb, NT, preferred_element_type=F32)
        t_ext = i * tm - HALO + lax.broadcasted_iota(jnp.int32, (tm + 2 * HALO, 1), 0)
        pooled = pooled_ref[...]
        mixed = lax.dot_general(pooled, linb, NN, preferred_element_type=F32)
        dsc_ref[...] += jnp.sum(cur_ref[...] * mixed, axis=0, keepdims=True)
        dlin_ref[...] += lax.dot_general(pooled, dpl_cur, TN, preferred_element_type=F32)
        for gi, half in enumerate(POOL_HALF):
            @pl.when(g == gi)
            def _(half=half):
                share = dpooled_ext / jnp.maximum(_pool_counts(t_ext, half, s), 1.0)
                du = _window_sum(share, -(half - 1), half, tm) - dpooled_ext[HALO:HALO + tm]
                du_ref[...] = du.astype(BF16)

    prev, cur, nxt = _halo_specs(tm, s, 0)
    out = pl.BlockSpec((tm, LANES), lambda g, i: (i, g))
    lin_spec = pl.BlockSpec((None, LANES, LANES), lambda g, i: (g, 0, 0))
    vec = pl.BlockSpec((1, LANES), lambda g, i: (0, g))
    return pl.pallas_call(
        body, name=name, grid=(len(POOL_HALF), n_tiles),
        in_specs=[prev, cur, nxt, out, lin_spec, vec],
        out_specs=[out, lin_spec, vec],
        out_shape=[jax.ShapeDtypeStruct((s, POOL_WIDTH), BF16),
                   jax.ShapeDtypeStruct((len(POOL_HALF), LANES, LANES), F32),
                   jax.ShapeDtypeStruct((1, POOL_WIDTH), F32)],
        compiler_params=_params(),
    )(dmixed, dmixed, dmixed, pooled, lin, scale)


GATE_TILE = 512
GATE_BLOCK0 = (QKV_WIDTH + POOL_WIDTH) // GATE_TILE
GATE_BLOCKS_PER_BRANCH = D_MODEL // GATE_TILE


def _sigmoid(v):
    return 1.0 / (1.0 + jnp.exp(-v))


def _gate_specs(tm):
    def zspec(br):
        return pl.BlockSpec((tm, GATE_TILE), lambda jj, i: (i, GATE_BLOCK0 + GATE_BLOCKS_PER_BRANCH * br + jj))

    def bspec(br):
        return pl.BlockSpec((1, GATE_TILE), lambda jj, i: (0, GATE_BLOCKS_PER_BRANCH * br + jj))

    row = pl.BlockSpec((tm, GATE_TILE), lambda jj, i: (i, jj))
    vec = pl.BlockSpec((1, GATE_TILE), lambda jj, i: (0, jj))
    return [zspec(0), zspec(1), zspec(2)], [bspec(0), bspec(1), bspec(2)], row, vec


def _gate_fwd(z, b_gate, ya, yb, yc, *, name, tm=512):
    s = z.shape[0]

    def body(z0, z1, z2, b0, b1, b2, ya_ref, yb_ref, yc_ref, out_ref):
        acc = _sigmoid(z0[...] + b0[...]) * ya_ref[...]
        acc = acc + _sigmoid(z1[...] + b1[...]) * yb_ref[...]
        acc = acc + _sigmoid(z2[...] + b2[...]) * yc_ref[...]
        out_ref[...] = acc.astype(BF16)

    zs, bs, row, _ = _gate_specs(tm)
    return pl.pallas_call(
        body, name=name, grid=(GATE_BLOCKS_PER_BRANCH, s // tm),
        in_specs=zs + bs + [row] * 3, out_specs=row,
        out_shape=jax.ShapeDtypeStruct((s, D_MODEL), BF16),
        compiler_params=_params(),
    )(z, z, z, b_gate, b_gate, b_gate, ya, yb, yc)


def _gate_bwd(z, b_gate, ya, yb, yc, dmerged, *, name, tm=512):
    s = z.shape[0]

    def body(z0, z1, z2, b0, b1, b2, ya_ref, yb_ref, yc_ref, dm_ref,
             dya_ref, dyb_ref, dyc_ref, dg0, dg1, dg2, db0, db1, db2):
        i = pl.program_id(1)
        dm = dm_ref[...]
        for z_ref, b_ref, y_ref, dy_ref, dg_ref, db_ref in (
                (z0, b0, ya_ref, dya_ref, dg0, db0), (z1, b1, yb_ref, dyb_ref, dg1, db1),
                (z2, b2, yc_ref, dyc_ref, dg2, db2)):
            gate = _sigmoid(z_ref[...] + b_ref[...])
            dy_ref[...] = (gate * dm).astype(BF16)
            dpre = dm * y_ref[...] * gate * (1.0 - gate)
            dg_ref[...] = dpre.astype(BF16)
            part = jnp.sum(dpre, axis=0, keepdims=True)

            @pl.when(i == 0)
            def _(db_ref=db_ref, part=part):
                db_ref[...] = part

            @pl.when(i > 0)
            def _(db_ref=db_ref, part=part):
                db_ref[...] += part

    zs, bs, row, vec = _gate_specs(tm)
    big = jax.ShapeDtypeStruct((s, D_MODEL), BF16)
    small = jax.ShapeDtypeStruct((1, D_MODEL), F32)
    return pl.pallas_call(
        body, name=name, grid=(GATE_BLOCKS_PER_BRANCH, s // tm),
        in_specs=zs + bs + [row] * 4, out_specs=[row] * 6 + [vec] * 3,
        out_shape=[big] * 6 + [small] * 3,
        compiler_params=_params(),
    )(z, z, z, b_gate, b_gate, b_gate, ya, yb, yc, dmerged)


def _swiglu_fwd(a, b, *, name, tm=512, tn=1408):
    s, f = a.shape

    def body(a_ref, b_ref, o_ref):
        av = a_ref[...]
        o_ref[...] = (av * _sigmoid(av) * b_ref[...]).astype(BF16)

    blk = pl.BlockSpec((tm, tn), lambda i, j: (i, j))
    return pl.pallas_call(
        body, name=name, grid=(s // tm, f // tn), in_specs=[blk, blk], out_specs=blk,
        out_shape=jax.ShapeDtypeStruct((s, f), BF16), compiler_params=_params(),
    )(a, b)


def _swiglu_bwd(a, b, df, *, name, tm=512, tn=1408):
    s, f = a.shape

    def body(a_ref, b_ref, df_ref, da_ref, db_ref):
        av = a_ref[...]
        dfv = df_ref[...]
        sg = _sigmoid(av)
        silu = av * sg
        da_ref[...] = (dfv * b_ref[...] * (sg + silu * (1.0 - sg))).astype(BF16)
        db_ref[...] = (dfv * silu).astype(BF16)

    blk = pl.BlockSpec((tm, tn), lambda i, j: (i, j))
    out = jax.ShapeDtypeStruct((s, f), BF16)
    return pl.pallas_call(
        body, name=name, grid=(s // tm, f // tn), in_specs=[blk] * 3, out_specs=[blk] * 2,
        out_shape=[out, out], compiler_params=_params(),
    )(a, b, df)


def _loss_head(y, target, *, name, tm=512):
    s, d = y.shape

    def body(y_ref, t_ref, part_ref, dy_ref):
        i = pl.program_id(0)
        err = y_ref[...] - t_ref[...]
        dy_ref[...] = err * (1.0 / d)
        part = jnp.sum(err * err, axis=0, keepdims=True) * (0.5 / d)

        @pl.when(i == 0)
        def _():
            part_ref[...] = part

        @pl.when(i > 0)
        def _():
            part_ref[...] += part

    row = pl.BlockSpec((tm, d), lambda i: (i, 0))
    vec = pl.BlockSpec((1, d), lambda i: (0, 0))
    return pl.pallas_call(
        body, name=name, grid=(s // tm,), in_specs=[row, row], out_specs=[vec, row],
        out_shape=[jax.ShapeDtypeStruct((1, d), F32), jax.ShapeDtypeStruct((s, d), F32)],
        compiler_params=_params(),
    )(y, target)


def _mesh_place():
    x, y, c = lax.axis_index('x'), lax.axis_index('y'), lax.axis_index('c')
    return x, y, c, 4 * x + 2 * y + c


def _peer(x, y, c, k):
    return (x ^ ((k >> 2) & 1), y ^ ((k >> 1) & 1), c ^ (k & 1))


def _exchange(buf, *, gather, name):
    def body(in_ref, out_ref, send_sems, recv_sems, local_sem):
        start, wait = _exchange_plan(in_ref, out_ref, send_sems, recv_sems, local_sem, gather)
        start()
        wait()

    return pl.pallas_call(
        body, name=name,
        in_specs=[_EXCHANGE_SPEC], out_specs=_EXCHANGE_SPEC,
        out_shape=_exchange_shape(buf), scratch_shapes=_exchange_scratch(),
    )(buf)


_EXCHANGE_SPEC = pl.BlockSpec(memory_space=pl.ANY)


class _Extra:
    def __init__(self, ride):
        self.in_specs = [] if ride is None else [_EXCHANGE_SPEC]
        self.out_specs = [] if ride is None else [_EXCHANGE_SPEC]
        self.out_shape = [] if ride is None else [_exchange_shape(ride[0])]
        self.scratch = [] if ride is None else _exchange_scratch()
        self.args = () if ride is None else (ride[0],)


def _carry_exchange(body, n_in, n_out, grid, ride):
    extra = _Extra(ride)
    if ride is None:
        return body, extra
    gather = ride[1]

    def carrying(*refs):
        ins, ride_in = refs[:n_in], refs[n_in]
        outs, ride_out = refs[n_in + 1:n_in + 1 + n_out], refs[n_in + 1 + n_out]
        sems = refs[n_in + 2 + n_out:]
        ids = [pl.program_id(a) for a in range(len(grid))]
        first = functools.reduce(jnp.logical_and, [i == 0 for i in ids])
        last = functools.reduce(jnp.logical_and, [i == n - 1 for i, n in zip(ids, grid)])

        @pl.when(first)
        def _():
            _exchange_plan(ride_in, ride_out, *sems, gather)[0]()

        body(*ins, *outs)

        @pl.when(last)
        def _():
            _exchange_plan(ride_in, ride_out, *sems, gather)[1]()

    return carrying, extra


def _exchange_shape(buf):
    return jax.ShapeDtypeStruct((N_DEV,) + buf.shape[-2:], buf.dtype)


def _exchange_scratch():
    return [pltpu.SemaphoreType.DMA((N_DEV - 1,)), pltpu.SemaphoreType.DMA((N_DEV - 1,)), pltpu.SemaphoreType.DMA]


def _exchange_plan(in_ref, out_ref, send_sems, recv_sems, local_sem, gather):
    x, y, c, me = _mesh_place()

    def src(slot):
        return in_ref if gather else in_ref.at[slot]

    def copy(k, dst_slot):
        return pltpu.make_async_remote_copy(
            src_ref=src(me ^ k), dst_ref=out_ref.at[dst_slot],
            send_sem=send_sems.at[k - 1], recv_sem=recv_sems.at[k - 1],
            device_id=_peer(x, y, c, k), device_id_type=pl.DeviceIdType.MESH)

    def mine():
        return pltpu.make_async_copy(src(me), out_ref.at[me], local_sem)

    def start():
        mine().start()
        for k in range(1, N_DEV):
            copy(k, me).start()

    def wait():
        for k in range(1, N_DEV):
            copy(k, me ^ k).wait_recv()
        for k in range(1, N_DEV):
            copy(k, me).wait_send()
        mine().wait()

    return start, wait


def _adamw(parts, w, m, v, *, name, tr):
    rows, cols = w.shape
    assert rows % tr == 0
    bias1 = 1.0 - ADAM_B1 ** ADAM_STEP
    bias2 = 1.0 - ADAM_B2 ** ADAM_STEP

    def body(p_ref, w_ref, m_ref, v_ref, g_ref, d_ref, nm_ref, nv_ref):
        g = p_ref[0].astype(F32)
        for j in range(1, N_DEV):
            g = g + p_ref[j].astype(F32)
        nm = ADAM_B1 * m_ref[...] + (1.0 - ADAM_B1) * g
        nv = ADAM_B2 * v_ref[...] + (1.0 - ADAM_B2) * (g * g)
        g_ref[...] = g
        nm_ref[...] = nm
        nv_ref[...] = nv
        d_ref[...] = -ADAM_LR * ((nm / bias1) / (jnp.sqrt(nv / bias2) + ADAM_EPS) + ADAM_WD * w_ref[...])

    blk = pl.BlockSpec((tr, cols), lambda i: (i, 0))
    out = jax.ShapeDtypeStruct((rows, cols), F32)
    return pl.pallas_call(
        body, name=name, grid=(rows // tr,),
        in_specs=[pl.BlockSpec((N_DEV, tr, cols), lambda i: (0, i, 0)), blk, blk, blk],
        out_specs=[blk] * 4, out_shape=[out] * 4, compiler_params=_params(),
    )(parts, w, m, v)


def _col_blocks(full):
    r, n = full.shape
    return full.reshape(r, N_DEV, n // N_DEV).transpose(1, 0, 2)


def _row_blocks(full):
    r, n = full.shape
    return full.reshape(N_DEV, r // N_DEV, n)


def _from_col_blocks(blocks):
    j, r, c = blocks.shape
    return blocks.transpose(1, 0, 2).reshape(r, j * c)


def _from_row_blocks(blocks):
    j, r, c = blocks.shape
    return blocks.reshape(j * r, c)


SHARD_ROWWISE = {'w_out', 'w_ffn_down'}


def _pack(shards):
    return jnp.concatenate([shards[n].reshape(-1, PACK_COLS) for n in SHARDED], axis=0)


def _unpack(packed, shapes):
    out, off = {}, 0
    lead = packed.shape[:-2]
    for n in SHARDED:
        size = shapes[n][0] * shapes[n][1] // PACK_COLS
        out[n] = packed[..., off:off + size, :].reshape(lead + tuple(shapes[n]))
        off += size
    return out


def _pack_small(vals):
    flat = jnp.concatenate([vals[n].reshape(-1) for n in SMALL])
    return flat.reshape(-1, PACK_COLS)


def _unpack_small(packed, shapes):
    flat = packed.reshape(-1)
    out, off = {}, 0
    for n in SMALL:
        size = 1
        for dim in shapes[n]:
            size *= dim
        out[n] = flat[off:off + size].reshape(shapes[n])
        off += size
    return out


def _band_views(qkv, s):
    views = []
    for gi, d in enumerate(B_DILATIONS):
        starts = [A_Q_WIDTH + 2 * A_KV_WIDTH + part * B_WIDTH + gi * B_OUT_WIDTH for part in range(3)]
        group = jnp.concatenate([qkv[:, c0:c0 + B_OUT_WIDTH] for c0 in starts], axis=1)
        views.append(group.reshape(s // d, d * GROUP_QKV))
    return views


def _layer_fwd(x, p, tables, ride=None):
    s = x.shape[0]
    cos, sin = tables
    h = _rms_fwd(x, p['norm_mix'], name='rms_mix_fwd')
    z = _mm(h, p['w_in'], dims='nn', out_dtype=F32, tm=1024, tn=1664, tk=1024, name='mm_in_fwd')
    gains = _qk_gains(p['qn_a'], p['kn_a'], p['qn_b'], p['kn_b'])
    qkv = _qkrope_fwd(z, gains, cos, sin, name='qkrope_fwd')

    kdup = _dup_halves(qkv[:, A_Q_WIDTH:A_Q_WIDTH + A_KV_WIDTH])
    va = qkv[:, A_Q_WIDTH + A_KV_WIDTH:A_Q_WIDTH + 2 * A_KV_WIDTH]
    vdup = _dup_halves(va)
    oa, lse_a, *exchanged = _attn_a_fwd(qkv, kdup, _with_ones(va), ride=ride, name='attn_a_fwd')
    ya = _mm(oa, p['w_branch_a'], dims='nn', out_dtype=F32, tm=1024, tn=1024, tk=512, name='mm_branch_a_fwd')

    views = _band_views(qkv, s)
    o_g, lse_g = [], []
    for gi, d in enumerate(B_DILATIONS):
        o, lse = _band_fwd(views[gi], gi, d, name=f'band_fwd_d{d}')
        o_g.append(o.reshape(s, B_OUT_WIDTH))
        lse_g.append(lse.reshape(s, B_OUT_WIDTH))
    ob = _merge_fwd(o_g, lse_g, name='merge_fwd')
    yb = _mm(ob, p['w_branch_b'], dims='nn', out_dtype=F32, tm=1024, tn=1024, tk=256, name='mm_branch_b_fwd')

    pooled, mixed = _pool_fwd(z, p['pool_lin'], p['pool_scale'], name='pool_fwd')
    yc = _mm(mixed, p['w_branch_c'], dims='nn', out_dtype=F32, tm=1024, tn=1024, tk=512, name='mm_branch_c_fwd')

    merged = _gate_fwd(z, p['b_gate'], ya, yb, yc, name='gate_fwd')
    x_mid = _mm(merged, p['w_out'], dims='nn', out_dtype=F32, tm=1024, tn=1024, tk=1024, res=x, name='mm_out_fwd')

    h2 = _rms_fwd(x_mid, p['norm_ffn'], name='rms_ffn_fwd')
    fa = _mm(h2, p['w_ffn_gate'], dims='nn', out_dtype=F32, tm=1024, tn=1408, tk=1024, name='mm_ffn_gate_fwd')
    fb = _mm(h2, p['w_ffn_up'], dims='nn', out_dtype=F32, tm=1024, tn=1408, tk=1024, name='mm_ffn_up_fwd')
    f = _swiglu_fwd(fa, fb, name='swiglu_fwd')
    x_out = _mm(f, p['w_ffn_down'], dims='nn', out_dtype=F32, tm=1024, tn=512, tk=2816, res=x_mid, name='mm_ffn_down_fwd')

    saved = dict(x=x, h=h, z=z, gains=gains, qkv=qkv, kdup=kdup, vdup=vdup, oa=oa, lse_a=lse_a, o_g=o_g, lse_g=lse_g,
                 ob=ob, pooled=pooled, mixed=mixed, ya=ya, yb=yb, yc=yc, merged=merged, x_mid=x_mid, h2=h2,
                 fa=fa, fb=fb, f=f, views=views)
    return x_out, saved, (exchanged[0] if exchanged else None)


def _fold_heads(v, heads):
    return v.reshape(heads, HEAD_DIM).sum(axis=0)


def _layer_bwd(dx, p, sv, tables, ride=None):
    s = dx.shape[0]
    cos, sin = tables
    g = {}

    df = _mm(dx, p['w_ffn_down'], dims='nt', out_dtype=F32, tm=1024, tn=1408, tk=1024, name='mm_ffn_down_dx')
    g['w_ffn_down'] = _mm(sv['f'], dx, dims='tn', out_dtype=BF16, tm=1408, tn=1024, tk=512, name='mm_ffn_down_dw')
    da, db = _swiglu_bwd(sv['fa'], sv['fb'], df, name='swiglu_bwd')
    dh2 = _mm(da, p['w_ffn_gate'], dims='nt', out_dtype=F32, tm=1024, tn=512, tk=2816, name='mm_ffn_gate_dx')
    dh2 = _mm(db, p['w_ffn_up'], dims='nt', out_dtype=F32, tm=1024, tn=512, tk=2816, res=dh2, name='mm_ffn_up_dx')
    g['w_ffn_gate'] = _mm(sv['h2'], da, dims='tn', out_dtype=BF16, tm=1024, tn=2816, tk=512, name='mm_ffn_gate_dw')
    g['w_ffn_up'] = _mm(sv['h2'], db, dims='tn', out_dtype=BF16, tm=1024, tn=2816, tk=512, name='mm_ffn_up_dw')
    dx_mid, g['norm_ffn'] = _rms_bwd(sv['x_mid'], p['norm_ffn'], dh2, dx, name='rms_ffn_bwd')

    dmerged = _mm(dx_mid, p['w_out'], dims='nt', out_dtype=F32, tm=1024, tn=1024, tk=1024, name='mm_out_dx')
    g['w_out'] = _mm(sv['merged'], dx_mid, dims='tn', out_dtype=BF16, tm=1024, tn=1024, tk=512, name='mm_out_dw')
    dya, dyb, dyc, dg0, dg1, dg2, db0, db1, db2 = _gate_bwd(
        sv['z'], p['b_gate'], sv['ya'], sv['yb'], sv['yc'], dmerged, name='gate_bwd')
    g['b_gate'] = jnp.concatenate([db0, db1, db2], axis=1)

    doa = _mm(dya, p['w_branch_a'], dims='nt', out_dtype=BF16, tm=1024, tn=512, tk=1024, name='mm_branch_a_dx')
    g['w_branch_a'] = _mm(sv['oa'], dya, dims='tn', out_dtype=BF16, tm=512, tn=1024, tk=512, name='mm_branch_a_dw')
    dqa, dkdup, dvdup, *exchanged = _attn_a_bwd(sv['qkv'], sv['kdup'], sv['vdup'], sv['oa'], sv['lse_a'], doa,
                                                ride=ride, name='attn_a_bwd')

    def fold(dup):
        return jnp.concatenate([dup[:, 0:64] + dup[:, 64:128], dup[:, 128:192] + dup[:, 192:256]], axis=1)

    dka, dva = fold(dkdup), fold(dvdup)

    dob = _mm(dyb, p['w_branch_b'], dims='nt', out_dtype=F32, tm=1024, tn=256, tk=1024, name='mm_branch_b_dx')
    g['w_branch_b'] = _mm(sv['ob'], dyb, dims='tn', out_dtype=BF16, tm=256, tn=1024, tk=512, name='mm_branch_b_dw')
    merged_b = _merge_bwd(sv['o_g'], sv['lse_g'], dob, name='merge_bwd')
    do_g, dd_g = merged_b[:3], merged_b[3:]
    views = sv['views']
    dq_parts, dk_parts, dv_parts = [], [], []
    for gi, d in enumerate(B_DILATIONS):
        ln = s // d
        dq, dk, dv = _band_bwd(views[gi], do_g[gi].reshape(ln, d * B_OUT_WIDTH),
                               sv['lse_g'][gi].reshape(ln, d * B_OUT_WIDTH), dd_g[gi].reshape(ln, d * B_OUT_WIDTH),
                               gi, d, name=f'band_bwd_d{d}')
        dq_parts.append(dq.reshape(s, B_OUT_WIDTH))
        dk_parts.append(dk.reshape(s, B_OUT_WIDTH))
        dv_parts.append(dv.reshape(s, B_OUT_WIDTH))

    dmixed = _mm(dyc, p['w_branch_c'], dims='nt', out_dtype=F32, tm=1024, tn=512, tk=1024, name='mm_branch_c_dx')
    g['w_branch_c'] = _mm(sv['mixed'], dyc, dims='tn', out_dtype=BF16, tm=512, tn=1024, tk=512, name='mm_branch_c_dw')
    du, g['pool_lin'], g['pool_scale'] = _pool_bwd(dmixed, sv['pooled'], p['pool_lin'], p['pool_scale'], name='pool_bwd')

    dqkv = jnp.concatenate([dqa, dka, dva] + dq_parts + dk_parts + dv_parts, axis=1)
    dz_qkv, dgains = _qkrope_bwd(sv['z'], sv['gains'], cos, sin, dqkv, name='qkrope_bwd')
    dgains = dgains[0]
    g['qn_a'] = _fold_heads(dgains[0:512], 8)
    g['kn_a'] = _fold_heads(dgains[512:640], 2)
    g['qn_b'] = _fold_heads(dgains[768:1536], 12)
    g['kn_b'] = _fold_heads(dgains[1536:2304], 12)

    dz = jnp.concatenate([dz_qkv, du, dg0, dg1, dg2], axis=1)
    dh = _mm(dz, p['w_in'], dims='nt', out_dtype=F32, tm=1024, tn=1024, tk=1664, name='mm_in_dx')
    g['w_in'] = _mm(sv['h'], dz, dims='tn', out_dtype=BF16, tm=1024, tn=3328, tk=512, name='mm_in_dw')
    dx_in, g['norm_mix'] = _rms_bwd(sv['x'], p['norm_mix'], dh, dx_mid, name='rms_mix_bwd')
    return dx_in, g, (exchanged[0] if exchanged else None)


def _small_views(vals, l):
    return {
        'norm_mix': vals['norm_mix'][l][None, :], 'b_gate': vals['b_gate'][l][None, :],
        'qn_a': vals['qn_a'][l], 'kn_a': vals['kn_a'][l], 'qn_b': vals['qn_b'][l], 'kn_b': vals['kn_b'][l],
        'pool_lin': vals['pool_lin'][l], 'pool_scale': vals['pool_scale'][l][None, :],
        'norm_ffn': vals['norm_ffn'][l][None, :],
    }


def kernel(x, norm_mix, w_in, b_gate, qn_a, kn_a, qn_b, kn_b, pool_lin, pool_scale, w_branch_a, w_branch_b, w_branch_c, w_out, norm_ffn, w_ffn_gate, w_ffn_up, w_ffn_down, loss_target, m_norm_mix, m_w_in, m_b_gate, m_qn_a, m_kn_a, m_qn_b, m_kn_b, m_pool_lin, m_pool_scale, m_w_branch_a, m_w_branch_b, m_w_branch_c, m_w_out, m_norm_ffn, m_w_ffn_gate, m_w_ffn_up, m_w_ffn_down, v_norm_mix, v_w_in, v_b_gate, v_qn_a, v_kn_a, v_qn_b, v_kn_b, v_pool_lin, v_pool_scale, v_w_branch_a, v_w_branch_b, v_w_branch_c, v_w_out, v_norm_ffn, v_w_ffn_gate, v_w_ffn_up, v_w_ffn_down):
    w = dict(norm_mix=norm_mix, w_in=w_in, b_gate=b_gate, qn_a=qn_a, kn_a=kn_a, qn_b=qn_b, kn_b=kn_b,
             pool_lin=pool_lin, pool_scale=pool_scale, w_branch_a=w_branch_a, w_branch_b=w_branch_b,
             w_branch_c=w_branch_c, w_out=w_out, norm_ffn=norm_ffn, w_ffn_gate=w_ffn_gate, w_ffn_up=w_ffn_up,
             w_ffn_down=w_ffn_down)
    m = dict(norm_mix=m_norm_mix, w_in=m_w_in, b_gate=m_b_gate, qn_a=m_qn_a, kn_a=m_kn_a, qn_b=m_qn_b, kn_b=m_kn_b,
             pool_lin=m_pool_lin, pool_scale=m_pool_scale, w_branch_a=m_w_branch_a, w_branch_b=m_w_branch_b,
             w_branch_c=m_w_branch_c, w_out=m_w_out, norm_ffn=m_norm_ffn, w_ffn_gate=m_w_ffn_gate,
             w_ffn_up=m_w_ffn_up, w_ffn_down=m_w_ffn_down)
    v = dict(norm_mix=v_norm_mix, w_in=v_w_in, b_gate=v_b_gate, qn_a=v_qn_a, kn_a=v_kn_a, qn_b=v_qn_b, kn_b=v_kn_b,
             pool_lin=v_pool_lin, pool_scale=v_pool_scale, w_branch_a=v_w_branch_a, w_branch_b=v_w_branch_b,
             w_branch_c=v_w_branch_c, w_out=v_w_out, norm_ffn=v_norm_ffn, w_ffn_gate=v_w_ffn_gate,
             w_ffn_up=v_w_ffn_up, w_ffn_down=v_w_ffn_down)
    depth = w_in.shape[0]
    shard_shapes = {n: w[n].shape[1:] for n in SHARDED}
    small_shapes = {n: w[n].shape for n in SMALL}

    def my_shards(l):
        return _pack({n: w[n][l].astype(BF16) for n in SHARDED})

    def full_layer(gathered, l):
        blocks = _unpack(gathered, shard_shapes)
        full = {n: (_from_row_blocks(blocks[n]) if n in SHARD_ROWWISE else _from_col_blocks(blocks[n]))
                for n in SHARDED}
        full.update(_small_views(w, l))
        return full

    def blocks_to_send(grad):
        blocks = {n: (_row_blocks(grad[n]) if n in SHARD_ROWWISE else _col_blocks(grad[n])) for n in SHARDED}
        return jnp.concatenate([blocks[n].reshape(N_DEV, -1, PACK_COLS) for n in SHARDED], axis=1)

    tables = _rope_tables(x.shape[1])
    layers = [full_layer(_exchange(my_shards(0), gather=True, name='gather_weights'), 0)]
    saved = []
    act = x[0]
    for l in range(depth):
        ride = (my_shards(l + 1), True) if l + 1 < depth else None
        act, sv, gathered = _layer_fwd(act, layers[l], tables, ride=ride)
        saved.append(sv)
        if ride is not None:
            layers.append(full_layer(gathered, l + 1))
    part, dx = _loss_head(act, loss_target[0], name='loss_head')
    loss = lax.psum(jnp.sum(part), ('x', 'y', 'c'))

    parts = [None] * depth
    grads = [None] * depth
    pending = None
    for l in reversed(range(depth)):
        ride = (pending, False) if pending is not None else None
        dx, grads[l], arrived = _layer_bwd(dx, layers[l], saved[l], tables, ride=ride)
        if ride is not None:
            parts[l + 1] = arrived
        pending = blocks_to_send(grads[l])
    parts[0] = _exchange(pending, gather=False, name='scatter_grads')
    grad_x = dx

    out_g, out_d, out_m, out_v = ({n: [] for n in SHARDED} for _ in range(4))
    for l in range(depth):
        res = _adamw(parts[l], _pack({n: w[n][l] for n in SHARDED}), _pack({n: m[n][l] for n in SHARDED}),
                     _pack({n: v[n][l] for n in SHARDED}), name='adamw_sharded', tr=128)
        for dst, packed in zip((out_g, out_d, out_m, out_v), res):
            un = _unpack(packed, shard_shapes)
            for n in SHARDED:
                dst[n].append(un[n])
    new = {}
    for n in SHARDED:
        new[n] = tuple(jnp.stack(dst[n]) for dst in (out_g, out_d, out_m, out_v))

    small_grad = {n: jnp.stack([grads[l][n].reshape(small_shapes[n][1:]) for l in range(depth)]) for n in SMALL}
    parts = _exchange(_pack_small(small_grad), gather=True, name='gather_small_grads')
    res = _adamw(parts, _pack_small({n: w[n] for n in SMALL}), _pack_small({n: m[n] for n in SMALL}),
                 _pack_small({n: v[n] for n in SMALL}), name='adamw_small', tr=parts.shape[1])
    unpacked = [_unpack_small(r, small_shapes) for r in res]
    for n in SMALL:
        new[n] = tuple(u[n] for u in unpacked)

    outs = [loss, grad_x[None]]
    for idx in range(4):
        outs.extend(new[n][idx] for n in WEIGHTS)
    return tuple(outs)
```

```python
import functools

import jax
import jax.numpy as jnp
from jax import lax
from jax.experimental import pallas as pl
from jax.experimental.pallas import tpu as pltpu

F32 = jnp.float32
BF16 = jnp.bfloat16

N_DEV = 8
D_MODEL = 1024
DEPTH = 4
HEAD_DIM = 64
LANES = 128
A_Q_WIDTH = 512
A_KV_WIDTH = 128
B_WIDTH = 768
B_GROUPS = 3
B_DILATIONS = (1, 4, 16)
B_HALF_SPAN = 64
B_OUT_WIDTH = 256
POOL_WIDTH = 512
POOL_HALF = (1, 2, 4, 8)
GATE_WIDTH = 3072
QKV_WIDTH = A_Q_WIDTH + 2 * A_KV_WIDTH + 3 * B_WIDTH
IN_WIDTH = QKV_WIDTH + POOL_WIDTH + GATE_WIDTH
D_FF = 2816
GRID_W = 64
ROPE_THETA = 10000.0
EPS = 1e-6
NEG_INF = -1e30
ATTN_SCALE = HEAD_DIM ** -0.5

ADAM_LR = 0.001
ADAM_B1 = 0.9
ADAM_B2 = 0.999
ADAM_EPS = 1e-08
ADAM_WD = 0.01
ADAM_STEP = 10

PACK_COLS = 1024
VMEM_LIMIT = 56 * 1024 * 1024

SHARDED = ('w_in', 'w_branch_a', 'w_branch_b', 'w_branch_c', 'w_out', 'w_ffn_gate', 'w_ffn_up', 'w_ffn_down')
SMALL = ('norm_mix', 'b_gate', 'qn_a', 'kn_a', 'qn_b', 'kn_b', 'pool_lin', 'pool_scale', 'norm_ffn')
WEIGHTS = ('norm_mix', 'w_in', 'b_gate', 'qn_a', 'kn_a', 'qn_b', 'kn_b', 'pool_lin', 'pool_scale',
           'w_branch_a', 'w_branch_b', 'w_branch_c', 'w_out', 'norm_ffn', 'w_ffn_gate', 'w_ffn_up', 'w_ffn_down')

NN = (((1,), (0,)), ((), ()))
NT = (((1,), (1,)), ((), ()))
TN = (((0,), (0,)), ((), ()))


def _params(vmem=None):
    return pltpu.CompilerParams(vmem_limit_bytes=VMEM_LIMIT if vmem is None else vmem)


def _lane_iota(n=LANES):
    return lax.broadcasted_iota(jnp.int32, (1, n), 1)


def _swap(x, sh, lane):
    n = x.shape[-1]
    down = pltpu.roll(x, sh, axis=1)
    up = pltpu.roll(x, n - sh, axis=1)
    return jnp.where((lane & sh) == 0, up, down)


def _head_sum(v):
    w = v.shape[-1]
    r = lax.broadcasted_iota(jnp.int32, (w, w), 0) // HEAD_DIM
    c = lax.broadcasted_iota(jnp.int32, (w, w), 1) // HEAD_DIM
    ones = (r == c).astype(BF16)
    hi = v.astype(BF16)
    lo = (v - hi.astype(F32)).astype(BF16)
    return (lax.dot_general(hi, ones, NN, preferred_element_type=F32)
            + lax.dot_general(lo, ones, NN, preferred_element_type=F32))


def _mm(a, b, *, dims, out_dtype, tm, tn, tk, name, res=None):
    if dims == 'nn':
        (m, k), n = a.shape, b.shape[1]
    elif dims == 'nt':
        (m, k), n = a.shape, b.shape[0]
    else:
        (k, m), n = a.shape, b.shape[1]
    tm, tn, tk = min(tm, m), min(tn, n), min(tk, k)
    assert m % tm == 0 and n % tn == 0 and k % tk == 0, (name, m, n, k, tm, tn, tk)
    nk = k // tk
    if dims == 'tn':
        a_spec = pl.BlockSpec((tk, tm), lambda i, j, kk: (kk, i))
    else:
        a_spec = pl.BlockSpec((tm, tk), lambda i, j, kk: (i, kk))
    if dims == 'nt':
        b_spec = pl.BlockSpec((tn, tk), lambda i, j, kk: (j, kk))
    else:
        b_spec = pl.BlockSpec((tk, tn), lambda i, j, kk: (kk, j))
    o_spec = pl.BlockSpec((tm, tn), lambda i, j, kk: (i, j))
    dn = {'nn': NN, 'nt': NT, 'tn': TN}[dims]
    has_res = res is not None

    def body(*refs):
        if has_res:
            a_ref, b_ref, r_ref, o_ref, acc_ref = refs
        else:
            a_ref, b_ref, o_ref, acc_ref = refs
        prod = lax.dot_general(a_ref[...].astype(BF16), b_ref[...].astype(BF16), dn,
                               preferred_element_type=F32)

        def finish(total):
            if has_res:
                total = total + r_ref[...]
            o_ref[...] = total.astype(out_dtype)

        if nk == 1:
            finish(prod)
        else:
            kk = pl.program_id(2)

            @pl.when(kk == 0)
            def _():
                acc_ref[...] = prod

            @pl.when(kk > 0)
            def _():
                acc_ref[...] += prod

            @pl.when(kk == nk - 1)
            def _():
                finish(acc_ref[...])

    in_specs = [a_spec, b_spec] + ([o_spec] if has_res else [])
    args = (a, b) + ((res,) if has_res else ())
    acc_shape = (tm, tn) if nk > 1 else (8, LANES)
    return pl.pallas_call(
        body, name=name, grid=(m // tm, n // tn, nk),
        in_specs=in_specs, out_specs=o_spec,
        out_shape=jax.ShapeDtypeStruct((m, n), out_dtype),
        scratch_shapes=[pltpu.VMEM(acc_shape, F32)],
        compiler_params=_params(),
    )(*args)


def _rms_fwd(x, g, *, name, tm=512):
    s, d = x.shape

    def body(x_ref, g_ref, h_ref):
        xv = x_ref[...]
        rstd = lax.rsqrt(jnp.mean(xv * xv, axis=-1, keepdims=True) + EPS)
        h_ref[...] = (xv * rstd * g_ref[...]).astype(BF16)

    return pl.pallas_call(
        body, name=name, grid=(s // tm,),
        in_specs=[pl.BlockSpec((tm, d), lambda i: (i, 0)), pl.BlockSpec((1, d), lambda i: (0, 0))],
        out_specs=pl.BlockSpec((tm, d), lambda i: (i, 0)),
        out_shape=jax.ShapeDtypeStruct((s, d), BF16),
        compiler_params=_params(),
    )(x, g)


def _rms_bwd(x, g, dh, dres, *, name, tm=512):
    s, d = x.shape

    def body(x_ref, g_ref, dh_ref, dres_ref, dx_ref, dg_ref):
        i = pl.program_id(0)
        xv = x_ref[...]
        rstd = lax.rsqrt(jnp.mean(xv * xv, axis=-1, keepdims=True) + EPS)
        xhat = xv * rstd
        dhv = dh_ref[...]
        dxhat = dhv * g_ref[...]
        proj = jnp.mean(dxhat * xhat, axis=-1, keepdims=True)
        dx_ref[...] = dres_ref[...] + rstd * (dxhat - xhat * proj)
        part = jnp.sum(dhv * xhat, axis=0, keepdims=True)

        @pl.when(i == 0)
        def _():
            dg_ref[...] = part

        @pl.when(i > 0)
        def _():
            dg_ref[...] += part

    row = pl.BlockSpec((tm, d), lambda i: (i, 0))
    vec = pl.BlockSpec((1, d), lambda i: (0, 0))
    return pl.pallas_call(
        body, name=name, grid=(s // tm,),
        in_specs=[row, vec, row, row], out_specs=[row, vec],
        out_shape=[jax.ShapeDtypeStruct((s, d), F32), jax.ShapeDtypeStruct((1, d), F32)],
        compiler_params=_params(),
    )(x, g, dh, dres)


N_QKV_BLOCKS = QKV_WIDTH // LANES
A_BLOCKS = (A_Q_WIDTH + 2 * A_KV_WIDTH) // LANES
V_A_BLOCK = A_BLOCKS - 1
V_B_FIRST = A_BLOCKS + 2 * (B_WIDTH // LANES)


def _qk_kind(j):
    return jnp.where(j < A_BLOCKS, 0, 1)


def _is_v_block(j):
    return (j == V_A_BLOCK) | (j >= V_B_FIRST)


def _rope_tables(s):
    def ang(pos, dim):
        inv = ROPE_THETA ** (-jnp.arange(0, dim, 2, dtype=F32) / dim)
        return pos.astype(F32)[:, None] * inv[None, :]
    t = jnp.arange(s)
    a_row = ang(t // GRID_W, HEAD_DIM // 2)
    a_col = ang(t % GRID_W, HEAD_DIM // 2)
    a_seq = ang(t, HEAD_DIM)
    cos_a = jnp.concatenate([jnp.cos(a_row)] * 2 + [jnp.cos(a_col)] * 2, axis=-1)
    sin_a = jnp.concatenate([-jnp.sin(a_row), jnp.sin(a_row), -jnp.sin(a_col), jnp.sin(a_col)], axis=-1)
    cos_b = jnp.concatenate([jnp.cos(a_seq)] * 2, axis=-1)
    sin_b = jnp.concatenate([-jnp.sin(a_seq), jnp.sin(a_seq)], axis=-1)
    cos = jnp.stack([jnp.tile(cos_a, (1, 2)), jnp.tile(cos_b, (1, 2))])
    sin = jnp.stack([jnp.tile(sin_a, (1, 2)), jnp.tile(sin_b, (1, 2))])
    return cos, sin


def _qk_gains(qn_a, kn_a, qn_b, kn_b):
    one = jnp.ones((HEAD_DIM,), F32)
    parts = [jnp.tile(qn_a, 8), jnp.tile(kn_a, 2), jnp.tile(one, 2),
             jnp.tile(qn_b, 12), jnp.tile(kn_b, 12), jnp.tile(one, 12)]
    return jnp.concatenate(parts)[None, :]


def _qkrope_fwd(z, gains, cos, sin, *, name, tm=2048):
    s = z.shape[0]
    tm = min(tm, s)

    def body(z_ref, g_ref, c_ref, s_ref, o_ref):
        j = pl.program_id(1)
        lane = _lane_iota()
        xv = z_ref[...]

        def normed_rope(pair):
            ms = _head_sum(xv * xv) * (1.0 / HEAD_DIM)
            n = xv * lax.rsqrt(ms + EPS) * g_ref[...]
            return n * c_ref[...] + _swap(n, pair, lane) * s_ref[...]

        @pl.when(_is_v_block(j))
        def _():
            o_ref[...] = xv.astype(BF16)

        @pl.when(jnp.logical_not(_is_v_block(j)) & (j < A_BLOCKS))
        def _():
            o_ref[...] = normed_rope(HEAD_DIM // 4).astype(BF16)

        @pl.when(jnp.logical_not(_is_v_block(j)) & (j >= A_BLOCKS))
        def _():
            o_ref[...] = normed_rope(HEAD_DIM // 2).astype(BF16)

    tab = pl.BlockSpec((None, tm, LANES), lambda i, j: (_qk_kind(j), i, 0))
    blk = pl.BlockSpec((tm, LANES), lambda i, j: (i, j))
    return pl.pallas_call(
        body, name=name, grid=(s // tm, N_QKV_BLOCKS),
        in_specs=[blk, pl.BlockSpec((1, LANES), lambda i, j: (0, j)), tab, tab],
        out_specs=blk,
        out_shape=jax.ShapeDtypeStruct((s, QKV_WIDTH), BF16),
        compiler_params=_params(),
    )(z, gains, cos, sin)


SUBLANES = 8


def _qkrope_bwd(z, gains, cos, sin, dqkv, *, name, tm=2048):
    s = z.shape[0]
    tm = min(tm, s)

    def body(z_ref, g_ref, c_ref, s_ref, dy_ref, dz_ref, dg_ref):
        j = pl.program_id(1)
        lane = _lane_iota()
        xv = z_ref[...]
        dy = dy_ref[...].astype(F32)

        def back(pair):
            ms = _head_sum(xv * xv) * (1.0 / HEAD_DIM)
            rstd = lax.rsqrt(ms + EPS)
            xhat = xv * rstd
            dn = dy * c_ref[...] + _swap(dy * s_ref[...], pair, lane)
            dg_ref[...] = jnp.sum((dn * xhat).reshape(tm // SUBLANES, SUBLANES, LANES), axis=0)
            dxhat = dn * g_ref[...]
            proj = _head_sum(dxhat * xhat) * (1.0 / HEAD_DIM)
            dz_ref[...] = (rstd * (dxhat - xhat * proj)).astype(BF16)

        @pl.when(_is_v_block(j))
        def _():
            dz_ref[...] = dy.astype(BF16)
            dg_ref[...] = jnp.zeros_like(dg_ref)

        @pl.when(jnp.logical_not(_is_v_block(j)) & (j < A_BLOCKS))
        def _():
            back(HEAD_DIM // 4)

        @pl.when(jnp.logical_not(_is_v_block(j)) & (j >= A_BLOCKS))
        def _():
            back(HEAD_DIM // 2)

    tab = pl.BlockSpec((None, tm, LANES), lambda i, j: (_qk_kind(j), i, 0))
    blk = pl.BlockSpec((tm, LANES), lambda i, j: (i, j))
    return pl.pallas_call(
        body, name=name, grid=(s // tm, N_QKV_BLOCKS),
        in_specs=[blk, pl.BlockSpec((1, LANES), lambda i, j: (0, j)), tab, tab, blk],
        out_specs=[blk, pl.BlockSpec((SUBLANES, LANES), lambda i, j: (i, j))],
        out_shape=[jax.ShapeDtypeStruct((s, QKV_WIDTH), BF16),
                   jax.ShapeDtypeStruct((s // tm * SUBLANES, QKV_WIDTH), F32)],
        compiler_params=_params(),
    )(z, gains, cos, sin, dqkv)


def _dup_halves(kv):
    h0, h1 = kv[:, :HEAD_DIM], kv[:, HEAD_DIM:]
    return jnp.concatenate([h0, h0, h1, h1], axis=1)


def _with_ones(kv):
    h0, h1 = kv[:, :HEAD_DIM], kv[:, HEAD_DIM:]
    one = jnp.ones_like(h0)
    return jnp.concatenate([h0, one, one, h0, h1, one, one, h1], axis=1)


def _attn_a_fwd(qkv, kdup, vones, *, name, ride=None, tq=256, tk=512, unroll=8):
    s = qkv.shape[0]
    tq, tk = min(tq, s), min(tk, s)
    n_chunks = s // tk
    unroll = min(unroll, n_chunks)
    assert n_chunks % unroll == 0

    def body(q_ref, k_ref, v_ref, o_ref, lse_ref):
        lane = _lane_iota()
        low = lane < HEAD_DIM
        q = q_ref[...]
        zero = jnp.zeros_like(q)
        qm = [jnp.where(low, q, zero) * ATTN_SCALE, jnp.where(low, zero, q) * ATTN_SCALE]

        def chunks(c, carry):
            state = list(carry)
            scs = []
            for u in range(unroll):
                off = pl.multiple_of((c * unroll + u) * tk, tk)
                kc = k_ref[pl.ds(off, tk), :]
                scs.append([lax.dot_general(qm[e], kc, NT, preferred_element_type=F32) for e in range(2)])
            for u in range(unroll):
                off = pl.multiple_of((c * unroll + u) * tk, tk)
                for e in range(2):
                    m, acc = state[2 * e], state[2 * e + 1]
                    ve = v_ref[pl.ds(off, tk), e * LANES:(e + 1) * LANES]
                    m_new = jnp.maximum(m, jnp.max(scs[u][e], axis=1, keepdims=True))
                    alpha = jnp.exp(m - m_new)
                    p = jnp.exp(scs[u][e] - m_new).astype(BF16)
                    state[2 * e] = m_new
                    state[2 * e + 1] = alpha * acc + lax.dot_general(p, ve, NN, preferred_element_type=F32)
            return tuple(state)

        m_init = jnp.full((tq, 1), NEG_INF, F32)
        a_init = jnp.zeros((tq, LANES), F32)
        m0, a0, m1, a1 = lax.fori_loop(0, n_chunks // unroll, chunks, (m_init, a_init, m_init, a_init))
        l0 = pltpu.roll(a0, HEAD_DIM, axis=1)
        l1 = pltpu.roll(a1, HEAD_DIM, axis=1)
        o_ref[...] = jnp.where(low, a0 / l0, a1 / l1).astype(BF16)
        lse_ref[...] = jnp.where(low, m0 + jnp.log(l0), m1 + jnp.log(l1))

    q_spec = pl.BlockSpec((tq, LANES), lambda hb, qi: (qi, hb))
    k_spec = pl.BlockSpec((s, LANES), lambda hb, qi: (0, hb // 2))
    v_spec = pl.BlockSpec((s, 2 * LANES), lambda hb, qi: (0, hb // 2))
    grid = (A_Q_WIDTH // LANES, s // tq)
    body, extra = _carry_exchange(body, 3, 2, grid, ride)
    return pl.pallas_call(
        body, name=name, grid=grid,
        in_specs=[q_spec, k_spec, v_spec] + extra.in_specs, out_specs=[q_spec, q_spec] + extra.out_specs,
        out_shape=[jax.ShapeDtypeStruct((s, A_Q_WIDTH), BF16), jax.ShapeDtypeStruct((s, A_Q_WIDTH), F32)]
        + extra.out_shape,
        scratch_shapes=extra.scratch, compiler_params=_params(),
    )(qkv, kdup, vones, *extra.args)


def _attn_a_bwd(qkv, kdup, vdup, o, lse, do, *, name, ride=None, tq=256, tk=512, unroll=4):
    s = qkv.shape[0]
    tq, tk = min(tq, s), min(tk, s)
    n_chunks = s // tk
    unroll = min(unroll, n_chunks)
    assert n_chunks % unroll == 0

    def body(q_ref, k_ref, v_ref, o_ref, lse_ref, do_ref, dq_ref, dk_ref, dv_ref):
        first = (pl.program_id(1) == 0) & (pl.program_id(2) == 0)

        @pl.when(first)
        def _():
            dk_ref[...] = jnp.zeros_like(dk_ref)
            dv_ref[...] = jnp.zeros_like(dv_ref)

        lane = _lane_iota()
        low = lane < HEAD_DIM
        q = q_ref[...]
        dov = do_ref[...]
        zero = jnp.zeros_like(q)
        prod = dov.astype(F32) * o_ref[...].astype(F32)
        lsev = lse_ref[...]
        qs = [jnp.where(low, q, zero) * ATTN_SCALE, jnp.where(low, zero, q) * ATTN_SCALE]
        dom = [jnp.where(low, dov, zero), jnp.where(low, zero, dov)]
        delta = [jnp.sum(jnp.where(low, prod, 0.0), axis=1, keepdims=True),
                 jnp.sum(jnp.where(low, 0.0, prod), axis=1, keepdims=True)]
        lse = [lsev[:, 0:1], lsev[:, HEAD_DIM:HEAD_DIM + 1]]
        qs_both = jnp.concatenate(qs, axis=0)
        dom_both = jnp.concatenate(dom, axis=0)

        def chunks(c, carry):
            dqs = list(carry)
            for u in range(unroll):
                off = pl.multiple_of((c * unroll + u) * tk, tk)
                kc = k_ref[pl.ds(off, tk), :]
                vc = v_ref[pl.ds(off, tk), :]
                ps, dss = [], []
                for e in range(2):
                    sc = lax.dot_general(qs[e], kc, NT, preferred_element_type=F32)
                    p = jnp.exp(sc - lse[e])
                    dp = lax.dot_general(dom[e], vc, NT, preferred_element_type=F32)
                    ds = (p * (dp - delta[e])).astype(BF16)
                    ps.append(p.astype(BF16))
                    dss.append(ds)
                    dqs[e] = dqs[e] + lax.dot_general(ds, kc, NN, preferred_element_type=F32)
                dv_ref[pl.ds(off, tk), :] += lax.dot_general(jnp.concatenate(ps, axis=0), dom_both, TN,
                                                             preferred_element_type=F32)
                dk_ref[pl.ds(off, tk), :] += lax.dot_general(jnp.concatenate(dss, axis=0), qs_both, TN,
                                                             preferred_element_type=F32)
            return tuple(dqs)

        dq_init = jnp.zeros((tq, LANES), F32)
        dq0, dq1 = lax.fori_loop(0, n_chunks // unroll, chunks, (dq_init, dq_init))
        dq_ref[...] = (jnp.where(low, dq0, dq1) * ATTN_SCALE).astype(BF16)

    q_spec = pl.BlockSpec((tq, LANES), lambda kvh, hb, qi: (qi, kvh * 2 + hb))
    kv_spec = pl.BlockSpec((s, LANES), lambda kvh, hb, qi: (0, kvh))
    grid = (2, 2, s // tq)
    body, extra = _carry_exchange(body, 6, 3, grid, ride)
    return pl.pallas_call(
        body, name=name, grid=grid,
        in_specs=[q_spec, kv_spec, kv_spec, q_spec, q_spec, q_spec] + extra.in_specs,
        out_specs=[q_spec, kv_spec, kv_spec] + extra.out_specs,
        out_shape=[jax.ShapeDtypeStruct((s, A_Q_WIDTH), BF16),
                   jax.ShapeDtypeStruct((s, 2 * LANES), F32), jax.ShapeDtypeStruct((s, 2 * LANES), F32)]
        + extra.out_shape,
        scratch_shapes=extra.scratch, compiler_params=_params(),
    )(qkv, kdup, vdup, o, lse, do, *extra.args)


BAND_Q = 128
GROUP_QKV = 3 * B_OUT_WIDTH
TOKEN_BLOCKS = GROUP_QKV // LANES
B_Q_BLOCK0 = 0
B_K_BLOCK0 = B_OUT_WIDTH // LANES
B_V_BLOCK0 = 2 * (B_OUT_WIDTH // LANES)


BAND_UNROLL = 4


def _unrolled_loop(n, step):
    unroll = BAND_UNROLL if n % BAND_UNROLL == 0 else 1

    def body(it, carry):
        for u in range(unroll):
            step(it * unroll + u)
        return carry

    lax.fori_loop(0, n // unroll, body, 0)


def _band_geometry(length):
    seg = min(length, 2048)
    win = min(2 * BAND_Q, length)
    return seg, win


def _band_window(qs, length, win):
    st = jnp.clip(qs - B_HALF_SPAN, 0, length - win)
    st = pl.multiple_of(st, B_HALF_SPAN)
    qpos = qs + lax.broadcasted_iota(jnp.int32, (BAND_Q, 1), 0)
    kpos = st + lax.broadcasted_iota(jnp.int32, (1, win), 1)
    return st, jnp.abs(qpos - kpos) <= B_HALF_SPAN


def _band_fwd(qkv_view, gi, dil, *, name):
    length = qkv_view.shape[0]
    seg, win = _band_geometry(length)
    n_sub = seg // BAND_Q

    def body(q_ref, k_ref, v_ref, o_ref, lse_ref):
        seg_i = pl.program_id(2)
        lane = _lane_iota()
        low = lane < HEAD_DIM

        def one(i):
            ql = pl.multiple_of(i * BAND_Q, BAND_Q)
            st, valid = _band_window(seg_i * seg + ql, length, win)
            q = q_ref[pl.ds(ql, BAND_Q), :]
            kw = k_ref[pl.ds(st, win), :]
            vw = v_ref[pl.ds(st, win), :]
            outs, lses = [], []
            for e in range(2):
                mine = (lane >= HEAD_DIM) if e else (lane < HEAD_DIM)
                qm = jnp.where(mine, q, jnp.zeros_like(q)) * ATTN_SCALE
                sc = lax.dot_general(qm, kw, NT, preferred_element_type=F32)
                sc = jnp.where(valid, sc, NEG_INF)
                m = jnp.max(sc, axis=1, keepdims=True)
                p = jnp.exp(sc - m)
                l = jnp.sum(p, axis=1, keepdims=True)
                outs.append(lax.dot_general(p.astype(BF16), vw, NN, preferred_element_type=F32) / l)
                lses.append(m + jnp.log(l))
            o_ref[pl.ds(ql, BAND_Q), :] = jnp.where(low, outs[0], outs[1]).astype(BF16)
            lse_ref[pl.ds(ql, BAND_Q), :] = jnp.where(low, lses[0], lses[1])

        _unrolled_loop(n_sub, one)

    def col(base):
        return lambda r, hp, sg: (0, r * TOKEN_BLOCKS + base + hp)

    q_spec = pl.BlockSpec((seg, LANES), lambda r, hp, sg: (sg, r * TOKEN_BLOCKS + B_Q_BLOCK0 + hp))
    out_spec = pl.BlockSpec((seg, LANES), lambda r, hp, sg: (sg, r * 2 + hp))
    return pl.pallas_call(
        body, name=name, grid=(dil, 2, length // seg),
        in_specs=[q_spec, pl.BlockSpec((length, LANES), col(B_K_BLOCK0)), pl.BlockSpec((length, LANES), col(B_V_BLOCK0))],
        out_specs=[out_spec, out_spec],
        out_shape=[jax.ShapeDtypeStruct((length, dil * B_OUT_WIDTH), BF16),
                   jax.ShapeDtypeStruct((length, dil * B_OUT_WIDTH), F32)],
        compiler_params=_params(),
    )(qkv_view, qkv_view, qkv_view)


def _band_bwd(qkv_view, do, lse, dd, gi, dil, *, name):
    length = qkv_view.shape[0]
    seg, win = _band_geometry(length)
    n_sub = seg // BAND_Q

    def body(q_ref, k_ref, v_ref, do_ref, lse_ref, dd_ref, dq_ref, dk_ref, dv_ref):
        seg_i = pl.program_id(2)
        lane = _lane_iota()

        @pl.when(seg_i == 0)
        def _():
            dk_ref[...] = jnp.zeros_like(dk_ref)
            dv_ref[...] = jnp.zeros_like(dv_ref)

        def one(i):
            ql = pl.multiple_of(i * BAND_Q, BAND_Q)
            st, valid = _band_window(seg_i * seg + ql, length, win)
            q = q_ref[pl.ds(ql, BAND_Q), :]
            dov = do_ref[pl.ds(ql, BAND_Q), :]
            lsev = lse_ref[pl.ds(ql, BAND_Q), :]
            ddv = dd_ref[pl.ds(ql, BAND_Q), :]
            kw = k_ref[pl.ds(st, win), :]
            vw = v_ref[pl.ds(st, win), :]
            dq = jnp.zeros((BAND_Q, LANES), F32)
            for e in range(2):
                mine = (lane >= HEAD_DIM) if e else (lane < HEAD_DIM)
                qs = jnp.where(mine, q, jnp.zeros_like(q)) * ATTN_SCALE
                dom = jnp.where(mine, dov, jnp.zeros_like(dov))
                lse_e = lsev[:, e * HEAD_DIM:e * HEAD_DIM + 1]
                dd_e = ddv[:, e * HEAD_DIM:e * HEAD_DIM + 1]
                sc = lax.dot_general(qs, kw, NT, preferred_element_type=F32)
                p = jnp.exp(jnp.where(valid, sc, NEG_INF) - lse_e)
                dp = lax.dot_general(dom, vw, NT, preferred_element_type=F32)
                ds = (p * (dp - dd_e)).astype(BF16)
                dv_ref[pl.ds(st, win), :] += lax.dot_general(p.astype(BF16), dom, TN, preferred_element_type=F32)
                dk_ref[pl.ds(st, win), :] += lax.dot_general(ds, qs, TN, preferred_element_type=F32)
                dq_e = lax.dot_general(ds, kw, NN, preferred_element_type=F32) * ATTN_SCALE
                dq = dq + jnp.where(mine, dq_e, 0.0)
            dq_ref[pl.ds(ql, BAND_Q), :] = dq.astype(BF16)

        _unrolled_loop(n_sub, one)

    def col(base):
        return lambda r, hp, sg: (0, r * TOKEN_BLOCKS + base + hp)

    q_spec = pl.BlockSpec((seg, LANES), lambda r, hp, sg: (sg, r * TOKEN_BLOCKS + B_Q_BLOCK0 + hp))
    seg_spec = pl.BlockSpec((seg, LANES), lambda r, hp, sg: (sg, r * 2 + hp))
    full_spec = pl.BlockSpec((length, LANES), lambda r, hp, sg: (0, r * 2 + hp))
    shp = jax.ShapeDtypeStruct((length, dil * B_OUT_WIDTH), F32)
    return pl.pallas_call(
        body, name=name, grid=(dil, 2, length // seg),
        in_specs=[q_spec, pl.BlockSpec((length, LANES), col(B_K_BLOCK0)), pl.BlockSpec((length, LANES), col(B_V_BLOCK0)),
                  seg_spec, seg_spec, seg_spec],
        out_specs=[seg_spec, full_spec, full_spec],
        out_shape=[jax.ShapeDtypeStruct(shp.shape, BF16), shp, shp],
        compiler_params=_params(),
    )(qkv_view, qkv_view, qkv_view, do, lse, dd)


def _merge_weights(lses):
    m = jnp.maximum(jnp.maximum(lses[0], lses[1]), lses[2])
    ex = [jnp.exp(v - m) for v in lses]
    tot = ex[0] + ex[1] + ex[2]
    return [v / tot for v in ex]


def _merge_fwd(os_, lses, *, name, tm=512):
    s = os_[0].shape[0]

    def body(o0, o1, o2, l0, l1, l2, ob_ref):
        w = _merge_weights([l0[...], l1[...], l2[...]])
        ob = w[0] * o0[...].astype(F32) + w[1] * o1[...].astype(F32) + w[2] * o2[...].astype(F32)
        ob_ref[...] = ob.astype(BF16)

    blk = pl.BlockSpec((tm, B_OUT_WIDTH), lambda i: (i, 0))
    return pl.pallas_call(
        body, name=name, grid=(s // tm,), in_specs=[blk] * 6, out_specs=blk,
        out_shape=jax.ShapeDtypeStruct((s, B_OUT_WIDTH), BF16),
        compiler_params=_params(),
    )(*os_, *lses)


def _merge_bwd(os_, lses, dob, *, name, tm=512):
    s = os_[0].shape[0]

    def body(o0, o1, o2, l0, l1, l2, dob_ref, d0, d1, d2, t0, t1, t2):
        w = _merge_weights([l0[...], l1[...], l2[...]])
        dv = dob_ref[...]
        ob = w[0] * o0[...].astype(F32) + w[1] * o1[...].astype(F32) + w[2] * o2[...].astype(F32)
        tot = _head_sum(dv * ob)
        for wg, d_ref, t_ref in zip(w, (d0, d1, d2), (t0, t1, t2)):
            d_ref[...] = (wg * dv).astype(BF16)
            t_ref[...] = wg * tot

    blk = pl.BlockSpec((tm, B_OUT_WIDTH), lambda i: (i, 0))
    return pl.pallas_call(
        body, name=name, grid=(s // tm,), in_specs=[blk] * 7, out_specs=[blk] * 6,
        out_shape=[jax.ShapeDtypeStruct((s, B_OUT_WIDTH), BF16)] * 3 + [jax.ShapeDtypeStruct((s, B_OUT_WIDTH), F32)] * 3,
        compiler_params=_params(),
    )(*os_, *lses, dob)


HALO = 8
POOL_BLOCK0 = QKV_WIDTH // LANES


def _window_sum(ext, lo, hi, tm):
    rows = ext.shape[0]
    acc = None
    for j in range(lo, hi + 1):
        r = ext if j == 0 else pltpu.roll(ext, (-j) % rows, axis=0)
        acc = r if acc is None else acc + r
    return acc[HALO:HALO + tm]


def _pool_counts(t, half, s):
    return (jnp.minimum(t + half, s) - jnp.maximum(t - half, 0)).astype(F32)


def _halo_specs(tm, s, col0):
    per = tm // HALO
    last = s // HALO - 1
    prev = pl.BlockSpec((HALO, LANES), lambda g, i: (jnp.maximum(i * per - 1, 0), col0 + g))
    cur = pl.BlockSpec((tm, LANES), lambda g, i: (i, col0 + g))
    nxt = pl.BlockSpec((HALO, LANES), lambda g, i: (jnp.minimum((i + 1) * per, last), col0 + g))
    return prev, cur, nxt


def _extended(prev_ref, cur_ref, next_ref, i, n_tiles):
    prev = jnp.where(i > 0, prev_ref[...], 0.0)
    nxt = jnp.where(i < n_tiles - 1, next_ref[...], 0.0)
    return jnp.concatenate([prev, cur_ref[...], nxt], axis=0)


def _pool_fwd(z, lin, scale, *, name, tm=512):
    s = z.shape[0]
    tm = min(tm, s)
    n_tiles = s // tm

    def body(prev_ref, cur_ref, next_ref, lin_ref, sc_ref, pooled_ref, mixed_ref):
        g = pl.program_id(0)
        i = pl.program_id(1)
        ext = _extended(prev_ref, cur_ref, next_ref, i, n_tiles)
        t = i * tm + lax.broadcasted_iota(jnp.int32, (tm, 1), 0)
        for gi, half in enumerate(POOL_HALF):
            @pl.when(g == gi)
            def _(half=half):
                mean = _window_sum(ext, -half, half - 1, tm) / _pool_counts(t, half, s)
                pooled = (mean - cur_ref[...]).astype(BF16)
                pooled_ref[...] = pooled
                mixed = lax.dot_general(pooled, lin_ref[...].astype(BF16), NN, preferred_element_type=F32)
                mixed_ref[...] = (mixed * sc_ref[...]).astype(BF16)

    prev, cur, nxt = _halo_specs(tm, s, POOL_BLOCK0)
    out = pl.BlockSpec((tm, LANES), lambda g, i: (i, g))
    return pl.pallas_call(
        body, name=name, grid=(len(POOL_HALF), n_tiles),
        in_specs=[prev, cur, nxt, pl.BlockSpec((None, LANES, LANES), lambda g, i: (g, 0, 0)),
                  pl.BlockSpec((1, LANES), lambda g, i: (0, g))],
        out_specs=[out, out],
        out_shape=[jax.ShapeDtypeStruct((s, POOL_WIDTH), BF16)] * 2,
        compiler_params=_params(),
    )(z, z, z, lin, scale)


def _pool_bwd(dmixed, pooled, lin, scale, *, name, tm=512):
    s = dmixed.shape[0]
    tm = min(tm, s)
    n_tiles = s // tm

    def body(prev_ref, cur_ref, next_ref, pooled_ref, lin_ref, sc_ref, du_ref, dlin_ref, dsc_ref):
        g = pl.program_id(0)
        i = pl.program_id(1)

        @pl.when(i == 0)
        def _():
            dlin_ref[...] = jnp.zeros_like(dlin_ref)
            dsc_ref[...] = jnp.zeros_like(dsc_ref)

        linb = lin_ref[...].astype(BF16)
        ext = _extended(prev_ref, cur_ref, next_ref, i, n_tiles)
        dpl_ext = (ext * sc_ref[...]).astype(BF16)
        dpl_cur = (cur_ref[...] * sc_ref[...]).astype(BF16)
        dpooled_ext = lax.dot_general(dpl_ext, linb, NT, preferred_element_type=F32)
        t_ext = i * tm - HALO + lax.broadcasted_iota(jnp.int32, (tm + 2 * HALO, 1), 0)
        pooled = pooled_ref[...]
        mixed = lax.dot_general(pooled, linb, NN, preferred_element_type=F32)
        dsc_ref[...] += jnp.sum(cur_ref[...] * mixed, axis=0, keepdims=True)
        dlin_ref[...] += lax.dot_general(pooled, dpl_cur, TN, preferred_element_type=F32)
        for gi, half in enumerate(POOL_HALF):
            @pl.when(g == gi)
            def _(half=half):
                share = dpooled_ext / jnp.maximum(_pool_counts(t_ext, half, s), 1.0)
                du = _window_sum(share, -(half - 1), half, tm) - dpooled_ext[HALO:HALO + tm]
                du_ref[...] = du.astype(BF16)

    prev, cur, nxt = _halo_specs(tm, s, 0)
    out = pl.BlockSpec((tm, LANES), lambda g, i: (i, g))
    lin_spec = pl.BlockSpec((None, LANES, LANES), lambda g, i: (g, 0, 0))
    vec = pl.BlockSpec((1, LANES), lambda g, i: (0, g))
    return pl.pallas_call(
        body, name=name, grid=(len(POOL_HALF), n_tiles),
        in_specs=[prev, cur, nxt, out, lin_spec, vec],
        out_specs=[out, lin_spec, vec],
        out_shape=[jax.ShapeDtypeStruct((s, POOL_WIDTH), BF16),
                   jax.ShapeDtypeStruct((len(POOL_HALF), LANES, LANES), F32),
                   jax.ShapeDtypeStruct((1, POOL_WIDTH), F32)],
        compiler_params=_params(),
    )(dmixed, dmixed, dmixed, pooled, lin, scale)


GATE_TILE = 512
GATE_BLOCK0 = (QKV_WIDTH + POOL_WIDTH) // GATE_TILE
GATE_BLOCKS_PER_BRANCH = D_MODEL // GATE_TILE


def _sigmoid(v):
    return 1.0 / (1.0 + jnp.exp(-v))


def _gate_specs(tm):
    def zspec(br):
        return pl.BlockSpec((tm, GATE_TILE), lambda jj, i: (i, GATE_BLOCK0 + GATE_BLOCKS_PER_BRANCH * br + jj))

    def bspec(br):
        return pl.BlockSpec((1, GATE_TILE), lambda jj, i: (0, GATE_BLOCKS_PER_BRANCH * br + jj))

    row = pl.BlockSpec((tm, GATE_TILE), lambda jj, i: (i, jj))
    vec = pl.BlockSpec((1, GATE_TILE), lambda jj, i: (0, jj))
    return [zspec(0), zspec(1), zspec(2)], [bspec(0), bspec(1), bspec(2)], row, vec


def _gate_fwd(z, b_gate, ya, yb, yc, *, name, tm=512):
    s = z.shape[0]

    def body(z0, z1, z2, b0, b1, b2, ya_ref, yb_ref, yc_ref, out_ref):
        acc = _sigmoid(z0[...] + b0[...]) * ya_ref[...]
        acc = acc + _sigmoid(z1[...] + b1[...]) * yb_ref[...]
        acc = acc + _sigmoid(z2[...] + b2[...]) * yc_ref[...]
        out_ref[...] = acc.astype(BF16)

    zs, bs, row, _ = _gate_specs(tm)
    return pl.pallas_call(
        body, name=name, grid=(GATE_BLOCKS_PER_BRANCH, s // tm),
        in_specs=zs + bs + [row] * 3, out_specs=row,
        out_shape=jax.ShapeDtypeStruct((s, D_MODEL), BF16),
        compiler_params=_params(),
    )(z, z, z, b_gate, b_gate, b_gate, ya, yb, yc)


def _gate_bwd(z, b_gate, ya, yb, yc, dmerged, *, name, tm=512):
    s = z.shape[0]

    def body(z0, z1, z2, b0, b1, b2, ya_ref, yb_ref, yc_ref, dm_ref,
             dya_ref, dyb_ref, dyc_ref, dg0, dg1, dg2, db0, db1, db2):
        i = pl.program_id(1)
        dm = dm_ref[...]
        for z_ref, b_ref, y_ref, dy_ref, dg_ref, db_ref in (
                (z0, b0, ya_ref, dya_ref, dg0, db0), (z1, b1, yb_ref, dyb_ref, dg1, db1),
                (z2, b2, yc_ref, dyc_ref, dg2, db2)):
            gate = _sigmoid(z_ref[...] + b_ref[...])
            dy_ref[...] = (gate * dm).astype(BF16)
            dpre = dm * y_ref[...] * gate * (1.0 - gate)
            dg_ref[...] = dpre.astype(BF16)
            part = jnp.sum(dpre, axis=0, keepdims=True)

            @pl.when(i == 0)
            def _(db_ref=db_ref, part=part):
                db_ref[...] = part

            @pl.when(i > 0)
            def _(db_ref=db_ref, part=part):
                db_ref[...] += part

    zs, bs, row, vec = _gate_specs(tm)
    big = jax.ShapeDtypeStruct((s, D_MODEL), BF16)
    small = jax.ShapeDtypeStruct((1, D_MODEL), F32)
    return pl.pallas_call(
        body, name=name, grid=(GATE_BLOCKS_PER_BRANCH, s // tm),
        in_specs=zs + bs + [row] * 4, out_specs=[row] * 6 + [vec] * 3,
        out_shape=[big] * 6 + [small] * 3,
        compiler_params=_params(),
    )(z, z, z, b_gate, b_gate, b_gate, ya, yb, yc, dmerged)


def _swiglu_fwd(a, b, *, name, tm=512, tn=1408):
    s, f = a.shape

    def body(a_ref, b_ref, o_ref):
        av = a_ref[...]
        o_ref[...] = (av * _sigmoid(av) * b_ref[...]).astype(BF16)

    blk = pl.BlockSpec((tm, tn), lambda i, j: (i, j))
    return pl.pallas_call(
        body, name=name, grid=(s // tm, f // tn), in_specs=[blk, blk], out_specs=blk,
        out_shape=jax.ShapeDtypeStruct((s, f), BF16), compiler_params=_params(),
    )(a, b)


def _swiglu_bwd(a, b, df, *, name, tm=512, tn=1408):
    s, f = a.shape

    def body(a_ref, b_ref, df_ref, da_ref, db_ref):
        av = a_ref[...]
        dfv = df_ref[...]
        sg = _sigmoid(av)
        silu = av * sg
        da_ref[...] = (dfv * b_ref[...] * (sg + silu * (1.0 - sg))).astype(BF16)
        db_ref[...] = (dfv * silu).astype(BF16)

    blk = pl.BlockSpec((tm, tn), lambda i, j: (i, j))
    out = jax.ShapeDtypeStruct((s, f), BF16)
    return pl.pallas_call(
        body, name=name, grid=(s // tm, f // tn), in_specs=[blk] * 3, out_specs=[blk] * 2,
        out_shape=[out, out], compiler_params=_params(),
    )(a, b, df)


def _loss_head(y, target, *, name, tm=512):
    s, d = y.shape

    def body(y_ref, t_ref, part_ref, dy_ref):
        i = pl.program_id(0)
        err = y_ref[...] - t_ref[...]
        dy_ref[...] = err * (1.0 / d)
        part = jnp.sum(err * err, axis=0, keepdims=True) * (0.5 / d)

        @pl.when(i == 0)
        def _():
            part_ref[...] = part

        @pl.when(i > 0)
        def _():
            part_ref[...] += part

    row = pl.BlockSpec((tm, d), lambda i: (i, 0))
    vec = pl.BlockSpec((1, d), lambda i: (0, 0))
    return pl.pallas_call(
        body, name=name, grid=(s // tm,), in_specs=[row, row], out_specs=[vec, row],
        out_shape=[jax.ShapeDtypeStruct((1, d), F32), jax.ShapeDtypeStruct((s, d), F32)],
        compiler_params=_params(),
    )(y, target)


def _mesh_place():
    x, y, c = lax.axis_index('x'), lax.axis_index('y'), lax.axis_index('c')
    return x, y, c, 4 * x + 2 * y + c


def _peer(x, y, c, k):
    return (x ^ ((k >> 2) & 1), y ^ ((k >> 1) & 1), c ^ (k & 1))


def _exchange(buf, *, gather, name):
    def body(in_ref, out_ref, send_sems, recv_sems, local_sem):
        start, wait = _exchange_plan(in_ref, out_ref, send_sems, recv_sems, local_sem, gather)
        start()
        wait()

    return pl.pallas_call(
        body, name=name,
        in_specs=[_EXCHANGE_SPEC], out_specs=_EXCHANGE_SPEC,
        out_shape=_exchange_shape(buf), scratch_shapes=_exchange_scratch(),
    )(buf)


_EXCHANGE_SPEC = pl.BlockSpec(memory_space=pl.ANY)


class _Extra:
    def __init__(self, ride):
        self.in_specs = [] if ride is None else [_EXCHANGE_SPEC]
        self.out_specs = [] if ride is None else [_EXCHANGE_SPEC]
        self.out_shape = [] if ride is None else [_exchange_shape(ride[0])]
        self.scratch = [] if ride is None else _exchange_scratch()
        self.args = () if ride is None else (ride[0],)


def _carry_exchange(body, n_in, n_out, grid, ride):
    extra = _Extra(ride)
    if ride is None:
        return body, extra
    gather = ride[1]

    def carrying(*refs):
        ins, ride_in = refs[:n_in], refs[n_in]
        outs, ride_out = refs[n_in + 1:n_in + 1 + n_out], refs[n_in + 1 + n_out]
        sems = refs[n_in + 2 + n_out:]
        ids = [pl.program_id(a) for a in range(len(grid))]
        first = functools.reduce(jnp.logical_and, [i == 0 for i in ids])
        last = functools.reduce(jnp.logical_and, [i == n - 1 for i, n in zip(ids, grid)])

        @pl.when(first)
        def _():
            _exchange_plan(ride_in, ride_out, *sems, gather)[0]()

        body(*ins, *outs)

        @pl.when(last)
        def _():
            _exchange_plan(ride_in, ride_out, *sems, gather)[1]()

    return carrying, extra


def _exchange_shape(buf):
    return jax.ShapeDtypeStruct((N_DEV,) + buf.shape[-2:], buf.dtype)


def _exchange_scratch():
    return [pltpu.SemaphoreType.DMA((N_DEV - 1,)), pltpu.SemaphoreType.DMA((N_DEV - 1,)), pltpu.SemaphoreType.DMA]


def _exchange_plan(in_ref, out_ref, send_sems, recv_sems, local_sem, gather):
    x, y, c, me = _mesh_place()

    def src(slot):
        return in_ref if gather else in_ref.at[slot]

    def copy(k, dst_slot):
        return pltpu.make_async_remote_copy(
            src_ref=src(me ^ k), dst_ref=out_ref.at[dst_slot],
            send_sem=send_sems.at[k - 1], recv_sem=recv_sems.at[k - 1],
            device_id=_peer(x, y, c, k), device_id_type=pl.DeviceIdType.MESH)

    def mine():
        return pltpu.make_async_copy(src(me), out_ref.at[me], local_sem)

    def start():
        mine().start()
        for k in range(1, N_DEV):
            copy(k, me).start()

    def wait():
        for k in range(1, N_DEV):
            copy(k, me ^ k).wait_recv()
        for k in range(1, N_DEV):
            copy(k, me).wait_send()
        mine().wait()

    return start, wait


def _adamw(parts, w, m, v, *, name, tr):
    n_layers = len(parts)
    rows, cols = parts[0].shape[1:]
    assert rows % tr == 0 and w.shape == (n_layers * rows, cols)
    tiles = rows // tr
    bias1 = 1.0 - ADAM_B1 ** ADAM_STEP
    bias2 = 1.0 - ADAM_B2 ** ADAM_STEP

    def body(*refs):
        p_refs = refs[:n_layers]
        w_ref, m_ref, v_ref, g_ref, d_ref, nm_ref, nv_ref = refs[n_layers:]

        def update(p_ref):
            g = p_ref[0].astype(F32)
            for j in range(1, N_DEV):
                g = g + p_ref[j].astype(F32)
            nm = ADAM_B1 * m_ref[...] + (1.0 - ADAM_B1) * g
            nv = ADAM_B2 * v_ref[...] + (1.0 - ADAM_B2) * (g * g)
            g_ref[...] = g
            nm_ref[...] = nm
            nv_ref[...] = nv
            d_ref[...] = -ADAM_LR * ((nm / bias1) / (jnp.sqrt(nv / bias2) + ADAM_EPS) + ADAM_WD * w_ref[...])

        for layer, p_ref in enumerate(p_refs):
            @pl.when(pl.program_id(0) == layer)
            def _(p_ref=p_ref):
                update(p_ref)

    def part_spec(layer):
        def index(l, i):
            return 0, jnp.where(l < layer, 0, jnp.where(l > layer, tiles - 1, i)), 0
        return pl.BlockSpec((N_DEV, tr, cols), index)

    blk = pl.BlockSpec((tr, cols), lambda l, i: (l * tiles + i, 0))
    out = jax.ShapeDtypeStruct((n_layers * rows, cols), F32)
    return pl.pallas_call(
        body, name=name, grid=(n_layers, tiles),
        in_specs=[part_spec(layer) for layer in range(n_layers)] + [blk, blk, blk],
        out_specs=[blk] * 4, out_shape=[out] * 4, compiler_params=_params(),
    )(*parts, w, m, v)


def _col_blocks(full):
    r, n = full.shape
    return full.reshape(r, N_DEV, n // N_DEV).transpose(1, 0, 2)


def _row_blocks(full):
    r, n = full.shape
    return full.reshape(N_DEV, r // N_DEV, n)


def _from_col_blocks(blocks):
    j, r, c = blocks.shape
    return blocks.transpose(1, 0, 2).reshape(r, j * c)


def _from_row_blocks(blocks):
    j, r, c = blocks.shape
    return blocks.reshape(j * r, c)


SHARD_ROWWISE = {'w_out', 'w_ffn_down'}


def _pack(shards):
    return jnp.concatenate([shards[n].reshape(-1, PACK_COLS) for n in SHARDED], axis=0)


def _unpack(packed, shapes):
    out, off = {}, 0
    lead = packed.shape[:-2]
    for n in SHARDED:
        size = shapes[n][0] * shapes[n][1] // PACK_COLS
        out[n] = packed[..., off:off + size, :].reshape(lead + tuple(shapes[n]))
        off += size
    return out


def _pack_small(vals):
    flat = jnp.concatenate([vals[n].reshape(-1) for n in SMALL])
    return flat.reshape(-1, PACK_COLS)


def _unpack_small(packed, shapes):
    flat = packed.reshape(-1)
    out, off = {}, 0
    for n in SMALL:
        size = 1
        for dim in shapes[n]:
            size *= dim
        out[n] = flat[off:off + size].reshape(shapes[n])
        off += size
    return out


def _band_views(qkv, s):
    views = []
    for gi, d in enumerate(B_DILATIONS):
        starts = [A_Q_WIDTH + 2 * A_KV_WIDTH + part * B_WIDTH + gi * B_OUT_WIDTH for part in range(3)]
        group = jnp.concatenate([qkv[:, c0:c0 + B_OUT_WIDTH] for c0 in starts], axis=1)
        views.append(group.reshape(s // d, d * GROUP_QKV))
    return views


def _layer_fwd(x, p, tables, ride=None):
    s = x.shape[0]
    cos, sin = tables
    h = _rms_fwd(x, p['norm_mix'], name='rms_mix_fwd')
    z = _mm(h, p['w_in'], dims='nn', out_dtype=F32, tm=1024, tn=1664, tk=1024, name='mm_in_fwd')
    gains = _qk_gains(p['qn_a'], p['kn_a'], p['qn_b'], p['kn_b'])
    qkv = _qkrope_fwd(z, gains, cos, sin, name='qkrope_fwd')

    kdup = _dup_halves(qkv[:, A_Q_WIDTH:A_Q_WIDTH + A_KV_WIDTH])
    va = qkv[:, A_Q_WIDTH + A_KV_WIDTH:A_Q_WIDTH + 2 * A_KV_WIDTH]
    vdup = _dup_halves(va)
    oa, lse_a, *exchanged = _attn_a_fwd(qkv, kdup, _with_ones(va), ride=ride, name='attn_a_fwd')
    ya = _mm(oa, p['w_branch_a'], dims='nn', out_dtype=F32, tm=1024, tn=1024, tk=512, name='mm_branch_a_fwd')

    views = _band_views(qkv, s)
    o_g, lse_g = [], []
    for gi, d in enumerate(B_DILATIONS):
        o, lse = _band_fwd(views[gi], gi, d, name=f'band_fwd_d{d}')
        o_g.append(o.reshape(s, B_OUT_WIDTH))
        lse_g.append(lse.reshape(s, B_OUT_WIDTH))
    ob = _merge_fwd(o_g, lse_g, name='merge_fwd')
    yb = _mm(ob, p['w_branch_b'], dims='nn', out_dtype=F32, tm=1024, tn=1024, tk=256, name='mm_branch_b_fwd')

    pooled, mixed = _pool_fwd(z, p['pool_lin'], p['pool_scale'], name='pool_fwd')
    yc = _mm(mixed, p['w_branch_c'], dims='nn', out_dtype=F32, tm=1024, tn=1024, tk=512, name='mm_branch_c_fwd')

    merged = _gate_fwd(z, p['b_gate'], ya, yb, yc, name='gate_fwd')
    x_mid = _mm(merged, p['w_out'], dims='nn', out_dtype=F32, tm=1024, tn=1024, tk=1024, res=x, name='mm_out_fwd')

    h2 = _rms_fwd(x_mid, p['norm_ffn'], name='rms_ffn_fwd')
    fa = _mm(h2, p['w_ffn_gate'], dims='nn', out_dtype=F32, tm=1024, tn=1408, tk=1024, name='mm_ffn_gate_fwd')
    fb = _mm(h2, p['w_ffn_up'], dims='nn', out_dtype=F32, tm=1024, tn=1408, tk=1024, name='mm_ffn_up_fwd')
    f = _swiglu_fwd(fa, fb, name='swiglu_fwd')
    x_out = _mm(f, p['w_ffn_down'], dims='nn', out_dtype=F32, tm=1024, tn=512, tk=2816, res=x_mid, name='mm_ffn_down_fwd')

    saved = dict(x=x, h=h, z=z, gains=gains, qkv=qkv, kdup=kdup, vdup=vdup, oa=oa, lse_a=lse_a, o_g=o_g, lse_g=lse_g,
                 ob=ob, pooled=pooled, mixed=mixed, ya=ya, yb=yb, yc=yc, merged=merged, x_mid=x_mid, h2=h2,
                 fa=fa, fb=fb, f=f, views=views)
    return x_out, saved, (exchanged[0] if exchanged else None)


def _fold_heads(v, heads):
    return v.reshape(heads, HEAD_DIM).sum(axis=0)


def _layer_bwd(dx, p, sv, tables, ride=None):
    s = dx.shape[0]
    cos, sin = tables
    g = {}

    df = _mm(dx, p['w_ffn_down'], dims='nt', out_dtype=F32, tm=1024, tn=1408, tk=1024, name='mm_ffn_down_dx')
    g['w_ffn_down'] = _mm(sv['f'], dx, dims='tn', out_dtype=BF16, tm=1408, tn=1024, tk=512, name='mm_ffn_down_dw')
    da, db = _swiglu_bwd(sv['fa'], sv['fb'], df, name='swiglu_bwd')
    dh2 = _mm(da, p['w_ffn_gate'], dims='nt', out_dtype=F32, tm=1024, tn=512, tk=2816, name='mm_ffn_gate_dx')
    dh2 = _mm(db, p['w_ffn_up'], dims='nt', out_dtype=F32, tm=1024, tn=512, tk=2816, res=dh2, name='mm_ffn_up_dx')
    g['w_ffn_gate'] = _mm(sv['h2'], da, dims='tn', out_dtype=BF16, tm=1024, tn=2816, tk=512, name='mm_ffn_gate_dw')
    g['w_ffn_up'] = _mm(sv['h2'], db, dims='tn', out_dtype=BF16, tm=1024, tn=2816, tk=512, name='mm_ffn_up_dw')
    dx_mid, g['norm_ffn'] = _rms_bwd(sv['x_mid'], p['norm_ffn'], dh2, dx, name='rms_ffn_bwd')

    dmerged = _mm(dx_mid, p['w_out'], dims='nt', out_dtype=F32, tm=1024, tn=1024, tk=1024, name='mm_out_dx')
    g['w_out'] = _mm(sv['merged'], dx_mid, dims='tn', out_dtype=BF16, tm=1024, tn=1024, tk=512, name='mm_out_dw')
    dya, dyb, dyc, dg0, dg1, dg2, db0, db1, db2 = _gate_bwd(
        sv['z'], p['b_gate'], sv['ya'], sv['yb'], sv['yc'], dmerged, name='gate_bwd')
    g['b_gate'] = jnp.concatenate([db0, db1, db2], axis=1)

    doa = _mm(dya, p['w_branch_a'], dims='nt', out_dtype=BF16, tm=1024, tn=512, tk=1024, name='mm_branch_a_dx')
    g['w_branch_a'] = _mm(sv['oa'], dya, dims='tn', out_dtype=BF16, tm=512, tn=1024, tk=512, name='mm_branch_a_dw')
    dqa, dkdup, dvdup, *exchanged = _attn_a_bwd(sv['qkv'], sv['kdup'], sv['vdup'], sv['oa'], sv['lse_a'], doa,
                                                ride=ride, name='attn_a_bwd')

    def fold(dup):
        return jnp.concatenate([dup[:, 0:64] + dup[:, 64:128], dup[:, 128:192] + dup[:, 192:256]], axis=1)

    dka, dva = fold(dkdup), fold(dvdup)

    dob = _mm(dyb, p['w_branch_b'], dims='nt', out_dtype=F32, tm=1024, tn=256, tk=1024, name='mm_branch_b_dx')
    g['w_branch_b'] = _mm(sv['ob'], dyb, dims='tn', out_dtype=BF16, tm=256, tn=1024, tk=512, name='mm_branch_b_dw')
    merged_b = _merge_bwd(sv['o_g'], sv['lse_g'], dob, name='merge_bwd')
    do_g, dd_g = merged_b[:3], merged_b[3:]
    views = sv['views']
    dq_parts, dk_parts, dv_parts = [], [], []
    for gi, d in enumerate(B_DILATIONS):
        ln = s // d
        dq, dk, dv = _band_bwd(views[gi], do_g[gi].reshape(ln, d * B_OUT_WIDTH),
                               sv['lse_g'][gi].reshape(ln, d * B_OUT_WIDTH), dd_g[gi].reshape(ln, d * B_OUT_WIDTH),
                               gi, d, name=f'band_bwd_d{d}')
        dq_parts.append(dq.reshape(s, B_OUT_WIDTH))
        dk_parts.append(dk.reshape(s, B_OUT_WIDTH))
        dv_parts.append(dv.reshape(s, B_OUT_WIDTH))

    dmixed = _mm(dyc, p['w_branch_c'], dims='nt', out_dtype=F32, tm=1024, tn=512, tk=1024, name='mm_branch_c_dx')
    g['w_branch_c'] = _mm(sv['mixed'], dyc, dims='tn', out_dtype=BF16, tm=512, tn=1024, tk=512, name='mm_branch_c_dw')
    du, g['pool_lin'], g['pool_scale'] = _pool_bwd(dmixed, sv['pooled'], p['pool_lin'], p['pool_scale'], name='pool_bwd')

    dqkv = jnp.concatenate([part.astype(BF16) for part in [dqa, dka, dva] + dq_parts + dk_parts + dv_parts], axis=1)
    dz_qkv, dgains = _qkrope_bwd(sv['z'], sv['gains'], cos, sin, dqkv, name='qkrope_bwd')
    dgains = jnp.sum(dgains, axis=0)
    g['qn_a'] = _fold_heads(dgains[0:512], 8)
    g['kn_a'] = _fold_heads(dgains[512:640], 2)
    g['qn_b'] = _fold_heads(dgains[768:1536], 12)
    g['kn_b'] = _fold_heads(dgains[1536:2304], 12)

    dz = jnp.concatenate([dz_qkv, du, dg0, dg1, dg2], axis=1)
    dh = _mm(dz, p['w_in'], dims='nt', out_dtype=F32, tm=1024, tn=1024, tk=1664, name='mm_in_dx')
    g['w_in'] = _mm(sv['h'], dz, dims='tn', out_dtype=BF16, tm=1024, tn=3328, tk=512, name='mm_in_dw')
    dx_in, g['norm_mix'] = _rms_bwd(sv['x'], p['norm_mix'], dh, dx_mid, name='rms_mix_bwd')
    return dx_in, g, (exchanged[0] if exchanged else None)


def _small_views(vals, l):
    return {
        'norm_mix': vals['norm_mix'][l][None, :], 'b_gate': vals['b_gate'][l][None, :],
        'qn_a': vals['qn_a'][l], 'kn_a': vals['kn_a'][l], 'qn_b': vals['qn_b'][l], 'kn_b': vals['kn_b'][l],
        'pool_lin': vals['pool_lin'][l], 'pool_scale': vals['pool_scale'][l][None, :],
        'norm_ffn': vals['norm_ffn'][l][None, :],
    }


def kernel(x, norm_mix, w_in, b_gate, qn_a, kn_a, qn_b, kn_b, pool_lin, pool_scale, w_branch_a, w_branch_b, w_branch_c, w_out, norm_ffn, w_ffn_gate, w_ffn_up, w_ffn_down, loss_target, m_norm_mix, m_w_in, m_b_gate, m_qn_a, m_kn_a, m_qn_b, m_kn_b, m_pool_lin, m_pool_scale, m_w_branch_a, m_w_branch_b, m_w_branch_c, m_w_out, m_norm_ffn, m_w_ffn_gate, m_w_ffn_up, m_w_ffn_down, v_norm_mix, v_w_in, v_b_gate, v_qn_a, v_kn_a, v_qn_b, v_kn_b, v_pool_lin, v_pool_scale, v_w_branch_a, v_w_branch_b, v_w_branch_c, v_w_out, v_norm_ffn, v_w_ffn_gate, v_w_ffn_up, v_w_ffn_down):
    w = dict(norm_mix=norm_mix, w_in=w_in, b_gate=b_gate, qn_a=qn_a, kn_a=kn_a, qn_b=qn_b, kn_b=kn_b,
             pool_lin=pool_lin, pool_scale=pool_scale, w_branch_a=w_branch_a, w_branch_b=w_branch_b,
             w_branch_c=w_branch_c, w_out=w_out, norm_ffn=norm_ffn, w_ffn_gate=w_ffn_gate, w_ffn_up=w_ffn_up,
             w_ffn_down=w_ffn_down)
    m = dict(norm_mix=m_norm_mix, w_in=m_w_in, b_gate=m_b_gate, qn_a=m_qn_a, kn_a=m_kn_a, qn_b=m_qn_b, kn_b=m_kn_b,
             pool_lin=m_pool_lin, pool_scale=m_pool_scale, w_branch_a=m_w_branch_a, w_branch_b=m_w_branch_b,
             w_branch_c=m_w_branch_c, w_out=m_w_out, norm_ffn=m_norm_ffn, w_ffn_gate=m_w_ffn_gate,
             w_ffn_up=m_w_ffn_up, w_ffn_down=m_w_ffn_down)
    v = dict(norm_mix=v_norm_mix, w_in=v_w_in, b_gate=v_b_gate, qn_a=v_qn_a, kn_a=v_kn_a, qn_b=v_qn_b, kn_b=v_kn_b,
             pool_lin=v_pool_lin, pool_scale=v_pool_scale, w_branch_a=v_w_branch_a, w_branch_b=v_w_branch_b,
             w_branch_c=v_w_branch_c, w_out=v_w_out, norm_ffn=v_norm_ffn, w_ffn_gate=v_w_ffn_gate,
             w_ffn_up=v_w_ffn_up, w_ffn_down=v_w_ffn_down)
    depth = w_in.shape[0]
    shard_shapes = {n: w[n].shape[1:] for n in SHARDED}
    small_shapes = {n: w[n].shape for n in SMALL}

    def my_shards(l):
        return _pack({n: w[n][l].astype(BF16) for n in SHARDED})

    def full_layer(gathered, l):
        blocks = _unpack(gathered, shard_shapes)
        full = {n: (_from_row_blocks(blocks[n]) if n in SHARD_ROWWISE else _from_col_blocks(blocks[n]))
                for n in SHARDED}
        full.update(_small_views(w, l))
        return full

    def blocks_to_send(grad):
        blocks = {n: (_row_blocks(grad[n]) if n in SHARD_ROWWISE else _col_blocks(grad[n])) for n in SHARDED}
        return jnp.concatenate([blocks[n].reshape(N_DEV, -1, PACK_COLS) for n in SHARDED], axis=1)

    tables = _rope_tables(x.shape[1])
    layers = [full_layer(_exchange(my_shards(0), gather=True, name='gather_weights'), 0)]
    saved = []
    act = x[0]
    for l in range(depth):
        ride = (my_shards(l + 1), True) if l + 1 < depth else None
        act, sv, gathered = _layer_fwd(act, layers[l], tables, ride=ride)
        saved.append(sv)
        if ride is not None:
            layers.append(full_layer(gathered, l + 1))
    part, dx = _loss_head(act, loss_target[0], name='loss_head')
    loss = lax.psum(jnp.sum(part), ('x', 'y', 'c'))

    parts = [None] * depth
    grads = [None] * depth
    pending = None
    for l in reversed(range(depth)):
        ride = (pending, False) if pending is not None else None
        dx, grads[l], arrived = _layer_bwd(dx, layers[l], saved[l], tables, ride=ride)
        if ride is not None:
            parts[l + 1] = arrived
        pending = blocks_to_send(grads[l])
    parts[0] = _exchange(pending, gather=False, name='scatter_grads')
    grad_x = dx

    def pack_layers(t):
        return jnp.concatenate([_pack({n: t[n][l] for n in SHARDED}) for l in range(depth)], axis=0)

    res = _adamw(parts, pack_layers(w), pack_layers(m), pack_layers(v), name='adamw_sharded', tr=128)
    unpacked = [_unpack(r.reshape(depth, -1, PACK_COLS), shard_shapes) for r in res]
    new = {n: tuple(u[n] for u in unpacked) for n in SHARDED}

    small_grad = {n: jnp.stack([grads[l][n].reshape(small_shapes[n][1:]) for l in range(depth)]) for n in SMALL}
    small_parts = _exchange(_pack_small(small_grad), gather=True, name='gather_small_grads')
    res = _adamw([small_parts], _pack_small({n: w[n] for n in SMALL}), _pack_small({n: m[n] for n in SMALL}),
                 _pack_small({n: v[n] for n in SMALL}), name='adamw_small', tr=small_parts.shape[1])
    unpacked = [_unpack_small(r, small_shapes) for r in res]
    for n in SMALL:
        new[n] = tuple(u[n] for u in unpacked)

    outs = [loss, grad_x[None]]
    for idx in range(4):
        outs.extend(new[n][idx] for n in WEIGHTS)
    return tuple(outs)
```

```python
import functools

import jax
import jax.numpy as jnp
from jax import lax
from jax.experimental import pallas as pl
from jax.experimental.pallas import tpu as pltpu

F32 = jnp.float32
BF16 = jnp.bfloat16

N_DEV = 8
D_MODEL = 1024
DEPTH = 4
HEAD_DIM = 64
LANES = 128
A_Q_WIDTH = 512
A_KV_WIDTH = 128
B_WIDTH = 768
B_GROUPS = 3
B_DILATIONS = (1, 4, 16)
B_HALF_SPAN = 64
B_OUT_WIDTH = 256
POOL_WIDTH = 512
POOL_HALF = (1, 2, 4, 8)
GATE_WIDTH = 3072
QKV_WIDTH = A_Q_WIDTH + 2 * A_KV_WIDTH + 3 * B_WIDTH
IN_WIDTH = QKV_WIDTH + POOL_WIDTH + GATE_WIDTH
D_FF = 2816
GRID_W = 64
ROPE_THETA = 10000.0
EPS = 1e-6
NEG_INF = -1e30
ATTN_SCALE = HEAD_DIM ** -0.5

ADAM_LR = 0.001
ADAM_B1 = 0.9
ADAM_B2 = 0.999
ADAM_EPS = 1e-08
ADAM_WD = 0.01
ADAM_STEP = 10

PACK_COLS = 1024
VMEM_LIMIT = 56 * 1024 * 1024

SHARDED = ('w_in', 'w_branch_a', 'w_branch_b', 'w_branch_c', 'w_out', 'w_ffn_gate', 'w_ffn_up', 'w_ffn_down')
SMALL = ('norm_mix', 'b_gate', 'qn_a', 'kn_a', 'qn_b', 'kn_b', 'pool_lin', 'pool_scale', 'norm_ffn')
WEIGHTS = ('norm_mix', 'w_in', 'b_gate', 'qn_a', 'kn_a', 'qn_b', 'kn_b', 'pool_lin', 'pool_scale',
           'w_branch_a', 'w_branch_b', 'w_branch_c', 'w_out', 'norm_ffn', 'w_ffn_gate', 'w_ffn_up', 'w_ffn_down')

NN = (((1,), (0,)), ((), ()))
NT = (((1,), (1,)), ((), ()))
TN = (((0,), (0,)), ((), ()))


def _params(vmem=None):
    return pltpu.CompilerParams(vmem_limit_bytes=VMEM_LIMIT if vmem is None else vmem)


def _lane_iota(n=LANES):
    return lax.broadcasted_iota(jnp.int32, (1, n), 1)


def _swap(x, sh, lane):
    n = x.shape[-1]
    down = pltpu.roll(x, sh, axis=1)
    up = pltpu.roll(x, n - sh, axis=1)
    return jnp.where((lane & sh) == 0, up, down)


def _head_sum(v):
    w = v.shape[-1]
    r = lax.broadcasted_iota(jnp.int32, (w, w), 0) // HEAD_DIM
    c = lax.broadcasted_iota(jnp.int32, (w, w), 1) // HEAD_DIM
    ones = (r == c).astype(BF16)
    hi = v.astype(BF16)
    lo = (v - hi.astype(F32)).astype(BF16)
    return (lax.dot_general(hi, ones, NN, preferred_element_type=F32)
            + lax.dot_general(lo, ones, NN, preferred_element_type=F32))


def _mm(a, b, *, dims, out_dtype, tm, tn, tk, name, res=None):
    if dims == 'nn':
        (m, k), n = a.shape, b.shape[1]
    elif dims == 'nt':
        (m, k), n = a.shape, b.shape[0]
    else:
        (k, m), n = a.shape, b.shape[1]
    tm, tn, tk = min(tm, m), min(tn, n), min(tk, k)
    assert m % tm == 0 and n % tn == 0 and k % tk == 0, (name, m, n, k, tm, tn, tk)
    nk = k // tk
    if dims == 'tn':
        a_spec = pl.BlockSpec((tk, tm), lambda i, j, kk: (kk, i))
    else:
        a_spec = pl.BlockSpec((tm, tk), lambda i, j, kk: (i, kk))
    if dims == 'nt':
        b_spec = pl.BlockSpec((tn, tk), lambda i, j, kk: (j, kk))
    else:
        b_spec = pl.BlockSpec((tk, tn), lambda i, j, kk: (kk, j))
    o_spec = pl.BlockSpec((tm, tn), lambda i, j, kk: (i, j))
    dn = {'nn': NN, 'nt': NT, 'tn': TN}[dims]
    has_res = res is not None

    def body(*refs):
        if has_res:
            a_ref, b_ref, r_ref, o_ref, acc_ref = refs
        else:
            a_ref, b_ref, o_ref, acc_ref = refs
        prod = lax.dot_general(a_ref[...].astype(BF16), b_ref[...].astype(BF16), dn,
                               preferred_element_type=F32)

        def finish(total):
            if has_res:
                total = total + r_ref[...]
            o_ref[...] = total.astype(out_dtype)

        if nk == 1:
            finish(prod)
        else:
            kk = pl.program_id(2)

            @pl.when(kk == 0)
            def _():
                acc_ref[...] = prod

            @pl.when(kk > 0)
            def _():
                acc_ref[...] += prod

            @pl.when(kk == nk - 1)
            def _():
                finish(acc_ref[...])

    in_specs = [a_spec, b_spec] + ([o_spec] if has_res else [])
    args = (a, b) + ((res,) if has_res else ())
    acc_shape = (tm, tn) if nk > 1 else (8, LANES)
    return pl.pallas_call(
        body, name=name, grid=(m // tm, n // tn, nk),
        in_specs=in_specs, out_specs=o_spec,
        out_shape=jax.ShapeDtypeStruct((m, n), out_dtype),
        scratch_shapes=[pltpu.VMEM(acc_shape, F32)],
        compiler_params=_params(),
    )(*args)


def _rms_fwd(x, g, *, name, tm=512):
    s, d = x.shape

    def body(x_ref, g_ref, h_ref):
        xv = x_ref[...]
        rstd = lax.rsqrt(jnp.mean(xv * xv, axis=-1, keepdims=True) + EPS)
        h_ref[...] = (xv * rstd * g_ref[...]).astype(BF16)

    return pl.pallas_call(
        body, name=name, grid=(s // tm,),
        in_specs=[pl.BlockSpec((tm, d), lambda i: (i, 0)), pl.BlockSpec((1, d), lambda i: (0, 0))],
        out_specs=pl.BlockSpec((tm, d), lambda i: (i, 0)),
        out_shape=jax.ShapeDtypeStruct((s, d), BF16),
        compiler_params=_params(),
    )(x, g)


def _rms_bwd(x, g, dh, dres, *, name, tm=512):
    s, d = x.shape

    def body(x_ref, g_ref, dh_ref, dres_ref, dx_ref, dg_ref):
        i = pl.program_id(0)
        xv = x_ref[...]
        rstd = lax.rsqrt(jnp.mean(xv * xv, axis=-1, keepdims=True) + EPS)
        xhat = xv * rstd
        dhv = dh_ref[...]
        dxhat = dhv * g_ref[...]
        proj = jnp.mean(dxhat * xhat, axis=-1, keepdims=True)
        dx_ref[...] = dres_ref[...] + rstd * (dxhat - xhat * proj)
        part = jnp.sum(dhv * xhat, axis=0, keepdims=True)

        @pl.when(i == 0)
        def _():
            dg_ref[...] = part

        @pl.when(i > 0)
        def _():
            dg_ref[...] += part

    row = pl.BlockSpec((tm, d), lambda i: (i, 0))
    vec = pl.BlockSpec((1, d), lambda i: (0, 0))
    return pl.pallas_call(
        body, name=name, grid=(s // tm,),
        in_specs=[row, vec, row, row], out_specs=[row, vec],
        out_shape=[jax.ShapeDtypeStruct((s, d), F32), jax.ShapeDtypeStruct((1, d), F32)],
        compiler_params=_params(),
    )(x, g, dh, dres)


N_QKV_BLOCKS = QKV_WIDTH // LANES
A_BLOCKS = (A_Q_WIDTH + 2 * A_KV_WIDTH) // LANES
V_A_BLOCK = A_BLOCKS - 1
V_B_FIRST = A_BLOCKS + 2 * (B_WIDTH // LANES)


def _qk_kind(j):
    return jnp.where(j < A_BLOCKS, 0, 1)


def _is_v_block(j):
    return (j == V_A_BLOCK) | (j >= V_B_FIRST)


def _rope_tables(s):
    def ang(pos, dim):
        inv = ROPE_THETA ** (-jnp.arange(0, dim, 2, dtype=F32) / dim)
        return pos.astype(F32)[:, None] * inv[None, :]
    t = jnp.arange(s)
    a_row = ang(t // GRID_W, HEAD_DIM // 2)
    a_col = ang(t % GRID_W, HEAD_DIM // 2)
    a_seq = ang(t, HEAD_DIM)
    cos_a = jnp.concatenate([jnp.cos(a_row)] * 2 + [jnp.cos(a_col)] * 2, axis=-1)
    sin_a = jnp.concatenate([-jnp.sin(a_row), jnp.sin(a_row), -jnp.sin(a_col), jnp.sin(a_col)], axis=-1)
    cos_b = jnp.concatenate([jnp.cos(a_seq)] * 2, axis=-1)
    sin_b = jnp.concatenate([-jnp.sin(a_seq), jnp.sin(a_seq)], axis=-1)
    cos = jnp.stack([jnp.tile(cos_a, (1, 2)), jnp.tile(cos_b, (1, 2))])
    sin = jnp.stack([jnp.tile(sin_a, (1, 2)), jnp.tile(sin_b, (1, 2))])
    return cos, sin


def _qk_gains(qn_a, kn_a, qn_b, kn_b):
    one = jnp.ones((HEAD_DIM,), F32)
    parts = [jnp.tile(qn_a, 8), jnp.tile(kn_a, 2), jnp.tile(one, 2),
             jnp.tile(qn_b, 12), jnp.tile(kn_b, 12), jnp.tile(one, 12)]
    return jnp.concatenate(parts)[None, :]


def _qkrope_fwd(z, gains, cos, sin, *, name, tm=2048):
    s = z.shape[0]
    tm = min(tm, s)

    def body(z_ref, g_ref, c_ref, s_ref, o_ref):
        j = pl.program_id(1)
        lane = _lane_iota()
        xv = z_ref[...]

        def normed_rope(pair):
            ms = _head_sum(xv * xv) * (1.0 / HEAD_DIM)
            n = xv * lax.rsqrt(ms + EPS) * g_ref[...]
            return n * c_ref[...] + _swap(n, pair, lane) * s_ref[...]

        @pl.when(_is_v_block(j))
        def _():
            o_ref[...] = xv.astype(BF16)

        @pl.when(jnp.logical_not(_is_v_block(j)) & (j < A_BLOCKS))
        def _():
            o_ref[...] = normed_rope(HEAD_DIM // 4).astype(BF16)

        @pl.when(jnp.logical_not(_is_v_block(j)) & (j >= A_BLOCKS))
        def _():
            o_ref[...] = normed_rope(HEAD_DIM // 2).astype(BF16)

    tab = pl.BlockSpec((None, tm, LANES), lambda i, j: (_qk_kind(j), i, 0))
    blk = pl.BlockSpec((tm, LANES), lambda i, j: (i, j))
    return pl.pallas_call(
        body, name=name, grid=(s // tm, N_QKV_BLOCKS),
        in_specs=[blk, pl.BlockSpec((1, LANES), lambda i, j: (0, j)), tab, tab],
        out_specs=blk,
        out_shape=jax.ShapeDtypeStruct((s, QKV_WIDTH), BF16),
        compiler_params=_params(),
    )(z, gains, cos, sin)


SUBLANES = 8


def _qkrope_bwd(z, gains, cos, sin, dqkv, *, name, tm=2048):
    s = z.shape[0]
    tm = min(tm, s)

    def body(z_ref, g_ref, c_ref, s_ref, dy_ref, dz_ref, dg_ref):
        j = pl.program_id(1)
        lane = _lane_iota()
        xv = z_ref[...]
        dy = dy_ref[...].astype(F32)

        def back(pair):
            ms = _head_sum(xv * xv) * (1.0 / HEAD_DIM)
            rstd = lax.rsqrt(ms + EPS)
            xhat = xv * rstd
            dn = dy * c_ref[...] + _swap(dy * s_ref[...], pair, lane)
            dg_ref[...] = jnp.sum((dn * xhat).reshape(tm // SUBLANES, SUBLANES, LANES), axis=0)
            dxhat = dn * g_ref[...]
            proj = _head_sum(dxhat * xhat) * (1.0 / HEAD_DIM)
            dz_ref[...] = (rstd * (dxhat - xhat * proj)).astype(BF16)

        @pl.when(_is_v_block(j))
        def _():
            dz_ref[...] = dy.astype(BF16)
            dg_ref[...] = jnp.zeros_like(dg_ref)

        @pl.when(jnp.logical_not(_is_v_block(j)) & (j < A_BLOCKS))
        def _():
            back(HEAD_DIM // 4)

        @pl.when(jnp.logical_not(_is_v_block(j)) & (j >= A_BLOCKS))
        def _():
            back(HEAD_DIM // 2)

    tab = pl.BlockSpec((None, tm, LANES), lambda i, j: (_qk_kind(j), i, 0))
    blk = pl.BlockSpec((tm, LANES), lambda i, j: (i, j))
    return pl.pallas_call(
        body, name=name, grid=(s // tm, N_QKV_BLOCKS),
        in_specs=[blk, pl.BlockSpec((1, LANES), lambda i, j: (0, j)), tab, tab, blk],
        out_specs=[blk, pl.BlockSpec((SUBLANES, LANES), lambda i, j: (i, j))],
        out_shape=[jax.ShapeDtypeStruct((s, QKV_WIDTH), BF16),
                   jax.ShapeDtypeStruct((s // tm * SUBLANES, QKV_WIDTH), F32)],
        compiler_params=_params(),
    )(z, gains, cos, sin, dqkv)


def _dup_halves(kv):
    h0, h1 = kv[:, :HEAD_DIM], kv[:, HEAD_DIM:]
    return jnp.concatenate([h0, h0, h1, h1], axis=1)


def _with_ones(kv):
    h0, h1 = kv[:, :HEAD_DIM], kv[:, HEAD_DIM:]
    one = jnp.ones_like(h0)
    return jnp.concatenate([h0, one, one, h0, h1, one, one, h1], axis=1)


def _attn_a_fwd(qkv, kdup, vones, *, name, ride=None, tq=256, tk=512, unroll=8):
    s = qkv.shape[0]
    tq, tk = min(tq, s), min(tk, s)
    n_chunks = s // tk
    unroll = min(unroll, n_chunks)
    assert n_chunks % unroll == 0

    def body(q_ref, k_ref, v_ref, o_ref, lse_ref):
        lane = _lane_iota()
        low = lane < HEAD_DIM
        q = q_ref[...]
        zero = jnp.zeros_like(q)
        qm = [jnp.where(low, q, zero) * ATTN_SCALE, jnp.where(low, zero, q) * ATTN_SCALE]

        def chunks(c, carry):
            state = list(carry)
            scs = []
            for u in range(unroll):
                off = pl.multiple_of((c * unroll + u) * tk, tk)
                kc = k_ref[pl.ds(off, tk), :]
                scs.append([lax.dot_general(qm[e], kc, NT, preferred_element_type=F32) for e in range(2)])
            for u in range(unroll):
                off = pl.multiple_of((c * unroll + u) * tk, tk)
                for e in range(2):
                    m, acc = state[2 * e], state[2 * e + 1]
                    ve = v_ref[pl.ds(off, tk), e * LANES:(e + 1) * LANES]
                    m_new = jnp.maximum(m, jnp.max(scs[u][e], axis=1, keepdims=True))
                    alpha = jnp.exp(m - m_new)
                    p = jnp.exp(scs[u][e] - m_new).astype(BF16)
                    state[2 * e] = m_new
                    state[2 * e + 1] = alpha * acc + lax.dot_general(p, ve, NN, preferred_element_type=F32)
            return tuple(state)

        m_init = jnp.full((tq, 1), NEG_INF, F32)
        a_init = jnp.zeros((tq, LANES), F32)
        m0, a0, m1, a1 = lax.fori_loop(0, n_chunks // unroll, chunks, (m_init, a_init, m_init, a_init))
        l0 = pltpu.roll(a0, HEAD_DIM, axis=1)
        l1 = pltpu.roll(a1, HEAD_DIM, axis=1)
        o_ref[...] = jnp.where(low, a0 / l0, a1 / l1).astype(BF16)
        lse_ref[...] = jnp.where(low, m0 + jnp.log(l0), m1 + jnp.log(l1))

    q_spec = pl.BlockSpec((tq, LANES), lambda hb, qi: (qi, hb))
    k_spec = pl.BlockSpec((s, LANES), lambda hb, qi: (0, hb // 2))
    v_spec = pl.BlockSpec((s, 2 * LANES), lambda hb, qi: (0, hb // 2))
    grid = (A_Q_WIDTH // LANES, s // tq)
    body, extra = _carry_exchange(body, 3, 2, grid, ride)
    return pl.pallas_call(
        body, name=name, grid=grid,
        in_specs=[q_spec, k_spec, v_spec] + extra.in_specs, out_specs=[q_spec, q_spec] + extra.out_specs,
        out_shape=[jax.ShapeDtypeStruct((s, A_Q_WIDTH), BF16), jax.ShapeDtypeStruct((s, A_Q_WIDTH), F32)]
        + extra.out_shape,
        scratch_shapes=extra.scratch, compiler_params=_params(),
    )(qkv, kdup, vones, *extra.args)


def _attn_a_bwd(qkv, kdup, vdup, o, lse, do, *, name, ride=None, tq=512, tk=512, unroll=2):
    s = qkv.shape[0]
    tq, tk = min(tq, s), min(tk, s)
    n_chunks = s // tk
    unroll = min(unroll, n_chunks)
    assert n_chunks % unroll == 0

    def body(q_ref, k_ref, v_ref, o_ref, lse_ref, do_ref, dq_ref, dk_ref, dv_ref):
        first = (pl.program_id(1) == 0) & (pl.program_id(2) == 0)

        @pl.when(first)
        def _():
            dk_ref[...] = jnp.zeros_like(dk_ref)
            dv_ref[...] = jnp.zeros_like(dv_ref)

        lane = _lane_iota()
        low = lane < HEAD_DIM
        q = q_ref[...]
        dov = do_ref[...]
        zero = jnp.zeros_like(q)
        prod = dov.astype(F32) * o_ref[...].astype(F32)
        lsev = lse_ref[...]
        qs = [jnp.where(low, q, zero) * ATTN_SCALE, jnp.where(low, zero, q) * ATTN_SCALE]
        dom = [jnp.where(low, dov, zero), jnp.where(low, zero, dov)]
        delta = [jnp.sum(jnp.where(low, prod, 0.0), axis=1, keepdims=True),
                 jnp.sum(jnp.where(low, 0.0, prod), axis=1, keepdims=True)]
        lse = [lsev[:, 0:1], lsev[:, HEAD_DIM:HEAD_DIM + 1]]
        qs_both = jnp.concatenate(qs, axis=0)
        dom_both = jnp.concatenate(dom, axis=0)

        def chunks(c, carry):
            dqs = list(carry)
            for u in range(unroll):
                off = pl.multiple_of((c * unroll + u) * tk, tk)
                kc = k_ref[pl.ds(off, tk), :]
                vc = v_ref[pl.ds(off, tk), :]
                ps, dss = [], []
                for e in range(2):
                    sc = lax.dot_general(qs[e], kc, NT, preferred_element_type=F32)
                    p = jnp.exp(sc - lse[e])
                    dp = lax.dot_general(dom[e], vc, NT, preferred_element_type=F32)
                    ds = (p * (dp - delta[e])).astype(BF16)
                    ps.append(p.astype(BF16))
                    dss.append(ds)
                    dqs[e] = dqs[e] + lax.dot_general(ds, kc, NN, preferred_element_type=F32)
                dv_ref[pl.ds(off, tk), :] += lax.dot_general(jnp.concatenate(ps, axis=0), dom_both, TN,
                                                             preferred_element_type=F32)
                dk_ref[pl.ds(off, tk), :] += lax.dot_general(jnp.concatenate(dss, axis=0), qs_both, TN,
                                                             preferred_element_type=F32)
            return tuple(dqs)

        dq_init = jnp.zeros((tq, LANES), F32)
        dq0, dq1 = lax.fori_loop(0, n_chunks // unroll, chunks, (dq_init, dq_init))
        dq_ref[...] = (jnp.where(low, dq0, dq1) * ATTN_SCALE).astype(BF16)

    q_spec = pl.BlockSpec((tq, LANES), lambda kvh, hb, qi: (qi, kvh * 2 + hb))
    kv_spec = pl.BlockSpec((s, LANES), lambda kvh, hb, qi: (0, kvh))
    grid = (2, 2, s // tq)
    body, extra = _carry_exchange(body, 6, 3, grid, ride)
    return pl.pallas_call(
        body, name=name, grid=grid,
        in_specs=[q_spec, kv_spec, kv_spec, q_spec, q_spec, q_spec] + extra.in_specs,
        out_specs=[q_spec, kv_spec, kv_spec] + extra.out_specs,
        out_shape=[jax.ShapeDtypeStruct((s, A_Q_WIDTH), BF16),
                   jax.ShapeDtypeStruct((s, 2 * LANES), F32), jax.ShapeDtypeStruct((s, 2 * LANES), F32)]
        + extra.out_shape,
        scratch_shapes=extra.scratch, compiler_params=_params(),
    )(qkv, kdup, vdup, o, lse, do, *extra.args)


BAND_Q = 128
GROUP_QKV = 3 * B_OUT_WIDTH
TOKEN_BLOCKS = GROUP_QKV // LANES
B_Q_BLOCK0 = 0
B_K_BLOCK0 = B_OUT_WIDTH // LANES
B_V_BLOCK0 = 2 * (B_OUT_WIDTH // LANES)


BAND_UNROLL = 4


def _unrolled_loop(n, step):
    unroll = BAND_UNROLL if n % BAND_UNROLL == 0 else 1

    def body(it, carry):
        for u in range(unroll):
            step(it * unroll + u)
        return carry

    lax.fori_loop(0, n // unroll, body, 0)


def _band_geometry(length):
    seg = min(length, 2048)
    win = min(2 * BAND_Q, length)
    return seg, win


def _band_window(qs, length, win):
    st = jnp.clip(qs - B_HALF_SPAN, 0, length - win)
    st = pl.multiple_of(st, B_HALF_SPAN)
    qpos = qs + lax.broadcasted_iota(jnp.int32, (BAND_Q, 1), 0)
    kpos = st + lax.broadcasted_iota(jnp.int32, (1, win), 1)
    return st, jnp.abs(qpos - kpos) <= B_HALF_SPAN


def _band_fwd(qkv_view, gi, dil, *, name):
    length = qkv_view.shape[0]
    seg, win = _band_geometry(length)
    n_sub = seg // BAND_Q

    def body(q_ref, k_ref, v_ref, o_ref, lse_ref):
        seg_i = pl.program_id(2)
        lane = _lane_iota()
        low = lane < HEAD_DIM

        def one(i):
            ql = pl.multiple_of(i * BAND_Q, BAND_Q)
            st, valid = _band_window(seg_i * seg + ql, length, win)
            q = q_ref[pl.ds(ql, BAND_Q), :]
            kw = k_ref[pl.ds(st, win), :]
            vw = v_ref[pl.ds(st, win), :]
            outs, lses = [], []
            for e in range(2):
                mine = (lane >= HEAD_DIM) if e else (lane < HEAD_DIM)
                qm = jnp.where(mine, q, jnp.zeros_like(q)) * ATTN_SCALE
                sc = lax.dot_general(qm, kw, NT, preferred_element_type=F32)
                sc = jnp.where(valid, sc, NEG_INF)
                m = jnp.max(sc, axis=1, keepdims=True)
                p = jnp.exp(sc - m)
                l = jnp.sum(p, axis=1, keepdims=True)
                outs.append(lax.dot_general(p.astype(BF16), vw, NN, preferred_element_type=F32) / l)
                lses.append(m + jnp.log(l))
            o_ref[pl.ds(ql, BAND_Q), :] = jnp.where(low, outs[0], outs[1]).astype(BF16)
            lse_ref[pl.ds(ql, BAND_Q), :] = jnp.where(low, lses[0], lses[1])

        _unrolled_loop(n_sub, one)

    def col(base):
        return lambda r, hp, sg: (0, r * TOKEN_BLOCKS + base + hp)

    q_spec = pl.BlockSpec((seg, LANES), lambda r, hp, sg: (sg, r * TOKEN_BLOCKS + B_Q_BLOCK0 + hp))
    out_spec = pl.BlockSpec((seg, LANES), lambda r, hp, sg: (sg, r * 2 + hp))
    return pl.pallas_call(
        body, name=name, grid=(dil, 2, length // seg),
        in_specs=[q_spec, pl.BlockSpec((length, LANES), col(B_K_BLOCK0)), pl.BlockSpec((length, LANES), col(B_V_BLOCK0))],
        out_specs=[out_spec, out_spec],
        out_shape=[jax.ShapeDtypeStruct((length, dil * B_OUT_WIDTH), BF16),
                   jax.ShapeDtypeStruct((length, dil * B_OUT_WIDTH), F32)],
        compiler_params=_params(),
    )(qkv_view, qkv_view, qkv_view)


def _band_bwd(qkv_view, do, lse, dd, gi, dil, *, name):
    length = qkv_view.shape[0]
    seg, win = _band_geometry(length)
    n_sub = seg // BAND_Q

    def body(q_ref, k_ref, v_ref, do_ref, lse_ref, dd_ref, dq_ref, dk_ref, dv_ref):
        seg_i = pl.program_id(2)
        lane = _lane_iota()

        @pl.when(seg_i == 0)
        def _():
            dk_ref[...] = jnp.zeros_like(dk_ref)
            dv_ref[...] = jnp.zeros_like(dv_ref)

        def one(i):
            ql = pl.multiple_of(i * BAND_Q, BAND_Q)
            st, valid = _band_window(seg_i * seg + ql, length, win)
            q = q_ref[pl.ds(ql, BAND_Q), :]
            dov = do_ref[pl.ds(ql, BAND_Q), :]
            lsev = lse_ref[pl.ds(ql, BAND_Q), :]
            ddv = dd_ref[pl.ds(ql, BAND_Q), :]
            kw = k_ref[pl.ds(st, win), :]
            vw = v_ref[pl.ds(st, win), :]
            dq = jnp.zeros((BAND_Q, LANES), F32)
            for e in range(2):
                mine = (lane >= HEAD_DIM) if e else (lane < HEAD_DIM)
                qs = jnp.where(mine, q, jnp.zeros_like(q)) * ATTN_SCALE
                dom = jnp.where(mine, dov, jnp.zeros_like(dov))
                lse_e = lsev[:, e * HEAD_DIM:e * HEAD_DIM + 1]
                dd_e = ddv[:, e * HEAD_DIM:e * HEAD_DIM + 1]
                sc = lax.dot_general(qs, kw, NT, preferred_element_type=F32)
                p = jnp.exp(jnp.where(valid, sc, NEG_INF) - lse_e)
                dp = lax.dot_general(dom, vw, NT, preferred_element_type=F32)
                ds = (p * (dp - dd_e)).astype(BF16)
                dv_ref[pl.ds(st, win), :] += lax.dot_general(p.astype(BF16), dom, TN, preferred_element_type=F32)
                dk_ref[pl.ds(st, win), :] += lax.dot_general(ds, qs, TN, preferred_element_type=F32)
                dq_e = lax.dot_general(ds, kw, NN, preferred_element_type=F32) * ATTN_SCALE
                dq = dq + jnp.where(mine, dq_e, 0.0)
            dq_ref[pl.ds(ql, BAND_Q), :] = dq.astype(BF16)

        _unrolled_loop(n_sub, one)

    def col(base):
        return lambda r, hp, sg: (0, r * TOKEN_BLOCKS + base + hp)

    q_spec = pl.BlockSpec((seg, LANES), lambda r, hp, sg: (sg, r * TOKEN_BLOCKS + B_Q_BLOCK0 + hp))
    seg_spec = pl.BlockSpec((seg, LANES), lambda r, hp, sg: (sg, r * 2 + hp))
    full_spec = pl.BlockSpec((length, LANES), lambda r, hp, sg: (0, r * 2 + hp))
    shp = jax.ShapeDtypeStruct((length, dil * B_OUT_WIDTH), F32)
    return pl.pallas_call(
        body, name=name, grid=(dil, 2, length // seg),
        in_specs=[q_spec, pl.BlockSpec((length, LANES), col(B_K_BLOCK0)), pl.BlockSpec((length, LANES), col(B_V_BLOCK0)),
                  seg_spec, seg_spec, seg_spec],
        out_specs=[seg_spec, full_spec, full_spec],
        out_shape=[jax.ShapeDtypeStruct(shp.shape, BF16), shp, shp],
        compiler_params=_params(),
    )(qkv_view, qkv_view, qkv_view, do, lse, dd)


def _merge_weights(lses):
    m = jnp.maximum(jnp.maximum(lses[0], lses[1]), lses[2])
    ex = [jnp.exp(v - m) for v in lses]
    tot = ex[0] + ex[1] + ex[2]
    return [v / tot for v in ex]


def _merge_fwd(os_, lses, *, name, tm=512):
    s = os_[0].shape[0]

    def body(o0, o1, o2, l0, l1, l2, ob_ref):
        w = _merge_weights([l0[...], l1[...], l2[...]])
        ob = w[0] * o0[...].astype(F32) + w[1] * o1[...].astype(F32) + w[2] * o2[...].astype(F32)
        ob_ref[...] = ob.astype(BF16)

    blk = pl.BlockSpec((tm, B_OUT_WIDTH), lambda i: (i, 0))
    return pl.pallas_call(
        body, name=name, grid=(s // tm,), in_specs=[blk] * 6, out_specs=blk,
        out_shape=jax.ShapeDtypeStruct((s, B_OUT_WIDTH), BF16),
        compiler_params=_params(),
    )(*os_, *lses)


def _merge_bwd(os_, lses, dob, *, name, tm=512):
    s = os_[0].shape[0]

    def body(o0, o1, o2, l0, l1, l2, dob_ref, d0, d1, d2, t0, t1, t2):
        w = _merge_weights([l0[...], l1[...], l2[...]])
        dv = dob_ref[...]
        ob = w[0] * o0[...].astype(F32) + w[1] * o1[...].astype(F32) + w[2] * o2[...].astype(F32)
        tot = _head_sum(dv * ob)
        for wg, d_ref, t_ref in zip(w, (d0, d1, d2), (t0, t1, t2)):
            d_ref[...] = (wg * dv).astype(BF16)
            t_ref[...] = wg * tot

    blk = pl.BlockSpec((tm, B_OUT_WIDTH), lambda i: (i, 0))
    return pl.pallas_call(
        body, name=name, grid=(s // tm,), in_specs=[blk] * 7, out_specs=[blk] * 6,
        out_shape=[jax.ShapeDtypeStruct((s, B_OUT_WIDTH), BF16)] * 3 + [jax.ShapeDtypeStruct((s, B_OUT_WIDTH), F32)] * 3,
        compiler_params=_params(),
    )(*os_, *lses, dob)


HALO = 8
POOL_BLOCK0 = QKV_WIDTH // LANES


def _window_sum(ext, lo, hi, tm):
    rows = ext.shape[0]
    acc = None
    for j in range(lo, hi + 1):
        r = ext if j == 0 else pltpu.roll(ext, (-j) % rows, axis=0)
        acc = r if acc is None else acc + r
    return acc[HALO:HALO + tm]


def _pool_counts(t, half, s):
    return (jnp.minimum(t + half, s) - jnp.maximum(t - half, 0)).astype(F32)


def _halo_specs(tm, s, col0):
    per = tm // HALO
    last = s // HALO - 1
    prev = pl.BlockSpec((HALO, LANES), lambda g, i: (jnp.maximum(i * per - 1, 0), col0 + g))
    cur = pl.BlockSpec((tm, LANES), lambda g, i: (i, col0 + g))
    nxt = pl.BlockSpec((HALO, LANES), lambda g, i: (jnp.minimum((i + 1) * per, last), col0 + g))
    return prev, cur, nxt


def _extended(prev_ref, cur_ref, next_ref, i, n_tiles):
    prev = jnp.where(i > 0, prev_ref[...], 0.0)
    nxt = jnp.where(i < n_tiles - 1, next_ref[...], 0.0)
    return jnp.concatenate([prev, cur_ref[...], nxt], axis=0)


def _pool_fwd(z, lin, scale, *, name, tm=512):
    s = z.shape[0]
    tm = min(tm, s)
    n_tiles = s // tm

    def body(prev_ref, cur_ref, next_ref, lin_ref, sc_ref, pooled_ref, mixed_ref):
        g = pl.program_id(0)
        i = pl.program_id(1)
        ext = _extended(prev_ref, cur_ref, next_ref, i, n_tiles)
        t = i * tm + lax.broadcasted_iota(jnp.int32, (tm, 1), 0)
        for gi, half in enumerate(POOL_HALF):
            @pl.when(g == gi)
            def _(half=half):
                mean = _window_sum(ext, -half, half - 1, tm) / _pool_counts(t, half, s)
                pooled = (mean - cur_ref[...]).astype(BF16)
                pooled_ref[...] = pooled
                mixed = lax.dot_general(pooled, lin_ref[...].astype(BF16), NN, preferred_element_type=F32)
                mixed_ref[...] = (mixed * sc_ref[...]).astype(BF16)

    prev, cur, nxt = _halo_specs(tm, s, POOL_BLOCK0)
    out = pl.BlockSpec((tm, LANES), lambda g, i: (i, g))
    return pl.pallas_call(
        body, name=name, grid=(len(POOL_HALF), n_tiles),
        in_specs=[prev, cur, nxt, pl.BlockSpec((None, LANES, LANES), lambda g, i: (g, 0, 0)),
                  pl.BlockSpec((1, LANES), lambda g, i: (0, g))],
        out_specs=[out, out],
        out_shape=[jax.ShapeDtypeStruct((s, POOL_WIDTH), BF16)] * 2,
        compiler_params=_params(),
    )(z, z, z, lin, scale)


def _pool_bwd(dmixed, pooled, lin, scale, *, name, tm=512):
    s = dmixed.shape[0]
    tm = min(tm, s)
    n_tiles = s // tm

    def body(prev_ref, cur_ref, next_ref, pooled_ref, lin_ref, sc_ref, du_ref, dlin_ref, dsc_ref):
        g = pl.program_id(0)
        i = pl.program_id(1)

        @pl.when(i == 0)
        def _():
            dlin_ref[...] = jnp.zeros_like(dlin_ref)
            dsc_ref[...] = jnp.zeros_like(dsc_ref)

        linb = lin_ref[...].astype(BF16)
        ext = _extended(prev_ref, cur_ref, next_ref, i, n_tiles)
        dpl_ext = (ext * sc_ref[...]).astype(BF16)
        dpl_cur = (cur_ref[...] * sc_ref[...]).astype(BF16)
        dpooled_ext = lax.dot_general(dpl_ext, linb, NT, preferred_element_type=F32)
        t_ext = i * tm - HALO + lax.broadcasted_iota(jnp.int32, (tm + 2 * HALO, 1), 0)
        pooled = pooled_ref[...]
        mixed = lax.dot_general(pooled, linb, NN, preferred_element_type=F32)
        dsc_ref[...] += jnp.sum(cur_ref[...] * mixed, axis=0, keepdims=True)
        dlin_ref[...] += lax.dot_general(pooled, dpl_cur, TN, preferred_element_type=F32)
        for gi, half in enumerate(POOL_HALF):
            @pl.when(g == gi)
            def _(half=half):
                share = dpooled_ext / jnp.maximum(_pool_counts(t_ext, half, s), 1.0)
                du = _window_sum(share, -(half - 1), half, tm) - dpooled_ext[HALO:HALO + tm]
                du_ref[...] = du.astype(BF16)

    prev, cur, nxt = _halo_specs(tm, s, 0)
    out = pl.BlockSpec((tm, LANES), lambda g, i: (i, g))
    lin_spec = pl.BlockSpec((None, LANES, LANES), lambda g, i: (g, 0, 0))
    vec = pl.BlockSpec((1, LANES), lambda g, i: (0, g))
    return pl.pallas_call(
        body, name=name, grid=(len(POOL_HALF), n_tiles),
        in_specs=[prev, cur, nxt, out, lin_spec, vec],
        out_specs=[out, lin_spec, vec],
        out_shape=[jax.ShapeDtypeStruct((s, POOL_WIDTH), BF16),
                   jax.ShapeDtypeStruct((len(POOL_HALF), LANES, LANES), F32),
                   jax.ShapeDtypeStruct((1, POOL_WIDTH), F32)],
        compiler_params=_params(),
    )(dmixed, dmixed, dmixed, pooled, lin, scale)


GATE_TILE = 512
GATE_BLOCK0 = (QKV_WIDTH + POOL_WIDTH) // GATE_TILE
GATE_BLOCKS_PER_BRANCH = D_MODEL // GATE_TILE


def _sigmoid(v):
    return 1.0 / (1.0 + jnp.exp(-v))


def _gate_specs(tm):
    def zspec(br):
        return pl.BlockSpec((tm, GATE_TILE), lambda jj, i: (i, GATE_BLOCK0 + GATE_BLOCKS_PER_BRANCH * br + jj))

    def bspec(br):
        return pl.BlockSpec((1, GATE_TILE), lambda jj, i: (0, GATE_BLOCKS_PER_BRANCH * br + jj))

    row = pl.BlockSpec((tm, GATE_TILE), lambda jj, i: (i, jj))
    vec = pl.BlockSpec((1, GATE_TILE), lambda jj, i: (0, jj))
    return [zspec(0), zspec(1), zspec(2)], [bspec(0), bspec(1), bspec(2)], row, vec


def _gate_fwd(z, b_gate, ya, yb, yc, *, name, tm=512):
    s = z.shape[0]

    def body(z0, z1, z2, b0, b1, b2, ya_ref, yb_ref, yc_ref, out_ref):
        acc = _sigmoid(z0[...] + b0[...]) * ya_ref[...]
        acc = acc + _sigmoid(z1[...] + b1[...]) * yb_ref[...]
        acc = acc + _sigmoid(z2[...] + b2[...]) * yc_ref[...]
        out_ref[...] = acc.astype(BF16)

    zs, bs, row, _ = _gate_specs(tm)
    return pl.pallas_call(
        body, name=name, grid=(GATE_BLOCKS_PER_BRANCH, s // tm),
        in_specs=zs + bs + [row] * 3, out_specs=row,
        out_shape=jax.ShapeDtypeStruct((s, D_MODEL), BF16),
        compiler_params=_params(),
    )(z, z, z, b_gate, b_gate, b_gate, ya, yb, yc)


def _gate_bwd(z, b_gate, ya, yb, yc, dmerged, *, name, tm=512):
    s = z.shape[0]

    def body(z0, z1, z2, b0, b1, b2, ya_ref, yb_ref, yc_ref, dm_ref,
             dya_ref, dyb_ref, dyc_ref, dg0, dg1, dg2, db0, db1, db2):
        i = pl.program_id(1)
        dm = dm_ref[...]
        for z_ref, b_ref, y_ref, dy_ref, dg_ref, db_ref in (
                (z0, b0, ya_ref, dya_ref, dg0, db0), (z1, b1, yb_ref, dyb_ref, dg1, db1),
                (z2, b2, yc_ref, dyc_ref, dg2, db2)):
            gate = _sigmoid(z_ref[...] + b_ref[...])
            dy_ref[...] = (gate * dm).astype(BF16)
            dpre = dm * y_ref[...] * gate * (1.0 - gate)
            dg_ref[...] = dpre.astype(BF16)
            part = jnp.sum(dpre, axis=0, keepdims=True)

            @pl.when(i == 0)
            def _(db_ref=db_ref, part=part):
                db_ref[...] = part

            @pl.when(i > 0)
            def _(db_ref=db_ref, part=part):
                db_ref[...] += part

    zs, bs, row, vec = _gate_specs(tm)
    big = jax.ShapeDtypeStruct((s, D_MODEL), BF16)
    small = jax.ShapeDtypeStruct((1, D_MODEL), F32)
    return pl.pallas_call(
        body, name=name, grid=(GATE_BLOCKS_PER_BRANCH, s // tm),
        in_specs=zs + bs + [row] * 4, out_specs=[row] * 6 + [vec] * 3,
        out_shape=[big] * 6 + [small] * 3,
        compiler_params=_params(),
    )(z, z, z, b_gate, b_gate, b_gate, ya, yb, yc, dmerged)


def _swiglu_fwd(a, b, *, name, tm=512, tn=1408):
    s, f = a.shape

    def body(a_ref, b_ref, o_ref):
        av = a_ref[...].astype(F32)
        o_ref[...] = (av * _sigmoid(av) * b_ref[...].astype(F32)).astype(BF16)

    blk = pl.BlockSpec((tm, tn), lambda i, j: (i, j))
    return pl.pallas_call(
        body, name=name, grid=(s // tm, f // tn), in_specs=[blk, blk], out_specs=blk,
        out_shape=jax.ShapeDtypeStruct((s, f), BF16), compiler_params=_params(),
    )(a, b)


def _swiglu_bwd(a, b, df, *, name, tm=512, tn=1408):
    s, f = a.shape

    def body(a_ref, b_ref, df_ref, da_ref, db_ref):
        av = a_ref[...].astype(F32)
        dfv = df_ref[...].astype(F32)
        sg = _sigmoid(av)
        silu = av * sg
        da_ref[...] = (dfv * b_ref[...].astype(F32) * (sg + silu * (1.0 - sg))).astype(BF16)
        db_ref[...] = (dfv * silu).astype(BF16)

    blk = pl.BlockSpec((tm, tn), lambda i, j: (i, j))
    out = jax.ShapeDtypeStruct((s, f), BF16)
    return pl.pallas_call(
        body, name=name, grid=(s // tm, f // tn), in_specs=[blk] * 3, out_specs=[blk] * 2,
        out_shape=[out, out], compiler_params=_params(),
    )(a, b, df)


def _loss_head(y, target, *, name, tm=512):
    s, d = y.shape

    def body(y_ref, t_ref, part_ref, dy_ref):
        i = pl.program_id(0)
        err = y_ref[...] - t_ref[...]
        dy_ref[...] = err * (1.0 / d)
        part = jnp.sum(err * err, axis=0, keepdims=True) * (0.5 / d)

        @pl.when(i == 0)
        def _():
            part_ref[...] = part

        @pl.when(i > 0)
        def _():
            part_ref[...] += part

    row = pl.BlockSpec((tm, d), lambda i: (i, 0))
    vec = pl.BlockSpec((1, d), lambda i: (0, 0))
    return pl.pallas_call(
        body, name=name, grid=(s // tm,), in_specs=[row, row], out_specs=[vec, row],
        out_shape=[jax.ShapeDtypeStruct((1, d), F32), jax.ShapeDtypeStruct((s, d), F32)],
        compiler_params=_params(),
    )(y, target)


def _mesh_place():
    x, y, c = lax.axis_index('x'), lax.axis_index('y'), lax.axis_index('c')
    return x, y, c, 4 * x + 2 * y + c


def _peer(x, y, c, k):
    return (x ^ ((k >> 2) & 1), y ^ ((k >> 1) & 1), c ^ (k & 1))


def _exchange(buf, *, gather, name):
    def body(in_ref, out_ref, send_sems, recv_sems, local_sem):
        start, wait = _exchange_plan(in_ref, out_ref, send_sems, recv_sems, local_sem, gather)
        start()
        wait()

    return pl.pallas_call(
        body, name=name,
        in_specs=[_EXCHANGE_SPEC], out_specs=_EXCHANGE_SPEC,
        out_shape=_exchange_shape(buf), scratch_shapes=_exchange_scratch(),
    )(buf)


_EXCHANGE_SPEC = pl.BlockSpec(memory_space=pl.ANY)


class _Extra:
    def __init__(self, ride):
        self.in_specs = [] if ride is None else [_EXCHANGE_SPEC]
        self.out_specs = [] if ride is None else [_EXCHANGE_SPEC]
        self.out_shape = [] if ride is None else [_exchange_shape(ride[0])]
        self.scratch = [] if ride is None else _exchange_scratch()
        self.args = () if ride is None else (ride[0],)


def _carry_exchange(body, n_in, n_out, grid, ride):
    extra = _Extra(ride)
    if ride is None:
        return body, extra
    gather = ride[1]

    def carrying(*refs):
        ins, ride_in = refs[:n_in], refs[n_in]
        outs, ride_out = refs[n_in + 1:n_in + 1 + n_out], refs[n_in + 1 + n_out]
        sems = refs[n_in + 2 + n_out:]
        ids = [pl.program_id(a) for a in range(len(grid))]
        first = functools.reduce(jnp.logical_and, [i == 0 for i in ids])
        last = functools.reduce(jnp.logical_and, [i == n - 1 for i, n in zip(ids, grid)])

        @pl.when(first)
        def _():
            _exchange_plan(ride_in, ride_out, *sems, gather)[0]()

        body(*ins, *outs)

        @pl.when(last)
        def _():
            _exchange_plan(ride_in, ride_out, *sems, gather)[1]()

    return carrying, extra


def _exchange_shape(buf):
    return jax.ShapeDtypeStruct((N_DEV,) + buf.shape[-2:], buf.dtype)


def _exchange_scratch():
    return [pltpu.SemaphoreType.DMA((N_DEV - 1,)), pltpu.SemaphoreType.DMA((N_DEV - 1,)), pltpu.SemaphoreType.DMA]


def _exchange_plan(in_ref, out_ref, send_sems, recv_sems, local_sem, gather):
    x, y, c, me = _mesh_place()

    def src(slot):
        return in_ref if gather else in_ref.at[slot]

    def copy(k, dst_slot):
        return pltpu.make_async_remote_copy(
            src_ref=src(me ^ k), dst_ref=out_ref.at[dst_slot],
            send_sem=send_sems.at[k - 1], recv_sem=recv_sems.at[k - 1],
            device_id=_peer(x, y, c, k), device_id_type=pl.DeviceIdType.MESH)

    def mine():
        return pltpu.make_async_copy(src(me), out_ref.at[me], local_sem)

    def start():
        mine().start()
        for k in range(1, N_DEV):
            copy(k, me).start()

    def wait():
        for k in range(1, N_DEV):
            copy(k, me ^ k).wait_recv()
        for k in range(1, N_DEV):
            copy(k, me).wait_send()
        mine().wait()

    return start, wait


def _adamw(parts, w, m, v, *, name, tr):
    n_layers = len(parts)
    rows, cols = parts[0].shape[1:]
    assert rows % tr == 0 and w.shape == (n_layers * rows, cols)
    tiles = rows // tr
    bias1 = 1.0 - ADAM_B1 ** ADAM_STEP
    bias2 = 1.0 - ADAM_B2 ** ADAM_STEP

    def body(*refs):
        p_refs = refs[:n_layers]
        w_ref, m_ref, v_ref, g_ref, d_ref, nm_ref, nv_ref = refs[n_layers:]

        def update(p_ref):
            g = p_ref[0].astype(F32)
            for j in range(1, N_DEV):
                g = g + p_ref[j].astype(F32)
            nm = ADAM_B1 * m_ref[...] + (1.0 - ADAM_B1) * g
            nv = ADAM_B2 * v_ref[...] + (1.0 - ADAM_B2) * (g * g)
            g_ref[...] = g
            nm_ref[...] = nm
            nv_ref[...] = nv
            d_ref[...] = -ADAM_LR * ((nm / bias1) / (jnp.sqrt(nv / bias2) + ADAM_EPS) + ADAM_WD * w_ref[...])

        for layer, p_ref in enumerate(p_refs):
            @pl.when(pl.program_id(0) == layer)
            def _(p_ref=p_ref):
                update(p_ref)

    def part_spec(layer):
        def index(l, i):
            return 0, jnp.where(l < layer, 0, jnp.where(l > layer, tiles - 1, i)), 0
        return pl.BlockSpec((N_DEV, tr, cols), index)

    blk = pl.BlockSpec((tr, cols), lambda l, i: (l * tiles + i, 0))
    out = jax.ShapeDtypeStruct((n_layers * rows, cols), F32)
    return pl.pallas_call(
        body, name=name, grid=(n_layers, tiles),
        in_specs=[part_spec(layer) for layer in range(n_layers)] + [blk, blk, blk],
        out_specs=[blk] * 4, out_shape=[out] * 4, compiler_params=_params(),
    )(*parts, w, m, v)


def _col_blocks(full):
    r, n = full.shape
    return full.reshape(r, N_DEV, n // N_DEV).transpose(1, 0, 2)


def _row_blocks(full):
    r, n = full.shape
    return full.reshape(N_DEV, r // N_DEV, n)


def _from_col_blocks(blocks):
    j, r, c = blocks.shape
    return blocks.transpose(1, 0, 2).reshape(r, j * c)


def _from_row_blocks(blocks):
    j, r, c = blocks.shape
    return blocks.reshape(j * r, c)


SHARD_ROWWISE = {'w_out', 'w_ffn_down'}


def _pack(shards):
    return jnp.concatenate([shards[n].reshape(-1, PACK_COLS) for n in SHARDED], axis=0)


def _unpack(packed, shapes):
    out, off = {}, 0
    lead = packed.shape[:-2]
    for n in SHARDED:
        size = shapes[n][0] * shapes[n][1] // PACK_COLS
        out[n] = packed[..., off:off + size, :].reshape(lead + tuple(shapes[n]))
        off += size
    return out


def _pack_small(vals):
    flat = jnp.concatenate([vals[n].reshape(-1) for n in SMALL])
    return flat.reshape(-1, PACK_COLS)


def _unpack_small(packed, shapes):
    flat = packed.reshape(-1)
    out, off = {}, 0
    for n in SMALL:
        size = 1
        for dim in shapes[n]:
            size *= dim
        out[n] = flat[off:off + size].reshape(shapes[n])
        off += size
    return out


def _band_views(qkv, s):
    views = []
    for gi, d in enumerate(B_DILATIONS):
        starts = [A_Q_WIDTH + 2 * A_KV_WIDTH + part * B_WIDTH + gi * B_OUT_WIDTH for part in range(3)]
        group = jnp.concatenate([qkv[:, c0:c0 + B_OUT_WIDTH] for c0 in starts], axis=1)
        views.append(group.reshape(s // d, d * GROUP_QKV))
    return views


def _layer_fwd(x, p, tables, ride=None):
    s = x.shape[0]
    cos, sin = tables
    h = _rms_fwd(x, p['norm_mix'], name='rms_mix_fwd')
    z = _mm(h, p['w_in'], dims='nn', out_dtype=F32, tm=1024, tn=1664, tk=1024, name='mm_in_fwd')
    gains = _qk_gains(p['qn_a'], p['kn_a'], p['qn_b'], p['kn_b'])
    qkv = _qkrope_fwd(z, gains, cos, sin, name='qkrope_fwd')

    kdup = _dup_halves(qkv[:, A_Q_WIDTH:A_Q_WIDTH + A_KV_WIDTH])
    va = qkv[:, A_Q_WIDTH + A_KV_WIDTH:A_Q_WIDTH + 2 * A_KV_WIDTH]
    vdup = _dup_halves(va)
    oa, lse_a, *exchanged = _attn_a_fwd(qkv, kdup, _with_ones(va), ride=ride, name='attn_a_fwd')
    ya = _mm(oa, p['w_branch_a'], dims='nn', out_dtype=BF16, tm=1024, tn=1024, tk=512, name='mm_branch_a_fwd')

    views = _band_views(qkv, s)
    o_g, lse_g = [], []
    for gi, d in enumerate(B_DILATIONS):
        o, lse = _band_fwd(views[gi], gi, d, name=f'band_fwd_d{d}')
        o_g.append(o.reshape(s, B_OUT_WIDTH))
        lse_g.append(lse.reshape(s, B_OUT_WIDTH))
    ob = _merge_fwd(o_g, lse_g, name='merge_fwd')
    yb = _mm(ob, p['w_branch_b'], dims='nn', out_dtype=BF16, tm=1024, tn=1024, tk=256, name='mm_branch_b_fwd')

    pooled, mixed = _pool_fwd(z, p['pool_lin'], p['pool_scale'], name='pool_fwd')
    yc = _mm(mixed, p['w_branch_c'], dims='nn', out_dtype=BF16, tm=1024, tn=1024, tk=512, name='mm_branch_c_fwd')

    merged = _gate_fwd(z, p['b_gate'], ya, yb, yc, name='gate_fwd')
    x_mid = _mm(merged, p['w_out'], dims='nn', out_dtype=F32, tm=1024, tn=1024, tk=1024, res=x, name='mm_out_fwd')

    h2 = _rms_fwd(x_mid, p['norm_ffn'], name='rms_ffn_fwd')
    fa = _mm(h2, p['w_ffn_gate'], dims='nn', out_dtype=BF16, tm=1024, tn=1408, tk=1024, name='mm_ffn_gate_fwd')
    fb = _mm(h2, p['w_ffn_up'], dims='nn', out_dtype=BF16, tm=1024, tn=1408, tk=1024, name='mm_ffn_up_fwd')
    f = _swiglu_fwd(fa, fb, name='swiglu_fwd')
    x_out = _mm(f, p['w_ffn_down'], dims='nn', out_dtype=F32, tm=1024, tn=512, tk=2816, res=x_mid, name='mm_ffn_down_fwd')

    saved = dict(x=x, h=h, z=z, gains=gains, qkv=qkv, kdup=kdup, vdup=vdup, oa=oa, lse_a=lse_a, o_g=o_g, lse_g=lse_g,
                 ob=ob, pooled=pooled, mixed=mixed, ya=ya, yb=yb, yc=yc, merged=merged, x_mid=x_mid, h2=h2,
                 fa=fa, fb=fb, f=f, views=views)
    return x_out, saved, (exchanged[0] if exchanged else None)


def _fold_heads(v, heads):
    return v.reshape(heads, HEAD_DIM).sum(axis=0)


def _layer_bwd(dx, p, sv, tables, ride=None):
    s = dx.shape[0]
    cos, sin = tables
    g = {}

    df = _mm(dx, p['w_ffn_down'], dims='nt', out_dtype=BF16, tm=1024, tn=1408, tk=1024, name='mm_ffn_down_dx')
    g['w_ffn_down'] = _mm(sv['f'], dx, dims='tn', out_dtype=BF16, tm=1408, tn=1024, tk=512, name='mm_ffn_down_dw')
    da, db = _swiglu_bwd(sv['fa'], sv['fb'], df, name='swiglu_bwd')
    dh2 = _mm(da, p['w_ffn_gate'], dims='nt', out_dtype=F32, tm=1024, tn=512, tk=2816, name='mm_ffn_gate_dx')
    dh2 = _mm(db, p['w_ffn_up'], dims='nt', out_dtype=F32, tm=1024, tn=512, tk=2816, res=dh2, name='mm_ffn_up_dx')
    g['w_ffn_gate'] = _mm(sv['h2'], da, dims='tn', out_dtype=BF16, tm=1024, tn=2816, tk=512, name='mm_ffn_gate_dw')
    g['w_ffn_up'] = _mm(sv['h2'], db, dims='tn', out_dtype=BF16, tm=1024, tn=2816, tk=512, name='mm_ffn_up_dw')
    dx_mid, g['norm_ffn'] = _rms_bwd(sv['x_mid'], p['norm_ffn'], dh2, dx, name='rms_ffn_bwd')

    dmerged = _mm(dx_mid, p['w_out'], dims='nt', out_dtype=F32, tm=1024, tn=1024, tk=1024, name='mm_out_dx')
    g['w_out'] = _mm(sv['merged'], dx_mid, dims='tn', out_dtype=BF16, tm=1024, tn=1024, tk=512, name='mm_out_dw')
    dya, dyb, dyc, dg0, dg1, dg2, db0, db1, db2 = _gate_bwd(
        sv['z'], p['b_gate'], sv['ya'], sv['yb'], sv['yc'], dmerged, name='gate_bwd')
    g['b_gate'] = jnp.concatenate([db0, db1, db2], axis=1)

    doa = _mm(dya, p['w_branch_a'], dims='nt', out_dtype=BF16, tm=1024, tn=512, tk=1024, name='mm_branch_a_dx')
    g['w_branch_a'] = _mm(sv['oa'], dya, dims='tn', out_dtype=BF16, tm=512, tn=1024, tk=512, name='mm_branch_a_dw')
    dqa, dkdup, dvdup, *exchanged = _attn_a_bwd(sv['qkv'], sv['kdup'], sv['vdup'], sv['oa'], sv['lse_a'], doa,
                                                ride=ride, name='attn_a_bwd')

    def fold(dup):
        return jnp.concatenate([dup[:, 0:64] + dup[:, 64:128], dup[:, 128:192] + dup[:, 192:256]], axis=1)

    dka, dva = fold(dkdup), fold(dvdup)

    dob = _mm(dyb, p['w_branch_b'], dims='nt', out_dtype=F32, tm=1024, tn=256, tk=1024, name='mm_branch_b_dx')
    g['w_branch_b'] = _mm(sv['ob'], dyb, dims='tn', out_dtype=BF16, tm=256, tn=1024, tk=512, name='mm_branch_b_dw')
    merged_b = _merge_bwd(sv['o_g'], sv['lse_g'], dob, name='merge_bwd')
    do_g, dd_g = merged_b[:3], merged_b[3:]
    views = sv['views']
    dq_parts, dk_parts, dv_parts = [], [], []
    for gi, d in enumerate(B_DILATIONS):
        ln = s // d
        dq, dk, dv = _band_bwd(views[gi], do_g[gi].reshape(ln, d * B_OUT_WIDTH),
                               sv['lse_g'][gi].reshape(ln, d * B_OUT_WIDTH), dd_g[gi].reshape(ln, d * B_OUT_WIDTH),
                               gi, d, name=f'band_bwd_d{d}')
        dq_parts.append(dq.reshape(s, B_OUT_WIDTH))
        dk_parts.append(dk.reshape(s, B_OUT_WIDTH))
        dv_parts.append(dv.reshape(s, B_OUT_WIDTH))

    dmixed = _mm(dyc, p['w_branch_c'], dims='nt', out_dtype=F32, tm=1024, tn=512, tk=1024, name='mm_branch_c_dx')
    g['w_branch_c'] = _mm(sv['mixed'], dyc, dims='tn', out_dtype=BF16, tm=512, tn=1024, tk=512, name='mm_branch_c_dw')
    du, g['pool_lin'], g['pool_scale'] = _pool_bwd(dmixed, sv['pooled'], p['pool_lin'], p['pool_scale'], name='pool_bwd')

    dqkv = jnp.concatenate([part.astype(BF16) for part in [dqa, dka, dva] + dq_parts + dk_parts + dv_parts], axis=1)
    dz_qkv, dgains = _qkrope_bwd(sv['z'], sv['gains'], cos, sin, dqkv, name='qkrope_bwd')
    dgains = jnp.sum(dgains, axis=0)
    g['qn_a'] = _fold_heads(dgains[0:512], 8)
    g['kn_a'] = _fold_heads(dgains[512:640], 2)
    g['qn_b'] = _fold_heads(dgains[768:1536], 12)
    g['kn_b'] = _fold_heads(dgains[1536:2304], 12)

    dz = jnp.concatenate([dz_qkv, du, dg0, dg1, dg2], axis=1)
    dh = _mm(dz, p['w_in'], dims='nt', out_dtype=F32, tm=1024, tn=1024, tk=1664, name='mm_in_dx')
    g['w_in'] = _mm(sv['h'], dz, dims='tn', out_dtype=BF16, tm=1024, tn=3328, tk=512, name='mm_in_dw')
    dx_in, g['norm_mix'] = _rms_bwd(sv['x'], p['norm_mix'], dh, dx_mid, name='rms_mix_bwd')
    return dx_in, g, (exchanged[0] if exchanged else None)


def _small_views(vals, l):
    return {
        'norm_mix': vals['norm_mix'][l][None, :], 'b_gate': vals['b_gate'][l][None, :],
        'qn_a': vals['qn_a'][l], 'kn_a': vals['kn_a'][l], 'qn_b': vals['qn_b'][l], 'kn_b': vals['kn_b'][l],
        'pool_lin': vals['pool_lin'][l], 'pool_scale': vals['pool_scale'][l][None, :],
        'norm_ffn': vals['norm_ffn'][l][None, :],
    }


def kernel(x, norm_mix, w_in, b_gate, qn_a, kn_a, qn_b, kn_b, pool_lin, pool_scale, w_branch_a, w_branch_b, w_branch_c, w_out, norm_ffn, w_ffn_gate, w_ffn_up, w_ffn_down, loss_target, m_norm_mix, m_w_in, m_b_gate, m_qn_a, m_kn_a, m_qn_b, m_kn_b, m_pool_lin, m_pool_scale, m_w_branch_a, m_w_branch_b, m_w_branch_c, m_w_out, m_norm_ffn, m_w_ffn_gate, m_w_ffn_up, m_w_ffn_down, v_norm_mix, v_w_in, v_b_gate, v_qn_a, v_kn_a, v_qn_b, v_kn_b, v_pool_lin, v_pool_scale, v_w_branch_a, v_w_branch_b, v_w_branch_c, v_w_out, v_norm_ffn, v_w_ffn_gate, v_w_ffn_up, v_w_ffn_down):
    w = dict(norm_mix=norm_mix, w_in=w_in, b_gate=b_gate, qn_a=qn_a, kn_a=kn_a, qn_b=qn_b, kn_b=kn_b,
             pool_lin=pool_lin, pool_scale=pool_scale, w_branch_a=w_branch_a, w_branch_b=w_branch_b,
             w_branch_c=w_branch_c, w_out=w_out, norm_ffn=norm_ffn, w_ffn_gate=w_ffn_gate, w_ffn_up=w_ffn_up,
             w_ffn_down=w_ffn_down)
    m = dict(norm_mix=m_norm_mix, w_in=m_w_in, b_gate=m_b_gate, qn_a=m_qn_a, kn_a=m_kn_a, qn_b=m_qn_b, kn_b=m_kn_b,
             pool_lin=m_pool_lin, pool_scale=m_pool_scale, w_branch_a=m_w_branch_a, w_branch_b=m_w_branch_b,
             w_branch_c=m_w_branch_c, w_out=m_w_out, norm_ffn=m_norm_ffn, w_ffn_gate=m_w_ffn_gate,
             w_ffn_up=m_w_ffn_up, w_ffn_down=m_w_ffn_down)
    v = dict(norm_mix=v_norm_mix, w_in=v_w_in, b_gate=v_b_gate, qn_a=v_qn_a, kn_a=v_kn_a, qn_b=v_qn_b, kn_b=v_kn_b,
             pool_lin=v_pool_lin, pool_scale=v_pool_scale, w_branch_a=v_w_branch_a, w_branch_b=v_w_branch_b,
             w_branch_c=v_w_branch_c, w_out=v_w_out, norm_ffn=v_norm_ffn, w_ffn_gate=v_w_ffn_gate,
             w_ffn_up=v_w_ffn_up, w_ffn_down=v_w_ffn_down)
    depth = w_in.shape[0]
    shard_shapes = {n: w[n].shape[1:] for n in SHARDED}
    small_shapes = {n: w[n].shape for n in SMALL}

    def my_shards(l):
        return _pack({n: w[n][l].astype(BF16) for n in SHARDED})

    def full_layer(gathered, l):
        blocks = _unpack(gathered, shard_shapes)
        full = {n: (_from_row_blocks(blocks[n]) if n in SHARD_ROWWISE else _from_col_blocks(blocks[n]))
                for n in SHARDED}
        full.update(_small_views(w, l))
        return full

    def blocks_to_send(grad):
        blocks = {n: (_row_blocks(grad[n]) if n in SHARD_ROWWISE else _col_blocks(grad[n])) for n in SHARDED}
        return jnp.concatenate([blocks[n].reshape(N_DEV, -1, PACK_COLS) for n in SHARDED], axis=1)

    tables = _rope_tables(x.shape[1])
    layers = [full_layer(_exchange(my_shards(0), gather=True, name='gather_weights'), 0)]
    saved = []
    act = x[0]
    for l in range(depth):
        ride = (my_shards(l + 1), True) if l + 1 < depth else None
        act, sv, gathered = _layer_fwd(act, layers[l], tables, ride=ride)
        saved.append(sv)
        if ride is not None:
            layers.append(full_layer(gathered, l + 1))
    part, dx = _loss_head(act, loss_target[0], name='loss_head')
    loss = lax.psum(jnp.sum(part), ('x', 'y', 'c'))

    parts = [None] * depth
    grads = [None] * depth
    pending = None
    for l in reversed(range(depth)):
        ride = (pending, False) if pending is not None else None
        dx, grads[l], arrived = _layer_bwd(dx, layers[l], saved[l], tables, ride=ride)
        if ride is not None:
            parts[l + 1] = arrived
        pending = blocks_to_send(grads[l])
    parts[0] = _exchange(pending, gather=False, name='scatter_grads')
    grad_x = dx

    def pack_layers(t):
        return jnp.concatenate([_pack({n: t[n][l] for n in SHARDED}) for l in range(depth)], axis=0)

    res = _adamw(parts, pack_layers(w), pack_layers(m), pack_layers(v), name='adamw_sharded', tr=128)
    unpacked = [_unpack(r.reshape(depth, -1, PACK_COLS), shard_shapes) for r in res]
    new = {n: tuple(u[n] for u in unpacked) for n in SHARDED}

    small_grad = {n: jnp.stack([grads[l][n].reshape(small_shapes[n][1:]) for l in range(depth)]) for n in SMALL}
    small_parts = _exchange(_pack_small(small_grad), gather=True, name='gather_small_grads')
    res = _adamw([small_parts], _pack_small({n: w[n] for n in SMALL}), _pack_small({n: m[n] for n in SMALL}),
                 _pack_small({n: v[n] for n in SMALL}), name='adamw_small', tr=small_parts.shape[1])
    unpacked = [_unpack_small(r, small_shapes) for r in res]
    for n in SMALL:
        new[n] = tuple(u[n] for u in unpacked)

    outs = [loss, grad_x[None]]
    for idx in range(4):
        outs.extend(new[n][idx] for n in WEIGHTS)
    return tuple(outs)
```

```python
import functools

import jax
import jax.numpy as jnp
from jax import lax
from jax.experimental import pallas as pl
from jax.experimental.pallas import tpu as pltpu

F32 = jnp.float32
BF16 = jnp.bfloat16

N_DEV = 8
D_MODEL = 1024
DEPTH = 4
HEAD_DIM = 64
LANES = 128
A_Q_WIDTH = 512
A_KV_WIDTH = 128
B_WIDTH = 768
B_GROUPS = 3
B_DILATIONS = (1, 4, 16)
B_HALF_SPAN = 64
B_OUT_WIDTH = 256
POOL_WIDTH = 512
POOL_HALF = (1, 2, 4, 8)
GATE_WIDTH = 3072
QKV_WIDTH = A_Q_WIDTH + 2 * A_KV_WIDTH + 3 * B_WIDTH
IN_WIDTH = QKV_WIDTH + POOL_WIDTH + GATE_WIDTH
D_FF = 2816
GRID_W = 64
ROPE_THETA = 10000.0
EPS = 1e-6
NEG_INF = -1e30
ATTN_SCALE = HEAD_DIM ** -0.5

ADAM_LR = 0.001
ADAM_B1 = 0.9
ADAM_B2 = 0.999
ADAM_EPS = 1e-08
ADAM_WD = 0.01
ADAM_STEP = 10

PACK_COLS = 1024
VMEM_LIMIT = 56 * 1024 * 1024

SHARDED = ('w_in', 'w_branch_a', 'w_branch_b', 'w_branch_c', 'w_out', 'w_ffn_gate', 'w_ffn_up', 'w_ffn_down')
SMALL = ('norm_mix', 'b_gate', 'qn_a', 'kn_a', 'qn_b', 'kn_b', 'pool_lin', 'pool_scale', 'norm_ffn')
WEIGHTS = ('norm_mix', 'w_in', 'b_gate', 'qn_a', 'kn_a', 'qn_b', 'kn_b', 'pool_lin', 'pool_scale',
           'w_branch_a', 'w_branch_b', 'w_branch_c', 'w_out', 'norm_ffn', 'w_ffn_gate', 'w_ffn_up', 'w_ffn_down')

NN = (((1,), (0,)), ((), ()))
NT = (((1,), (1,)), ((), ()))
TN = (((0,), (0,)), ((), ()))


def _params(vmem=None):
    return pltpu.CompilerParams(vmem_limit_bytes=VMEM_LIMIT if vmem is None else vmem)


def _lane_iota(n=LANES):
    return lax.broadcasted_iota(jnp.int32, (1, n), 1)


def _swap(x, sh, lane):
    n = x.shape[-1]
    down = pltpu.roll(x, sh, axis=1)
    up = pltpu.roll(x, n - sh, axis=1)
    return jnp.where((lane & sh) == 0, up, down)


def _head_sum(v):
    w = v.shape[-1]
    r = lax.broadcasted_iota(jnp.int32, (w, w), 0) // HEAD_DIM
    c = lax.broadcasted_iota(jnp.int32, (w, w), 1) // HEAD_DIM
    ones = (r == c).astype(BF16)
    hi = v.astype(BF16)
    lo = (v - hi.astype(F32)).astype(BF16)
    return (lax.dot_general(hi, ones, NN, preferred_element_type=F32)
            + lax.dot_general(lo, ones, NN, preferred_element_type=F32))


def _mm(a, b, *, dims, out_dtype, tm, tn, tk, name, res=None):
    if dims == 'nn':
        (m, k), n = a.shape, b.shape[1]
    elif dims == 'nt':
        (m, k), n = a.shape, b.shape[0]
    else:
        (k, m), n = a.shape, b.shape[1]
    tm, tn, tk = min(tm, m), min(tn, n), min(tk, k)
    assert m % tm == 0 and n % tn == 0 and k % tk == 0, (name, m, n, k, tm, tn, tk)
    nk = k // tk
    if dims == 'tn':
        a_spec = pl.BlockSpec((tk, tm), lambda i, j, kk: (kk, i))
    else:
        a_spec = pl.BlockSpec((tm, tk), lambda i, j, kk: (i, kk))
    if dims == 'nt':
        b_spec = pl.BlockSpec((tn, tk), lambda i, j, kk: (j, kk))
    else:
        b_spec = pl.BlockSpec((tk, tn), lambda i, j, kk: (kk, j))
    o_spec = pl.BlockSpec((tm, tn), lambda i, j, kk: (i, j))
    dn = {'nn': NN, 'nt': NT, 'tn': TN}[dims]
    has_res = res is not None

    def body(*refs):
        if has_res:
            a_ref, b_ref, r_ref, o_ref, acc_ref = refs
        else:
            a_ref, b_ref, o_ref, acc_ref = refs
        prod = lax.dot_general(a_ref[...].astype(BF16), b_ref[...].astype(BF16), dn,
                               preferred_element_type=F32)

        def finish(total):
            if has_res:
                total = total + r_ref[...]
            o_ref[...] = total.astype(out_dtype)

        if nk == 1:
            finish(prod)
        else:
            kk = pl.program_id(2)

            @pl.when(kk == 0)
            def _():
                acc_ref[...] = prod

            @pl.when(kk > 0)
            def _():
                acc_ref[...] += prod

            @pl.when(kk == nk - 1)
            def _():
                finish(acc_ref[...])

    in_specs = [a_spec, b_spec] + ([o_spec] if has_res else [])
    args = (a, b) + ((res,) if has_res else ())
    acc_shape = (tm, tn) if nk > 1 else (8, LANES)
    return pl.pallas_call(
        body, name=name, grid=(m // tm, n // tn, nk),
        in_specs=in_specs, out_specs=o_spec,
        out_shape=jax.ShapeDtypeStruct((m, n), out_dtype),
        scratch_shapes=[pltpu.VMEM(acc_shape, F32)],
        compiler_params=_params(),
    )(*args)


def _rms_fwd(x, g, *, name, tm=512):
    s, d = x.shape

    def body(x_ref, g_ref, h_ref):
        xv = x_ref[...]
        rstd = lax.rsqrt(jnp.mean(xv * xv, axis=-1, keepdims=True) + EPS)
        h_ref[...] = (xv * rstd * g_ref[...]).astype(BF16)

    return pl.pallas_call(
        body, name=name, grid=(s // tm,),
        in_specs=[pl.BlockSpec((tm, d), lambda i: (i, 0)), pl.BlockSpec((1, d), lambda i: (0, 0))],
        out_specs=pl.BlockSpec((tm, d), lambda i: (i, 0)),
        out_shape=jax.ShapeDtypeStruct((s, d), BF16),
        compiler_params=_params(),
    )(x, g)


def _rms_bwd(x, g, dh, dres, *, name, tm=512):
    s, d = x.shape

    def body(x_ref, g_ref, dh_ref, dres_ref, dx_ref, dg_ref):
        i = pl.program_id(0)
        xv = x_ref[...]
        rstd = lax.rsqrt(jnp.mean(xv * xv, axis=-1, keepdims=True) + EPS)
        xhat = xv * rstd
        dhv = dh_ref[...]
        dxhat = dhv * g_ref[...]
        proj = jnp.mean(dxhat * xhat, axis=-1, keepdims=True)
        dx_ref[...] = dres_ref[...] + rstd * (dxhat - xhat * proj)
        part = jnp.sum(dhv * xhat, axis=0, keepdims=True)

        @pl.when(i == 0)
        def _():
            dg_ref[...] = part

        @pl.when(i > 0)
        def _():
            dg_ref[...] += part

    row = pl.BlockSpec((tm, d), lambda i: (i, 0))
    vec = pl.BlockSpec((1, d), lambda i: (0, 0))
    return pl.pallas_call(
        body, name=name, grid=(s // tm,),
        in_specs=[row, vec, row, row], out_specs=[row, vec],
        out_shape=[jax.ShapeDtypeStruct((s, d), F32), jax.ShapeDtypeStruct((1, d), F32)],
        compiler_params=_params(),
    )(x, g, dh, dres)


N_QKV_BLOCKS = QKV_WIDTH // LANES
A_BLOCKS = (A_Q_WIDTH + 2 * A_KV_WIDTH) // LANES
V_A_BLOCK = A_BLOCKS - 1
V_B_FIRST = A_BLOCKS + 2 * (B_WIDTH // LANES)


def _qk_kind(j):
    return jnp.where(j < A_BLOCKS, 0, 1)


def _is_v_block(j):
    return (j == V_A_BLOCK) | (j >= V_B_FIRST)


def _rope_tables(s):
    def ang(pos, dim):
        inv = ROPE_THETA ** (-jnp.arange(0, dim, 2, dtype=F32) / dim)
        return pos.astype(F32)[:, None] * inv[None, :]
    t = jnp.arange(s)
    a_row = ang(t // GRID_W, HEAD_DIM // 2)
    a_col = ang(t % GRID_W, HEAD_DIM // 2)
    a_seq = ang(t, HEAD_DIM)
    cos_a = jnp.concatenate([jnp.cos(a_row)] * 2 + [jnp.cos(a_col)] * 2, axis=-1)
    sin_a = jnp.concatenate([-jnp.sin(a_row), jnp.sin(a_row), -jnp.sin(a_col), jnp.sin(a_col)], axis=-1)
    cos_b = jnp.concatenate([jnp.cos(a_seq)] * 2, axis=-1)
    sin_b = jnp.concatenate([-jnp.sin(a_seq), jnp.sin(a_seq)], axis=-1)
    cos = jnp.stack([jnp.tile(cos_a, (1, 2)), jnp.tile(cos_b, (1, 2))])
    sin = jnp.stack([jnp.tile(sin_a, (1, 2)), jnp.tile(sin_b, (1, 2))])
    return cos, sin


def _qk_gains(qn_a, kn_a, qn_b, kn_b):
    one = jnp.ones((HEAD_DIM,), F32)
    parts = [jnp.tile(qn_a, 8), jnp.tile(kn_a, 2), jnp.tile(one, 2),
             jnp.tile(qn_b, 12), jnp.tile(kn_b, 12), jnp.tile(one, 12)]
    return jnp.concatenate(parts)[None, :]


def _qkrope_fwd(z, gains, cos, sin, *, name, tm=2048):
    s = z.shape[0]
    tm = min(tm, s)

    def body(z_ref, g_ref, c_ref, s_ref, o_ref):
        j = pl.program_id(1)
        lane = _lane_iota()
        xv = z_ref[...]

        def normed_rope(pair):
            ms = _head_sum(xv * xv) * (1.0 / HEAD_DIM)
            n = xv * lax.rsqrt(ms + EPS) * g_ref[...]
            return n * c_ref[...] + _swap(n, pair, lane) * s_ref[...]

        @pl.when(_is_v_block(j))
        def _():
            o_ref[...] = xv.astype(BF16)

        @pl.when(jnp.logical_not(_is_v_block(j)) & (j < A_BLOCKS))
        def _():
            o_ref[...] = normed_rope(HEAD_DIM // 4).astype(BF16)

        @pl.when(jnp.logical_not(_is_v_block(j)) & (j >= A_BLOCKS))
        def _():
            o_ref[...] = normed_rope(HEAD_DIM // 2).astype(BF16)

    tab = pl.BlockSpec((None, tm, LANES), lambda i, j: (_qk_kind(j), i, 0))
    blk = pl.BlockSpec((tm, LANES), lambda i, j: (i, j))
    return pl.pallas_call(
        body, name=name, grid=(s // tm, N_QKV_BLOCKS),
        in_specs=[blk, pl.BlockSpec((1, LANES), lambda i, j: (0, j)), tab, tab],
        out_specs=blk,
        out_shape=jax.ShapeDtypeStruct((s, QKV_WIDTH), BF16),
        compiler_params=_params(),
    )(z, gains, cos, sin)


SUBLANES = 8


def _qkrope_bwd(z, gains, cos, sin, dqkv, *, name, tm=2048):
    s = z.shape[0]
    tm = min(tm, s)

    def body(z_ref, g_ref, c_ref, s_ref, dy_ref, dz_ref, dg_ref):
        j = pl.program_id(1)
        lane = _lane_iota()
        xv = z_ref[...]
        dy = dy_ref[...].astype(F32)

        def back(pair):
            ms = _head_sum(xv * xv) * (1.0 / HEAD_DIM)
            rstd = lax.rsqrt(ms + EPS)
            xhat = xv * rstd
            dn = dy * c_ref[...] + _swap(dy * s_ref[...], pair, lane)
            dg_ref[...] = jnp.sum((dn * xhat).reshape(tm // SUBLANES, SUBLANES, LANES), axis=0)
            dxhat = dn * g_ref[...]
            proj = _head_sum(dxhat * xhat) * (1.0 / HEAD_DIM)
            dz_ref[...] = (rstd * (dxhat - xhat * proj)).astype(BF16)

        @pl.when(_is_v_block(j))
        def _():
            dz_ref[...] = dy.astype(BF16)
            dg_ref[...] = jnp.zeros_like(dg_ref)

        @pl.when(jnp.logical_not(_is_v_block(j)) & (j < A_BLOCKS))
        def _():
            back(HEAD_DIM // 4)

        @pl.when(jnp.logical_not(_is_v_block(j)) & (j >= A_BLOCKS))
        def _():
            back(HEAD_DIM // 2)

    tab = pl.BlockSpec((None, tm, LANES), lambda i, j: (_qk_kind(j), i, 0))
    blk = pl.BlockSpec((tm, LANES), lambda i, j: (i, j))
    return pl.pallas_call(
        body, name=name, grid=(s // tm, N_QKV_BLOCKS),
        in_specs=[blk, pl.BlockSpec((1, LANES), lambda i, j: (0, j)), tab, tab, blk],
        out_specs=[blk, pl.BlockSpec((SUBLANES, LANES), lambda i, j: (i, j))],
        out_shape=[jax.ShapeDtypeStruct((s, QKV_WIDTH), BF16),
                   jax.ShapeDtypeStruct((s // tm * SUBLANES, QKV_WIDTH), F32)],
        compiler_params=_params(),
    )(z, gains, cos, sin, dqkv)


def _dup_halves(kv):
    h0, h1 = kv[:, :HEAD_DIM], kv[:, HEAD_DIM:]
    return jnp.concatenate([h0, h0, h1, h1], axis=1)


def _with_ones(kv):
    h0, h1 = kv[:, :HEAD_DIM], kv[:, HEAD_DIM:]
    one = jnp.ones_like(h0)
    return jnp.concatenate([h0, one, one, h0, h1, one, one, h1], axis=1)


def _attn_a_fwd(qkv, kdup, vones, *, name, ride=None, tq=512, tk=512, unroll=4):
    s = qkv.shape[0]
    tq, tk = min(tq, s), min(tk, s)
    n_chunks = s // tk
    unroll = min(unroll, n_chunks)
    assert n_chunks % unroll == 0

    def body(q_ref, k_ref, v_ref, o_ref, lse_ref):
        lane = _lane_iota()
        low = lane < HEAD_DIM
        q = q_ref[...]
        zero = jnp.zeros_like(q)
        qm = [jnp.where(low, q, zero) * ATTN_SCALE, jnp.where(low, zero, q) * ATTN_SCALE]

        def chunks(c, carry):
            state = list(carry)
            scs = []
            for u in range(unroll):
                off = pl.multiple_of((c * unroll + u) * tk, tk)
                kc = k_ref[pl.ds(off, tk), :]
                scs.append([lax.dot_general(qm[e], kc, NT, preferred_element_type=F32) for e in range(2)])
            for u in range(unroll):
                off = pl.multiple_of((c * unroll + u) * tk, tk)
                for e in range(2):
                    m, acc = state[2 * e], state[2 * e + 1]
                    ve = v_ref[pl.ds(off, tk), e * LANES:(e + 1) * LANES]
                    m_new = jnp.maximum(m, jnp.max(scs[u][e], axis=1, keepdims=True))
                    alpha = jnp.exp(m - m_new)
                    p = jnp.exp(scs[u][e] - m_new).astype(BF16)
                    state[2 * e] = m_new
                    state[2 * e + 1] = alpha * acc + lax.dot_general(p, ve, NN, preferred_element_type=F32)
            return tuple(state)

        m_init = jnp.full((tq, 1), NEG_INF, F32)
        a_init = jnp.zeros((tq, LANES), F32)
        m0, a0, m1, a1 = lax.fori_loop(0, n_chunks // unroll, chunks, (m_init, a_init, m_init, a_init))
        l0 = pltpu.roll(a0, HEAD_DIM, axis=1)
        l1 = pltpu.roll(a1, HEAD_DIM, axis=1)
        o_ref[...] = jnp.where(low, a0 / l0, a1 / l1).astype(BF16)
        lse_ref[...] = jnp.where(low, m0 + jnp.log(l0), m1 + jnp.log(l1))

    q_spec = pl.BlockSpec((tq, LANES), lambda hb, qi: (qi, hb))
    k_spec = pl.BlockSpec((s, LANES), lambda hb, qi: (0, hb // 2))
    v_spec = pl.BlockSpec((s, 2 * LANES), lambda hb, qi: (0, hb // 2))
    grid = (A_Q_WIDTH // LANES, s // tq)
    body, extra = _carry_exchange(body, 3, 2, grid, ride)
    return pl.pallas_call(
        body, name=name, grid=grid,
        in_specs=[q_spec, k_spec, v_spec] + extra.in_specs, out_specs=[q_spec, q_spec] + extra.out_specs,
        out_shape=[jax.ShapeDtypeStruct((s, A_Q_WIDTH), BF16), jax.ShapeDtypeStruct((s, A_Q_WIDTH), F32)]
        + extra.out_shape,
        scratch_shapes=extra.scratch, compiler_params=_params(),
    )(qkv, kdup, vones, *extra.args)


def _attn_a_bwd(qkv, kdup, vdup, o, lse, do, *, name, ride=None, tq=512, tk=512, unroll=2):
    s = qkv.shape[0]
    tq, tk = min(tq, s), min(tk, s)
    n_chunks = s // tk
    unroll = min(unroll, n_chunks)
    assert n_chunks % unroll == 0

    def body(q_ref, k_ref, v_ref, o_ref, lse_ref, do_ref, dq_ref, dk_ref, dv_ref):
        first = (pl.program_id(1) == 0) & (pl.program_id(2) == 0)

        @pl.when(first)
        def _():
            dk_ref[...] = jnp.zeros_like(dk_ref)
            dv_ref[...] = jnp.zeros_like(dv_ref)

        lane = _lane_iota()
        low = lane < HEAD_DIM
        q = q_ref[...]
        dov = do_ref[...]
        zero = jnp.zeros_like(q)
        prod = dov.astype(F32) * o_ref[...].astype(F32)
        lsev = lse_ref[...]
        qs = [jnp.where(low, q, zero) * ATTN_SCALE, jnp.where(low, zero, q) * ATTN_SCALE]
        dom = [jnp.where(low, dov, zero), jnp.where(low, zero, dov)]
        delta = [jnp.sum(jnp.where(low, prod, 0.0), axis=1, keepdims=True),
                 jnp.sum(jnp.where(low, 0.0, prod), axis=1, keepdims=True)]
        lse = [lsev[:, 0:1], lsev[:, HEAD_DIM:HEAD_DIM + 1]]
        qs_both = jnp.concatenate(qs, axis=0)
        dom_both = jnp.concatenate(dom, axis=0)

        def chunks(c, carry):
            dqs = list(carry)
            for u in range(unroll):
                off = pl.multiple_of((c * unroll + u) * tk, tk)
                kc = k_ref[pl.ds(off, tk), :]
                vc = v_ref[pl.ds(off, tk), :]
                ps, dss = [], []
                for e in range(2):
                    sc = lax.dot_general(qs[e], kc, NT, preferred_element_type=F32)
                    p = jnp.exp(sc - lse[e])
                    dp = lax.dot_general(dom[e], vc, NT, preferred_element_type=F32)
                    ds = (p * (dp - delta[e])).astype(BF16)
                    ps.append(p.astype(BF16))
                    dss.append(ds)
                    dqs[e] = dqs[e] + lax.dot_general(ds, kc, NN, preferred_element_type=F32)
                dv_ref[pl.ds(off, tk), :] += lax.dot_general(jnp.concatenate(ps, axis=0), dom_both, TN,
                                                             preferred_element_type=F32)
                dk_ref[pl.ds(off, tk), :] += lax.dot_general(jnp.concatenate(dss, axis=0), qs_both, TN,
                                                             preferred_element_type=F32)
            return tuple(dqs)

        dq_init = jnp.zeros((tq, LANES), F32)
        dq0, dq1 = lax.fori_loop(0, n_chunks // unroll, chunks, (dq_init, dq_init))
        dq_ref[...] = (jnp.where(low, dq0, dq1) * ATTN_SCALE).astype(BF16)

    q_spec = pl.BlockSpec((tq, LANES), lambda kvh, hb, qi: (qi, kvh * 2 + hb))
    kv_spec = pl.BlockSpec((s, LANES), lambda kvh, hb, qi: (0, kvh))
    grid = (2, 2, s // tq)
    body, extra = _carry_exchange(body, 6, 3, grid, ride)
    return pl.pallas_call(
        body, name=name, grid=grid,
        in_specs=[q_spec, kv_spec, kv_spec, q_spec, q_spec, q_spec] + extra.in_specs,
        out_specs=[q_spec, kv_spec, kv_spec] + extra.out_specs,
        out_shape=[jax.ShapeDtypeStruct((s, A_Q_WIDTH), BF16),
                   jax.ShapeDtypeStruct((s, 2 * LANES), F32), jax.ShapeDtypeStruct((s, 2 * LANES), F32)]
        + extra.out_shape,
        scratch_shapes=extra.scratch, compiler_params=_params(),
    )(qkv, kdup, vdup, o, lse, do, *extra.args)


BAND_Q = 128
GROUP_QKV = 3 * B_OUT_WIDTH
TOKEN_BLOCKS = GROUP_QKV // LANES
B_Q_BLOCK0 = 0
B_K_BLOCK0 = B_OUT_WIDTH // LANES
B_V_BLOCK0 = 2 * (B_OUT_WIDTH // LANES)


BAND_UNROLL = 4


def _unrolled_loop(n, step):
    unroll = BAND_UNROLL if n % BAND_UNROLL == 0 else 1

    def body(it, carry):
        for u in range(unroll):
            step(it * unroll + u)
        return carry

    lax.fori_loop(0, n // unroll, body, 0)


def _band_geometry(length):
    seg = min(length, 2048)
    win = min(2 * BAND_Q, length)
    return seg, win


def _band_window(qs, length, win):
    st = jnp.clip(qs - B_HALF_SPAN, 0, length - win)
    st = pl.multiple_of(st, B_HALF_SPAN)
    qpos = qs + lax.broadcasted_iota(jnp.int32, (BAND_Q, 1), 0)
    kpos = st + lax.broadcasted_iota(jnp.int32, (1, win), 1)
    return st, jnp.abs(qpos - kpos) <= B_HALF_SPAN


def _band_fwd(qkv_view, gi, dil, *, name):
    length = qkv_view.shape[0]
    seg, win = _band_geometry(length)
    n_sub = seg // BAND_Q

    def body(q_ref, k_ref, v_ref, o_ref, lse_ref):
        seg_i = pl.program_id(2)
        lane = _lane_iota()
        low = lane < HEAD_DIM

        def one(i):
            ql = pl.multiple_of(i * BAND_Q, BAND_Q)
            st, valid = _band_window(seg_i * seg + ql, length, win)
            q = q_ref[pl.ds(ql, BAND_Q), :]
            kw = k_ref[pl.ds(st, win), :]
            vw = v_ref[pl.ds(st, win), :]
            outs, lses = [], []
            for e in range(2):
                mine = (lane >= HEAD_DIM) if e else (lane < HEAD_DIM)
                qm = jnp.where(mine, q, jnp.zeros_like(q)) * ATTN_SCALE
                sc = lax.dot_general(qm, kw, NT, preferred_element_type=F32)
                sc = jnp.where(valid, sc, NEG_INF)
                m = jnp.max(sc, axis=1, keepdims=True)
                p = jnp.exp(sc - m)
                l = jnp.sum(p, axis=1, keepdims=True)
                outs.append(lax.dot_general(p.astype(BF16), vw, NN, preferred_element_type=F32) / l)
                lses.append(m + jnp.log(l))
            o_ref[pl.ds(ql, BAND_Q), :] = jnp.where(low, outs[0], outs[1]).astype(BF16)
            lse_ref[pl.ds(ql, BAND_Q), :] = jnp.where(low, lses[0], lses[1])

        _unrolled_loop(n_sub, one)

    def col(base):
        return lambda r, hp, sg: (0, r * TOKEN_BLOCKS + base + hp)

    q_spec = pl.BlockSpec((seg, LANES), lambda r, hp, sg: (sg, r * TOKEN_BLOCKS + B_Q_BLOCK0 + hp))
    out_spec = pl.BlockSpec((seg, LANES), lambda r, hp, sg: (sg, r * 2 + hp))
    return pl.pallas_call(
        body, name=name, grid=(dil, 2, length // seg),
        in_specs=[q_spec, pl.BlockSpec((length, LANES), col(B_K_BLOCK0)), pl.BlockSpec((length, LANES), col(B_V_BLOCK0))],
        out_specs=[out_spec, out_spec],
        out_shape=[jax.ShapeDtypeStruct((length, dil * B_OUT_WIDTH), BF16),
                   jax.ShapeDtypeStruct((length, dil * B_OUT_WIDTH), F32)],
        compiler_params=_params(),
    )(qkv_view, qkv_view, qkv_view)


def _band_bwd(qkv_view, do, lse, dd, gi, dil, *, name):
    length = qkv_view.shape[0]
    seg, win = _band_geometry(length)
    n_sub = seg // BAND_Q

    def body(q_ref, k_ref, v_ref, do_ref, lse_ref, dd_ref, dq_ref, dk_ref, dv_ref):
        seg_i = pl.program_id(2)
        lane = _lane_iota()

        @pl.when(seg_i == 0)
        def _():
            dk_ref[...] = jnp.zeros_like(dk_ref)
            dv_ref[...] = jnp.zeros_like(dv_ref)

        def one(i):
            ql = pl.multiple_of(i * BAND_Q, BAND_Q)
            st, valid = _band_window(seg_i * seg + ql, length, win)
            q = q_ref[pl.ds(ql, BAND_Q), :]
            dov = do_ref[pl.ds(ql, BAND_Q), :]
            lsev = lse_ref[pl.ds(ql, BAND_Q), :]
            ddv = dd_ref[pl.ds(ql, BAND_Q), :]
            kw = k_ref[pl.ds(st, win), :]
            vw = v_ref[pl.ds(st, win), :]
            dq = jnp.zeros((BAND_Q, LANES), F32)
            for e in range(2):
                mine = (lane >= HEAD_DIM) if e else (lane < HEAD_DIM)
                qs = jnp.where(mine, q, jnp.zeros_like(q)) * ATTN_SCALE
                dom = jnp.where(mine, dov, jnp.zeros_like(dov))
                lse_e = lsev[:, e * HEAD_DIM:e * HEAD_DIM + 1]
                dd_e = ddv[:, e * HEAD_DIM:e * HEAD_DIM + 1]
                sc = lax.dot_general(qs, kw, NT, preferred_element_type=F32)
                p = jnp.exp(jnp.where(valid, sc, NEG_INF) - lse_e)
                dp = lax.dot_general(dom, vw, NT, preferred_element_type=F32)
                ds = (p * (dp - dd_e)).astype(BF16)
                dv_ref[pl.ds(st, win), :] += lax.dot_general(p.astype(BF16), dom, TN, preferred_element_type=F32)
                dk_ref[pl.ds(st, win), :] += lax.dot_general(ds, qs, TN, preferred_element_type=F32)
                dq_e = lax.dot_general(ds, kw, NN, preferred_element_type=F32) * ATTN_SCALE
                dq = dq + jnp.where(mine, dq_e, 0.0)
            dq_ref[pl.ds(ql, BAND_Q), :] = dq.astype(BF16)

        _unrolled_loop(n_sub, one)

    def col(base):
        return lambda r, hp, sg: (0, r * TOKEN_BLOCKS + base + hp)

    q_spec = pl.BlockSpec((seg, LANES), lambda r, hp, sg: (sg, r * TOKEN_BLOCKS + B_Q_BLOCK0 + hp))
    seg_spec = pl.BlockSpec((seg, LANES), lambda r, hp, sg: (sg, r * 2 + hp))
    full_spec = pl.BlockSpec((length, LANES), lambda r, hp, sg: (0, r * 2 + hp))
    shp = jax.ShapeDtypeStruct((length, dil * B_OUT_WIDTH), F32)
    return pl.pallas_call(
        body, name=name, grid=(dil, 2, length // seg),
        in_specs=[q_spec, pl.BlockSpec((length, LANES), col(B_K_BLOCK0)), pl.BlockSpec((length, LANES), col(B_V_BLOCK0)),
                  seg_spec, seg_spec, seg_spec],
        out_specs=[seg_spec, full_spec, full_spec],
        out_shape=[jax.ShapeDtypeStruct(shp.shape, BF16), shp, shp],
        compiler_params=_params(),
    )(qkv_view, qkv_view, qkv_view, do, lse, dd)


def _merge_weights(lses):
    m = jnp.maximum(jnp.maximum(lses[0], lses[1]), lses[2])
    ex = [jnp.exp(v - m) for v in lses]
    tot = ex[0] + ex[1] + ex[2]
    return [v / tot for v in ex]


def _merge_fwd(os_, lses, *, name, tm=512):
    s = os_[0].shape[0]

    def body(o0, o1, o2, l0, l1, l2, ob_ref):
        w = _merge_weights([l0[...], l1[...], l2[...]])
        ob = w[0] * o0[...].astype(F32) + w[1] * o1[...].astype(F32) + w[2] * o2[...].astype(F32)
        ob_ref[...] = ob.astype(BF16)

    blk = pl.BlockSpec((tm, B_OUT_WIDTH), lambda i: (i, 0))
    return pl.pallas_call(
        body, name=name, grid=(s // tm,), in_specs=[blk] * 6, out_specs=blk,
        out_shape=jax.ShapeDtypeStruct((s, B_OUT_WIDTH), BF16),
        compiler_params=_params(),
    )(*os_, *lses)


def _merge_bwd(os_, lses, dob, *, name, tm=512):
    s = os_[0].shape[0]

    def body(o0, o1, o2, l0, l1, l2, dob_ref, d0, d1, d2, t0, t1, t2):
        w = _merge_weights([l0[...], l1[...], l2[...]])
        dv = dob_ref[...]
        ob = w[0] * o0[...].astype(F32) + w[1] * o1[...].astype(F32) + w[2] * o2[...].astype(F32)
        tot = _head_sum(dv * ob)
        for wg, d_ref, t_ref in zip(w, (d0, d1, d2), (t0, t1, t2)):
            d_ref[...] = (wg * dv).astype(BF16)
            t_ref[...] = wg * tot

    blk = pl.BlockSpec((tm, B_OUT_WIDTH), lambda i: (i, 0))
    return pl.pallas_call(
        body, name=name, grid=(s // tm,), in_specs=[blk] * 7, out_specs=[blk] * 6,
        out_shape=[jax.ShapeDtypeStruct((s, B_OUT_WIDTH), BF16)] * 3 + [jax.ShapeDtypeStruct((s, B_OUT_WIDTH), F32)] * 3,
        compiler_params=_params(),
    )(*os_, *lses, dob)


HALO = 8
POOL_BLOCK0 = QKV_WIDTH // LANES


def _window_sum(ext, lo, hi, tm):
    rows = ext.shape[0]
    acc = None
    for j in range(lo, hi + 1):
        r = ext if j == 0 else pltpu.roll(ext, (-j) % rows, axis=0)
        acc = r if acc is None else acc + r
    return acc[HALO:HALO + tm]


def _pool_counts(t, half, s):
    return (jnp.minimum(t + half, s) - jnp.maximum(t - half, 0)).astype(F32)


def _halo_specs(tm, s, col0):
    per = tm // HALO
    last = s // HALO - 1
    prev = pl.BlockSpec((HALO, LANES), lambda g, i: (jnp.maximum(i * per - 1, 0), col0 + g))
    cur = pl.BlockSpec((tm, LANES), lambda g, i: (i, col0 + g))
    nxt = pl.BlockSpec((HALO, LANES), lambda g, i: (jnp.minimum((i + 1) * per, last), col0 + g))
    return prev, cur, nxt


def _extended(prev_ref, cur_ref, next_ref, i, n_tiles):
    prev = jnp.where(i > 0, prev_ref[...], 0.0)
    nxt = jnp.where(i < n_tiles - 1, next_ref[...], 0.0)
    return jnp.concatenate([prev, cur_ref[...], nxt], axis=0)


def _pool_fwd(z, lin, scale, *, name, tm=512):
    s = z.shape[0]
    tm = min(tm, s)
    n_tiles = s // tm

    def body(prev_ref, cur_ref, next_ref, lin_ref, sc_ref, pooled_ref, mixed_ref):
        g = pl.program_id(0)
        i = pl.program_id(1)
        ext = _extended(prev_ref, cur_ref, next_ref, i, n_tiles)
        t = i * tm + lax.broadcasted_iota(jnp.int32, (tm, 1), 0)
        for gi, half in enumerate(POOL_HALF):
            @pl.when(g == gi)
            def _(half=half):
                mean = _window_sum(ext, -half, half - 1, tm) / _pool_counts(t, half, s)
                pooled = (mean - cur_ref[...]).astype(BF16)
                pooled_ref[...] = pooled
                mixed = lax.dot_general(pooled, lin_ref[...].astype(BF16), NN, preferred_element_type=F32)
                mixed_ref[...] = (mixed * sc_ref[...]).astype(BF16)

    prev, cur, nxt = _halo_specs(tm, s, POOL_BLOCK0)
    out = pl.BlockSpec((tm, LANES), lambda g, i: (i, g))
    return pl.pallas_call(
        body, name=name, grid=(len(POOL_HALF), n_tiles),
        in_specs=[prev, cur, nxt, pl.BlockSpec((None, LANES, LANES), lambda g, i: (g, 0, 0)),
                  pl.BlockSpec((1, LANES), lambda g, i: (0, g))],
        out_specs=[out, out],
        out_shape=[jax.ShapeDtypeStruct((s, POOL_WIDTH), BF16)] * 2,
        compiler_params=_params(),
    )(z, z, z, lin, scale)


def _pool_bwd(dmixed, pooled, lin, scale, *, name, tm=512):
    s = dmixed.shape[0]
    tm = min(tm, s)
    n_tiles = s // tm

    def body(prev_ref, cur_ref, next_ref, pooled_ref, lin_ref, sc_ref, du_ref, dlin_ref, dsc_ref):
        g = pl.program_id(0)
        i = pl.program_id(1)

        @pl.when(i == 0)
        def _():
            dlin_ref[...] = jnp.zeros_like(dlin_ref)
            dsc_ref[...] = jnp.zeros_like(dsc_ref)

        linb = lin_ref[...].astype(BF16)
        ext = _extended(prev_ref, cur_ref, next_ref, i, n_tiles)
        dpl_ext = (ext * sc_ref[...]).astype(BF16)
        dpl_cur = (cur_ref[...] * sc_ref[...]).astype(BF16)
        dpooled_ext = lax.dot_general(dpl_ext, linb, NT, preferred_element_type=F32)
        t_ext = i * tm - HALO + lax.broadcasted_iota(jnp.int32, (tm + 2 * HALO, 1), 0)
        pooled = pooled_ref[...]
        mixed = lax.dot_general(pooled, linb, NN, preferred_element_type=F32)
        dsc_ref[...] += jnp.sum(cur_ref[...] * mixed, axis=0, keepdims=True)
        dlin_ref[...] += lax.dot_general(pooled, dpl_cur, TN, preferred_element_type=F32)
        for gi, half in enumerate(POOL_HALF):
            @pl.when(g == gi)
            def _(half=half):
                share = dpooled_ext / jnp.maximum(_pool_counts(t_ext, half, s), 1.0)
                du = _window_sum(share, -(half - 1), half, tm) - dpooled_ext[HALO:HALO + tm]
                du_ref[...] = du.astype(BF16)

    prev, cur, nxt = _halo_specs(tm, s, 0)
    out = pl.BlockSpec((tm, LANES), lambda g, i: (i, g))
    lin_spec = pl.BlockSpec((None, LANES, LANES), lambda g, i: (g, 0, 0))
    vec = pl.BlockSpec((1, LANES), lambda g, i: (0, g))
    return pl.pallas_call(
        body, name=name, grid=(len(POOL_HALF), n_tiles),
        in_specs=[prev, cur, nxt, out, lin_spec, vec],
        out_specs=[out, lin_spec, vec],
        out_shape=[jax.ShapeDtypeStruct((s, POOL_WIDTH), BF16),
                   jax.ShapeDtypeStruct((len(POOL_HALF), LANES, LANES), F32),
                   jax.ShapeDtypeStruct((1, POOL_WIDTH), F32)],
        compiler_params=_params(),
    )(dmixed, dmixed, dmixed, pooled, lin, scale)


GATE_TILE = 512
GATE_BLOCK0 = (QKV_WIDTH + POOL_WIDTH) // GATE_TILE
GATE_BLOCKS_PER_BRANCH = D_MODEL // GATE_TILE


def _sigmoid(v):
    return 1.0 / (1.0 + jnp.exp(-v))


def _gate_specs(tm):
    def zspec(br):
        return pl.BlockSpec((tm, GATE_TILE), lambda jj, i: (i, GATE_BLOCK0 + GATE_BLOCKS_PER_BRANCH * br + jj))

    def bspec(br):
        return pl.BlockSpec((1, GATE_TILE), lambda jj, i: (0, GATE_BLOCKS_PER_BRANCH * br + jj))

    row = pl.BlockSpec((tm, GATE_TILE), lambda jj, i: (i, jj))
    vec = pl.BlockSpec((1, GATE_TILE), lambda jj, i: (0, jj))
    return [zspec(0), zspec(1), zspec(2)], [bspec(0), bspec(1), bspec(2)], row, vec


def _gate_fwd(z, b_gate, ya, yb, yc, *, name, tm=512):
    s = z.shape[0]

    def body(z0, z1, z2, b0, b1, b2, ya_ref, yb_ref, yc_ref, out_ref):
        acc = _sigmoid(z0[...] + b0[...]) * ya_ref[...]
        acc = acc + _sigmoid(z1[...] + b1[...]) * yb_ref[...]
        acc = acc + _sigmoid(z2[...] + b2[...]) * yc_ref[...]
        out_ref[...] = acc.astype(BF16)

    zs, bs, row, _ = _gate_specs(tm)
    return pl.pallas_call(
        body, name=name, grid=(GATE_BLOCKS_PER_BRANCH, s // tm),
        in_specs=zs + bs + [row] * 3, out_specs=row,
        out_shape=jax.ShapeDtypeStruct((s, D_MODEL), BF16),
        compiler_params=_params(),
    )(z, z, z, b_gate, b_gate, b_gate, ya, yb, yc)


def _gate_bwd(z, b_gate, ya, yb, yc, dmerged, *, name, tm=512):
    s = z.shape[0]

    def body(z0, z1, z2, b0, b1, b2, ya_ref, yb_ref, yc_ref, dm_ref,
             dya_ref, dyb_ref, dyc_ref, dg0, dg1, dg2, db0, db1, db2):
        i = pl.program_id(1)
        dm = dm_ref[...]
        for z_ref, b_ref, y_ref, dy_ref, dg_ref, db_ref in (
                (z0, b0, ya_ref, dya_ref, dg0, db0), (z1, b1, yb_ref, dyb_ref, dg1, db1),
                (z2, b2, yc_ref, dyc_ref, dg2, db2)):
            gate = _sigmoid(z_ref[...] + b_ref[...])
            dy_ref[...] = (gate * dm).astype(BF16)
            dpre = dm * y_ref[...] * gate * (1.0 - gate)
            dg_ref[...] = dpre.astype(BF16)
            part = jnp.sum(dpre, axis=0, keepdims=True)

            @pl.when(i == 0)
            def _(db_ref=db_ref, part=part):
                db_ref[...] = part

            @pl.when(i > 0)
            def _(db_ref=db_ref, part=part):
                db_ref[...] += part

    zs, bs, row, vec = _gate_specs(tm)
    big = jax.ShapeDtypeStruct((s, D_MODEL), BF16)
    small = jax.ShapeDtypeStruct((1, D_MODEL), F32)
    return pl.pallas_call(
        body, name=name, grid=(GATE_BLOCKS_PER_BRANCH, s // tm),
        in_specs=zs + bs + [row] * 4, out_specs=[row] * 6 + [vec] * 3,
        out_shape=[big] * 6 + [small] * 3,
        compiler_params=_params(),
    )(z, z, z, b_gate, b_gate, b_gate, ya, yb, yc, dmerged)


def _swiglu_fwd(a, b, *, name, tm=512, tn=1408):
    s, f = a.shape

    def body(a_ref, b_ref, o_ref):
        av = a_ref[...].astype(F32)
        o_ref[...] = (av * _sigmoid(av) * b_ref[...].astype(F32)).astype(BF16)

    blk = pl.BlockSpec((tm, tn), lambda i, j: (i, j))
    return pl.pallas_call(
        body, name=name, grid=(s // tm, f // tn), in_specs=[blk, blk], out_specs=blk,
        out_shape=jax.ShapeDtypeStruct((s, f), BF16), compiler_params=_params(),
    )(a, b)


def _swiglu_bwd(a, b, df, *, name, tm=512, tn=1408):
    s, f = a.shape

    def body(a_ref, b_ref, df_ref, da_ref, db_ref):
        av = a_ref[...].astype(F32)
        dfv = df_ref[...].astype(F32)
        sg = _sigmoid(av)
        silu = av * sg
        da_ref[...] = (dfv * b_ref[...].astype(F32) * (sg + silu * (1.0 - sg))).astype(BF16)
        db_ref[...] = (dfv * silu).astype(BF16)

    blk = pl.BlockSpec((tm, tn), lambda i, j: (i, j))
    out = jax.ShapeDtypeStruct((s, f), BF16)
    return pl.pallas_call(
        body, name=name, grid=(s // tm, f // tn), in_specs=[blk] * 3, out_specs=[blk] * 2,
        out_shape=[out, out], compiler_params=_params(),
    )(a, b, df)


def _loss_head(y, target, *, name, tm=512):
    s, d = y.shape

    def body(y_ref, t_ref, part_ref, dy_ref):
        i = pl.program_id(0)
        err = y_ref[...] - t_ref[...]
        dy_ref[...] = err * (1.0 / d)
        part = jnp.sum(err * err, axis=0, keepdims=True) * (0.5 / d)

        @pl.when(i == 0)
        def _():
            part_ref[...] = part

        @pl.when(i > 0)
        def _():
            part_ref[...] += part

    row = pl.BlockSpec((tm, d), lambda i: (i, 0))
    vec = pl.BlockSpec((1, d), lambda i: (0, 0))
    return pl.pallas_call(
        body, name=name, grid=(s // tm,), in_specs=[row, row], out_specs=[vec, row],
        out_shape=[jax.ShapeDtypeStruct((1, d), F32), jax.ShapeDtypeStruct((s, d), F32)],
        compiler_params=_params(),
    )(y, target)


def _mesh_place():
    x, y, c = lax.axis_index('x'), lax.axis_index('y'), lax.axis_index('c')
    return x, y, c, 4 * x + 2 * y + c


def _peer(x, y, c, k):
    return (x ^ ((k >> 2) & 1), y ^ ((k >> 1) & 1), c ^ (k & 1))


def _exchange(bufs, *, gather, name):
    n = len(bufs)

    def body(*refs):
        start, wait = _exchange_plan(refs[:n], refs[n:2 * n], refs[2 * n:], gather)
        start()
        wait()

    extra = _Extra((bufs, gather))
    return pl.pallas_call(
        body, name=name, in_specs=extra.in_specs, out_specs=extra.out_specs,
        out_shape=extra.out_shape, scratch_shapes=extra.scratch,
    )(*bufs)


_EXCHANGE_SPEC = pl.BlockSpec(memory_space=pl.ANY)
SEMS_PER_BUFFER = 3


class _Extra:
    def __init__(self, ride):
        bufs = [] if ride is None else list(ride[0])
        self.in_specs = [_EXCHANGE_SPEC] * len(bufs)
        self.out_specs = [_EXCHANGE_SPEC] * len(bufs)
        self.out_shape = [jax.ShapeDtypeStruct((N_DEV,) + b.shape[-2:], b.dtype) for b in bufs]
        self.scratch = [pltpu.SemaphoreType.DMA((N_DEV - 1,)), pltpu.SemaphoreType.DMA((N_DEV - 1,)),
                        pltpu.SemaphoreType.DMA] * len(bufs)
        self.args = tuple(bufs)


def _carry_exchange(body, n_in, n_out, grid, ride):
    extra = _Extra(ride)
    if ride is None:
        return body, extra
    n, gather = len(ride[0]), ride[1]

    def carrying(*refs):
        ins, ride_ins = refs[:n_in], refs[n_in:n_in + n]
        outs, ride_outs = refs[n_in + n:n_in + n + n_out], refs[n_in + n + n_out:n_in + 2 * n + n_out]
        sems = refs[n_in + 2 * n + n_out:]
        ids = [pl.program_id(a) for a in range(len(grid))]
        first = functools.reduce(jnp.logical_and, [i == 0 for i in ids])
        last = functools.reduce(jnp.logical_and, [i == size - 1 for i, size in zip(ids, grid)])

        @pl.when(first)
        def _():
            _exchange_plan(ride_ins, ride_outs, sems, gather)[0]()

        body(*ins, *outs)

        @pl.when(last)
        def _():
            _exchange_plan(ride_ins, ride_outs, sems, gather)[1]()

    return carrying, extra


def _exchange_plan(in_refs, out_refs, sems, gather):
    x, y, c, me = _mesh_place()

    def src(b, slot):
        return in_refs[b] if gather else in_refs[b].at[slot]

    def copy(b, k, dst_slot):
        send_sems, recv_sems = sems[SEMS_PER_BUFFER * b], sems[SEMS_PER_BUFFER * b + 1]
        return pltpu.make_async_remote_copy(
            src_ref=src(b, me ^ k), dst_ref=out_refs[b].at[dst_slot],
            send_sem=send_sems.at[k - 1], recv_sem=recv_sems.at[k - 1],
            device_id=_peer(x, y, c, k), device_id_type=pl.DeviceIdType.MESH)

    def mine(b):
        return pltpu.make_async_copy(src(b, me), out_refs[b].at[me], sems[SEMS_PER_BUFFER * b + 2])

    buffers = range(len(in_refs))

    def start():
        for b in buffers:
            mine(b).start()
            for k in range(1, N_DEV):
                copy(b, k, me).start()

    def wait():
        for b in buffers:
            for k in range(1, N_DEV):
                copy(b, k, me ^ k).wait_recv()
        for b in buffers:
            for k in range(1, N_DEV):
                copy(b, k, me).wait_send()
            mine(b).wait()

    return start, wait


def _adamw(parts, w, m, v, *, name, tr):
    n_layers = len(parts)
    rows, cols = parts[0].shape[1:]
    assert rows % tr == 0 and w.shape == (n_layers * rows, cols)
    tiles = rows // tr
    bias1 = 1.0 - ADAM_B1 ** ADAM_STEP
    bias2 = 1.0 - ADAM_B2 ** ADAM_STEP

    def body(*refs):
        p_refs = refs[:n_layers]
        w_ref, m_ref, v_ref, g_ref, d_ref, nm_ref, nv_ref = refs[n_layers:]

        def update(p_ref):
            g = p_ref[0].astype(F32)
            for j in range(1, N_DEV):
                g = g + p_ref[j].astype(F32)
            nm = ADAM_B1 * m_ref[...] + (1.0 - ADAM_B1) * g
            nv = ADAM_B2 * v_ref[...] + (1.0 - ADAM_B2) * (g * g)
            g_ref[...] = g
            nm_ref[...] = nm
            nv_ref[...] = nv
            d_ref[...] = -ADAM_LR * ((nm / bias1) / (jnp.sqrt(nv / bias2) + ADAM_EPS) + ADAM_WD * w_ref[...])

        for layer, p_ref in enumerate(p_refs):
            @pl.when(pl.program_id(0) == layer)
            def _(p_ref=p_ref):
                update(p_ref)

    def part_spec(layer):
        def index(l, i):
            return 0, jnp.where(l < layer, 0, jnp.where(l > layer, tiles - 1, i)), 0
        return pl.BlockSpec((N_DEV, tr, cols), index)

    blk = pl.BlockSpec((tr, cols), lambda l, i: (l * tiles + i, 0))
    out = jax.ShapeDtypeStruct((n_layers * rows, cols), F32)
    return pl.pallas_call(
        body, name=name, grid=(n_layers, tiles),
        in_specs=[part_spec(layer) for layer in range(n_layers)] + [blk, blk, blk],
        out_specs=[blk] * 4, out_shape=[out] * 4, compiler_params=_params(),
    )(*parts, w, m, v)


def _col_blocks(full):
    r, n = full.shape
    return full.reshape(r, N_DEV, n // N_DEV).transpose(1, 0, 2)


def _row_blocks(full):
    r, n = full.shape
    return full.reshape(N_DEV, r // N_DEV, n)


def _from_col_blocks(blocks):
    j, r, c = blocks.shape
    return blocks.transpose(1, 0, 2).reshape(r, j * c)


def _from_row_blocks(blocks):
    j, r, c = blocks.shape
    return blocks.reshape(j * r, c)


SHARD_ROWWISE = {'w_out', 'w_ffn_down'}
EARLY = ('w_ffn_down', 'w_ffn_gate', 'w_ffn_up', 'w_out', 'w_branch_a')
LATE = ('w_branch_b', 'w_branch_c')


def _blocks(name, full):
    return _row_blocks(full) if name in SHARD_ROWWISE else _col_blocks(full)


def _from_blocks(name, blocks):
    return _from_row_blocks(blocks) if name in SHARD_ROWWISE else _from_col_blocks(blocks)


def _pack(shards, names):
    lead = shards[names[0]].shape[:-2]
    return jnp.concatenate([shards[n].reshape(lead + (-1, PACK_COLS)) for n in names], axis=-2)


def _unpack(packed, shapes, names):
    out, off = {}, 0
    lead = packed.shape[:-2]
    for n in names:
        size = shapes[n][0] * shapes[n][1] // PACK_COLS
        out[n] = packed[..., off:off + size, :].reshape(lead + tuple(shapes[n]))
        off += size
    return out


def _pack_small(vals):
    flat = jnp.concatenate([vals[n].reshape(-1) for n in SMALL])
    return flat.reshape(-1, PACK_COLS)


def _unpack_small(packed, shapes):
    flat = packed.reshape(-1)
    out, off = {}, 0
    for n in SMALL:
        size = 1
        for dim in shapes[n]:
            size *= dim
        out[n] = flat[off:off + size].reshape(shapes[n])
        off += size
    return out


def _band_views(qkv, s):
    views = []
    for gi, d in enumerate(B_DILATIONS):
        starts = [A_Q_WIDTH + 2 * A_KV_WIDTH + part * B_WIDTH + gi * B_OUT_WIDTH for part in range(3)]
        group = jnp.concatenate([qkv[:, c0:c0 + B_OUT_WIDTH] for c0 in starts], axis=1)
        views.append(group.reshape(s // d, d * GROUP_QKV))
    return views


def _layer_fwd(x, p, tables, ride=None, after_ride=None):
    s = x.shape[0]
    cos, sin = tables
    h = _rms_fwd(x, p['norm_mix'], name='rms_mix_fwd')
    z = _mm(h, p['w_in'], dims='nn', out_dtype=F32, tm=1024, tn=1664, tk=1024, name='mm_in_fwd')
    gains = _qk_gains(p['qn_a'], p['kn_a'], p['qn_b'], p['kn_b'])
    qkv = _qkrope_fwd(z, gains, cos, sin, name='qkrope_fwd')

    kdup = _dup_halves(qkv[:, A_Q_WIDTH:A_Q_WIDTH + A_KV_WIDTH])
    va = qkv[:, A_Q_WIDTH + A_KV_WIDTH:A_Q_WIDTH + 2 * A_KV_WIDTH]
    vdup = _dup_halves(va)
    oa, lse_a, *gathered = _attn_a_fwd(qkv, kdup, _with_ones(va), ride=ride, name='attn_a_fwd')
    if after_ride is not None:
        p = {**p, **after_ride(gathered)}
    ya = _mm(oa, p['w_branch_a'], dims='nn', out_dtype=BF16, tm=1024, tn=1024, tk=512, name='mm_branch_a_fwd')

    views = _band_views(qkv, s)
    o_g, lse_g = [], []
    for gi, d in enumerate(B_DILATIONS):
        o, lse = _band_fwd(views[gi], gi, d, name=f'band_fwd_d{d}')
        o_g.append(o.reshape(s, B_OUT_WIDTH))
        lse_g.append(lse.reshape(s, B_OUT_WIDTH))
    ob = _merge_fwd(o_g, lse_g, name='merge_fwd')
    yb = _mm(ob, p['w_branch_b'], dims='nn', out_dtype=BF16, tm=1024, tn=1024, tk=256, name='mm_branch_b_fwd')

    pooled, mixed = _pool_fwd(z, p['pool_lin'], p['pool_scale'], name='pool_fwd')
    yc = _mm(mixed, p['w_branch_c'], dims='nn', out_dtype=BF16, tm=1024, tn=1024, tk=512, name='mm_branch_c_fwd')

    merged = _gate_fwd(z, p['b_gate'], ya, yb, yc, name='gate_fwd')
    x_mid = _mm(merged, p['w_out'], dims='nn', out_dtype=F32, tm=1024, tn=1024, tk=1024, res=x, name='mm_out_fwd')

    h2 = _rms_fwd(x_mid, p['norm_ffn'], name='rms_ffn_fwd')
    fa = _mm(h2, p['w_ffn_gate'], dims='nn', out_dtype=BF16, tm=1024, tn=1408, tk=1024, name='mm_ffn_gate_fwd')
    fb = _mm(h2, p['w_ffn_up'], dims='nn', out_dtype=BF16, tm=1024, tn=1408, tk=1024, name='mm_ffn_up_fwd')
    f = _swiglu_fwd(fa, fb, name='swiglu_fwd')
    x_out = _mm(f, p['w_ffn_down'], dims='nn', out_dtype=F32, tm=1024, tn=512, tk=2816, res=x_mid, name='mm_ffn_down_fwd')

    saved = dict(x=x, h=h, z=z, gains=gains, qkv=qkv, kdup=kdup, vdup=vdup, oa=oa, lse_a=lse_a, o_g=o_g, lse_g=lse_g,
                 ob=ob, pooled=pooled, mixed=mixed, ya=ya, yb=yb, yc=yc, merged=merged, x_mid=x_mid, h2=h2,
                 fa=fa, fb=fb, f=f, views=views)
    return x_out, saved, p


def _fold_heads(v, heads):
    return v.reshape(heads, HEAD_DIM).sum(axis=0)


def _layer_bwd(dx, p, sv, tables, make_ride=None):
    s = dx.shape[0]
    cos, sin = tables
    g = {}

    df = _mm(dx, p['w_ffn_down'], dims='nt', out_dtype=BF16, tm=1024, tn=1408, tk=1024, name='mm_ffn_down_dx')
    g['w_ffn_down'] = _mm(sv['f'], dx, dims='tn', out_dtype=BF16, tm=1408, tn=1024, tk=512, name='mm_ffn_down_dw')
    da, db = _swiglu_bwd(sv['fa'], sv['fb'], df, name='swiglu_bwd')
    dh2 = _mm(da, p['w_ffn_gate'], dims='nt', out_dtype=F32, tm=1024, tn=512, tk=2816, name='mm_ffn_gate_dx')
    dh2 = _mm(db, p['w_ffn_up'], dims='nt', out_dtype=F32, tm=1024, tn=512, tk=2816, res=dh2, name='mm_ffn_up_dx')
    g['w_ffn_gate'] = _mm(sv['h2'], da, dims='tn', out_dtype=BF16, tm=1024, tn=2816, tk=512, name='mm_ffn_gate_dw')
    g['w_ffn_up'] = _mm(sv['h2'], db, dims='tn', out_dtype=BF16, tm=1024, tn=2816, tk=512, name='mm_ffn_up_dw')
    dx_mid, g['norm_ffn'] = _rms_bwd(sv['x_mid'], p['norm_ffn'], dh2, dx, name='rms_ffn_bwd')

    dmerged = _mm(dx_mid, p['w_out'], dims='nt', out_dtype=F32, tm=1024, tn=1024, tk=1024, name='mm_out_dx')
    g['w_out'] = _mm(sv['merged'], dx_mid, dims='tn', out_dtype=BF16, tm=1024, tn=1024, tk=512, name='mm_out_dw')
    dya, dyb, dyc, dg0, dg1, dg2, db0, db1, db2 = _gate_bwd(
        sv['z'], p['b_gate'], sv['ya'], sv['yb'], sv['yc'], dmerged, name='gate_bwd')
    g['b_gate'] = jnp.concatenate([db0, db1, db2], axis=1)

    doa = _mm(dya, p['w_branch_a'], dims='nt', out_dtype=BF16, tm=1024, tn=512, tk=1024, name='mm_branch_a_dx')
    g['w_branch_a'] = _mm(sv['oa'], dya, dims='tn', out_dtype=BF16, tm=512, tn=1024, tk=512, name='mm_branch_a_dw')
    ride = None if make_ride is None else make_ride(g)
    dqa, dkdup, dvdup, *arrived = _attn_a_bwd(sv['qkv'], sv['kdup'], sv['vdup'], sv['oa'], sv['lse_a'], doa,
                                              ride=ride, name='attn_a_bwd')

    def fold(dup):
        return jnp.concatenate([dup[:, 0:64] + dup[:, 64:128], dup[:, 128:192] + dup[:, 192:256]], axis=1)

    dka, dva = fold(dkdup), fold(dvdup)

    dob = _mm(dyb, p['w_branch_b'], dims='nt', out_dtype=F32, tm=1024, tn=256, tk=1024, name='mm_branch_b_dx')
    g['w_branch_b'] = _mm(sv['ob'], dyb, dims='tn', out_dtype=BF16, tm=256, tn=1024, tk=512, name='mm_branch_b_dw')
    merged_b = _merge_bwd(sv['o_g'], sv['lse_g'], dob, name='merge_bwd')
    do_g, dd_g = merged_b[:3], merged_b[3:]
    views = sv['views']
    dq_parts, dk_parts, dv_parts = [], [], []
    for gi, d in enumerate(B_DILATIONS):
        ln = s // d
        dq, dk, dv = _band_bwd(views[gi], do_g[gi].reshape(ln, d * B_OUT_WIDTH),
                               sv['lse_g'][gi].reshape(ln, d * B_OUT_WIDTH), dd_g[gi].reshape(ln, d * B_OUT_WIDTH),
                               gi, d, name=f'band_bwd_d{d}')
        dq_parts.append(dq.reshape(s, B_OUT_WIDTH))
        dk_parts.append(dk.reshape(s, B_OUT_WIDTH))
        dv_parts.append(dv.reshape(s, B_OUT_WIDTH))

    dmixed = _mm(dyc, p['w_branch_c'], dims='nt', out_dtype=F32, tm=1024, tn=512, tk=1024, name='mm_branch_c_dx')
    g['w_branch_c'] = _mm(sv['mixed'], dyc, dims='tn', out_dtype=BF16, tm=512, tn=1024, tk=512, name='mm_branch_c_dw')
    du, g['pool_lin'], g['pool_scale'] = _pool_bwd(dmixed, sv['pooled'], p['pool_lin'], p['pool_scale'], name='pool_bwd')

    dqkv = jnp.concatenate([part.astype(BF16) for part in [dqa, dka, dva] + dq_parts + dk_parts + dv_parts], axis=1)
    dz_qkv, dgains = _qkrope_bwd(sv['z'], sv['gains'], cos, sin, dqkv, name='qkrope_bwd')
    dgains = jnp.sum(dgains, axis=0)
    g['qn_a'] = _fold_heads(dgains[0:512], 8)
    g['kn_a'] = _fold_heads(dgains[512:640], 2)
    g['qn_b'] = _fold_heads(dgains[768:1536], 12)
    g['kn_b'] = _fold_heads(dgains[1536:2304], 12)

    dz = jnp.concatenate([dz_qkv, du, dg0, dg1, dg2], axis=1)
    dh = _mm(dz, p['w_in'], dims='nt', out_dtype=F32, tm=1024, tn=1024, tk=1664, name='mm_in_dx')
    g['w_in'] = _mm(sv['h'], dz, dims='tn', out_dtype=BF16, tm=1024, tn=3328, tk=512, name='mm_in_dw')
    dx_in, g['norm_mix'] = _rms_bwd(sv['x'], p['norm_mix'], dh, dx_mid, name='rms_mix_bwd')
    return dx_in, g, arrived


def _small_views(vals, l):
    return {
        'norm_mix': vals['norm_mix'][l][None, :], 'b_gate': vals['b_gate'][l][None, :],
        'qn_a': vals['qn_a'][l], 'kn_a': vals['kn_a'][l], 'qn_b': vals['qn_b'][l], 'kn_b': vals['kn_b'][l],
        'pool_lin': vals['pool_lin'][l], 'pool_scale': vals['pool_scale'][l][None, :],
        'norm_ffn': vals['norm_ffn'][l][None, :],
    }


def kernel(x, norm_mix, w_in, b_gate, qn_a, kn_a, qn_b, kn_b, pool_lin, pool_scale, w_branch_a, w_branch_b, w_branch_c, w_out, norm_ffn, w_ffn_gate, w_ffn_up, w_ffn_down, loss_target, m_norm_mix, m_w_in, m_b_gate, m_qn_a, m_kn_a, m_qn_b, m_kn_b, m_pool_lin, m_pool_scale, m_w_branch_a, m_w_branch_b, m_w_branch_c, m_w_out, m_norm_ffn, m_w_ffn_gate, m_w_ffn_up, m_w_ffn_down, v_norm_mix, v_w_in, v_b_gate, v_qn_a, v_kn_a, v_qn_b, v_kn_b, v_pool_lin, v_pool_scale, v_w_branch_a, v_w_branch_b, v_w_branch_c, v_w_out, v_norm_ffn, v_w_ffn_gate, v_w_ffn_up, v_w_ffn_down):
    w = dict(norm_mix=norm_mix, w_in=w_in, b_gate=b_gate, qn_a=qn_a, kn_a=kn_a, qn_b=qn_b, kn_b=kn_b,
             pool_lin=pool_lin, pool_scale=pool_scale, w_branch_a=w_branch_a, w_branch_b=w_branch_b,
             w_branch_c=w_branch_c, w_out=w_out, norm_ffn=norm_ffn, w_ffn_gate=w_ffn_gate, w_ffn_up=w_ffn_up,
             w_ffn_down=w_ffn_down)
    m = dict(norm_mix=m_norm_mix, w_in=m_w_in, b_gate=m_b_gate, qn_a=m_qn_a, kn_a=m_kn_a, qn_b=m_qn_b, kn_b=m_kn_b,
             pool_lin=m_pool_lin, pool_scale=m_pool_scale, w_branch_a=m_w_branch_a, w_branch_b=m_w_branch_b,
             w_branch_c=m_w_branch_c, w_out=m_w_out, norm_ffn=m_norm_ffn, w_ffn_gate=m_w_ffn_gate,
             w_ffn_up=m_w_ffn_up, w_ffn_down=m_w_ffn_down)
    v = dict(norm_mix=v_norm_mix, w_in=v_w_in, b_gate=v_b_gate, qn_a=v_qn_a, kn_a=v_kn_a, qn_b=v_qn_b, kn_b=v_kn_b,
             pool_lin=v_pool_lin, pool_scale=v_pool_scale, w_branch_a=v_w_branch_a, w_branch_b=v_w_branch_b,
             w_branch_c=v_w_branch_c, w_out=v_w_out, norm_ffn=v_norm_ffn, w_ffn_gate=v_w_ffn_gate,
             w_ffn_up=v_w_ffn_up, w_ffn_down=v_w_ffn_down)
    depth = w_in.shape[0]
    shard_shapes = {n: w[n].shape[1:] for n in SHARDED}
    small_shapes = {n: w[n].shape for n in SMALL}

    def my_shards(l, names):
        return _pack({n: w[n][l].astype(BF16) for n in names}, names)

    def full_weights(gathered, names):
        blocks = _unpack(gathered, shard_shapes, names)
        return {n: _from_blocks(n, blocks[n]) for n in names}

    def blocks_to_send(grad, names):
        return _pack({n: _blocks(n, grad[n]) for n in names}, names)

    tables = _rope_tables(x.shape[1])
    (w_in_blocks,) = _exchange([w['w_in'][0].astype(BF16)], gather=True, name='gather_weights')
    layers, saved = [], []
    act = x[0]
    for l in range(depth):
        p = {'w_in': _from_col_blocks(w_in_blocks), **_small_views(w, l)}
        bufs = [my_shards(l, EARLY), my_shards(l, LATE)]
        if l + 1 < depth:
            bufs.append(w['w_in'][l + 1].astype(BF16))
        stash = {}

        def after_ride(gathered, stash=stash):
            stash['next_w_in'] = gathered[2] if len(gathered) > 2 else None
            return {**full_weights(gathered[0], EARLY), **full_weights(gathered[1], LATE)}

        act, sv, p = _layer_fwd(act, p, tables, ride=(bufs, True), after_ride=after_ride)
        w_in_blocks = stash['next_w_in']
        layers.append(p)
        saved.append(sv)
    part, dx = _loss_head(act, loss_target[0], name='loss_head')
    loss = lax.psum(jnp.sum(part), ('x', 'y', 'c'))

    early_parts, late_parts, w_in_parts = ([None] * depth for _ in range(3))
    grads = [None] * depth
    pending = []
    for l in reversed(range(depth)):
        def make_ride(g, pending=pending):
            return [blocks_to_send(g, EARLY)] + pending, False

        dx, grads[l], arrived = _layer_bwd(dx, layers[l], saved[l], tables, make_ride=make_ride)
        early_parts[l] = arrived[0]
        if pending:
            w_in_parts[l + 1], late_parts[l + 1] = arrived[1], arrived[2]
        pending = [_col_blocks(grads[l]['w_in']), blocks_to_send(grads[l], LATE)]
    w_in_parts[0], late_parts[0] = _exchange(pending, gather=False, name='scatter_grads')
    grad_x = dx

    new = {}
    w_in_shape = w['w_in'].shape
    res = _adamw(w_in_parts, *(t['w_in'].reshape(-1, w_in_shape[-1]) for t in (w, m, v)), name='adamw_w_in', tr=128)
    new['w_in'] = tuple(r.reshape(w_in_shape) for r in res)
    for names, parts, label in ((EARLY, early_parts, 'adamw_early'), (LATE, late_parts, 'adamw_late')):
        packed = [_pack({n: t[n] for n in names}, names).reshape(-1, PACK_COLS) for t in (w, m, v)]
        res = _adamw(parts, *packed, name=label, tr=96)
        unpacked = [_unpack(r.reshape(depth, -1, PACK_COLS), shard_shapes, names) for r in res]
        for n in names:
            new[n] = tuple(u[n] for u in unpacked)

    small_grad = {n: jnp.stack([grads[l][n].reshape(small_shapes[n][1:]) for l in range(depth)]) for n in SMALL}
    (small_parts,) = _exchange([_pack_small(small_grad)], gather=True, name='gather_small_grads')
    res = _adamw([small_parts], _pack_small({n: w[n] for n in SMALL}), _pack_small({n: m[n] for n in SMALL}),
                 _pack_small({n: v[n] for n in SMALL}), name='adamw_small', tr=small_parts.shape[1])
    unpacked = [_unpack_small(r, small_shapes) for r in res]
    for n in SMALL:
        new[n] = tuple(u[n] for u in unpacked)

    outs = [loss, grad_x[None]]
    for idx in range(4):
        outs.extend(new[n][idx] for n in WEIGHTS)
    return tuple(outs)
```

```python
import functools

import jax
import jax.numpy as jnp
from jax import lax
from jax.experimental import pallas as pl
from jax.experimental.pallas import tpu as pltpu

F32 = jnp.float32
BF16 = jnp.bfloat16

N_DEV = 8
D_MODEL = 1024
DEPTH = 4
HEAD_DIM = 64
LANES = 128
A_Q_WIDTH = 512
A_KV_WIDTH = 128
B_WIDTH = 768
B_GROUPS = 3
B_DILATIONS = (1, 4, 16)
B_HALF_SPAN = 64
B_OUT_WIDTH = 256
POOL_WIDTH = 512
POOL_HALF = (1, 2, 4, 8)
GATE_WIDTH = 3072
QKV_WIDTH = A_Q_WIDTH + 2 * A_KV_WIDTH + 3 * B_WIDTH
IN_WIDTH = QKV_WIDTH + POOL_WIDTH + GATE_WIDTH
D_FF = 2816
GRID_W = 64
ROPE_THETA = 10000.0
EPS = 1e-6
NEG_INF = -1e30
ATTN_SCALE = HEAD_DIM ** -0.5

ADAM_LR = 0.001
ADAM_B1 = 0.9
ADAM_B2 = 0.999
ADAM_EPS = 1e-08
ADAM_WD = 0.01
ADAM_STEP = 10

PACK_COLS = 1024
VMEM_LIMIT = 56 * 1024 * 1024

SHARDED = ('w_in', 'w_branch_a', 'w_branch_b', 'w_branch_c', 'w_out', 'w_ffn_gate', 'w_ffn_up', 'w_ffn_down')
SMALL = ('norm_mix', 'b_gate', 'qn_a', 'kn_a', 'qn_b', 'kn_b', 'pool_lin', 'pool_scale', 'norm_ffn')
WEIGHTS = ('norm_mix', 'w_in', 'b_gate', 'qn_a', 'kn_a', 'qn_b', 'kn_b', 'pool_lin', 'pool_scale',
           'w_branch_a', 'w_branch_b', 'w_branch_c', 'w_out', 'norm_ffn', 'w_ffn_gate', 'w_ffn_up', 'w_ffn_down')

NN = (((1,), (0,)), ((), ()))
NT = (((1,), (1,)), ((), ()))
TN = (((0,), (0,)), ((), ()))


def _params(vmem=None):
    return pltpu.CompilerParams(vmem_limit_bytes=VMEM_LIMIT if vmem is None else vmem)


def _lane_iota(n=LANES):
    return lax.broadcasted_iota(jnp.int32, (1, n), 1)


def _swap(x, sh, lane):
    n = x.shape[-1]
    down = pltpu.roll(x, sh, axis=1)
    up = pltpu.roll(x, n - sh, axis=1)
    return jnp.where((lane & sh) == 0, up, down)


def _head_sum(v):
    w = v.shape[-1]
    r = lax.broadcasted_iota(jnp.int32, (w, w), 0) // HEAD_DIM
    c = lax.broadcasted_iota(jnp.int32, (w, w), 1) // HEAD_DIM
    ones = (r == c).astype(BF16)
    hi = v.astype(BF16)
    lo = (v - hi.astype(F32)).astype(BF16)
    return (lax.dot_general(hi, ones, NN, preferred_element_type=F32)
            + lax.dot_general(lo, ones, NN, preferred_element_type=F32))


def _mm(a, b, *, dims, out_dtype, tm, tn, tk, name, res=None):
    if dims == 'nn':
        (m, k), n = a.shape, b.shape[1]
    elif dims == 'nt':
        (m, k), n = a.shape, b.shape[0]
    else:
        (k, m), n = a.shape, b.shape[1]
    tm, tn, tk = min(tm, m), min(tn, n), min(tk, k)
    assert m % tm == 0 and n % tn == 0 and k % tk == 0, (name, m, n, k, tm, tn, tk)
    nk = k // tk
    if dims == 'tn':
        a_spec = pl.BlockSpec((tk, tm), lambda i, j, kk: (kk, i))
    else:
        a_spec = pl.BlockSpec((tm, tk), lambda i, j, kk: (i, kk))
    if dims == 'nt':
        b_spec = pl.BlockSpec((tn, tk), lambda i, j, kk: (j, kk))
    else:
        b_spec = pl.BlockSpec((tk, tn), lambda i, j, kk: (kk, j))
    o_spec = pl.BlockSpec((tm, tn), lambda i, j, kk: (i, j))
    dn = {'nn': NN, 'nt': NT, 'tn': TN}[dims]
    has_res = res is not None

    def body(*refs):
        if has_res:
            a_ref, b_ref, r_ref, o_ref, acc_ref = refs
        else:
            a_ref, b_ref, o_ref, acc_ref = refs
        prod = lax.dot_general(a_ref[...].astype(BF16), b_ref[...].astype(BF16), dn,
                               preferred_element_type=F32)

        def finish(total):
            if has_res:
                total = total + r_ref[...]
            o_ref[...] = total.astype(out_dtype)

        if nk == 1:
            finish(prod)
        else:
            kk = pl.program_id(2)

            @pl.when(kk == 0)
            def _():
                acc_ref[...] = prod

            @pl.when(kk > 0)
            def _():
                acc_ref[...] += prod

            @pl.when(kk == nk - 1)
            def _():
                finish(acc_ref[...])

    in_specs = [a_spec, b_spec] + ([o_spec] if has_res else [])
    args = (a, b) + ((res,) if has_res else ())
    acc_shape = (tm, tn) if nk > 1 else (8, LANES)
    return pl.pallas_call(
        body, name=name, grid=(m // tm, n // tn, nk),
        in_specs=in_specs, out_specs=o_spec,
        out_shape=jax.ShapeDtypeStruct((m, n), out_dtype),
        scratch_shapes=[pltpu.VMEM(acc_shape, F32)],
        compiler_params=_params(),
    )(*args)


def _rms_fwd(x, g, *, name, tm=512):
    s, d = x.shape

    def body(x_ref, g_ref, h_ref):
        xv = x_ref[...]
        rstd = lax.rsqrt(jnp.mean(xv * xv, axis=-1, keepdims=True) + EPS)
        h_ref[...] = (xv * rstd * g_ref[...]).astype(BF16)

    return pl.pallas_call(
        body, name=name, grid=(s // tm,),
        in_specs=[pl.BlockSpec((tm, d), lambda i: (i, 0)), pl.BlockSpec((1, d), lambda i: (0, 0))],
        out_specs=pl.BlockSpec((tm, d), lambda i: (i, 0)),
        out_shape=jax.ShapeDtypeStruct((s, d), BF16),
        compiler_params=_params(),
    )(x, g)


def _rms_bwd(x, g, dh, dres, *, name, tm=512):
    s, d = x.shape

    def body(x_ref, g_ref, dh_ref, dres_ref, dx_ref, dg_ref):
        i = pl.program_id(0)
        xv = x_ref[...]
        rstd = lax.rsqrt(jnp.mean(xv * xv, axis=-1, keepdims=True) + EPS)
        xhat = xv * rstd
        dhv = dh_ref[...]
        dxhat = dhv * g_ref[...]
        proj = jnp.mean(dxhat * xhat, axis=-1, keepdims=True)
        dx_ref[...] = dres_ref[...] + rstd * (dxhat - xhat * proj)
        part = jnp.sum(dhv * xhat, axis=0, keepdims=True)

        @pl.when(i == 0)
        def _():
            dg_ref[...] = part

        @pl.when(i > 0)
        def _():
            dg_ref[...] += part

    row = pl.BlockSpec((tm, d), lambda i: (i, 0))
    vec = pl.BlockSpec((1, d), lambda i: (0, 0))
    return pl.pallas_call(
        body, name=name, grid=(s // tm,),
        in_specs=[row, vec, row, row], out_specs=[row, vec],
        out_shape=[jax.ShapeDtypeStruct((s, d), F32), jax.ShapeDtypeStruct((1, d), F32)],
        compiler_params=_params(),
    )(x, g, dh, dres)


N_QKV_BLOCKS = QKV_WIDTH // LANES
A_BLOCKS = (A_Q_WIDTH + 2 * A_KV_WIDTH) // LANES
V_A_BLOCK = A_BLOCKS - 1
V_B_FIRST = A_BLOCKS + 2 * (B_WIDTH // LANES)


def _qk_kind(j):
    return jnp.where(j < A_BLOCKS, 0, 1)


def _is_v_block(j):
    return (j == V_A_BLOCK) | (j >= V_B_FIRST)


def _rope_tables(s):
    def ang(pos, dim):
        inv = ROPE_THETA ** (-jnp.arange(0, dim, 2, dtype=F32) / dim)
        return pos.astype(F32)[:, None] * inv[None, :]
    t = jnp.arange(s)
    a_row = ang(t // GRID_W, HEAD_DIM // 2)
    a_col = ang(t % GRID_W, HEAD_DIM // 2)
    a_seq = ang(t, HEAD_DIM)
    cos_a = jnp.concatenate([jnp.cos(a_row)] * 2 + [jnp.cos(a_col)] * 2, axis=-1)
    sin_a = jnp.concatenate([-jnp.sin(a_row), jnp.sin(a_row), -jnp.sin(a_col), jnp.sin(a_col)], axis=-1)
    cos_b = jnp.concatenate([jnp.cos(a_seq)] * 2, axis=-1)
    sin_b = jnp.concatenate([-jnp.sin(a_seq), jnp.sin(a_seq)], axis=-1)
    cos = jnp.stack([jnp.tile(cos_a, (1, 2)), jnp.tile(cos_b, (1, 2))])
    sin = jnp.stack([jnp.tile(sin_a, (1, 2)), jnp.tile(sin_b, (1, 2))])
    return cos, sin


def _qk_gains(qn_a, kn_a, qn_b, kn_b):
    one = jnp.ones((HEAD_DIM,), F32)
    parts = [jnp.tile(qn_a, 8), jnp.tile(kn_a, 2), jnp.tile(one, 2),
             jnp.tile(qn_b, 12), jnp.tile(kn_b, 12), jnp.tile(one, 12)]
    return jnp.concatenate(parts)[None, :]


def _qkrope_fwd(z, gains, cos, sin, *, name, tm=2048):
    s = z.shape[0]
    tm = min(tm, s)

    def body(z_ref, g_ref, c_ref, s_ref, o_ref):
        j = pl.program_id(1)
        lane = _lane_iota()
        xv = z_ref[...].astype(F32)

        def normed_rope(pair):
            ms = _head_sum(xv * xv) * (1.0 / HEAD_DIM)
            n = xv * lax.rsqrt(ms + EPS) * g_ref[...]
            return n * c_ref[...] + _swap(n, pair, lane) * s_ref[...]

        @pl.when(_is_v_block(j))
        def _():
            o_ref[...] = xv.astype(BF16)

        @pl.when(jnp.logical_not(_is_v_block(j)) & (j < A_BLOCKS))
        def _():
            o_ref[...] = normed_rope(HEAD_DIM // 4).astype(BF16)

        @pl.when(jnp.logical_not(_is_v_block(j)) & (j >= A_BLOCKS))
        def _():
            o_ref[...] = normed_rope(HEAD_DIM // 2).astype(BF16)

    tab = pl.BlockSpec((None, tm, LANES), lambda i, j: (_qk_kind(j), i, 0))
    blk = pl.BlockSpec((tm, LANES), lambda i, j: (i, j))
    return pl.pallas_call(
        body, name=name, grid=(s // tm, N_QKV_BLOCKS),
        in_specs=[blk, pl.BlockSpec((1, LANES), lambda i, j: (0, j)), tab, tab],
        out_specs=blk,
        out_shape=jax.ShapeDtypeStruct((s, QKV_WIDTH), BF16),
        compiler_params=_params(),
    )(z, gains, cos, sin)


SUBLANES = 8


def _qkrope_bwd(z, gains, cos, sin, sources, *, name, tm=2048):
    s = z.shape[0]
    tm = min(tm, s)
    n_src = len(sources)
    widths = [a.shape[1] // LANES for a in sources]
    firsts = [sum(widths[:k]) for k in range(n_src)]
    assert sum(widths) == N_QKV_BLOCKS

    def body(z_ref, g_ref, c_ref, s_ref, *rest):
        src_refs, (dz_ref, dg_ref, dy_ref) = rest[:n_src], rest[n_src:]
        j = pl.program_id(1)
        lane = _lane_iota()
        xv = z_ref[...].astype(F32)
        for src_ref, first, width in zip(src_refs, firsts, widths):
            @pl.when((j >= first) & (j < first + width))
            def _(src_ref=src_ref):
                dy_ref[...] = src_ref[...].astype(F32)
        dy = dy_ref[...]

        def back(pair):
            ms = _head_sum(xv * xv) * (1.0 / HEAD_DIM)
            rstd = lax.rsqrt(ms + EPS)
            xhat = xv * rstd
            dn = dy * c_ref[...] + _swap(dy * s_ref[...], pair, lane)
            dg_ref[...] = jnp.sum((dn * xhat).reshape(tm // SUBLANES, SUBLANES, LANES), axis=0)
            dxhat = dn * g_ref[...]
            proj = _head_sum(dxhat * xhat) * (1.0 / HEAD_DIM)
            dz_ref[...] = (rstd * (dxhat - xhat * proj)).astype(BF16)

        @pl.when(_is_v_block(j))
        def _():
            dz_ref[...] = dy.astype(BF16)
            dg_ref[...] = jnp.zeros_like(dg_ref)

        @pl.when(jnp.logical_not(_is_v_block(j)) & (j < A_BLOCKS))
        def _():
            back(HEAD_DIM // 4)

        @pl.when(jnp.logical_not(_is_v_block(j)) & (j >= A_BLOCKS))
        def _():
            back(HEAD_DIM // 2)

    def src_spec(first, width):
        return pl.BlockSpec((tm, LANES), lambda i, j: (i, jnp.clip(j - first, 0, width - 1)))

    tab = pl.BlockSpec((None, tm, LANES), lambda i, j: (_qk_kind(j), i, 0))
    blk = pl.BlockSpec((tm, LANES), lambda i, j: (i, j))
    return pl.pallas_call(
        body, name=name, grid=(s // tm, N_QKV_BLOCKS),
        in_specs=[blk, pl.BlockSpec((1, LANES), lambda i, j: (0, j)), tab, tab]
        + [src_spec(first, width) for first, width in zip(firsts, widths)],
        out_specs=[blk, pl.BlockSpec((SUBLANES, LANES), lambda i, j: (i, j))],
        out_shape=[jax.ShapeDtypeStruct((s, QKV_WIDTH), BF16),
                   jax.ShapeDtypeStruct((s // tm * SUBLANES, QKV_WIDTH), F32)],
        scratch_shapes=[pltpu.VMEM((tm, LANES), F32)],
        compiler_params=_params(),
    )(z, gains, cos, sin, *sources)


def _dup_halves(kv):
    h0, h1 = kv[:, :HEAD_DIM], kv[:, HEAD_DIM:]
    return jnp.concatenate([h0, h0, h1, h1], axis=1)


def _with_ones(kv):
    h0, h1 = kv[:, :HEAD_DIM], kv[:, HEAD_DIM:]
    one = jnp.ones_like(h0)
    return jnp.concatenate([h0, one, one, h0, h1, one, one, h1], axis=1)


def _attn_a_fwd(qkv, kdup, vones, *, name, ride=None, tq=256, tk=512, unroll=8):
    s = qkv.shape[0]
    tq, tk = min(tq, s), min(tk, s)
    n_chunks = s // tk
    unroll = min(unroll, n_chunks)
    assert n_chunks % unroll == 0

    def body(q_ref, k_ref, v_ref, o_ref, lse_ref):
        lane = _lane_iota()
        low = lane < HEAD_DIM
        q = q_ref[...]
        zero = jnp.zeros_like(q)
        qm = [jnp.where(low, q, zero) * ATTN_SCALE, jnp.where(low, zero, q) * ATTN_SCALE]

        def chunks(c, carry):
            state = list(carry)
            scs = []
            for u in range(unroll):
                off = pl.multiple_of((c * unroll + u) * tk, tk)
                kc = k_ref[pl.ds(off, tk), :]
                scs.append([lax.dot_general(qm[e], kc, NT, preferred_element_type=F32) for e in range(2)])
            for u in range(unroll):
                off = pl.multiple_of((c * unroll + u) * tk, tk)
                for e in range(2):
                    m, acc = state[2 * e], state[2 * e + 1]
                    ve = v_ref[pl.ds(off, tk), e * LANES:(e + 1) * LANES]
                    m_new = jnp.maximum(m, jnp.max(scs[u][e], axis=1, keepdims=True))
                    alpha = jnp.exp(m - m_new)
                    p = jnp.exp(scs[u][e] - m_new).astype(BF16)
                    state[2 * e] = m_new
                    state[2 * e + 1] = alpha * acc + lax.dot_general(p, ve, NN, preferred_element_type=F32)
            return tuple(state)

        m_init = jnp.full((tq, 1), NEG_INF, F32)
        a_init = jnp.zeros((tq, LANES), F32)
        m0, a0, m1, a1 = lax.fori_loop(0, n_chunks // unroll, chunks, (m_init, a_init, m_init, a_init))
        l0 = pltpu.roll(a0, HEAD_DIM, axis=1)
        l1 = pltpu.roll(a1, HEAD_DIM, axis=1)
        o_ref[...] = jnp.where(low, a0 / l0, a1 / l1).astype(BF16)
        lse_ref[...] = jnp.where(low, m0 + jnp.log(l0), m1 + jnp.log(l1))

    q_spec = pl.BlockSpec((tq, LANES), lambda hb, qi: (qi, hb))
    k_spec = pl.BlockSpec((s, LANES), lambda hb, qi: (0, hb // 2))
    v_spec = pl.BlockSpec((s, 2 * LANES), lambda hb, qi: (0, hb // 2))
    grid = (A_Q_WIDTH // LANES, s // tq)
    body, extra = _carry_exchange(body, 3, 2, grid, ride)
    return pl.pallas_call(
        body, name=name, grid=grid,
        in_specs=[q_spec, k_spec, v_spec] + extra.in_specs, out_specs=[q_spec, q_spec] + extra.out_specs,
        out_shape=[jax.ShapeDtypeStruct((s, A_Q_WIDTH), BF16), jax.ShapeDtypeStruct((s, A_Q_WIDTH), F32)]
        + extra.out_shape,
        scratch_shapes=extra.scratch, compiler_params=_params(),
    )(qkv, kdup, vones, *extra.args)


def _attn_a_bwd(qkv, kdup, vdup, o, lse, do, *, name, ride=None, tq=512, tk=512, unroll=2):
    s = qkv.shape[0]
    tq, tk = min(tq, s), min(tk, s)
    n_chunks = s // tk
    unroll = min(unroll, n_chunks)
    assert n_chunks % unroll == 0

    def body(q_ref, k_ref, v_ref, o_ref, lse_ref, do_ref, dq_ref, dk_ref, dv_ref):
        first = (pl.program_id(1) == 0) & (pl.program_id(2) == 0)

        @pl.when(first)
        def _():
            dk_ref[...] = jnp.zeros_like(dk_ref)
            dv_ref[...] = jnp.zeros_like(dv_ref)

        lane = _lane_iota()
        low = lane < HEAD_DIM
        q = q_ref[...]
        dov = do_ref[...]
        zero = jnp.zeros_like(q)
        prod = dov.astype(F32) * o_ref[...].astype(F32)
        lsev = lse_ref[...]
        qs = [jnp.where(low, q, zero) * ATTN_SCALE, jnp.where(low, zero, q) * ATTN_SCALE]
        dom = [jnp.where(low, dov, zero), jnp.where(low, zero, dov)]
        delta = [jnp.sum(jnp.where(low, prod, 0.0), axis=1, keepdims=True),
                 jnp.sum(jnp.where(low, 0.0, prod), axis=1, keepdims=True)]
        lse = [lsev[:, 0:1], lsev[:, HEAD_DIM:HEAD_DIM + 1]]
        qs_both = jnp.concatenate(qs, axis=0)
        dom_both = jnp.concatenate(dom, axis=0)

        def chunks(c, carry):
            dqs = list(carry)
            for u in range(unroll):
                off = pl.multiple_of((c * unroll + u) * tk, tk)
                kc = k_ref[pl.ds(off, tk), :]
                vc = v_ref[pl.ds(off, tk), :]
                ps, dss = [], []
                for e in range(2):
                    sc = lax.dot_general(qs[e], kc, NT, preferred_element_type=F32)
                    p = jnp.exp(sc - lse[e])
                    dp = lax.dot_general(dom[e], vc, NT, preferred_element_type=F32)
                    ds = (p * (dp - delta[e])).astype(BF16)
                    ps.append(p.astype(BF16))
                    dss.append(ds)
                    dqs[e] = dqs[e] + lax.dot_general(ds, kc, NN, preferred_element_type=F32)
                dv_ref[pl.ds(off, tk), :] += lax.dot_general(jnp.concatenate(ps, axis=0), dom_both, TN,
                                                             preferred_element_type=F32)
                dk_ref[pl.ds(off, tk), :] += lax.dot_general(jnp.concatenate(dss, axis=0), qs_both, TN,
                                                             preferred_element_type=F32)
            return tuple(dqs)

        dq_init = jnp.zeros((tq, LANES), F32)
        dq0, dq1 = lax.fori_loop(0, n_chunks // unroll, chunks, (dq_init, dq_init))
        dq_ref[...] = (jnp.where(low, dq0, dq1) * ATTN_SCALE).astype(BF16)

    q_spec = pl.BlockSpec((tq, LANES), lambda kvh, hb, qi: (qi, kvh * 2 + hb))
    kv_spec = pl.BlockSpec((s, LANES), lambda kvh, hb, qi: (0, kvh))
    grid = (2, 2, s // tq)
    body, extra = _carry_exchange(body, 6, 3, grid, ride)
    return pl.pallas_call(
        body, name=name, grid=grid,
        in_specs=[q_spec, kv_spec, kv_spec, q_spec, q_spec, q_spec] + extra.in_specs,
        out_specs=[q_spec, kv_spec, kv_spec] + extra.out_specs,
        out_shape=[jax.ShapeDtypeStruct((s, A_Q_WIDTH), BF16),
                   jax.ShapeDtypeStruct((s, 2 * LANES), F32), jax.ShapeDtypeStruct((s, 2 * LANES), F32)]
        + extra.out_shape,
        scratch_shapes=extra.scratch, compiler_params=_params(),
    )(qkv, kdup, vdup, o, lse, do, *extra.args)


BAND_Q = 128
GROUP_QKV = 3 * B_OUT_WIDTH
TOKEN_BLOCKS = GROUP_QKV // LANES
B_Q_BLOCK0 = 0
B_K_BLOCK0 = B_OUT_WIDTH // LANES
B_V_BLOCK0 = 2 * (B_OUT_WIDTH // LANES)


BAND_UNROLL = 4


def _unrolled_loop(n, step):
    unroll = BAND_UNROLL if n % BAND_UNROLL == 0 else 1

    def body(it, carry):
        for u in range(unroll):
            step(it * unroll + u)
        return carry

    lax.fori_loop(0, n // unroll, body, 0)


def _band_geometry(length):
    seg = min(length, 2048)
    win = min(2 * BAND_Q, length)
    return seg, win


def _band_window(qs, length, win):
    st = jnp.clip(qs - B_HALF_SPAN, 0, length - win)
    st = pl.multiple_of(st, B_HALF_SPAN)
    qpos = qs + lax.broadcasted_iota(jnp.int32, (BAND_Q, 1), 0)
    kpos = st + lax.broadcasted_iota(jnp.int32, (1, win), 1)
    return st, jnp.abs(qpos - kpos) <= B_HALF_SPAN


def _band_fwd(qkv_view, gi, dil, *, name):
    length = qkv_view.shape[0]
    seg, win = _band_geometry(length)
    n_sub = seg // BAND_Q

    def body(q_ref, k_ref, v_ref, o_ref, lse_ref):
        seg_i = pl.program_id(2)
        lane = _lane_iota()
        low = lane < HEAD_DIM

        def one(i):
            ql = pl.multiple_of(i * BAND_Q, BAND_Q)
            st, valid = _band_window(seg_i * seg + ql, length, win)
            q = q_ref[pl.ds(ql, BAND_Q), :]
            kw = k_ref[pl.ds(st, win), :]
            vw = v_ref[pl.ds(st, win), :]
            outs, lses = [], []
            for e in range(2):
                mine = (lane >= HEAD_DIM) if e else (lane < HEAD_DIM)
                qm = jnp.where(mine, q, jnp.zeros_like(q)) * ATTN_SCALE
                sc = lax.dot_general(qm, kw, NT, preferred_element_type=F32)
                sc = jnp.where(valid, sc, NEG_INF)
                m = jnp.max(sc, axis=1, keepdims=True)
                p = jnp.exp(sc - m)
                l = jnp.sum(p, axis=1, keepdims=True)
                outs.append(lax.dot_general(p.astype(BF16), vw, NN, preferred_element_type=F32) / l)
                lses.append(m + jnp.log(l))
            o_ref[pl.ds(ql, BAND_Q), :] = jnp.where(low, outs[0], outs[1]).astype(BF16)
            lse_ref[pl.ds(ql, BAND_Q), :] = jnp.where(low, lses[0], lses[1])

        _unrolled_loop(n_sub, one)

    def col(base):
        return lambda r, hp, sg: (0, r * TOKEN_BLOCKS + base + hp)

    q_spec = pl.BlockSpec((seg, LANES), lambda r, hp, sg: (sg, r * TOKEN_BLOCKS + B_Q_BLOCK0 + hp))
    out_spec = pl.BlockSpec((seg, LANES), lambda r, hp, sg: (sg, r * 2 + hp))
    return pl.pallas_call(
        body, name=name, grid=(dil, 2, length // seg),
        in_specs=[q_spec, pl.BlockSpec((length, LANES), col(B_K_BLOCK0)), pl.BlockSpec((length, LANES), col(B_V_BLOCK0))],
        out_specs=[out_spec, out_spec],
        out_shape=[jax.ShapeDtypeStruct((length, dil * B_OUT_WIDTH), BF16),
                   jax.ShapeDtypeStruct((length, dil * B_OUT_WIDTH), F32)],
        compiler_params=_params(),
    )(qkv_view, qkv_view, qkv_view)


def _band_bwd(qkv_view, do, lse, dd, gi, dil, *, name):
    length = qkv_view.shape[0]
    seg, win = _band_geometry(length)
    n_sub = seg // BAND_Q

    def body(q_ref, k_ref, v_ref, do_ref, lse_ref, dd_ref, dq_ref, dk_ref, dv_ref):
        seg_i = pl.program_id(2)
        lane = _lane_iota()

        @pl.when(seg_i == 0)
        def _():
            dk_ref[...] = jnp.zeros_like(dk_ref)
            dv_ref[...] = jnp.zeros_like(dv_ref)

        def one(i):
            ql = pl.multiple_of(i * BAND_Q, BAND_Q)
            st, valid = _band_window(seg_i * seg + ql, length, win)
            q = q_ref[pl.ds(ql, BAND_Q), :]
            dov = do_ref[pl.ds(ql, BAND_Q), :]
            lsev = lse_ref[pl.ds(ql, BAND_Q), :]
            ddv = dd_ref[pl.ds(ql, BAND_Q), :]
            kw = k_ref[pl.ds(st, win), :]
            vw = v_ref[pl.ds(st, win), :]
            dq = jnp.zeros((BAND_Q, LANES), F32)
            for e in range(2):
                mine = (lane >= HEAD_DIM) if e else (lane < HEAD_DIM)
                qs = jnp.where(mine, q, jnp.zeros_like(q)) * ATTN_SCALE
                dom = jnp.where(mine, dov, jnp.zeros_like(dov))
                lse_e = lsev[:, e * HEAD_DIM:e * HEAD_DIM + 1]
                dd_e = ddv[:, e * HEAD_DIM:e * HEAD_DIM + 1]
                sc = lax.dot_general(qs, kw, NT, preferred_element_type=F32)
                p = jnp.exp(jnp.where(valid, sc, NEG_INF) - lse_e)
                dp = lax.dot_general(dom, vw, NT, preferred_element_type=F32)
                ds = (p * (dp - dd_e)).astype(BF16)
                dv_ref[pl.ds(st, win), :] += lax.dot_general(p.astype(BF16), dom, TN, preferred_element_type=F32)
                dk_ref[pl.ds(st, win), :] += lax.dot_general(ds, qs, TN, preferred_element_type=F32)
                dq_e = lax.dot_general(ds, kw, NN, preferred_element_type=F32) * ATTN_SCALE
                dq = dq + jnp.where(mine, dq_e, 0.0)
            dq_ref[pl.ds(ql, BAND_Q), :] = dq.astype(BF16)

        _unrolled_loop(n_sub, one)

    def col(base):
        return lambda r, hp, sg: (0, r * TOKEN_BLOCKS + base + hp)

    q_spec = pl.BlockSpec((seg, LANES), lambda r, hp, sg: (sg, r * TOKEN_BLOCKS + B_Q_BLOCK0 + hp))
    seg_spec = pl.BlockSpec((seg, LANES), lambda r, hp, sg: (sg, r * 2 + hp))
    full_spec = pl.BlockSpec((length, LANES), lambda r, hp, sg: (0, r * 2 + hp))
    shp = jax.ShapeDtypeStruct((length, dil * B_OUT_WIDTH), F32)
    return pl.pallas_call(
        body, name=name, grid=(dil, 2, length // seg),
        in_specs=[q_spec, pl.BlockSpec((length, LANES), col(B_K_BLOCK0)), pl.BlockSpec((length, LANES), col(B_V_BLOCK0)),
                  seg_spec, seg_spec, seg_spec],
        out_specs=[seg_spec, full_spec, full_spec],
        out_shape=[jax.ShapeDtypeStruct(shp.shape, BF16), shp, shp],
        compiler_params=_params(),
    )(qkv_view, qkv_view, qkv_view, do, lse, dd)


def _merge_weights(lses):
    m = jnp.maximum(jnp.maximum(lses[0], lses[1]), lses[2])
    ex = [jnp.exp(v - m) for v in lses]
    tot = ex[0] + ex[1] + ex[2]
    return [v / tot for v in ex]


def _merge_fwd(os_, lses, *, name, tm=512):
    s = os_[0].shape[0]

    def body(o0, o1, o2, l0, l1, l2, ob_ref):
        w = _merge_weights([l0[...], l1[...], l2[...]])
        ob = w[0] * o0[...].astype(F32) + w[1] * o1[...].astype(F32) + w[2] * o2[...].astype(F32)
        ob_ref[...] = ob.astype(BF16)

    blk = pl.BlockSpec((tm, B_OUT_WIDTH), lambda i: (i, 0))
    return pl.pallas_call(
        body, name=name, grid=(s // tm,), in_specs=[blk] * 6, out_specs=blk,
        out_shape=jax.ShapeDtypeStruct((s, B_OUT_WIDTH), BF16),
        compiler_params=_params(),
    )(*os_, *lses)


def _merge_bwd(os_, lses, dob, *, name, tm=512):
    s = os_[0].shape[0]

    def body(o0, o1, o2, l0, l1, l2, dob_ref, d0, d1, d2, t0, t1, t2):
        w = _merge_weights([l0[...], l1[...], l2[...]])
        dv = dob_ref[...]
        ob = w[0] * o0[...].astype(F32) + w[1] * o1[...].astype(F32) + w[2] * o2[...].astype(F32)
        tot = _head_sum(dv * ob)
        for wg, d_ref, t_ref in zip(w, (d0, d1, d2), (t0, t1, t2)):
            d_ref[...] = (wg * dv).astype(BF16)
            t_ref[...] = wg * tot

    blk = pl.BlockSpec((tm, B_OUT_WIDTH), lambda i: (i, 0))
    return pl.pallas_call(
        body, name=name, grid=(s // tm,), in_specs=[blk] * 7, out_specs=[blk] * 6,
        out_shape=[jax.ShapeDtypeStruct((s, B_OUT_WIDTH), BF16)] * 3 + [jax.ShapeDtypeStruct((s, B_OUT_WIDTH), F32)] * 3,
        compiler_params=_params(),
    )(*os_, *lses, dob)


HALO = 16
POOL_BLOCK0 = QKV_WIDTH // LANES


def _window_sum(ext, lo, hi, tm):
    rows = ext.shape[0]
    acc = None
    for j in range(lo, hi + 1):
        r = ext if j == 0 else pltpu.roll(ext, (-j) % rows, axis=0)
        acc = r if acc is None else acc + r
    return acc[HALO:HALO + tm]


def _pool_counts(t, half, s):
    return (jnp.minimum(t + half, s) - jnp.maximum(t - half, 0)).astype(F32)


def _halo_specs(tm, s, col0):
    per = tm // HALO
    last = s // HALO - 1
    prev = pl.BlockSpec((HALO, LANES), lambda g, i: (jnp.maximum(i * per - 1, 0), col0 + g))
    cur = pl.BlockSpec((tm, LANES), lambda g, i: (i, col0 + g))
    nxt = pl.BlockSpec((HALO, LANES), lambda g, i: (jnp.minimum((i + 1) * per, last), col0 + g))
    return prev, cur, nxt


def _extended(prev_ref, cur_ref, next_ref, i, n_tiles):
    prev = jnp.where(i > 0, prev_ref[...].astype(F32), 0.0)
    nxt = jnp.where(i < n_tiles - 1, next_ref[...].astype(F32), 0.0)
    return jnp.concatenate([prev, cur_ref[...].astype(F32), nxt], axis=0)


def _pool_fwd(z, lin, scale, *, name, tm=512):
    s = z.shape[0]
    tm = min(tm, s)
    n_tiles = s // tm

    def body(prev_ref, cur_ref, next_ref, lin_ref, sc_ref, pooled_ref, mixed_ref):
        g = pl.program_id(0)
        i = pl.program_id(1)
        ext = _extended(prev_ref, cur_ref, next_ref, i, n_tiles)
        t = i * tm + lax.broadcasted_iota(jnp.int32, (tm, 1), 0)
        for gi, half in enumerate(POOL_HALF):
            @pl.when(g == gi)
            def _(half=half):
                mean = _window_sum(ext, -half, half - 1, tm) / _pool_counts(t, half, s)
                pooled = (mean - cur_ref[...]).astype(BF16)
                pooled_ref[...] = pooled
                mixed = lax.dot_general(pooled, lin_ref[...].astype(BF16), NN, preferred_element_type=F32)
                mixed_ref[...] = (mixed * sc_ref[...]).astype(BF16)

    prev, cur, nxt = _halo_specs(tm, s, POOL_BLOCK0)
    out = pl.BlockSpec((tm, LANES), lambda g, i: (i, g))
    return pl.pallas_call(
        body, name=name, grid=(len(POOL_HALF), n_tiles),
        in_specs=[prev, cur, nxt, pl.BlockSpec((None, LANES, LANES), lambda g, i: (g, 0, 0)),
                  pl.BlockSpec((1, LANES), lambda g, i: (0, g))],
        out_specs=[out, out],
        out_shape=[jax.ShapeDtypeStruct((s, POOL_WIDTH), BF16)] * 2,
        compiler_params=_params(),
    )(z, z, z, lin, scale)


def _pool_bwd(dmixed, pooled, lin, scale, *, name, tm=512):
    s = dmixed.shape[0]
    tm = min(tm, s)
    n_tiles = s // tm

    def body(prev_ref, cur_ref, next_ref, pooled_ref, lin_ref, sc_ref, du_ref, dlin_ref, dsc_ref):
        g = pl.program_id(0)
        i = pl.program_id(1)

        @pl.when(i == 0)
        def _():
            dlin_ref[...] = jnp.zeros_like(dlin_ref)
            dsc_ref[...] = jnp.zeros_like(dsc_ref)

        linb = lin_ref[...].astype(BF16)
        ext = _extended(prev_ref, cur_ref, next_ref, i, n_tiles)
        dpl_ext = (ext * sc_ref[...]).astype(BF16)
        dpl_cur = (cur_ref[...] * sc_ref[...]).astype(BF16)
        dpooled_ext = lax.dot_general(dpl_ext, linb, NT, preferred_element_type=F32)
        t_ext = i * tm - HALO + lax.broadcasted_iota(jnp.int32, (tm + 2 * HALO, 1), 0)
        pooled = pooled_ref[...]
        mixed = lax.dot_general(pooled, linb, NN, preferred_element_type=F32)
        dsc_ref[...] += jnp.sum(cur_ref[...] * mixed, axis=0, keepdims=True)
        dlin_ref[...] += lax.dot_general(pooled, dpl_cur, TN, preferred_element_type=F32)
        for gi, half in enumerate(POOL_HALF):
            @pl.when(g == gi)
            def _(half=half):
                share = dpooled_ext / jnp.maximum(_pool_counts(t_ext, half, s), 1.0)
                du = _window_sum(share, -(half - 1), half, tm) - dpooled_ext[HALO:HALO + tm]
                du_ref[...] = du.astype(BF16)

    prev, cur, nxt = _halo_specs(tm, s, 0)
    out = pl.BlockSpec((tm, LANES), lambda g, i: (i, g))
    lin_spec = pl.BlockSpec((None, LANES, LANES), lambda g, i: (g, 0, 0))
    vec = pl.BlockSpec((1, LANES), lambda g, i: (0, g))
    return pl.pallas_call(
        body, name=name, grid=(len(POOL_HALF), n_tiles),
        in_specs=[prev, cur, nxt, out, lin_spec, vec],
        out_specs=[out, lin_spec, vec],
        out_shape=[jax.ShapeDtypeStruct((s, POOL_WIDTH), BF16),
                   jax.ShapeDtypeStruct((len(POOL_HALF), LANES, LANES), F32),
                   jax.ShapeDtypeStruct((1, POOL_WIDTH), F32)],
        compiler_params=_params(),
    )(dmixed, dmixed, dmixed, pooled, lin, scale)


GATE_TILE = 512
GATE_BLOCK0 = (QKV_WIDTH + POOL_WIDTH) // GATE_TILE
GATE_BLOCKS_PER_BRANCH = D_MODEL // GATE_TILE


def _sigmoid(v):
    return 1.0 / (1.0 + jnp.exp(-v))


def _gate_specs(tm):
    def zspec(br):
        return pl.BlockSpec((tm, GATE_TILE), lambda jj, i: (i, GATE_BLOCK0 + GATE_BLOCKS_PER_BRANCH * br + jj))

    def bspec(br):
        return pl.BlockSpec((1, GATE_TILE), lambda jj, i: (0, GATE_BLOCKS_PER_BRANCH * br + jj))

    row = pl.BlockSpec((tm, GATE_TILE), lambda jj, i: (i, jj))
    vec = pl.BlockSpec((1, GATE_TILE), lambda jj, i: (0, jj))
    return [zspec(0), zspec(1), zspec(2)], [bspec(0), bspec(1), bspec(2)], row, vec


def _gate_fwd(z, b_gate, ya, yb, yc, *, name, tm=512):
    s = z.shape[0]

    def body(z0, z1, z2, b0, b1, b2, ya_ref, yb_ref, yc_ref, out_ref):
        acc = _sigmoid(z0[...] + b0[...]) * ya_ref[...]
        acc = acc + _sigmoid(z1[...] + b1[...]) * yb_ref[...]
        acc = acc + _sigmoid(z2[...] + b2[...]) * yc_ref[...]
        out_ref[...] = acc.astype(BF16)

    zs, bs, row, _ = _gate_specs(tm)
    return pl.pallas_call(
        body, name=name, grid=(GATE_BLOCKS_PER_BRANCH, s // tm),
        in_specs=zs + bs + [row] * 3, out_specs=row,
        out_shape=jax.ShapeDtypeStruct((s, D_MODEL), BF16),
        compiler_params=_params(),
    )(z, z, z, b_gate, b_gate, b_gate, ya, yb, yc)


def _gate_bwd(z, b_gate, ya, yb, yc, dmerged, *, name, tm=512):
    s = z.shape[0]

    def body(z0, z1, z2, b0, b1, b2, ya_ref, yb_ref, yc_ref, dm_ref,
             dya_ref, dyb_ref, dyc_ref, dg0, dg1, dg2, db0, db1, db2):
        i = pl.program_id(1)
        dm = dm_ref[...]
        for z_ref, b_ref, y_ref, dy_ref, dg_ref, db_ref in (
                (z0, b0, ya_ref, dya_ref, dg0, db0), (z1, b1, yb_ref, dyb_ref, dg1, db1),
                (z2, b2, yc_ref, dyc_ref, dg2, db2)):
            gate = _sigmoid(z_ref[...] + b_ref[...])
            dy_ref[...] = (gate * dm).astype(BF16)
            dpre = dm * y_ref[...] * gate * (1.0 - gate)
            dg_ref[...] = dpre.astype(BF16)
            part = jnp.sum(dpre, axis=0, keepdims=True)

            @pl.when(i == 0)
            def _(db_ref=db_ref, part=part):
                db_ref[...] = part

            @pl.when(i > 0)
            def _(db_ref=db_ref, part=part):
                db_ref[...] += part

    zs, bs, row, vec = _gate_specs(tm)
    big = jax.ShapeDtypeStruct((s, D_MODEL), BF16)
    small = jax.ShapeDtypeStruct((1, D_MODEL), F32)
    return pl.pallas_call(
        body, name=name, grid=(GATE_BLOCKS_PER_BRANCH, s // tm),
        in_specs=zs + bs + [row] * 4, out_specs=[row] * 6 + [vec] * 3,
        out_shape=[big] * 6 + [small] * 3,
        compiler_params=_params(),
    )(z, z, z, b_gate, b_gate, b_gate, ya, yb, yc, dmerged)


def _swiglu_fwd(a, b, *, name, tm=512, tn=1408):
    s, f = a.shape

    def body(a_ref, b_ref, o_ref):
        av = a_ref[...].astype(F32)
        o_ref[...] = (av * _sigmoid(av) * b_ref[...].astype(F32)).astype(BF16)

    blk = pl.BlockSpec((tm, tn), lambda i, j: (i, j))
    return pl.pallas_call(
        body, name=name, grid=(s // tm, f // tn), in_specs=[blk, blk], out_specs=blk,
        out_shape=jax.ShapeDtypeStruct((s, f), BF16), compiler_params=_params(),
    )(a, b)


def _swiglu_bwd(a, b, df, *, name, tm=512, tn=1408):
    s, f = a.shape

    def body(a_ref, b_ref, df_ref, da_ref, db_ref):
        av = a_ref[...].astype(F32)
        dfv = df_ref[...].astype(F32)
        sg = _sigmoid(av)
        silu = av * sg
        da_ref[...] = (dfv * b_ref[...].astype(F32) * (sg + silu * (1.0 - sg))).astype(BF16)
        db_ref[...] = (dfv * silu).astype(BF16)

    blk = pl.BlockSpec((tm, tn), lambda i, j: (i, j))
    out = jax.ShapeDtypeStruct((s, f), BF16)
    return pl.pallas_call(
        body, name=name, grid=(s // tm, f // tn), in_specs=[blk] * 3, out_specs=[blk] * 2,
        out_shape=[out, out], compiler_params=_params(),
    )(a, b, df)


def _loss_head(y, target, *, name, tm=512):
    s, d = y.shape

    def body(y_ref, t_ref, part_ref, dy_ref):
        i = pl.program_id(0)
        err = y_ref[...] - t_ref[...]
        dy_ref[...] = err * (1.0 / d)
        part = jnp.sum(err * err, axis=0, keepdims=True) * (0.5 / d)

        @pl.when(i == 0)
        def _():
            part_ref[...] = part

        @pl.when(i > 0)
        def _():
            part_ref[...] += part

    row = pl.BlockSpec((tm, d), lambda i: (i, 0))
    vec = pl.BlockSpec((1, d), lambda i: (0, 0))
    return pl.pallas_call(
        body, name=name, grid=(s // tm,), in_specs=[row, row], out_specs=[vec, row],
        out_shape=[jax.ShapeDtypeStruct((1, d), F32), jax.ShapeDtypeStruct((s, d), F32)],
        compiler_params=_params(),
    )(y, target)


def _mesh_place():
    x, y, c = lax.axis_index('x'), lax.axis_index('y'), lax.axis_index('c')
    return x, y, c, 4 * x + 2 * y + c


def _peer(x, y, c, k):
    return (x ^ ((k >> 2) & 1), y ^ ((k >> 1) & 1), c ^ (k & 1))


def _exchange(bufs, *, gather, name):
    n = len(bufs)

    def body(*refs):
        start, wait = _exchange_plan(refs[:n], refs[n:2 * n], refs[2 * n:], gather)
        start()
        wait()

    extra = _Extra((bufs, gather))
    return pl.pallas_call(
        body, name=name, in_specs=extra.in_specs, out_specs=extra.out_specs,
        out_shape=extra.out_shape, scratch_shapes=extra.scratch,
    )(*bufs)


_EXCHANGE_SPEC = pl.BlockSpec(memory_space=pl.ANY)
SEMS_PER_BUFFER = 3


class _Extra:
    def __init__(self, ride):
        bufs = [] if ride is None else list(ride[0])
        self.in_specs = [_EXCHANGE_SPEC] * len(bufs)
        self.out_specs = [_EXCHANGE_SPEC] * len(bufs)
        self.out_shape = [jax.ShapeDtypeStruct((N_DEV,) + b.shape[-2:], b.dtype) for b in bufs]
        self.scratch = [pltpu.SemaphoreType.DMA((N_DEV - 1,)), pltpu.SemaphoreType.DMA((N_DEV - 1,)),
                        pltpu.SemaphoreType.DMA] * len(bufs)
        self.args = tuple(bufs)


def _carry_exchange(body, n_in, n_out, grid, ride):
    extra = _Extra(ride)
    if ride is None:
        return body, extra
    n, gather = len(ride[0]), ride[1]

    def carrying(*refs):
        ins, ride_ins = refs[:n_in], refs[n_in:n_in + n]
        outs, ride_outs = refs[n_in + n:n_in + n + n_out], refs[n_in + n + n_out:n_in + 2 * n + n_out]
        sems = refs[n_in + 2 * n + n_out:]
        ids = [pl.program_id(a) for a in range(len(grid))]
        first = functools.reduce(jnp.logical_and, [i == 0 for i in ids])
        last = functools.reduce(jnp.logical_and, [i == size - 1 for i, size in zip(ids, grid)])

        @pl.when(first)
        def _():
            _exchange_plan(ride_ins, ride_outs, sems, gather)[0]()

        body(*ins, *outs)

        @pl.when(last)
        def _():
            _exchange_plan(ride_ins, ride_outs, sems, gather)[1]()

    return carrying, extra


def _exchange_plan(in_refs, out_refs, sems, gather):
    x, y, c, me = _mesh_place()

    def src(b, slot):
        return in_refs[b] if gather else in_refs[b].at[slot]

    def copy(b, k, dst_slot):
        send_sems, recv_sems = sems[SEMS_PER_BUFFER * b], sems[SEMS_PER_BUFFER * b + 1]
        return pltpu.make_async_remote_copy(
            src_ref=src(b, me ^ k), dst_ref=out_refs[b].at[dst_slot],
            send_sem=send_sems.at[k - 1], recv_sem=recv_sems.at[k - 1],
            device_id=_peer(x, y, c, k), device_id_type=pl.DeviceIdType.MESH)

    def mine(b):
        return pltpu.make_async_copy(src(b, me), out_refs[b].at[me], sems[SEMS_PER_BUFFER * b + 2])

    buffers = range(len(in_refs))

    def start():
        for b in buffers:
            mine(b).start()
            for k in range(1, N_DEV):
                copy(b, k, me).start()

    def wait():
        for b in buffers:
            for k in range(1, N_DEV):
                copy(b, k, me ^ k).wait_recv()
        for b in buffers:
            for k in range(1, N_DEV):
                copy(b, k, me).wait_send()
            mine(b).wait()

    return start, wait


def _adamw(parts, w, m, v, *, name, tr):
    n_layers = len(parts)
    rows, cols = parts[0].shape[1:]
    assert rows % tr == 0 and w.shape == (n_layers * rows, cols)
    tiles = rows // tr
    bias1 = 1.0 - ADAM_B1 ** ADAM_STEP
    bias2 = 1.0 - ADAM_B2 ** ADAM_STEP

    def body(*refs):
        p_refs = refs[:n_layers]
        w_ref, m_ref, v_ref, g_ref, d_ref, nm_ref, nv_ref = refs[n_layers:]

        def update(p_ref):
            g = p_ref[0].astype(F32)
            for j in range(1, N_DEV):
                g = g + p_ref[j].astype(F32)
            nm = ADAM_B1 * m_ref[...] + (1.0 - ADAM_B1) * g
            nv = ADAM_B2 * v_ref[...] + (1.0 - ADAM_B2) * (g * g)
            g_ref[...] = g
            nm_ref[...] = nm
            nv_ref[...] = nv
            d_ref[...] = -ADAM_LR * ((nm / bias1) / (jnp.sqrt(nv / bias2) + ADAM_EPS) + ADAM_WD * w_ref[...])

        for layer, p_ref in enumerate(p_refs):
            @pl.when(pl.program_id(0) == layer)
            def _(p_ref=p_ref):
                update(p_ref)

    def part_spec(layer):
        def index(l, i):
            return 0, jnp.where(l < layer, 0, jnp.where(l > layer, tiles - 1, i)), 0
        return pl.BlockSpec((N_DEV, tr, cols), index)

    blk = pl.BlockSpec((tr, cols), lambda l, i: (l * tiles + i, 0))
    out = jax.ShapeDtypeStruct((n_layers * rows, cols), F32)
    return pl.pallas_call(
        body, name=name, grid=(n_layers, tiles),
        in_specs=[part_spec(layer) for layer in range(n_layers)] + [blk, blk, blk],
        out_specs=[blk] * 4, out_shape=[out] * 4, compiler_params=_params(),
    )(*parts, w, m, v)


def _col_blocks(full):
    r, n = full.shape
    return full.reshape(r, N_DEV, n // N_DEV).transpose(1, 0, 2)


def _row_blocks(full):
    r, n = full.shape
    return full.reshape(N_DEV, r // N_DEV, n)


def _from_col_blocks(blocks):
    j, r, c = blocks.shape
    return blocks.transpose(1, 0, 2).reshape(r, j * c)


def _from_row_blocks(blocks):
    j, r, c = blocks.shape
    return blocks.reshape(j * r, c)


SHARD_ROWWISE = {'w_out', 'w_ffn_down'}
EARLY = ('w_ffn_down', 'w_ffn_gate', 'w_ffn_up', 'w_out', 'w_branch_a')
LATE = ('w_branch_b', 'w_branch_c')


def _blocks(name, full):
    return _row_blocks(full) if name in SHARD_ROWWISE else _col_blocks(full)


def _from_blocks(name, blocks):
    return _from_row_blocks(blocks) if name in SHARD_ROWWISE else _from_col_blocks(blocks)


def _pack(shards, names):
    lead = shards[names[0]].shape[:-2]
    return jnp.concatenate([shards[n].reshape(lead + (-1, PACK_COLS)) for n in names], axis=-2)


def _unpack(packed, shapes, names):
    out, off = {}, 0
    lead = packed.shape[:-2]
    for n in names:
        size = shapes[n][0] * shapes[n][1] // PACK_COLS
        out[n] = packed[..., off:off + size, :].reshape(lead + tuple(shapes[n]))
        off += size
    return out


def _pack_small(vals):
    flat = jnp.concatenate([vals[n].reshape(-1) for n in SMALL])
    return flat.reshape(-1, PACK_COLS)


def _unpack_small(packed, shapes):
    flat = packed.reshape(-1)
    out, off = {}, 0
    for n in SMALL:
        size = 1
        for dim in shapes[n]:
            size *= dim
        out[n] = flat[off:off + size].reshape(shapes[n])
        off += size
    return out


def _band_views(qkv, s):
    views = []
    for gi, d in enumerate(B_DILATIONS):
        starts = [A_Q_WIDTH + 2 * A_KV_WIDTH + part * B_WIDTH + gi * B_OUT_WIDTH for part in range(3)]
        group = jnp.concatenate([qkv[:, c0:c0 + B_OUT_WIDTH] for c0 in starts], axis=1)
        views.append(group.reshape(s // d, d * GROUP_QKV))
    return views


def _layer_fwd(x, p, tables, ride=None, after_ride=None):
    s = x.shape[0]
    cos, sin = tables
    h = _rms_fwd(x, p['norm_mix'], name='rms_mix_fwd')
    z = _mm(h, p['w_in'], dims='nn', out_dtype=BF16, tm=1024, tn=1664, tk=1024, name='mm_in_fwd')
    gains = _qk_gains(p['qn_a'], p['kn_a'], p['qn_b'], p['kn_b'])
    qkv = _qkrope_fwd(z, gains, cos, sin, name='qkrope_fwd')

    kdup = _dup_halves(qkv[:, A_Q_WIDTH:A_Q_WIDTH + A_KV_WIDTH])
    va = qkv[:, A_Q_WIDTH + A_KV_WIDTH:A_Q_WIDTH + 2 * A_KV_WIDTH]
    vdup = _dup_halves(va)
    oa, lse_a, *gathered = _attn_a_fwd(qkv, kdup, _with_ones(va), ride=ride, name='attn_a_fwd')
    if after_ride is not None:
        p = {**p, **after_ride(gathered)}
    ya = _mm(oa, p['w_branch_a'], dims='nn', out_dtype=BF16, tm=1024, tn=1024, tk=512, name='mm_branch_a_fwd')

    views = _band_views(qkv, s)
    o_g, lse_g = [], []
    for gi, d in enumerate(B_DILATIONS):
        o, lse = _band_fwd(views[gi], gi, d, name=f'band_fwd_d{d}')
        o_g.append(o.reshape(s, B_OUT_WIDTH))
        lse_g.append(lse.reshape(s, B_OUT_WIDTH))
    ob = _merge_fwd(o_g, lse_g, name='merge_fwd')
    yb = _mm(ob, p['w_branch_b'], dims='nn', out_dtype=BF16, tm=1024, tn=1024, tk=256, name='mm_branch_b_fwd')

    pooled, mixed = _pool_fwd(z, p['pool_lin'], p['pool_scale'], name='pool_fwd')
    yc = _mm(mixed, p['w_branch_c'], dims='nn', out_dtype=BF16, tm=1024, tn=1024, tk=512, name='mm_branch_c_fwd')

    merged = _gate_fwd(z, p['b_gate'], ya, yb, yc, name='gate_fwd')
    x_mid = _mm(merged, p['w_out'], dims='nn', out_dtype=F32, tm=1024, tn=1024, tk=1024, res=x, name='mm_out_fwd')

    h2 = _rms_fwd(x_mid, p['norm_ffn'], name='rms_ffn_fwd')
    fa = _mm(h2, p['w_ffn_gate'], dims='nn', out_dtype=BF16, tm=1024, tn=1408, tk=1024, name='mm_ffn_gate_fwd')
    fb = _mm(h2, p['w_ffn_up'], dims='nn', out_dtype=BF16, tm=1024, tn=1408, tk=1024, name='mm_ffn_up_fwd')
    f = _swiglu_fwd(fa, fb, name='swiglu_fwd')
    x_out = _mm(f, p['w_ffn_down'], dims='nn', out_dtype=F32, tm=1024, tn=512, tk=2816, res=x_mid, name='mm_ffn_down_fwd')

    saved = dict(x=x, h=h, z=z, gains=gains, qkv=qkv, kdup=kdup, vdup=vdup, oa=oa, lse_a=lse_a, o_g=o_g, lse_g=lse_g,
                 ob=ob, pooled=pooled, mixed=mixed, ya=ya, yb=yb, yc=yc, merged=merged, x_mid=x_mid, h2=h2,
                 fa=fa, fb=fb, f=f, views=views)
    return x_out, saved, p


def _fold_heads(v, heads):
    return v.reshape(heads, HEAD_DIM).sum(axis=0)


def _layer_bwd(dx, p, sv, tables, make_ride=None):
    s = dx.shape[0]
    cos, sin = tables
    g = {}

    df = _mm(dx, p['w_ffn_down'], dims='nt', out_dtype=BF16, tm=1024, tn=1408, tk=1024, name='mm_ffn_down_dx')
    g['w_ffn_down'] = _mm(sv['f'], dx, dims='tn', out_dtype=BF16, tm=1408, tn=1024, tk=512, name='mm_ffn_down_dw')
    da, db = _swiglu_bwd(sv['fa'], sv['fb'], df, name='swiglu_bwd')
    dh2 = _mm(da, p['w_ffn_gate'], dims='nt', out_dtype=F32, tm=1024, tn=512, tk=2816, name='mm_ffn_gate_dx')
    dh2 = _mm(db, p['w_ffn_up'], dims='nt', out_dtype=F32, tm=1024, tn=512, tk=2816, res=dh2, name='mm_ffn_up_dx')
    g['w_ffn_gate'] = _mm(sv['h2'], da, dims='tn', out_dtype=BF16, tm=1024, tn=2816, tk=512, name='mm_ffn_gate_dw')
    g['w_ffn_up'] = _mm(sv['h2'], db, dims='tn', out_dtype=BF16, tm=1024, tn=2816, tk=512, name='mm_ffn_up_dw')
    dx_mid, g['norm_ffn'] = _rms_bwd(sv['x_mid'], p['norm_ffn'], dh2, dx, name='rms_ffn_bwd')

    dmerged = _mm(dx_mid, p['w_out'], dims='nt', out_dtype=F32, tm=1024, tn=1024, tk=1024, name='mm_out_dx')
    g['w_out'] = _mm(sv['merged'], dx_mid, dims='tn', out_dtype=BF16, tm=1024, tn=1024, tk=512, name='mm_out_dw')
    dya, dyb, dyc, dg0, dg1, dg2, db0, db1, db2 = _gate_bwd(
        sv['z'], p['b_gate'], sv['ya'], sv['yb'], sv['yc'], dmerged, name='gate_bwd')
    g['b_gate'] = jnp.concatenate([db0, db1, db2], axis=1)

    doa = _mm(dya, p['w_branch_a'], dims='nt', out_dtype=BF16, tm=1024, tn=512, tk=1024, name='mm_branch_a_dx')
    g['w_branch_a'] = _mm(sv['oa'], dya, dims='tn', out_dtype=BF16, tm=512, tn=1024, tk=512, name='mm_branch_a_dw')
    ride = None if make_ride is None else make_ride(g)
    dqa, dkdup, dvdup, *arrived = _attn_a_bwd(sv['qkv'], sv['kdup'], sv['vdup'], sv['oa'], sv['lse_a'], doa,
                                              ride=ride, name='attn_a_bwd')

    def fold(dup):
        return jnp.concatenate([dup[:, 0:64] + dup[:, 64:128], dup[:, 128:192] + dup[:, 192:256]], axis=1)

    dka, dva = fold(dkdup), fold(dvdup)

    dob = _mm(dyb, p['w_branch_b'], dims='nt', out_dtype=F32, tm=1024, tn=256, tk=1024, name='mm_branch_b_dx')
    g['w_branch_b'] = _mm(sv['ob'], dyb, dims='tn', out_dtype=BF16, tm=256, tn=1024, tk=512, name='mm_branch_b_dw')
    merged_b = _merge_bwd(sv['o_g'], sv['lse_g'], dob, name='merge_bwd')
    do_g, dd_g = merged_b[:3], merged_b[3:]
    views = sv['views']
    dq_parts, dk_parts, dv_parts = [], [], []
    for gi, d in enumerate(B_DILATIONS):
        ln = s // d
        dq, dk, dv = _band_bwd(views[gi], do_g[gi].reshape(ln, d * B_OUT_WIDTH),
                               sv['lse_g'][gi].reshape(ln, d * B_OUT_WIDTH), dd_g[gi].reshape(ln, d * B_OUT_WIDTH),
                               gi, d, name=f'band_bwd_d{d}')
        dq_parts.append(dq.reshape(s, B_OUT_WIDTH))
        dk_parts.append(dk.reshape(s, B_OUT_WIDTH))
        dv_parts.append(dv.reshape(s, B_OUT_WIDTH))

    dmixed = _mm(dyc, p['w_branch_c'], dims='nt', out_dtype=F32, tm=1024, tn=512, tk=1024, name='mm_branch_c_dx')
    g['w_branch_c'] = _mm(sv['mixed'], dyc, dims='tn', out_dtype=BF16, tm=512, tn=1024, tk=512, name='mm_branch_c_dw')
    du, g['pool_lin'], g['pool_scale'] = _pool_bwd(dmixed, sv['pooled'], p['pool_lin'], p['pool_scale'], name='pool_bwd')

    dz_qkv, dgains = _qkrope_bwd(sv['z'], sv['gains'], cos, sin, [dqa, dka, dva] + dq_parts + dk_parts + dv_parts,
                                 name='qkrope_bwd')
    dgains = jnp.sum(dgains, axis=0)
    g['qn_a'] = _fold_heads(dgains[0:512], 8)
    g['kn_a'] = _fold_heads(dgains[512:640], 2)
    g['qn_b'] = _fold_heads(dgains[768:1536], 12)
    g['kn_b'] = _fold_heads(dgains[1536:2304], 12)

    dz = jnp.concatenate([dz_qkv, du, dg0, dg1, dg2], axis=1)
    dh = _mm(dz, p['w_in'], dims='nt', out_dtype=F32, tm=1024, tn=1024, tk=1664, name='mm_in_dx')
    g['w_in'] = _mm(sv['h'], dz, dims='tn', out_dtype=BF16, tm=1024, tn=3328, tk=512, name='mm_in_dw')
    dx_in, g['norm_mix'] = _rms_bwd(sv['x'], p['norm_mix'], dh, dx_mid, name='rms_mix_bwd')
    return dx_in, g, arrived


def _small_views(vals, l):
    return {
        'norm_mix': vals['norm_mix'][l][None, :], 'b_gate': vals['b_gate'][l][None, :],
        'qn_a': vals['qn_a'][l], 'kn_a': vals['kn_a'][l], 'qn_b': vals['qn_b'][l], 'kn_b': vals['kn_b'][l],
        'pool_lin': vals['pool_lin'][l], 'pool_scale': vals['pool_scale'][l][None, :],
        'norm_ffn': vals['norm_ffn'][l][None, :],
    }


def kernel(x, norm_mix, w_in, b_gate, qn_a, kn_a, qn_b, kn_b, pool_lin, pool_scale, w_branch_a, w_branch_b, w_branch_c, w_out, norm_ffn, w_ffn_gate, w_ffn_up, w_ffn_down, loss_target, m_norm_mix, m_w_in, m_b_gate, m_qn_a, m_kn_a, m_qn_b, m_kn_b, m_pool_lin, m_pool_scale, m_w_branch_a, m_w_branch_b, m_w_branch_c, m_w_out, m_norm_ffn, m_w_ffn_gate, m_w_ffn_up, m_w_ffn_down, v_norm_mix, v_w_in, v_b_gate, v_qn_a, v_kn_a, v_qn_b, v_kn_b, v_pool_lin, v_pool_scale, v_w_branch_a, v_w_branch_b, v_w_branch_c, v_w_out, v_norm_ffn, v_w_ffn_gate, v_w_ffn_up, v_w_ffn_down):
    w = dict(norm_mix=norm_mix, w_in=w_in, b_gate=b_gate, qn_a=qn_a, kn_a=kn_a, qn_b=qn_b, kn_b=kn_b,
             pool_lin=pool_lin, pool_scale=pool_scale, w_branch_a=w_branch_a, w_branch_b=w_branch_b,
             w_branch_c=w_branch_c, w_out=w_out, norm_ffn=norm_ffn, w_ffn_gate=w_ffn_gate, w_ffn_up=w_ffn_up,
             w_ffn_down=w_ffn_down)
    m = dict(norm_mix=m_norm_mix, w_in=m_w_in, b_gate=m_b_gate, qn_a=m_qn_a, kn_a=m_kn_a, qn_b=m_qn_b, kn_b=m_kn_b,
             pool_lin=m_pool_lin, pool_scale=m_pool_scale, w_branch_a=m_w_branch_a, w_branch_b=m_w_branch_b,
             w_branch_c=m_w_branch_c, w_out=m_w_out, norm_ffn=m_norm_ffn, w_ffn_gate=m_w_ffn_gate,
             w_ffn_up=m_w_ffn_up, w_ffn_down=m_w_ffn_down)
    v = dict(norm_mix=v_norm_mix, w_in=v_w_in, b_gate=v_b_gate, qn_a=v_qn_a, kn_a=v_kn_a, qn_b=v_qn_b, kn_b=v_kn_b,
             pool_lin=v_pool_lin, pool_scale=v_pool_scale, w_branch_a=v_w_branch_a, w_branch_b=v_w_branch_b,
             w_branch_c=v_w_branch_c, w_out=v_w_out, norm_ffn=v_norm_ffn, w_ffn_gate=v_w_ffn_gate,
             w_ffn_up=v_w_ffn_up, w_ffn_down=v_w_ffn_down)
    depth = w_in.shape[0]
    shard_shapes = {n: w[n].shape[1:] for n in SHARDED}
    small_shapes = {n: w[n].shape for n in SMALL}

    def my_shards(l, names):
        return _pack({n: w[n][l].astype(BF16) for n in names}, names)

    def full_weights(gathered, names):
        blocks = _unpack(gathered, shard_shapes, names)
        return {n: _from_blocks(n, blocks[n]) for n in names}

    def blocks_to_send(grad, names):
        return _pack({n: _blocks(n, grad[n]) for n in names}, names)

    tables = _rope_tables(x.shape[1])
    (w_in_blocks,) = _exchange([w['w_in'][0].astype(BF16)], gather=True, name='gather_weights')
    layers, saved = [], []
    act = x[0]
    for l in range(depth):
        p = {'w_in': _from_col_blocks(w_in_blocks), **_small_views(w, l)}
        bufs = [my_shards(l, EARLY), my_shards(l, LATE)]
        if l + 1 < depth:
            bufs.append(w['w_in'][l + 1].astype(BF16))
        stash = {}

        def after_ride(gathered, stash=stash):
            stash['next_w_in'] = gathered[2] if len(gathered) > 2 else None
            return {**full_weights(gathered[0], EARLY), **full_weights(gathered[1], LATE)}

        act, sv, p = _layer_fwd(act, p, tables, ride=(bufs, True), after_ride=after_ride)
        w_in_blocks = stash['next_w_in']
        layers.append(p)
        saved.append(sv)
    part, dx = _loss_head(act, loss_target[0], name='loss_head')
    loss = lax.psum(jnp.sum(part), ('x', 'y', 'c'))

    early_parts, late_parts, w_in_parts = ([None] * depth for _ in range(3))
    grads = [None] * depth
    pending = []
    for l in reversed(range(depth)):
        def make_ride(g, pending=pending):
            return [blocks_to_send(g, EARLY)] + pending, False

        dx, grads[l], arrived = _layer_bwd(dx, layers[l], saved[l], tables, make_ride=make_ride)
        early_parts[l] = arrived[0]
        if pending:
            w_in_parts[l + 1], late_parts[l + 1] = arrived[1], arrived[2]
        pending = [_col_blocks(grads[l]['w_in']), blocks_to_send(grads[l], LATE)]
    w_in_parts[0], late_parts[0] = _exchange(pending, gather=False, name='scatter_grads')
    grad_x = dx

    new = {}
    w_in_shape = w['w_in'].shape
    res = _adamw(w_in_parts, *(t['w_in'].reshape(-1, w_in_shape[-1]) for t in (w, m, v)), name='adamw_w_in', tr=128)
    new['w_in'] = tuple(r.reshape(w_in_shape) for r in res)
    for names, parts, label in ((EARLY, early_parts, 'adamw_early'), (LATE, late_parts, 'adamw_late')):
        packed = [_pack({n: t[n] for n in names}, names).reshape(-1, PACK_COLS) for t in (w, m, v)]
        res = _adamw(parts, *packed, name=label, tr=96)
        unpacked = [_unpack(r.reshape(depth, -1, PACK_COLS), shard_shapes, names) for r in res]
        for n in names:
            new[n] = tuple(u[n] for u in unpacked)

    small_grad = {n: jnp.stack([grads[l][n].reshape(small_shapes[n][1:]) for l in range(depth)]) for n in SMALL}
    (small_parts,) = _exchange([_pack_small(small_grad)], gather=True, name='gather_small_grads')
    res = _adamw([small_parts], _pack_small({n: w[n] for n in SMALL}), _pack_small({n: m[n] for n in SMALL}),
                 _pack_small({n: v[n] for n in SMALL}), name='adamw_small', tr=small_parts.shape[1])
    unpacked = [_unpack_small(r, small_shapes) for r in res]
    for n in SMALL:
        new[n] = tuple(u[n] for u in unpacked)

    outs = [loss, grad_x[None]]
    for idx in range(4):
        outs.extend(new[n][idx] for n in WEIGHTS)
    return tuple(outs)
```

```python
import functools

import jax
import jax.numpy as jnp
from jax import lax
from jax.experimental import pallas as pl
from jax.experimental.pallas import tpu as pltpu

F32 = jnp.float32
BF16 = jnp.bfloat16

N_DEV = 8
D_MODEL = 1024
DEPTH = 4
HEAD_DIM = 64
LANES = 128
A_Q_WIDTH = 512
A_KV_WIDTH = 128
B_WIDTH = 768
B_GROUPS = 3
B_DILATIONS = (1, 4, 16)
B_HALF_SPAN = 64
B_OUT_WIDTH = 256
POOL_WIDTH = 512
POOL_HALF = (1, 2, 4, 8)
GATE_WIDTH = 3072
QKV_WIDTH = A_Q_WIDTH + 2 * A_KV_WIDTH + 3 * B_WIDTH
IN_WIDTH = QKV_WIDTH + POOL_WIDTH + GATE_WIDTH
D_FF = 2816
GRID_W = 64
ROPE_THETA = 10000.0
EPS = 1e-6
NEG_INF = -1e30
ATTN_SCALE = HEAD_DIM ** -0.5

ADAM_LR = 0.001
ADAM_B1 = 0.9
ADAM_B2 = 0.999
ADAM_EPS = 1e-08
ADAM_WD = 0.01
ADAM_STEP = 10

PACK_COLS = 1024
VMEM_LIMIT = 56 * 1024 * 1024

SHARDED = ('w_in', 'w_branch_a', 'w_branch_b', 'w_branch_c', 'w_out', 'w_ffn_gate', 'w_ffn_up', 'w_ffn_down')
SMALL = ('norm_mix', 'b_gate', 'qn_a', 'kn_a', 'qn_b', 'kn_b', 'pool_lin', 'pool_scale', 'norm_ffn')
WEIGHTS = ('norm_mix', 'w_in', 'b_gate', 'qn_a', 'kn_a', 'qn_b', 'kn_b', 'pool_lin', 'pool_scale',
           'w_branch_a', 'w_branch_b', 'w_branch_c', 'w_out', 'norm_ffn', 'w_ffn_gate', 'w_ffn_up', 'w_ffn_down')

NN = (((1,), (0,)), ((), ()))
NT = (((1,), (1,)), ((), ()))
TN = (((0,), (0,)), ((), ()))


def _params(vmem=None):
    return pltpu.CompilerParams(vmem_limit_bytes=VMEM_LIMIT if vmem is None else vmem)


def _lane_iota(n=LANES):
    return lax.broadcasted_iota(jnp.int32, (1, n), 1)


def _swap(x, sh, lane):
    n = x.shape[-1]
    down = pltpu.roll(x, sh, axis=1)
    up = pltpu.roll(x, n - sh, axis=1)
    return jnp.where((lane & sh) == 0, up, down)


def _head_sum(v):
    w = v.shape[-1]
    r = lax.broadcasted_iota(jnp.int32, (w, w), 0) // HEAD_DIM
    c = lax.broadcasted_iota(jnp.int32, (w, w), 1) // HEAD_DIM
    ones = (r == c).astype(BF16)
    hi = v.astype(BF16)
    lo = (v - hi.astype(F32)).astype(BF16)
    return (lax.dot_general(hi, ones, NN, preferred_element_type=F32)
            + lax.dot_general(lo, ones, NN, preferred_element_type=F32))


def _mm(a, b, *, dims, out_dtype, tm, tn, tk, name, res=None):
    if dims == 'nn':
        (m, k), n = a.shape, b.shape[1]
    elif dims == 'nt':
        (m, k), n = a.shape, b.shape[0]
    else:
        (k, m), n = a.shape, b.shape[1]
    tm, tn, tk = min(tm, m), min(tn, n), min(tk, k)
    assert m % tm == 0 and n % tn == 0 and k % tk == 0, (name, m, n, k, tm, tn, tk)
    nk = k // tk
    if dims == 'tn':
        a_spec = pl.BlockSpec((tk, tm), lambda i, j, kk: (kk, i))
    else:
        a_spec = pl.BlockSpec((tm, tk), lambda i, j, kk: (i, kk))
    if dims == 'nt':
        b_spec = pl.BlockSpec((tn, tk), lambda i, j, kk: (j, kk))
    else:
        b_spec = pl.BlockSpec((tk, tn), lambda i, j, kk: (kk, j))
    o_spec = pl.BlockSpec((tm, tn), lambda i, j, kk: (i, j))
    dn = {'nn': NN, 'nt': NT, 'tn': TN}[dims]
    has_res = res is not None

    def body(*refs):
        if has_res:
            a_ref, b_ref, r_ref, o_ref, acc_ref = refs
        else:
            a_ref, b_ref, o_ref, acc_ref = refs
        prod = lax.dot_general(a_ref[...].astype(BF16), b_ref[...].astype(BF16), dn,
                               preferred_element_type=F32)

        def finish(total):
            if has_res:
                total = total + r_ref[...]
            o_ref[...] = total.astype(out_dtype)

        if nk == 1:
            finish(prod)
        else:
            kk = pl.program_id(2)

            @pl.when(kk == 0)
            def _():
                acc_ref[...] = prod

            @pl.when(kk > 0)
            def _():
                acc_ref[...] += prod

            @pl.when(kk == nk - 1)
            def _():
                finish(acc_ref[...])

    in_specs = [a_spec, b_spec] + ([o_spec] if has_res else [])
    args = (a, b) + ((res,) if has_res else ())
    acc_shape = (tm, tn) if nk > 1 else (8, LANES)
    return pl.pallas_call(
        body, name=name, grid=(m // tm, n // tn, nk),
        in_specs=in_specs, out_specs=o_spec,
        out_shape=jax.ShapeDtypeStruct((m, n), out_dtype),
        scratch_shapes=[pltpu.VMEM(acc_shape, F32)],
        compiler_params=_params(),
    )(*args)


def _rms_fwd(x, g, *, name, tm=512):
    s, d = x.shape

    def body(x_ref, g_ref, h_ref, ht_ref):
        xv = x_ref[...]
        rstd = lax.rsqrt(jnp.mean(xv * xv, axis=-1, keepdims=True) + EPS)
        h = xv * rstd * g_ref[...]
        h_ref[...] = h.astype(BF16)
        ht_ref[...] = h.T.astype(BF16)

    return pl.pallas_call(
        body, name=name, grid=(s // tm,),
        in_specs=[pl.BlockSpec((tm, d), lambda i: (i, 0)), pl.BlockSpec((1, d), lambda i: (0, 0))],
        out_specs=[pl.BlockSpec((tm, d), lambda i: (i, 0)), pl.BlockSpec((d, tm), lambda i: (0, i))],
        out_shape=[jax.ShapeDtypeStruct((s, d), BF16), jax.ShapeDtypeStruct((d, s), BF16)],
        compiler_params=_params(),
    )(x, g)


def _rms_bwd(x, g, dh, dres, *, name, tm=512):
    s, d = x.shape

    def body(x_ref, g_ref, dh_ref, dres_ref, dx_ref, dg_ref):
        i = pl.program_id(0)
        xv = x_ref[...]
        rstd = lax.rsqrt(jnp.mean(xv * xv, axis=-1, keepdims=True) + EPS)
        xhat = xv * rstd
        dhv = dh_ref[...]
        dxhat = dhv * g_ref[...]
        proj = jnp.mean(dxhat * xhat, axis=-1, keepdims=True)
        dx_ref[...] = dres_ref[...] + rstd * (dxhat - xhat * proj)
        part = jnp.sum(dhv * xhat, axis=0, keepdims=True)

        @pl.when(i == 0)
        def _():
            dg_ref[...] = part

        @pl.when(i > 0)
        def _():
            dg_ref[...] += part

    row = pl.BlockSpec((tm, d), lambda i: (i, 0))
    vec = pl.BlockSpec((1, d), lambda i: (0, 0))
    return pl.pallas_call(
        body, name=name, grid=(s // tm,),
        in_specs=[row, vec, row, row], out_specs=[row, vec],
        out_shape=[jax.ShapeDtypeStruct((s, d), F32), jax.ShapeDtypeStruct((1, d), F32)],
        compiler_params=_params(),
    )(x, g, dh, dres)


N_QKV_BLOCKS = QKV_WIDTH // LANES
A_BLOCKS = (A_Q_WIDTH + 2 * A_KV_WIDTH) // LANES
V_A_BLOCK = A_BLOCKS - 1
V_B_FIRST = A_BLOCKS + 2 * (B_WIDTH // LANES)


def _qk_kind(j):
    return jnp.where(j < A_BLOCKS, 0, 1)


def _is_v_block(j):
    return (j == V_A_BLOCK) | (j >= V_B_FIRST)


def _rope_tables(s):
    def ang(pos, dim):
        inv = ROPE_THETA ** (-jnp.arange(0, dim, 2, dtype=F32) / dim)
        return pos.astype(F32)[:, None] * inv[None, :]
    t = jnp.arange(s)
    a_row = ang(t // GRID_W, HEAD_DIM // 2)
    a_col = ang(t % GRID_W, HEAD_DIM // 2)
    a_seq = ang(t, HEAD_DIM)
    cos_a = jnp.concatenate([jnp.cos(a_row)] * 2 + [jnp.cos(a_col)] * 2, axis=-1)
    sin_a = jnp.concatenate([-jnp.sin(a_row), jnp.sin(a_row), -jnp.sin(a_col), jnp.sin(a_col)], axis=-1)
    cos_b = jnp.concatenate([jnp.cos(a_seq)] * 2, axis=-1)
    sin_b = jnp.concatenate([-jnp.sin(a_seq), jnp.sin(a_seq)], axis=-1)
    cos = jnp.stack([jnp.tile(cos_a, (1, 2)), jnp.tile(cos_b, (1, 2))])
    sin = jnp.stack([jnp.tile(sin_a, (1, 2)), jnp.tile(sin_b, (1, 2))])
    return cos, sin


def _qk_gains(qn_a, kn_a, qn_b, kn_b):
    one = jnp.ones((HEAD_DIM,), F32)
    parts = [jnp.tile(qn_a, 8), jnp.tile(kn_a, 2), jnp.tile(one, 2),
             jnp.tile(qn_b, 12), jnp.tile(kn_b, 12), jnp.tile(one, 12)]
    return jnp.concatenate(parts)[None, :]


def _qkrope_fwd(z, gains, cos, sin, *, name, tm=2048):
    s = z.shape[0]
    tm = min(tm, s)

    def body(z_ref, g_ref, c_ref, s_ref, o_ref):
        j = pl.program_id(1)
        lane = _lane_iota()
        xv = z_ref[...].astype(F32)

        def normed_rope(pair):
            ms = _head_sum(xv * xv) * (1.0 / HEAD_DIM)
            n = xv * lax.rsqrt(ms + EPS) * g_ref[...]
            return n * c_ref[...] + _swap(n, pair, lane) * s_ref[...]

        @pl.when(_is_v_block(j))
        def _():
            o_ref[...] = xv.astype(BF16)

        @pl.when(jnp.logical_not(_is_v_block(j)) & (j < A_BLOCKS))
        def _():
            o_ref[...] = normed_rope(HEAD_DIM // 4).astype(BF16)

        @pl.when(jnp.logical_not(_is_v_block(j)) & (j >= A_BLOCKS))
        def _():
            o_ref[...] = normed_rope(HEAD_DIM // 2).astype(BF16)

    tab = pl.BlockSpec((None, tm, LANES), lambda i, j: (_qk_kind(j), i, 0))
    blk = pl.BlockSpec((tm, LANES), lambda i, j: (i, j))
    return pl.pallas_call(
        body, name=name, grid=(s // tm, N_QKV_BLOCKS),
        in_specs=[blk, pl.BlockSpec((1, LANES), lambda i, j: (0, j)), tab, tab],
        out_specs=blk,
        out_shape=jax.ShapeDtypeStruct((s, QKV_WIDTH), BF16),
        compiler_params=_params(),
    )(z, gains, cos, sin)


SUBLANES = 8


def _qkrope_bwd(z, gains, cos, sin, sources, *, name, tm=2048):
    s = z.shape[0]
    tm = min(tm, s)
    n_src = len(sources)
    widths = [a.shape[1] // LANES for a in sources]
    firsts = [sum(widths[:k]) for k in range(n_src)]
    assert sum(widths) == N_QKV_BLOCKS

    def body(z_ref, g_ref, c_ref, s_ref, *rest):
        src_refs, (dz_ref, dg_ref, dy_ref) = rest[:n_src], rest[n_src:]
        j = pl.program_id(1)
        lane = _lane_iota()
        xv = z_ref[...].astype(F32)
        for src_ref, first, width in zip(src_refs, firsts, widths):
            @pl.when((j >= first) & (j < first + width))
            def _(src_ref=src_ref):
                dy_ref[...] = src_ref[...].astype(F32)
        dy = dy_ref[...]

        def back(pair):
            ms = _head_sum(xv * xv) * (1.0 / HEAD_DIM)
            rstd = lax.rsqrt(ms + EPS)
            xhat = xv * rstd
            dn = dy * c_ref[...] + _swap(dy * s_ref[...], pair, lane)
            dg_ref[...] = jnp.sum((dn * xhat).reshape(tm // SUBLANES, SUBLANES, LANES), axis=0)
            dxhat = dn * g_ref[...]
            proj = _head_sum(dxhat * xhat) * (1.0 / HEAD_DIM)
            dz_ref[...] = (rstd * (dxhat - xhat * proj)).astype(BF16)

        @pl.when(_is_v_block(j))
        def _():
            dz_ref[...] = dy.astype(BF16)
            dg_ref[...] = jnp.zeros_like(dg_ref)

        @pl.when(jnp.logical_not(_is_v_block(j)) & (j < A_BLOCKS))
        def _():
            back(HEAD_DIM // 4)

        @pl.when(jnp.logical_not(_is_v_block(j)) & (j >= A_BLOCKS))
        def _():
            back(HEAD_DIM // 2)

    def src_spec(first, width):
        return pl.BlockSpec((tm, LANES), lambda i, j: (i, jnp.clip(j - first, 0, width - 1)))

    tab = pl.BlockSpec((None, tm, LANES), lambda i, j: (_qk_kind(j), i, 0))
    blk = pl.BlockSpec((tm, LANES), lambda i, j: (i, j))
    return pl.pallas_call(
        body, name=name, grid=(s // tm, N_QKV_BLOCKS),
        in_specs=[blk, pl.BlockSpec((1, LANES), lambda i, j: (0, j)), tab, tab]
        + [src_spec(first, width) for first, width in zip(firsts, widths)],
        out_specs=[blk, pl.BlockSpec((SUBLANES, LANES), lambda i, j: (i, j))],
        out_shape=[jax.ShapeDtypeStruct((s, QKV_WIDTH), BF16),
                   jax.ShapeDtypeStruct((s // tm * SUBLANES, QKV_WIDTH), F32)],
        scratch_shapes=[pltpu.VMEM((tm, LANES), F32)],
        compiler_params=_params(),
    )(z, gains, cos, sin, *sources)


def _dup_halves(kv):
    h0, h1 = kv[:, :HEAD_DIM], kv[:, HEAD_DIM:]
    return jnp.concatenate([h0, h0, h1, h1], axis=1)


def _with_ones(kv):
    h0, h1 = kv[:, :HEAD_DIM], kv[:, HEAD_DIM:]
    one = jnp.ones_like(h0)
    return jnp.concatenate([h0, one, one, h0, h1, one, one, h1], axis=1)


def _attn_a_fwd(qkv, kdup, vones, *, name, ride=None, tq=256, tk=512, unroll=8):
    s = qkv.shape[0]
    tq, tk = min(tq, s), min(tk, s)
    n_chunks = s // tk
    unroll = min(unroll, n_chunks)
    assert n_chunks % unroll == 0

    def body(q_ref, k_ref, v_ref, o_ref, lse_ref):
        lane = _lane_iota()
        low = lane < HEAD_DIM
        q = q_ref[...]
        zero = jnp.zeros_like(q)
        qm = [jnp.where(low, q, zero) * ATTN_SCALE, jnp.where(low, zero, q) * ATTN_SCALE]

        def chunks(c, carry):
            state = list(carry)
            scs = []
            for u in range(unroll):
                off = pl.multiple_of((c * unroll + u) * tk, tk)
                kc = k_ref[pl.ds(off, tk), :]
                scs.append([lax.dot_general(qm[e], kc, NT, preferred_element_type=F32) for e in range(2)])
            for u in range(unroll):
                off = pl.multiple_of((c * unroll + u) * tk, tk)
                for e in range(2):
                    m, acc = state[2 * e], state[2 * e + 1]
                    ve = v_ref[pl.ds(off, tk), e * LANES:(e + 1) * LANES]
                    m_new = jnp.maximum(m, jnp.max(scs[u][e], axis=1, keepdims=True))
                    alpha = jnp.exp(m - m_new)
                    p = jnp.exp(scs[u][e] - m_new).astype(BF16)
                    state[2 * e] = m_new
                    state[2 * e + 1] = alpha * acc + lax.dot_general(p, ve, NN, preferred_element_type=F32)
            return tuple(state)

        m_init = jnp.full((tq, 1), NEG_INF, F32)
        a_init = jnp.zeros((tq, LANES), F32)
        m0, a0, m1, a1 = lax.fori_loop(0, n_chunks // unroll, chunks, (m_init, a_init, m_init, a_init))
        l0 = pltpu.roll(a0, HEAD_DIM, axis=1)
        l1 = pltpu.roll(a1, HEAD_DIM, axis=1)
        o_ref[...] = jnp.where(low, a0 / l0, a1 / l1).astype(BF16)
        lse_ref[...] = jnp.where(low, m0 + jnp.log(l0), m1 + jnp.log(l1))

    q_spec = pl.BlockSpec((tq, LANES), lambda hb, qi: (qi, hb))
    k_spec = pl.BlockSpec((s, LANES), lambda hb, qi: (0, hb // 2))
    v_spec = pl.BlockSpec((s, 2 * LANES), lambda hb, qi: (0, hb // 2))
    grid = (A_Q_WIDTH // LANES, s // tq)
    body, extra = _carry_exchange(body, 3, 2, grid, ride)
    return pl.pallas_call(
        body, name=name, grid=grid,
        in_specs=[q_spec, k_spec, v_spec] + extra.in_specs, out_specs=[q_spec, q_spec] + extra.out_specs,
        out_shape=[jax.ShapeDtypeStruct((s, A_Q_WIDTH), BF16), jax.ShapeDtypeStruct((s, A_Q_WIDTH), F32)]
        + extra.out_shape,
        scratch_shapes=extra.scratch, compiler_params=_params(),
    )(qkv, kdup, vones, *extra.args)


def _attn_a_bwd(qkv, kdup, vdup, o, lse, do, *, name, ride=None, tq=512, tk=512, unroll=2):
    s = qkv.shape[0]
    tq, tk = min(tq, s), min(tk, s)
    n_chunks = s // tk
    unroll = min(unroll, n_chunks)
    assert n_chunks % unroll == 0

    def body(q_ref, k_ref, v_ref, o_ref, lse_ref, do_ref, dq_ref, dk_ref, dv_ref):
        first = (pl.program_id(1) == 0) & (pl.program_id(2) == 0)

        @pl.when(first)
        def _():
            dk_ref[...] = jnp.zeros_like(dk_ref)
            dv_ref[...] = jnp.zeros_like(dv_ref)

        lane = _lane_iota()
        low = lane < HEAD_DIM
        q = q_ref[...]
        dov = do_ref[...]
        zero = jnp.zeros_like(q)
        prod = dov.astype(F32) * o_ref[...].astype(F32)
        lsev = lse_ref[...]
        qs = [jnp.where(low, q, zero) * ATTN_SCALE, jnp.where(low, zero, q) * ATTN_SCALE]
        dom = [jnp.where(low, dov, zero), jnp.where(low, zero, dov)]
        delta = [jnp.sum(jnp.where(low, prod, 0.0), axis=1, keepdims=True),
                 jnp.sum(jnp.where(low, 0.0, prod), axis=1, keepdims=True)]
        lse = [lsev[:, 0:1], lsev[:, HEAD_DIM:HEAD_DIM + 1]]
        qs_both = jnp.concatenate(qs, axis=0)
        dom_both = jnp.concatenate(dom, axis=0)

        def chunks(c, carry):
            dqs = list(carry)
            for u in range(unroll):
                off = pl.multiple_of((c * unroll + u) * tk, tk)
                kc = k_ref[pl.ds(off, tk), :]
                vc = v_ref[pl.ds(off, tk), :]
                ps, dss = [], []
                for e in range(2):
                    sc = lax.dot_general(qs[e], kc, NT, preferred_element_type=F32)
                    p = jnp.exp(sc - lse[e])
                    dp = lax.dot_general(dom[e], vc, NT, preferred_element_type=F32)
                    ds = (p * (dp - delta[e])).astype(BF16)
                    ps.append(p.astype(BF16))
                    dss.append(ds)
                    dqs[e] = dqs[e] + lax.dot_general(ds, kc, NN, preferred_element_type=F32)
                dv_ref[pl.ds(off, tk), :] += lax.dot_general(jnp.concatenate(ps, axis=0), dom_both, TN,
                                                             preferred_element_type=F32)
                dk_ref[pl.ds(off, tk), :] += lax.dot_general(jnp.concatenate(dss, axis=0), qs_both, TN,
                                                             preferred_element_type=F32)
            return tuple(dqs)

        dq_init = jnp.zeros((tq, LANES), F32)
        dq0, dq1 = lax.fori_loop(0, n_chunks // unroll, chunks, (dq_init, dq_init))
        dq_ref[...] = (jnp.where(low, dq0, dq1) * ATTN_SCALE).astype(BF16)

    q_spec = pl.BlockSpec((tq, LANES), lambda kvh, hb, qi: (qi, kvh * 2 + hb))
    kv_spec = pl.BlockSpec((s, LANES), lambda kvh, hb, qi: (0, kvh))
    grid = (2, 2, s // tq)
    body, extra = _carry_exchange(body, 6, 3, grid, ride)
    return pl.pallas_call(
        body, name=name, grid=grid,
        in_specs=[q_spec, kv_spec, kv_spec, q_spec, q_spec, q_spec] + extra.in_specs,
        out_specs=[q_spec, kv_spec, kv_spec] + extra.out_specs,
        out_shape=[jax.ShapeDtypeStruct((s, A_Q_WIDTH), BF16),
                   jax.ShapeDtypeStruct((s, 2 * LANES), F32), jax.ShapeDtypeStruct((s, 2 * LANES), F32)]
        + extra.out_shape,
        scratch_shapes=extra.scratch, compiler_params=_params(),
    )(qkv, kdup, vdup, o, lse, do, *extra.args)


BAND_Q = 128
GROUP_QKV = 3 * B_OUT_WIDTH
TOKEN_BLOCKS = GROUP_QKV // LANES
B_Q_BLOCK0 = 0
B_K_BLOCK0 = B_OUT_WIDTH // LANES
B_V_BLOCK0 = 2 * (B_OUT_WIDTH // LANES)


BAND_UNROLL = 4


def _unrolled_loop(n, step):
    unroll = BAND_UNROLL
    while n % unroll:
        unroll //= 2

    def body(it, carry):
        for u in range(unroll):
            step(it * unroll + u)
        return carry

    lax.fori_loop(0, n // unroll, body, 0)


def _band_geometry(length):
    seg = min(length, 2048)
    win = min(2 * BAND_Q, length)
    return seg, win


def _band_window(qs, length, win):
    st = jnp.clip(qs - B_HALF_SPAN, 0, length - win)
    st = pl.multiple_of(st, B_HALF_SPAN)
    qpos = qs + lax.broadcasted_iota(jnp.int32, (BAND_Q, 1), 0)
    kpos = st + lax.broadcasted_iota(jnp.int32, (1, win), 1)
    return st, jnp.abs(qpos - kpos) <= B_HALF_SPAN


def _band_fwd(qkv_view, gi, dil, *, name):
    length = qkv_view.shape[0]
    seg, win = _band_geometry(length)
    n_sub = seg // BAND_Q

    def body(q_ref, k_ref, v_ref, o_ref, lse_ref):
        seg_i = pl.program_id(2)
        lane = _lane_iota()
        low = lane < HEAD_DIM

        def one(i):
            ql = pl.multiple_of(i * BAND_Q, BAND_Q)
            st, valid = _band_window(seg_i * seg + ql, length, win)
            q = q_ref[pl.ds(ql, BAND_Q), :]
            kw = k_ref[pl.ds(st, win), :]
            vw = v_ref[pl.ds(st, win), :]
            outs, lses = [], []
            for e in range(2):
                mine = (lane >= HEAD_DIM) if e else (lane < HEAD_DIM)
                qm = jnp.where(mine, q, jnp.zeros_like(q)) * ATTN_SCALE
                sc = lax.dot_general(qm, kw, NT, preferred_element_type=F32)
                sc = jnp.where(valid, sc, NEG_INF)
                m = jnp.max(sc, axis=1, keepdims=True)
                p = jnp.exp(sc - m)
                l = jnp.sum(p, axis=1, keepdims=True)
                outs.append(lax.dot_general(p.astype(BF16), vw, NN, preferred_element_type=F32) / l)
                lses.append(m + jnp.log(l))
            o_ref[pl.ds(ql, BAND_Q), :] = jnp.where(low, outs[0], outs[1]).astype(BF16)
            lse_ref[pl.ds(ql, BAND_Q), :] = jnp.where(low, lses[0], lses[1])

        _unrolled_loop(n_sub, one)

    def col(base):
        return lambda r, hp, sg: (0, r * TOKEN_BLOCKS + base + hp)

    q_spec = pl.BlockSpec((seg, LANES), lambda r, hp, sg: (sg, r * TOKEN_BLOCKS + B_Q_BLOCK0 + hp))
    out_spec = pl.BlockSpec((seg, LANES), lambda r, hp, sg: (sg, r * 2 + hp))
    return pl.pallas_call(
        body, name=name, grid=(dil, 2, length // seg),
        in_specs=[q_spec, pl.BlockSpec((length, LANES), col(B_K_BLOCK0)), pl.BlockSpec((length, LANES), col(B_V_BLOCK0))],
        out_specs=[out_spec, out_spec],
        out_shape=[jax.ShapeDtypeStruct((length, dil * B_OUT_WIDTH), BF16),
                   jax.ShapeDtypeStruct((length, dil * B_OUT_WIDTH), F32)],
        compiler_params=_params(),
    )(qkv_view, qkv_view, qkv_view)


def _band_bwd(qkv_view, do, lse, dd, gi, dil, *, name):
    length = qkv_view.shape[0]
    seg, win = _band_geometry(length)
    n_sub = seg // BAND_Q

    def body(q_ref, k_ref, v_ref, do_ref, lse_ref, dd_ref, dq_ref, dk_ref, dv_ref):
        seg_i = pl.program_id(2)
        lane = _lane_iota()

        @pl.when(seg_i == 0)
        def _():
            dk_ref[...] = jnp.zeros_like(dk_ref)
            dv_ref[...] = jnp.zeros_like(dv_ref)

        def one(i):
            ql = pl.multiple_of(i * BAND_Q, BAND_Q)
            st, valid = _band_window(seg_i * seg + ql, length, win)
            q = q_ref[pl.ds(ql, BAND_Q), :]
            dov = do_ref[pl.ds(ql, BAND_Q), :]
            lsev = lse_ref[pl.ds(ql, BAND_Q), :]
            ddv = dd_ref[pl.ds(ql, BAND_Q), :]
            kw = k_ref[pl.ds(st, win), :]
            vw = v_ref[pl.ds(st, win), :]
            dq = jnp.zeros((BAND_Q, LANES), F32)
            for e in range(2):
                mine = (lane >= HEAD_DIM) if e else (lane < HEAD_DIM)
                qs = jnp.where(mine, q, jnp.zeros_like(q)) * ATTN_SCALE
                dom = jnp.where(mine, dov, jnp.zeros_like(dov))
                lse_e = lsev[:, e * HEAD_DIM:e * HEAD_DIM + 1]
                dd_e = ddv[:, e * HEAD_DIM:e * HEAD_DIM + 1]
                sc = lax.dot_general(qs, kw, NT, preferred_element_type=F32)
                p = jnp.exp(jnp.where(valid, sc, NEG_INF) - lse_e)
                dp = lax.dot_general(dom, vw, NT, preferred_element_type=F32)
                ds = (p * (dp - dd_e)).astype(BF16)
                dv_ref[pl.ds(st, win), :] += lax.dot_general(p.astype(BF16), dom, TN, preferred_element_type=F32)
                dk_ref[pl.ds(st, win), :] += lax.dot_general(ds, qs, TN, preferred_element_type=F32)
                dq_e = lax.dot_general(ds, kw, NN, preferred_element_type=F32) * ATTN_SCALE
                dq = dq + jnp.where(mine, dq_e, 0.0)
            dq_ref[pl.ds(ql, BAND_Q), :] = dq.astype(BF16)

        _unrolled_loop(n_sub, one)

    def col(base):
        return lambda r, hp, sg: (0, r * TOKEN_BLOCKS + base + hp)

    q_spec = pl.BlockSpec((seg, LANES), lambda r, hp, sg: (sg, r * TOKEN_BLOCKS + B_Q_BLOCK0 + hp))
    seg_spec = pl.BlockSpec((seg, LANES), lambda r, hp, sg: (sg, r * 2 + hp))
    full_spec = pl.BlockSpec((length, LANES), lambda r, hp, sg: (0, r * 2 + hp))
    shp = jax.ShapeDtypeStruct((length, dil * B_OUT_WIDTH), F32)
    return pl.pallas_call(
        body, name=name, grid=(dil, 2, length // seg),
        in_specs=[q_spec, pl.BlockSpec((length, LANES), col(B_K_BLOCK0)), pl.BlockSpec((length, LANES), col(B_V_BLOCK0)),
                  seg_spec, seg_spec, seg_spec],
        out_specs=[seg_spec, full_spec, full_spec],
        out_shape=[jax.ShapeDtypeStruct(shp.shape, BF16), shp, shp],
        compiler_params=_params(),
    )(qkv_view, qkv_view, qkv_view, do, lse, dd)


def _merge_weights(lses):
    m = jnp.maximum(jnp.maximum(lses[0], lses[1]), lses[2])
    ex = [jnp.exp(v - m) for v in lses]
    tot = ex[0] + ex[1] + ex[2]
    return [v / tot for v in ex]


def _merge_fwd(os_, lses, *, name, tm=512):
    s = os_[0].shape[0]

    def body(o0, o1, o2, l0, l1, l2, ob_ref):
        w = _merge_weights([l0[...], l1[...], l2[...]])
        ob = w[0] * o0[...].astype(F32) + w[1] * o1[...].astype(F32) + w[2] * o2[...].astype(F32)
        ob_ref[...] = ob.astype(BF16)

    blk = pl.BlockSpec((tm, B_OUT_WIDTH), lambda i: (i, 0))
    return pl.pallas_call(
        body, name=name, grid=(s // tm,), in_specs=[blk] * 6, out_specs=blk,
        out_shape=jax.ShapeDtypeStruct((s, B_OUT_WIDTH), BF16),
        compiler_params=_params(),
    )(*os_, *lses)


def _merge_bwd(os_, lses, dob, *, name, tm=512):
    s = os_[0].shape[0]

    def body(o0, o1, o2, l0, l1, l2, dob_ref, d0, d1, d2, t0, t1, t2):
        w = _merge_weights([l0[...], l1[...], l2[...]])
        dv = dob_ref[...]
        ob = w[0] * o0[...].astype(F32) + w[1] * o1[...].astype(F32) + w[2] * o2[...].astype(F32)
        tot = _head_sum(dv * ob)
        for wg, d_ref, t_ref in zip(w, (d0, d1, d2), (t0, t1, t2)):
            d_ref[...] = (wg * dv).astype(BF16)
            t_ref[...] = wg * tot

    blk = pl.BlockSpec((tm, B_OUT_WIDTH), lambda i: (i, 0))
    return pl.pallas_call(
        body, name=name, grid=(s // tm,), in_specs=[blk] * 7, out_specs=[blk] * 6,
        out_shape=[jax.ShapeDtypeStruct((s, B_OUT_WIDTH), BF16)] * 3 + [jax.ShapeDtypeStruct((s, B_OUT_WIDTH), F32)] * 3,
        compiler_params=_params(),
    )(*os_, *lses, dob)


HALO = 16
POOL_BLOCK0 = QKV_WIDTH // LANES


def _window_sum(ext, lo, hi, tm):
    rows = ext.shape[0]
    acc = None
    for j in range(lo, hi + 1):
        r = ext if j == 0 else pltpu.roll(ext, (-j) % rows, axis=0)
        acc = r if acc is None else acc + r
    return acc[HALO:HALO + tm]


def _pool_counts(t, half, s):
    return (jnp.minimum(t + half, s) - jnp.maximum(t - half, 0)).astype(F32)


def _halo_specs(tm, s, col0):
    per = tm // HALO
    last = s // HALO - 1
    prev = pl.BlockSpec((HALO, LANES), lambda g, i: (jnp.maximum(i * per - 1, 0), col0 + g))
    cur = pl.BlockSpec((tm, LANES), lambda g, i: (i, col0 + g))
    nxt = pl.BlockSpec((HALO, LANES), lambda g, i: (jnp.minimum((i + 1) * per, last), col0 + g))
    return prev, cur, nxt


def _extended(prev_ref, cur_ref, next_ref, i, n_tiles):
    prev = jnp.where(i > 0, prev_ref[...].astype(F32), 0.0)
    nxt = jnp.where(i < n_tiles - 1, next_ref[...].astype(F32), 0.0)
    return jnp.concatenate([prev, cur_ref[...].astype(F32), nxt], axis=0)


def _pool_fwd(z, lin, scale, *, name, tm=512):
    s = z.shape[0]
    tm = min(tm, s)
    n_tiles = s // tm

    def body(prev_ref, cur_ref, next_ref, lin_ref, sc_ref, pooled_ref, mixed_ref):
        g = pl.program_id(0)
        i = pl.program_id(1)
        ext = _extended(prev_ref, cur_ref, next_ref, i, n_tiles)
        t = i * tm + lax.broadcasted_iota(jnp.int32, (tm, 1), 0)
        for gi, half in enumerate(POOL_HALF):
            @pl.when(g == gi)
            def _(half=half):
                mean = _window_sum(ext, -half, half - 1, tm) / _pool_counts(t, half, s)
                pooled = (mean - cur_ref[...]).astype(BF16)
                pooled_ref[...] = pooled
                mixed = lax.dot_general(pooled, lin_ref[...].astype(BF16), NN, preferred_element_type=F32)
                mixed_ref[...] = (mixed * sc_ref[...]).astype(BF16)

    prev, cur, nxt = _halo_specs(tm, s, POOL_BLOCK0)
    out = pl.BlockSpec((tm, LANES), lambda g, i: (i, g))
    return pl.pallas_call(
        body, name=name, grid=(len(POOL_HALF), n_tiles),
        in_specs=[prev, cur, nxt, pl.BlockSpec((None, LANES, LANES), lambda g, i: (g, 0, 0)),
                  pl.BlockSpec((1, LANES), lambda g, i: (0, g))],
        out_specs=[out, out],
        out_shape=[jax.ShapeDtypeStruct((s, POOL_WIDTH), BF16)] * 2,
        compiler_params=_params(),
    )(z, z, z, lin, scale)


def _pool_bwd(dmixed, pooled, lin, scale, *, name, tm=512):
    s = dmixed.shape[0]
    tm = min(tm, s)
    n_tiles = s // tm

    def body(prev_ref, cur_ref, next_ref, pooled_ref, lin_ref, sc_ref, du_ref, dlin_ref, dsc_ref):
        g = pl.program_id(0)
        i = pl.program_id(1)

        @pl.when(i == 0)
        def _():
            dlin_ref[...] = jnp.zeros_like(dlin_ref)
            dsc_ref[...] = jnp.zeros_like(dsc_ref)

        linb = lin_ref[...].astype(BF16)
        ext = _extended(prev_ref, cur_ref, next_ref, i, n_tiles)
        dpl_ext = (ext * sc_ref[...]).astype(BF16)
        dpl_cur = (cur_ref[...] * sc_ref[...]).astype(BF16)
        dpooled_ext = lax.dot_general(dpl_ext, linb, NT, preferred_element_type=F32)
        t_ext = i * tm - HALO + lax.broadcasted_iota(jnp.int32, (tm + 2 * HALO, 1), 0)
        pooled = pooled_ref[...]
        mixed = lax.dot_general(pooled, linb, NN, preferred_element_type=F32)
        dsc_ref[...] += jnp.sum(cur_ref[...] * mixed, axis=0, keepdims=True)
        dlin_ref[...] += lax.dot_general(pooled, dpl_cur, TN, preferred_element_type=F32)
        for gi, half in enumerate(POOL_HALF):
            @pl.when(g == gi)
            def _(half=half):
                share = dpooled_ext / jnp.maximum(_pool_counts(t_ext, half, s), 1.0)
                du = _window_sum(share, -(half - 1), half, tm) - dpooled_ext[HALO:HALO + tm]
                du_ref[...] = du.astype(BF16)

    prev, cur, nxt = _halo_specs(tm, s, 0)
    out = pl.BlockSpec((tm, LANES), lambda g, i: (i, g))
    lin_spec = pl.BlockSpec((None, LANES, LANES), lambda g, i: (g, 0, 0))
    vec = pl.BlockSpec((1, LANES), lambda g, i: (0, g))
    return pl.pallas_call(
        body, name=name, grid=(len(POOL_HALF), n_tiles),
        in_specs=[prev, cur, nxt, out, lin_spec, vec],
        out_specs=[out, lin_spec, vec],
        out_shape=[jax.ShapeDtypeStruct((s, POOL_WIDTH), BF16),
                   jax.ShapeDtypeStruct((len(POOL_HALF), LANES, LANES), F32),
                   jax.ShapeDtypeStruct((1, POOL_WIDTH), F32)],
        compiler_params=_params(),
    )(dmixed, dmixed, dmixed, pooled, lin, scale)


GATE_TILE = 512
GATE_BLOCK0 = (QKV_WIDTH + POOL_WIDTH) // GATE_TILE
GATE_BLOCKS_PER_BRANCH = D_MODEL // GATE_TILE


def _sigmoid(v):
    return 1.0 / (1.0 + jnp.exp(-v))


def _gate_specs(tm):
    def zspec(br):
        return pl.BlockSpec((tm, GATE_TILE), lambda jj, i: (i, GATE_BLOCK0 + GATE_BLOCKS_PER_BRANCH * br + jj))

    def bspec(br):
        return pl.BlockSpec((1, GATE_TILE), lambda jj, i: (0, GATE_BLOCKS_PER_BRANCH * br + jj))

    row = pl.BlockSpec((tm, GATE_TILE), lambda jj, i: (i, jj))
    vec = pl.BlockSpec((1, GATE_TILE), lambda jj, i: (0, jj))
    return [zspec(0), zspec(1), zspec(2)], [bspec(0), bspec(1), bspec(2)], row, vec


def _gate_fwd(z, b_gate, ya, yb, yc, *, name, tm=512):
    s = z.shape[0]

    def body(z0, z1, z2, b0, b1, b2, ya_ref, yb_ref, yc_ref, out_ref):
        acc = _sigmoid(z0[...] + b0[...]) * ya_ref[...]
        acc = acc + _sigmoid(z1[...] + b1[...]) * yb_ref[...]
        acc = acc + _sigmoid(z2[...] + b2[...]) * yc_ref[...]
        out_ref[...] = acc.astype(BF16)

    zs, bs, row, _ = _gate_specs(tm)
    return pl.pallas_call(
        body, name=name, grid=(GATE_BLOCKS_PER_BRANCH, s // tm),
        in_specs=zs + bs + [row] * 3, out_specs=row,
        out_shape=jax.ShapeDtypeStruct((s, D_MODEL), BF16),
        compiler_params=_params(),
    )(z, z, z, b_gate, b_gate, b_gate, ya, yb, yc)


def _gate_bwd(z, b_gate, ya, yb, yc, dmerged, *, name, tm=512):
    s = z.shape[0]

    def body(z0, z1, z2, b0, b1, b2, ya_ref, yb_ref, yc_ref, dm_ref,
             dya_ref, dyb_ref, dyc_ref, dg0, dg1, dg2, db0, db1, db2):
        i = pl.program_id(1)
        dm = dm_ref[...]
        for z_ref, b_ref, y_ref, dy_ref, dg_ref, db_ref in (
                (z0, b0, ya_ref, dya_ref, dg0, db0), (z1, b1, yb_ref, dyb_ref, dg1, db1),
                (z2, b2, yc_ref, dyc_ref, dg2, db2)):
            gate = _sigmoid(z_ref[...] + b_ref[...])
            dy_ref[...] = (gate * dm).astype(BF16)
            dpre = dm * y_ref[...] * gate * (1.0 - gate)
            dg_ref[...] = dpre.astype(BF16)
            part = jnp.sum(dpre, axis=0, keepdims=True)

            @pl.when(i == 0)
            def _(db_ref=db_ref, part=part):
                db_ref[...] = part

            @pl.when(i > 0)
            def _(db_ref=db_ref, part=part):
                db_ref[...] += part

    zs, bs, row, vec = _gate_specs(tm)
    big = jax.ShapeDtypeStruct((s, D_MODEL), BF16)
    small = jax.ShapeDtypeStruct((1, D_MODEL), F32)
    return pl.pallas_call(
        body, name=name, grid=(GATE_BLOCKS_PER_BRANCH, s // tm),
        in_specs=zs + bs + [row] * 4, out_specs=[row] * 6 + [vec] * 3,
        out_shape=[big] * 6 + [small] * 3,
        compiler_params=_params(),
    )(z, z, z, b_gate, b_gate, b_gate, ya, yb, yc, dmerged)


def _swiglu_fwd(a, b, *, name, tm=512, tn=1408):
    s, f = a.shape

    def body(a_ref, b_ref, o_ref):
        av = a_ref[...].astype(F32)
        o_ref[...] = (av * _sigmoid(av) * b_ref[...].astype(F32)).astype(BF16)

    blk = pl.BlockSpec((tm, tn), lambda i, j: (i, j))
    return pl.pallas_call(
        body, name=name, grid=(s // tm, f // tn), in_specs=[blk, blk], out_specs=blk,
        out_shape=jax.ShapeDtypeStruct((s, f), BF16), compiler_params=_params(),
    )(a, b)


def _swiglu_bwd(a, b, df, *, name, tm=512, tn=1408):
    s, f = a.shape

    def body(a_ref, b_ref, df_ref, da_ref, db_ref):
        av = a_ref[...].astype(F32)
        dfv = df_ref[...].astype(F32)
        sg = _sigmoid(av)
        silu = av * sg
        da_ref[...] = (dfv * b_ref[...].astype(F32) * (sg + silu * (1.0 - sg))).astype(BF16)
        db_ref[...] = (dfv * silu).astype(BF16)

    blk = pl.BlockSpec((tm, tn), lambda i, j: (i, j))
    out = jax.ShapeDtypeStruct((s, f), BF16)
    return pl.pallas_call(
        body, name=name, grid=(s // tm, f // tn), in_specs=[blk] * 3, out_specs=[blk] * 2,
        out_shape=[out, out], compiler_params=_params(),
    )(a, b, df)


def _loss_head(y, target, *, name, tm=512):
    s, d = y.shape

    def body(y_ref, t_ref, part_ref, dy_ref):
        i = pl.program_id(0)
        err = y_ref[...] - t_ref[...]
        dy_ref[...] = err * (1.0 / d)
        part = jnp.sum(err * err, axis=0, keepdims=True) * (0.5 / d)

        @pl.when(i == 0)
        def _():
            part_ref[...] = part

        @pl.when(i > 0)
        def _():
            part_ref[...] += part

    row = pl.BlockSpec((tm, d), lambda i: (i, 0))
    vec = pl.BlockSpec((1, d), lambda i: (0, 0))
    return pl.pallas_call(
        body, name=name, grid=(s // tm,), in_specs=[row, row], out_specs=[vec, row],
        out_shape=[jax.ShapeDtypeStruct((1, d), F32), jax.ShapeDtypeStruct((s, d), F32)],
        compiler_params=_params(),
    )(y, target)


def _mesh_place():
    x, y, c = lax.axis_index('x'), lax.axis_index('y'), lax.axis_index('c')
    return x, y, c, 4 * x + 2 * y + c


def _peer(x, y, c, k):
    return (x ^ ((k >> 2) & 1), y ^ ((k >> 1) & 1), c ^ (k & 1))


def _exchange(bufs, *, gather, name):
    n = len(bufs)

    def body(*refs):
        start, wait = _exchange_plan(refs[:n], refs[n:2 * n], refs[2 * n:], gather)
        start()
        wait()

    extra = _Extra((bufs, gather))
    return pl.pallas_call(
        body, name=name, in_specs=extra.in_specs, out_specs=extra.out_specs,
        out_shape=extra.out_shape, scratch_shapes=extra.scratch,
    )(*bufs)


_EXCHANGE_SPEC = pl.BlockSpec(memory_space=pl.ANY)
SEMS_PER_BUFFER = 3


class _Extra:
    def __init__(self, ride):
        bufs = [] if ride is None else list(ride[0])
        self.in_specs = [_EXCHANGE_SPEC] * len(bufs)
        self.out_specs = [_EXCHANGE_SPEC] * len(bufs)
        self.out_shape = [jax.ShapeDtypeStruct((N_DEV,) + b.shape[-2:], b.dtype) for b in bufs]
        self.scratch = [pltpu.SemaphoreType.DMA((N_DEV - 1,)), pltpu.SemaphoreType.DMA((N_DEV - 1,)),
                        pltpu.SemaphoreType.DMA] * len(bufs)
        self.args = tuple(bufs)


def _carry_exchange(body, n_in, n_out, grid, ride):
    extra = _Extra(ride)
    if ride is None:
        return body, extra
    n, gather = len(ride[0]), ride[1]

    def carrying(*refs):
        ins, ride_ins = refs[:n_in], refs[n_in:n_in + n]
        outs, ride_outs = refs[n_in + n:n_in + n + n_out], refs[n_in + n + n_out:n_in + 2 * n + n_out]
        sems = refs[n_in + 2 * n + n_out:]
        ids = [pl.program_id(a) for a in range(len(grid))]
        first = functools.reduce(jnp.logical_and, [i == 0 for i in ids])
        last = functools.reduce(jnp.logical_and, [i == size - 1 for i, size in zip(ids, grid)])

        @pl.when(first)
        def _():
            _exchange_plan(ride_ins, ride_outs, sems, gather)[0]()

        body(*ins, *outs)

        @pl.when(last)
        def _():
            _exchange_plan(ride_ins, ride_outs, sems, gather)[1]()

    return carrying, extra


def _exchange_plan(in_refs, out_refs, sems, gather):
    x, y, c, me = _mesh_place()

    def src(b, slot):
        return in_refs[b] if gather else in_refs[b].at[slot]

    def copy(b, k, dst_slot):
        send_sems, recv_sems = sems[SEMS_PER_BUFFER * b], sems[SEMS_PER_BUFFER * b + 1]
        return pltpu.make_async_remote_copy(
            src_ref=src(b, me ^ k), dst_ref=out_refs[b].at[dst_slot],
            send_sem=send_sems.at[k - 1], recv_sem=recv_sems.at[k - 1],
            device_id=_peer(x, y, c, k), device_id_type=pl.DeviceIdType.MESH)

    def mine(b):
        return pltpu.make_async_copy(src(b, me), out_refs[b].at[me], sems[SEMS_PER_BUFFER * b + 2])

    buffers = range(len(in_refs))

    def start():
        for b in buffers:
            mine(b).start()
            for k in range(1, N_DEV):
                copy(b, k, me).start()

    def wait():
        for b in buffers:
            for k in range(1, N_DEV):
                copy(b, k, me ^ k).wait_recv()
        for b in buffers:
            for k in range(1, N_DEV):
                copy(b, k, me).wait_send()
            mine(b).wait()

    return start, wait


def _adamw(parts, w, m, v, *, name, tr, ride=None):
    n_layers = len(parts)
    rows, cols = parts[0].shape[1:]
    assert rows % tr == 0 and w.shape == (n_layers * rows, cols)
    tiles = rows // tr
    bias1 = 1.0 - ADAM_B1 ** ADAM_STEP
    bias2 = 1.0 - ADAM_B2 ** ADAM_STEP

    def body(*refs):
        p_refs = refs[:n_layers]
        w_ref, m_ref, v_ref, g_ref, d_ref, nm_ref, nv_ref = refs[n_layers:]

        def update(p_ref):
            g = p_ref[0].astype(F32)
            for j in range(1, N_DEV):
                g = g + p_ref[j].astype(F32)
            nm = ADAM_B1 * m_ref[...] + (1.0 - ADAM_B1) * g
            nv = ADAM_B2 * v_ref[...] + (1.0 - ADAM_B2) * (g * g)
            g_ref[...] = g
            nm_ref[...] = nm
            nv_ref[...] = nv
            d_ref[...] = -ADAM_LR * ((nm / bias1) / (jnp.sqrt(nv / bias2) + ADAM_EPS) + ADAM_WD * w_ref[...])

        for layer, p_ref in enumerate(p_refs):
            @pl.when(pl.program_id(0) == layer)
            def _(p_ref=p_ref):
                update(p_ref)

    def part_spec(layer):
        def index(l, i):
            return 0, jnp.where(l < layer, 0, jnp.where(l > layer, tiles - 1, i)), 0
        return pl.BlockSpec((N_DEV, tr, cols), index)

    blk = pl.BlockSpec((tr, cols), lambda l, i: (l * tiles + i, 0))
    out = jax.ShapeDtypeStruct((n_layers * rows, cols), F32)
    grid = (n_layers, tiles)
    body, extra = _carry_exchange(body, n_layers + 3, 4, grid, ride)
    res = pl.pallas_call(
        body, name=name, grid=grid,
        in_specs=[part_spec(layer) for layer in range(n_layers)] + [blk, blk, blk] + extra.in_specs,
        out_specs=[blk] * 4 + extra.out_specs, out_shape=[out] * 4 + extra.out_shape,
        scratch_shapes=extra.scratch, compiler_params=_params(),
    )(*parts, w, m, v, *extra.args)
    return res[:4], res[4:]


def _col_blocks(full):
    r, n = full.shape
    return full.reshape(r, N_DEV, n // N_DEV).transpose(1, 0, 2)


def _row_blocks(full):
    r, n = full.shape
    return full.reshape(N_DEV, r // N_DEV, n)


def _from_col_blocks(blocks):
    j, r, c = blocks.shape
    return blocks.transpose(1, 0, 2).reshape(r, j * c)


def _from_row_blocks(blocks):
    j, r, c = blocks.shape
    return blocks.reshape(j * r, c)


SHARD_ROWWISE = {'w_out', 'w_ffn_down'}
EARLY = ('w_ffn_down', 'w_ffn_gate', 'w_ffn_up', 'w_out', 'w_branch_a')
LATE = ('w_branch_b', 'w_branch_c')


def _blocks(name, full):
    return _row_blocks(full) if name in SHARD_ROWWISE else _col_blocks(full)


def _from_blocks(name, blocks):
    return _from_row_blocks(blocks) if name in SHARD_ROWWISE else _from_col_blocks(blocks)


def _pack(shards, names):
    lead = shards[names[0]].shape[:-2]
    return jnp.concatenate([shards[n].reshape(lead + (-1, PACK_COLS)) for n in names], axis=-2)


def _unpack(packed, shapes, names):
    out, off = {}, 0
    lead = packed.shape[:-2]
    for n in names:
        size = shapes[n][0] * shapes[n][1] // PACK_COLS
        out[n] = packed[..., off:off + size, :].reshape(lead + tuple(shapes[n]))
        off += size
    return out


def _pack_small(vals):
    flat = jnp.concatenate([vals[n].reshape(-1) for n in SMALL])
    return flat.reshape(-1, PACK_COLS)


def _unpack_small(packed, shapes):
    flat = packed.reshape(-1)
    out, off = {}, 0
    for n in SMALL:
        size = 1
        for dim in shapes[n]:
            size *= dim
        out[n] = flat[off:off + size].reshape(shapes[n])
        off += size
    return out


def _band_views(qkv, s):
    views = []
    for gi, d in enumerate(B_DILATIONS):
        starts = [A_Q_WIDTH + 2 * A_KV_WIDTH + part * B_WIDTH + gi * B_OUT_WIDTH for part in range(3)]
        group = jnp.concatenate([qkv[:, c0:c0 + B_OUT_WIDTH] for c0 in starts], axis=1)
        views.append(group.reshape(s // d, d * GROUP_QKV))
    return views


def _layer_fwd(x, p, tables, ride=None, after_ride=None):
    s = x.shape[0]
    cos, sin = tables
    h, h_t = _rms_fwd(x, p['norm_mix'], name='rms_mix_fwd')
    z = _mm(h, p['w_in'], dims='nn', out_dtype=BF16, tm=1024, tn=1664, tk=1024, name='mm_in_fwd')
    gains = _qk_gains(p['qn_a'], p['kn_a'], p['qn_b'], p['kn_b'])
    qkv = _qkrope_fwd(z, gains, cos, sin, name='qkrope_fwd')

    kdup = _dup_halves(qkv[:, A_Q_WIDTH:A_Q_WIDTH + A_KV_WIDTH])
    va = qkv[:, A_Q_WIDTH + A_KV_WIDTH:A_Q_WIDTH + 2 * A_KV_WIDTH]
    vdup = _dup_halves(va)
    oa, lse_a, *gathered = _attn_a_fwd(qkv, kdup, _with_ones(va), ride=ride, name='attn_a_fwd')
    if after_ride is not None:
        p = {**p, **after_ride(gathered)}
    ya = _mm(oa, p['w_branch_a'], dims='nn', out_dtype=BF16, tm=1024, tn=1024, tk=512, name='mm_branch_a_fwd')

    views = _band_views(qkv, s)
    o_g, lse_g = [], []
    for gi, d in enumerate(B_DILATIONS):
        o, lse = _band_fwd(views[gi], gi, d, name=f'band_fwd_d{d}')
        o_g.append(o.reshape(s, B_OUT_WIDTH))
        lse_g.append(lse.reshape(s, B_OUT_WIDTH))
    ob = _merge_fwd(o_g, lse_g, name='merge_fwd')
    yb = _mm(ob, p['w_branch_b'], dims='nn', out_dtype=BF16, tm=1024, tn=1024, tk=256, name='mm_branch_b_fwd')

    pooled, mixed = _pool_fwd(z, p['pool_lin'], p['pool_scale'], name='pool_fwd')
    yc = _mm(mixed, p['w_branch_c'], dims='nn', out_dtype=BF16, tm=1024, tn=1024, tk=512, name='mm_branch_c_fwd')

    merged = _gate_fwd(z, p['b_gate'], ya, yb, yc, name='gate_fwd')
    x_mid = _mm(merged, p['w_out'], dims='nn', out_dtype=F32, tm=1024, tn=1024, tk=1024, res=x, name='mm_out_fwd')

    h2, h2_t = _rms_fwd(x_mid, p['norm_ffn'], name='rms_ffn_fwd')
    fa = _mm(h2, p['w_ffn_gate'], dims='nn', out_dtype=BF16, tm=1024, tn=1408, tk=1024, name='mm_ffn_gate_fwd')
    fb = _mm(h2, p['w_ffn_up'], dims='nn', out_dtype=BF16, tm=1024, tn=1408, tk=1024, name='mm_ffn_up_fwd')
    f = _swiglu_fwd(fa, fb, name='swiglu_fwd')
    x_out = _mm(f, p['w_ffn_down'], dims='nn', out_dtype=F32, tm=1024, tn=512, tk=2816, res=x_mid, name='mm_ffn_down_fwd')

    saved = dict(x=x, h=h, z=z, gains=gains, qkv=qkv, kdup=kdup, vdup=vdup, oa=oa, lse_a=lse_a, o_g=o_g, lse_g=lse_g,
                 ob=ob, pooled=pooled, mixed=mixed, ya=ya, yb=yb, yc=yc, merged=merged, x_mid=x_mid, h2=h2,
                 fa=fa, fb=fb, f=f, views=views, h_t=h_t, h2_t=h2_t)
    return x_out, saved, p


def _fold_heads(v, heads):
    return v.reshape(heads, HEAD_DIM).sum(axis=0)


def _layer_bwd(dx, p, sv, tables, make_ride=None):
    s = dx.shape[0]
    cos, sin = tables
    g = {}

    df = _mm(dx, p['w_ffn_down'], dims='nt', out_dtype=BF16, tm=1024, tn=1408, tk=1024, name='mm_ffn_down_dx')
    g['w_ffn_down'] = _mm(sv['f'], dx, dims='tn', out_dtype=BF16, tm=1408, tn=1024, tk=512, name='mm_ffn_down_dw')
    da, db = _swiglu_bwd(sv['fa'], sv['fb'], df, name='swiglu_bwd')
    dh2 = _mm(da, p['w_ffn_gate'], dims='nt', out_dtype=F32, tm=1024, tn=512, tk=2816, name='mm_ffn_gate_dx')
    dh2 = _mm(db, p['w_ffn_up'], dims='nt', out_dtype=F32, tm=1024, tn=512, tk=2816, res=dh2, name='mm_ffn_up_dx')
    g['w_ffn_gate'] = _mm(sv['h2_t'], da, dims='nn', out_dtype=BF16, tm=1024, tn=2816, tk=512, name='mm_ffn_gate_dw')
    g['w_ffn_up'] = _mm(sv['h2_t'], db, dims='nn', out_dtype=BF16, tm=1024, tn=2816, tk=512, name='mm_ffn_up_dw')
    dx_mid, g['norm_ffn'] = _rms_bwd(sv['x_mid'], p['norm_ffn'], dh2, dx, name='rms_ffn_bwd')

    dmerged = _mm(dx_mid, p['w_out'], dims='nt', out_dtype=F32, tm=1024, tn=1024, tk=1024, name='mm_out_dx')
    g['w_out'] = _mm(sv['merged'], dx_mid, dims='tn', out_dtype=BF16, tm=1024, tn=1024, tk=512, name='mm_out_dw')
    dya, dyb, dyc, dg0, dg1, dg2, db0, db1, db2 = _gate_bwd(
        sv['z'], p['b_gate'], sv['ya'], sv['yb'], sv['yc'], dmerged, name='gate_bwd')
    g['b_gate'] = jnp.concatenate([db0, db1, db2], axis=1)

    doa = _mm(dya, p['w_branch_a'], dims='nt', out_dtype=BF16, tm=1024, tn=512, tk=1024, name='mm_branch_a_dx')
    g['w_branch_a'] = _mm(sv['oa'], dya, dims='tn', out_dtype=BF16, tm=512, tn=1024, tk=512, name='mm_branch_a_dw')
    ride = None if make_ride is None else make_ride(g)
    dqa, dkdup, dvdup, *arrived = _attn_a_bwd(sv['qkv'], sv['kdup'], sv['vdup'], sv['oa'], sv['lse_a'], doa,
                                              ride=ride, name='attn_a_bwd')

    def fold(dup):
        return jnp.concatenate([dup[:, 0:64] + dup[:, 64:128], dup[:, 128:192] + dup[:, 192:256]], axis=1)

    dka, dva = fold(dkdup), fold(dvdup)

    dob = _mm(dyb, p['w_branch_b'], dims='nt', out_dtype=F32, tm=1024, tn=256, tk=1024, name='mm_branch_b_dx')
    g['w_branch_b'] = _mm(sv['ob'], dyb, dims='tn', out_dtype=BF16, tm=256, tn=1024, tk=512, name='mm_branch_b_dw')
    merged_b = _merge_bwd(sv['o_g'], sv['lse_g'], dob, name='merge_bwd')
    do_g, dd_g = merged_b[:3], merged_b[3:]
    views = sv['views']
    dq_parts, dk_parts, dv_parts = [], [], []
    for gi, d in enumerate(B_DILATIONS):
        ln = s // d
        dq, dk, dv = _band_bwd(views[gi], do_g[gi].reshape(ln, d * B_OUT_WIDTH),
                               sv['lse_g'][gi].reshape(ln, d * B_OUT_WIDTH), dd_g[gi].reshape(ln, d * B_OUT_WIDTH),
                               gi, d, name=f'band_bwd_d{d}')
        dq_parts.append(dq.reshape(s, B_OUT_WIDTH))
        dk_parts.append(dk.reshape(s, B_OUT_WIDTH))
        dv_parts.append(dv.reshape(s, B_OUT_WIDTH))

    dmixed = _mm(dyc, p['w_branch_c'], dims='nt', out_dtype=F32, tm=1024, tn=512, tk=1024, name='mm_branch_c_dx')
    g['w_branch_c'] = _mm(sv['mixed'], dyc, dims='tn', out_dtype=BF16, tm=512, tn=1024, tk=512, name='mm_branch_c_dw')
    du, g['pool_lin'], g['pool_scale'] = _pool_bwd(dmixed, sv['pooled'], p['pool_lin'], p['pool_scale'], name='pool_bwd')

    dz_qkv, dgains = _qkrope_bwd(sv['z'], sv['gains'], cos, sin, [dqa, dka, dva] + dq_parts + dk_parts + dv_parts,
                                 name='qkrope_bwd')
    dgains = jnp.sum(dgains, axis=0)
    g['qn_a'] = _fold_heads(dgains[0:512], 8)
    g['kn_a'] = _fold_heads(dgains[512:640], 2)
    g['qn_b'] = _fold_heads(dgains[768:1536], 12)
    g['kn_b'] = _fold_heads(dgains[1536:2304], 12)

    dz = jnp.concatenate([dz_qkv, du, dg0, dg1, dg2], axis=1)
    dh = _mm(dz, p['w_in'], dims='nt', out_dtype=F32, tm=1024, tn=1024, tk=1664, name='mm_in_dx')
    g['w_in'] = _mm(sv['h_t'], dz, dims='nn', out_dtype=BF16, tm=1024, tn=3328, tk=512, name='mm_in_dw')
    dx_in, g['norm_mix'] = _rms_bwd(sv['x'], p['norm_mix'], dh, dx_mid, name='rms_mix_bwd')
    return dx_in, g, arrived


def _small_views(vals, l):
    return {
        'norm_mix': vals['norm_mix'][l][None, :], 'b_gate': vals['b_gate'][l][None, :],
        'qn_a': vals['qn_a'][l], 'kn_a': vals['kn_a'][l], 'qn_b': vals['qn_b'][l], 'kn_b': vals['kn_b'][l],
        'pool_lin': vals['pool_lin'][l], 'pool_scale': vals['pool_scale'][l][None, :],
        'norm_ffn': vals['norm_ffn'][l][None, :],
    }


def kernel(x, norm_mix, w_in, b_gate, qn_a, kn_a, qn_b, kn_b, pool_lin, pool_scale, w_branch_a, w_branch_b, w_branch_c, w_out, norm_ffn, w_ffn_gate, w_ffn_up, w_ffn_down, loss_target, m_norm_mix, m_w_in, m_b_gate, m_qn_a, m_kn_a, m_qn_b, m_kn_b, m_pool_lin, m_pool_scale, m_w_branch_a, m_w_branch_b, m_w_branch_c, m_w_out, m_norm_ffn, m_w_ffn_gate, m_w_ffn_up, m_w_ffn_down, v_norm_mix, v_w_in, v_b_gate, v_qn_a, v_kn_a, v_qn_b, v_kn_b, v_pool_lin, v_pool_scale, v_w_branch_a, v_w_branch_b, v_w_branch_c, v_w_out, v_norm_ffn, v_w_ffn_gate, v_w_ffn_up, v_w_ffn_down):
    w = dict(norm_mix=norm_mix, w_in=w_in, b_gate=b_gate, qn_a=qn_a, kn_a=kn_a, qn_b=qn_b, kn_b=kn_b,
             pool_lin=pool_lin, pool_scale=pool_scale, w_branch_a=w_branch_a, w_branch_b=w_branch_b,
             w_branch_c=w_branch_c, w_out=w_out, norm_ffn=norm_ffn, w_ffn_gate=w_ffn_gate, w_ffn_up=w_ffn_up,
             w_ffn_down=w_ffn_down)
    m = dict(norm_mix=m_norm_mix, w_in=m_w_in, b_gate=m_b_gate, qn_a=m_qn_a, kn_a=m_kn_a, qn_b=m_qn_b, kn_b=m_kn_b,
             pool_lin=m_pool_lin, pool_scale=m_pool_scale, w_branch_a=m_w_branch_a, w_branch_b=m_w_branch_b,
             w_branch_c=m_w_branch_c, w_out=m_w_out, norm_ffn=m_norm_ffn, w_ffn_gate=m_w_ffn_gate,
             w_ffn_up=m_w_ffn_up, w_ffn_down=m_w_ffn_down)
    v = dict(norm_mix=v_norm_mix, w_in=v_w_in, b_gate=v_b_gate, qn_a=v_qn_a, kn_a=v_kn_a, qn_b=v_qn_b, kn_b=v_kn_b,
             pool_lin=v_pool_lin, pool_scale=v_pool_scale, w_branch_a=v_w_branch_a, w_branch_b=v_w_branch_b,
             w_branch_c=v_w_branch_c, w_out=v_w_out, norm_ffn=v_norm_ffn, w_ffn_gate=v_w_ffn_gate,
             w_ffn_up=v_w_ffn_up, w_ffn_down=v_w_ffn_down)
    depth = w_in.shape[0]
    shard_shapes = {n: w[n].shape[1:] for n in SHARDED}
    small_shapes = {n: w[n].shape for n in SMALL}

    def my_shards(l, names):
        return _pack({n: w[n][l].astype(BF16) for n in names}, names)

    def full_weights(gathered, names):
        blocks = _unpack(gathered, shard_shapes, names)
        return {n: _from_blocks(n, blocks[n]) for n in names}

    def blocks_to_send(grad, names):
        return _pack({n: _blocks(n, grad[n]) for n in names}, names)

    tables = _rope_tables(x.shape[1])
    (w_in_blocks,) = _exchange([w['w_in'][0].astype(BF16)], gather=True, name='gather_weights')
    layers, saved = [], []
    act = x[0]
    for l in range(depth):
        p = {'w_in': _from_col_blocks(w_in_blocks), **_small_views(w, l)}
        bufs = [my_shards(l, EARLY), my_shards(l, LATE)]
        if l + 1 < depth:
            bufs.append(w['w_in'][l + 1].astype(BF16))
        stash = {}

        def after_ride(gathered, stash=stash):
            stash['next_w_in'] = gathered[2] if len(gathered) > 2 else None
            return {**full_weights(gathered[0], EARLY), **full_weights(gathered[1], LATE)}

        act, sv, p = _layer_fwd(act, p, tables, ride=(bufs, True), after_ride=after_ride)
        w_in_blocks = stash['next_w_in']
        layers.append(p)
        saved.append(sv)
    part, dx = _loss_head(act, loss_target[0], name='loss_head')
    loss = lax.psum(jnp.sum(part), ('x', 'y', 'c'))

    early_parts, late_parts, w_in_parts = ([None] * depth for _ in range(3))
    grads = [None] * depth
    pending = []
    for l in reversed(range(depth)):
        def make_ride(g, pending=pending):
            return [blocks_to_send(g, EARLY)] + pending, False

        dx, grads[l], arrived = _layer_bwd(dx, layers[l], saved[l], tables, make_ride=make_ride)
        early_parts[l] = arrived[0]
        if pending:
            w_in_parts[l + 1], late_parts[l + 1] = arrived[1], arrived[2]
        pending = [_col_blocks(grads[l]['w_in']), blocks_to_send(grads[l], LATE)]
    grad_x = dx

    new = {}

    def update_packed(names, parts, label, ride=None):
        packed = [_pack({n: t[n] for n in names}, names).reshape(-1, PACK_COLS) for t in (w, m, v)]
        res, arrived = _adamw(parts, *packed, name=label, tr=96, ride=ride)
        unpacked = [_unpack(r.reshape(depth, -1, PACK_COLS), shard_shapes, names) for r in res]
        for n in names:
            new[n] = tuple(u[n] for u in unpacked)
        return arrived

    w_in_parts[0], late_parts[0] = update_packed(EARLY, early_parts, 'adamw_early', ride=(pending, False))
    small_grad = {n: jnp.stack([grads[l][n].reshape(small_shapes[n][1:]) for l in range(depth)]) for n in SMALL}
    w_in_shape = w['w_in'].shape
    res, (small_parts,) = _adamw(w_in_parts, *(t['w_in'].reshape(-1, w_in_shape[-1]) for t in (w, m, v)),
                                 name='adamw_w_in', tr=128, ride=([_pack_small(small_grad)], True))
    new['w_in'] = tuple(r.reshape(w_in_shape) for r in res)
    update_packed(LATE, late_parts, 'adamw_late')

    res, _ = _adamw([small_parts], _pack_small({n: w[n] for n in SMALL}), _pack_small({n: m[n] for n in SMALL}),
                    _pack_small({n: v[n] for n in SMALL}), name='adamw_small', tr=small_parts.shape[1])
    unpacked = [_unpack_small(r, small_shapes) for r in res]
    for n in SMALL:
        new[n] = tuple(u[n] for u in unpacked)

    outs = [loss, grad_x[None]]
    for idx in range(4):
        outs.extend(new[n][idx] for n in WEIGHTS)
    return tuple(outs)
```

```python
import functools

import jax
import jax.numpy as jnp
from jax import lax
from jax.experimental import pallas as pl
from jax.experimental.pallas import tpu as pltpu

F32 = jnp.float32
BF16 = jnp.bfloat16

N_DEV = 8
D_MODEL = 1024
DEPTH = 4
HEAD_DIM = 64
LANES = 128
A_Q_WIDTH = 512
A_KV_WIDTH = 128
B_WIDTH = 768
B_GROUPS = 3
B_DILATIONS = (1, 4, 16)
B_HALF_SPAN = 64
B_OUT_WIDTH = 256
POOL_WIDTH = 512
POOL_HALF = (1, 2, 4, 8)
GATE_WIDTH = 3072
QKV_WIDTH = A_Q_WIDTH + 2 * A_KV_WIDTH + 3 * B_WIDTH
IN_WIDTH = QKV_WIDTH + POOL_WIDTH + GATE_WIDTH
D_FF = 2816
GRID_W = 64
ROPE_THETA = 10000.0
EPS = 1e-6
NEG_INF = -1e30
ATTN_SCALE = HEAD_DIM ** -0.5

ADAM_LR = 0.001
ADAM_B1 = 0.9
ADAM_B2 = 0.999
ADAM_EPS = 1e-08
ADAM_WD = 0.01
ADAM_STEP = 10

PACK_COLS = 1024
VMEM_LIMIT = 56 * 1024 * 1024

SHARDED = ('w_in', 'w_branch_a', 'w_branch_b', 'w_branch_c', 'w_out', 'w_ffn_gate', 'w_ffn_up', 'w_ffn_down')
SMALL = ('norm_mix', 'b_gate', 'qn_a', 'kn_a', 'qn_b', 'kn_b', 'pool_lin', 'pool_scale', 'norm_ffn')
WEIGHTS = ('norm_mix', 'w_in', 'b_gate', 'qn_a', 'kn_a', 'qn_b', 'kn_b', 'pool_lin', 'pool_scale',
           'w_branch_a', 'w_branch_b', 'w_branch_c', 'w_out', 'norm_ffn', 'w_ffn_gate', 'w_ffn_up', 'w_ffn_down')

NN = (((1,), (0,)), ((), ()))
NT = (((1,), (1,)), ((), ()))
TN = (((0,), (0,)), ((), ()))


def _params(vmem=None):
    return pltpu.CompilerParams(vmem_limit_bytes=VMEM_LIMIT if vmem is None else vmem)


def _lane_iota(n=LANES):
    return lax.broadcasted_iota(jnp.int32, (1, n), 1)


def _swap(x, sh, lane):
    n = x.shape[-1]
    down = pltpu.roll(x, sh, axis=1)
    up = pltpu.roll(x, n - sh, axis=1)
    return jnp.where((lane & sh) == 0, up, down)


def _head_sum(v):
    w = v.shape[-1]
    r = lax.broadcasted_iota(jnp.int32, (w, w), 0) // HEAD_DIM
    c = lax.broadcasted_iota(jnp.int32, (w, w), 1) // HEAD_DIM
    ones = (r == c).astype(BF16)
    hi = v.astype(BF16)
    lo = (v - hi.astype(F32)).astype(BF16)
    return (lax.dot_general(hi, ones, NN, preferred_element_type=F32)
            + lax.dot_general(lo, ones, NN, preferred_element_type=F32))


def _mm(a, b, *, dims, out_dtype, tm, tn, tk, name, res=None):
    if dims == 'nn':
        (m, k), n = a.shape, b.shape[1]
    elif dims == 'nt':
        (m, k), n = a.shape, b.shape[0]
    else:
        (k, m), n = a.shape, b.shape[1]
    tm, tn, tk = min(tm, m), min(tn, n), min(tk, k)
    assert m % tm == 0 and n % tn == 0 and k % tk == 0, (name, m, n, k, tm, tn, tk)
    nk = k // tk
    if dims == 'tn':
        a_spec = pl.BlockSpec((tk, tm), lambda i, j, kk: (kk, i))
    else:
        a_spec = pl.BlockSpec((tm, tk), lambda i, j, kk: (i, kk))
    if dims == 'nt':
        b_spec = pl.BlockSpec((tn, tk), lambda i, j, kk: (j, kk))
    else:
        b_spec = pl.BlockSpec((tk, tn), lambda i, j, kk: (kk, j))
    o_spec = pl.BlockSpec((tm, tn), lambda i, j, kk: (i, j))
    dn = {'nn': NN, 'nt': NT, 'tn': TN}[dims]
    has_res = res is not None

    def body(*refs):
        if has_res:
            a_ref, b_ref, r_ref, o_ref, acc_ref = refs
        else:
            a_ref, b_ref, o_ref, acc_ref = refs
        prod = lax.dot_general(a_ref[...].astype(BF16), b_ref[...].astype(BF16), dn,
                               preferred_element_type=F32)

        def finish(total):
            if has_res:
                total = total + r_ref[...]
            o_ref[...] = total.astype(out_dtype)

        if nk == 1:
            finish(prod)
        else:
            kk = pl.program_id(2)

            @pl.when(kk == 0)
            def _():
                acc_ref[...] = prod

            @pl.when(kk > 0)
            def _():
                acc_ref[...] += prod

            @pl.when(kk == nk - 1)
            def _():
                finish(acc_ref[...])

    in_specs = [a_spec, b_spec] + ([o_spec] if has_res else [])
    args = (a, b) + ((res,) if has_res else ())
    acc_shape = (tm, tn) if nk > 1 else (8, LANES)
    return pl.pallas_call(
        body, name=name, grid=(m // tm, n // tn, nk),
        in_specs=in_specs, out_specs=o_spec,
        out_shape=jax.ShapeDtypeStruct((m, n), out_dtype),
        scratch_shapes=[pltpu.VMEM(acc_shape, F32)],
        compiler_params=_params(),
    )(*args)


def _rms_fwd(x, g, *, name, tm=512):
    s, d = x.shape

    def body(x_ref, g_ref, h_ref, ht_ref):
        xv = x_ref[...]
        rstd = lax.rsqrt(jnp.mean(xv * xv, axis=-1, keepdims=True) + EPS)
        h = xv * rstd * g_ref[...]
        h_ref[...] = h.astype(BF16)
        ht_ref[...] = h.T.astype(BF16)

    return pl.pallas_call(
        body, name=name, grid=(s // tm,),
        in_specs=[pl.BlockSpec((tm, d), lambda i: (i, 0)), pl.BlockSpec((1, d), lambda i: (0, 0))],
        out_specs=[pl.BlockSpec((tm, d), lambda i: (i, 0)), pl.BlockSpec((d, tm), lambda i: (0, i))],
        out_shape=[jax.ShapeDtypeStruct((s, d), BF16), jax.ShapeDtypeStruct((d, s), BF16)],
        compiler_params=_params(),
    )(x, g)


def _rms_bwd(x, g, dh, dres, *, name, tm=512):
    s, d = x.shape

    def body(x_ref, g_ref, dh_ref, dres_ref, dx_ref, dg_ref):
        i = pl.program_id(0)
        xv = x_ref[...]
        rstd = lax.rsqrt(jnp.mean(xv * xv, axis=-1, keepdims=True) + EPS)
        xhat = xv * rstd
        dhv = dh_ref[...]
        dxhat = dhv * g_ref[...]
        proj = jnp.mean(dxhat * xhat, axis=-1, keepdims=True)
        dx_ref[...] = dres_ref[...] + rstd * (dxhat - xhat * proj)
        part = jnp.sum(dhv * xhat, axis=0, keepdims=True)

        @pl.when(i == 0)
        def _():
            dg_ref[...] = part

        @pl.when(i > 0)
        def _():
            dg_ref[...] += part

    row = pl.BlockSpec((tm, d), lambda i: (i, 0))
    vec = pl.BlockSpec((1, d), lambda i: (0, 0))
    return pl.pallas_call(
        body, name=name, grid=(s // tm,),
        in_specs=[row, vec, row, row], out_specs=[row, vec],
        out_shape=[jax.ShapeDtypeStruct((s, d), F32), jax.ShapeDtypeStruct((1, d), F32)],
        compiler_params=_params(),
    )(x, g, dh, dres)


N_QKV_BLOCKS = QKV_WIDTH // LANES
A_BLOCKS = (A_Q_WIDTH + 2 * A_KV_WIDTH) // LANES
V_A_BLOCK = A_BLOCKS - 1
V_B_FIRST = A_BLOCKS + 2 * (B_WIDTH // LANES)


def _qk_kind(j):
    return jnp.where(j < A_BLOCKS, 0, 1)


def _is_v_block(j):
    return (j == V_A_BLOCK) | (j >= V_B_FIRST)


def _rope_tables(s):
    def ang(pos, dim):
        inv = ROPE_THETA ** (-jnp.arange(0, dim, 2, dtype=F32) / dim)
        return pos.astype(F32)[:, None] * inv[None, :]
    t = jnp.arange(s)
    a_row = ang(t // GRID_W, HEAD_DIM // 2)
    a_col = ang(t % GRID_W, HEAD_DIM // 2)
    a_seq = ang(t, HEAD_DIM)
    cos_a = jnp.concatenate([jnp.cos(a_row)] * 2 + [jnp.cos(a_col)] * 2, axis=-1)
    sin_a = jnp.concatenate([-jnp.sin(a_row), jnp.sin(a_row), -jnp.sin(a_col), jnp.sin(a_col)], axis=-1)
    cos_b = jnp.concatenate([jnp.cos(a_seq)] * 2, axis=-1)
    sin_b = jnp.concatenate([-jnp.sin(a_seq), jnp.sin(a_seq)], axis=-1)
    cos = jnp.stack([jnp.tile(cos_a, (1, 2)), jnp.tile(cos_b, (1, 2))])
    sin = jnp.stack([jnp.tile(sin_a, (1, 2)), jnp.tile(sin_b, (1, 2))])
    return cos, sin


def _qk_gains(qn_a, kn_a, qn_b, kn_b):
    one = jnp.ones((HEAD_DIM,), F32)
    parts = [jnp.tile(qn_a, 8), jnp.tile(kn_a, 2), jnp.tile(one, 2),
             jnp.tile(qn_b, 12), jnp.tile(kn_b, 12), jnp.tile(one, 12)]
    return jnp.concatenate(parts)[None, :]


def _qkrope_fwd(z, gains, cos, sin, *, name, tm=2048):
    s = z.shape[0]
    tm = min(tm, s)

    def body(z_ref, g_ref, c_ref, s_ref, o_ref):
        j = pl.program_id(1)
        lane = _lane_iota()
        xv = z_ref[...].astype(F32)

        def normed_rope(pair):
            ms = _head_sum(xv * xv) * (1.0 / HEAD_DIM)
            n = xv * lax.rsqrt(ms + EPS) * g_ref[...]
            return n * c_ref[...] + _swap(n, pair, lane) * s_ref[...]

        @pl.when(_is_v_block(j))
        def _():
            o_ref[...] = xv.astype(BF16)

        @pl.when(jnp.logical_not(_is_v_block(j)) & (j < A_BLOCKS))
        def _():
            o_ref[...] = normed_rope(HEAD_DIM // 4).astype(BF16)

        @pl.when(jnp.logical_not(_is_v_block(j)) & (j >= A_BLOCKS))
        def _():
            o_ref[...] = normed_rope(HEAD_DIM // 2).astype(BF16)

    tab = pl.BlockSpec((None, tm, LANES), lambda i, j: (_qk_kind(j), i, 0))
    blk = pl.BlockSpec((tm, LANES), lambda i, j: (i, j))
    return pl.pallas_call(
        body, name=name, grid=(s // tm, N_QKV_BLOCKS),
        in_specs=[blk, pl.BlockSpec((1, LANES), lambda i, j: (0, j)), tab, tab],
        out_specs=blk,
        out_shape=jax.ShapeDtypeStruct((s, QKV_WIDTH), BF16),
        compiler_params=_params(),
    )(z, gains, cos, sin)


SUBLANES = 8


def _qkrope_bwd(z, gains, cos, sin, sources, *, name, tm=2048):
    s = z.shape[0]
    tm = min(tm, s)
    n_src = len(sources)
    widths = [a.shape[1] // LANES for a in sources]
    firsts = [sum(widths[:k]) for k in range(n_src)]
    assert sum(widths) == N_QKV_BLOCKS

    def body(z_ref, g_ref, c_ref, s_ref, *rest):
        src_refs, (dz_ref, dg_ref, dy_ref) = rest[:n_src], rest[n_src:]
        j = pl.program_id(1)
        lane = _lane_iota()
        xv = z_ref[...].astype(F32)
        for src_ref, first, width in zip(src_refs, firsts, widths):
            @pl.when((j >= first) & (j < first + width))
            def _(src_ref=src_ref):
                dy_ref[...] = src_ref[...].astype(F32)
        dy = dy_ref[...]

        def back(pair):
            ms = _head_sum(xv * xv) * (1.0 / HEAD_DIM)
            rstd = lax.rsqrt(ms + EPS)
            xhat = xv * rstd
            dn = dy * c_ref[...] + _swap(dy * s_ref[...], pair, lane)
            dg_ref[...] = jnp.sum((dn * xhat).reshape(tm // SUBLANES, SUBLANES, LANES), axis=0)
            dxhat = dn * g_ref[...]
            proj = _head_sum(dxhat * xhat) * (1.0 / HEAD_DIM)
            dz_ref[...] = (rstd * (dxhat - xhat * proj)).astype(BF16)

        @pl.when(_is_v_block(j))
        def _():
            dz_ref[...] = dy.astype(BF16)
            dg_ref[...] = jnp.zeros_like(dg_ref)

        @pl.when(jnp.logical_not(_is_v_block(j)) & (j < A_BLOCKS))
        def _():
            back(HEAD_DIM // 4)

        @pl.when(jnp.logical_not(_is_v_block(j)) & (j >= A_BLOCKS))
        def _():
            back(HEAD_DIM // 2)

    def src_spec(first, width):
        return pl.BlockSpec((tm, LANES), lambda i, j: (i, jnp.clip(j - first, 0, width - 1)))

    tab = pl.BlockSpec((None, tm, LANES), lambda i, j: (_qk_kind(j), i, 0))
    blk = pl.BlockSpec((tm, LANES), lambda i, j: (i, j))
    return pl.pallas_call(
        body, name=name, grid=(s // tm, N_QKV_BLOCKS),
        in_specs=[blk, pl.BlockSpec((1, LANES), lambda i, j: (0, j)), tab, tab]
        + [src_spec(first, width) for first, width in zip(firsts, widths)],
        out_specs=[blk, pl.BlockSpec((SUBLANES, LANES), lambda i, j: (i, j))],
        out_shape=[jax.ShapeDtypeStruct((s, QKV_WIDTH), BF16),
                   jax.ShapeDtypeStruct((s // tm * SUBLANES, QKV_WIDTH), F32)],
        scratch_shapes=[pltpu.VMEM((tm, LANES), F32)],
        compiler_params=_params(),
    )(z, gains, cos, sin, *sources)


def _dup_halves(kv):
    h0, h1 = kv[:, :HEAD_DIM], kv[:, HEAD_DIM:]
    return jnp.concatenate([h0, h0, h1, h1], axis=1)


def _with_ones(kv):
    h0, h1 = kv[:, :HEAD_DIM], kv[:, HEAD_DIM:]
    one = jnp.ones_like(h0)
    return jnp.concatenate([h0, one, one, h0, h1, one, one, h1], axis=1)


def _attn_a_fwd(qkv, kdup, vones, *, name, ride=None, tq=256, tk=512, unroll=8):
    s = qkv.shape[0]
    tq, tk = min(tq, s), min(tk, s)
    n_chunks = s // tk
    unroll = min(unroll, n_chunks)
    assert n_chunks % unroll == 0

    def body(q_ref, k_ref, v_ref, o_ref, lse_ref):
        lane = _lane_iota()
        low = lane < HEAD_DIM
        q = q_ref[...]
        zero = jnp.zeros_like(q)
        qm = [jnp.where(low, q, zero) * ATTN_SCALE, jnp.where(low, zero, q) * ATTN_SCALE]

        def chunks(c, carry):
            state = list(carry)
            scs = []
            for u in range(unroll):
                off = pl.multiple_of((c * unroll + u) * tk, tk)
                kc = k_ref[pl.ds(off, tk), :]
                scs.append([lax.dot_general(qm[e], kc, NT, preferred_element_type=F32) for e in range(2)])
            for u in range(unroll):
                off = pl.multiple_of((c * unroll + u) * tk, tk)
                for e in range(2):
                    m, acc = state[2 * e], state[2 * e + 1]
                    ve = v_ref[pl.ds(off, tk), e * LANES:(e + 1) * LANES]
                    m_new = jnp.maximum(m, jnp.max(scs[u][e], axis=1, keepdims=True))
                    alpha = jnp.exp(m - m_new)
                    p = jnp.exp(scs[u][e] - m_new).astype(BF16)
                    state[2 * e] = m_new
                    state[2 * e + 1] = alpha * acc + lax.dot_general(p, ve, NN, preferred_element_type=F32)
            return tuple(state)

        m_init = jnp.full((tq, 1), NEG_INF, F32)
        a_init = jnp.zeros((tq, LANES), F32)
        m0, a0, m1, a1 = lax.fori_loop(0, n_chunks // unroll, chunks, (m_init, a_init, m_init, a_init))
        l0 = pltpu.roll(a0, HEAD_DIM, axis=1)
        l1 = pltpu.roll(a1, HEAD_DIM, axis=1)
        o_ref[...] = jnp.where(low, a0 / l0, a1 / l1).astype(BF16)
        lse_ref[...] = jnp.where(low, m0 + jnp.log(l0), m1 + jnp.log(l1))

    q_spec = pl.BlockSpec((tq, LANES), lambda hb, qi: (qi, hb))
    k_spec = pl.BlockSpec((s, LANES), lambda hb, qi: (0, hb // 2))
    v_spec = pl.BlockSpec((s, 2 * LANES), lambda hb, qi: (0, hb // 2))
    grid = (A_Q_WIDTH // LANES, s // tq)
    body, extra = _carry_exchange(body, 3, 2, grid, ride)
    return pl.pallas_call(
        body, name=name, grid=grid,
        in_specs=[q_spec, k_spec, v_spec] + extra.in_specs, out_specs=[q_spec, q_spec] + extra.out_specs,
        out_shape=[jax.ShapeDtypeStruct((s, A_Q_WIDTH), BF16), jax.ShapeDtypeStruct((s, A_Q_WIDTH), F32)]
        + extra.out_shape,
        scratch_shapes=extra.scratch, compiler_params=_params(),
    )(qkv, kdup, vones, *extra.args)


def _attn_a_bwd(qkv, kdup, vdup, o, lse, do, *, name, ride=None, tq=512, tk=512, unroll=2):
    s = qkv.shape[0]
    tq, tk = min(tq, s), min(tk, s)
    n_chunks = s // tk
    unroll = min(unroll, n_chunks)
    assert n_chunks % unroll == 0

    def body(q_ref, k_ref, v_ref, o_ref, lse_ref, do_ref, dq_ref, dk_ref, dv_ref):
        first = (pl.program_id(1) == 0) & (pl.program_id(2) == 0)

        @pl.when(first)
        def _():
            dk_ref[...] = jnp.zeros_like(dk_ref)
            dv_ref[...] = jnp.zeros_like(dv_ref)

        lane = _lane_iota()
        low = lane < HEAD_DIM
        q = q_ref[...]
        dov = do_ref[...]
        zero = jnp.zeros_like(q)
        prod = dov.astype(F32) * o_ref[...].astype(F32)
        lsev = lse_ref[...]
        qs = [jnp.where(low, q, zero) * ATTN_SCALE, jnp.where(low, zero, q) * ATTN_SCALE]
        dom = [jnp.where(low, dov, zero), jnp.where(low, zero, dov)]
        delta = [jnp.sum(jnp.where(low, prod, 0.0), axis=1, keepdims=True),
                 jnp.sum(jnp.where(low, 0.0, prod), axis=1, keepdims=True)]
        lse = [lsev[:, 0:1], lsev[:, HEAD_DIM:HEAD_DIM + 1]]
        qs_both = jnp.concatenate(qs, axis=0)
        dom_both = jnp.concatenate(dom, axis=0)

        def chunks(c, carry):
            dqs = list(carry)
            for u in range(unroll):
                off = pl.multiple_of((c * unroll + u) * tk, tk)
                kc = k_ref[pl.ds(off, tk), :]
                vc = v_ref[pl.ds(off, tk), :]
                ps, dss = [], []
                for e in range(2):
                    sc = lax.dot_general(qs[e], kc, NT, preferred_element_type=F32)
                    p = jnp.exp(sc - lse[e])
                    dp = lax.dot_general(dom[e], vc, NT, preferred_element_type=F32)
                    ds = (p * (dp - delta[e])).astype(BF16)
                    ps.append(p.astype(BF16))
                    dss.append(ds)
                    dqs[e] = dqs[e] + lax.dot_general(ds, kc, NN, preferred_element_type=F32)
                dv_ref[pl.ds(off, tk), :] += lax.dot_general(jnp.concatenate(ps, axis=0), dom_both, TN,
                                                             preferred_element_type=F32)
                dk_ref[pl.ds(off, tk), :] += lax.dot_general(jnp.concatenate(dss, axis=0), qs_both, TN,
                                                             preferred_element_type=F32)
            return tuple(dqs)

        dq_init = jnp.zeros((tq, LANES), F32)
        dq0, dq1 = lax.fori_loop(0, n_chunks // unroll, chunks, (dq_init, dq_init))
        dq_ref[...] = (jnp.where(low, dq0, dq1) * ATTN_SCALE).astype(BF16)

    q_spec = pl.BlockSpec((tq, LANES), lambda kvh, hb, qi: (qi, kvh * 2 + hb))
    kv_spec = pl.BlockSpec((s, LANES), lambda kvh, hb, qi: (0, kvh))
    grid = (2, 2, s // tq)
    body, extra = _carry_exchange(body, 6, 3, grid, ride)
    return pl.pallas_call(
        body, name=name, grid=grid,
        in_specs=[q_spec, kv_spec, kv_spec, q_spec, q_spec, q_spec] + extra.in_specs,
        out_specs=[q_spec, kv_spec, kv_spec] + extra.out_specs,
        out_shape=[jax.ShapeDtypeStruct((s, A_Q_WIDTH), BF16),
                   jax.ShapeDtypeStruct((s, 2 * LANES), F32), jax.ShapeDtypeStruct((s, 2 * LANES), F32)]
        + extra.out_shape,
        scratch_shapes=extra.scratch, compiler_params=_params(),
    )(qkv, kdup, vdup, o, lse, do, *extra.args)


BAND_Q = 128
GROUP_QKV = 3 * B_OUT_WIDTH
TOKEN_BLOCKS = GROUP_QKV // LANES
B_Q_BLOCK0 = 0
B_K_BLOCK0 = B_OUT_WIDTH // LANES
B_V_BLOCK0 = 2 * (B_OUT_WIDTH // LANES)


BAND_UNROLL = 4


def _unrolled_loop(n, step):
    unroll = BAND_UNROLL
    while n % unroll:
        unroll //= 2

    def body(it, carry):
        for u in range(unroll):
            step(it * unroll + u)
        return carry

    lax.fori_loop(0, n // unroll, body, 0)


def _band_geometry(length):
    seg = min(length, 2048)
    win = min(2 * BAND_Q, length)
    return seg, win


def _band_window(qs, length, win):
    st = jnp.clip(qs - B_HALF_SPAN, 0, length - win)
    st = pl.multiple_of(st, B_HALF_SPAN)
    qpos = qs + lax.broadcasted_iota(jnp.int32, (BAND_Q, 1), 0)
    kpos = st + lax.broadcasted_iota(jnp.int32, (1, win), 1)
    return st, jnp.abs(qpos - kpos) <= B_HALF_SPAN


def _band_fwd(qkv_view, gi, dil, *, name):
    length = qkv_view.shape[0]
    seg, win = _band_geometry(length)
    n_sub = seg // BAND_Q

    def body(q_ref, k_ref, v_ref, o_ref, lse_ref):
        seg_i = pl.program_id(2)
        lane = _lane_iota()
        low = lane < HEAD_DIM

        def one(i):
            ql = pl.multiple_of(i * BAND_Q, BAND_Q)
            st, valid = _band_window(seg_i * seg + ql, length, win)
            q = q_ref[pl.ds(ql, BAND_Q), :]
            kw = k_ref[pl.ds(st, win), :]
            vw = v_ref[pl.ds(st, win), :]
            outs, lses = [], []
            for e in range(2):
                mine = (lane >= HEAD_DIM) if e else (lane < HEAD_DIM)
                qm = jnp.where(mine, q, jnp.zeros_like(q)) * ATTN_SCALE
                sc = lax.dot_general(qm, kw, NT, preferred_element_type=F32)
                sc = jnp.where(valid, sc, NEG_INF)
                m = jnp.max(sc, axis=1, keepdims=True)
                p = jnp.exp(sc - m)
                l = jnp.sum(p, axis=1, keepdims=True)
                outs.append(lax.dot_general(p.astype(BF16), vw, NN, preferred_element_type=F32) / l)
                lses.append(m + jnp.log(l))
            o_ref[pl.ds(ql, BAND_Q), :] = jnp.where(low, outs[0], outs[1]).astype(BF16)
            lse_ref[pl.ds(ql, BAND_Q), :] = jnp.where(low, lses[0], lses[1])

        _unrolled_loop(n_sub, one)

    def col(base):
        return lambda r, hp, sg: (0, r * TOKEN_BLOCKS + base + hp)

    q_spec = pl.BlockSpec((seg, LANES), lambda r, hp, sg: (sg, r * TOKEN_BLOCKS + B_Q_BLOCK0 + hp))
    out_spec = pl.BlockSpec((seg, LANES), lambda r, hp, sg: (sg, r * 2 + hp))
    return pl.pallas_call(
        body, name=name, grid=(dil, 2, length // seg),
        in_specs=[q_spec, pl.BlockSpec((length, LANES), col(B_K_BLOCK0)), pl.BlockSpec((length, LANES), col(B_V_BLOCK0))],
        out_specs=[out_spec, out_spec],
        out_shape=[jax.ShapeDtypeStruct((length, dil * B_OUT_WIDTH), BF16),
                   jax.ShapeDtypeStruct((length, dil * B_OUT_WIDTH), F32)],
        compiler_params=_params(),
    )(qkv_view, qkv_view, qkv_view)


def _band_bwd(qkv_view, do, lse, dd, gi, dil, *, name):
    length = qkv_view.shape[0]
    seg, win = _band_geometry(length)
    n_sub = seg // BAND_Q

    def body(q_ref, k_ref, v_ref, do_ref, lse_ref, dd_ref, dq_ref, dk_ref, dv_ref):
        seg_i = pl.program_id(2)
        lane = _lane_iota()

        @pl.when(seg_i == 0)
        def _():
            dk_ref[...] = jnp.zeros_like(dk_ref)
            dv_ref[...] = jnp.zeros_like(dv_ref)

        def one(i):
            ql = pl.multiple_of(i * BAND_Q, BAND_Q)
            st, valid = _band_window(seg_i * seg + ql, length, win)
            q = q_ref[pl.ds(ql, BAND_Q), :]
            dov = do_ref[pl.ds(ql, BAND_Q), :]
            lsev = lse_ref[pl.ds(ql, BAND_Q), :]
            ddv = dd_ref[pl.ds(ql, BAND_Q), :]
            kw = k_ref[pl.ds(st, win), :]
            vw = v_ref[pl.ds(st, win), :]
            dq = jnp.zeros((BAND_Q, LANES), F32)
            for e in range(2):
                mine = (lane >= HEAD_DIM) if e else (lane < HEAD_DIM)
                qs = jnp.where(mine, q, jnp.zeros_like(q)) * ATTN_SCALE
                dom = jnp.where(mine, dov, jnp.zeros_like(dov))
                lse_e = lsev[:, e * HEAD_DIM:e * HEAD_DIM + 1]
                dd_e = ddv[:, e * HEAD_DIM:e * HEAD_DIM + 1]
                sc = lax.dot_general(qs, kw, NT, preferred_element_type=F32)
                p = jnp.exp(jnp.where(valid, sc, NEG_INF) - lse_e)
                dp = lax.dot_general(dom, vw, NT, preferred_element_type=F32)
                ds = (p * (dp - dd_e)).astype(BF16)
                dv_ref[pl.ds(st, win), :] += lax.dot_general(p.astype(BF16), dom, TN, preferred_element_type=F32)
                dk_ref[pl.ds(st, win), :] += lax.dot_general(ds, qs, TN, preferred_element_type=F32)
                dq_e = lax.dot_general(ds, kw, NN, preferred_element_type=F32) * ATTN_SCALE
                dq = dq + jnp.where(mine, dq_e, 0.0)
            dq_ref[pl.ds(ql, BAND_Q), :] = dq.astype(BF16)

        _unrolled_loop(n_sub, one)

    def col(base):
        return lambda r, hp, sg: (0, r * TOKEN_BLOCKS + base + hp)

    q_spec = pl.BlockSpec((seg, LANES), lambda r, hp, sg: (sg, r * TOKEN_BLOCKS + B_Q_BLOCK0 + hp))
    seg_spec = pl.BlockSpec((seg, LANES), lambda r, hp, sg: (sg, r * 2 + hp))
    full_spec = pl.BlockSpec((length, LANES), lambda r, hp, sg: (0, r * 2 + hp))
    shp = jax.ShapeDtypeStruct((length, dil * B_OUT_WIDTH), F32)
    return pl.pallas_call(
        body, name=name, grid=(dil, 2, length // seg),
        in_specs=[q_spec, pl.BlockSpec((length, LANES), col(B_K_BLOCK0)), pl.BlockSpec((length, LANES), col(B_V_BLOCK0)),
                  seg_spec, seg_spec, seg_spec],
        out_specs=[seg_spec, full_spec, full_spec],
        out_shape=[jax.ShapeDtypeStruct(shp.shape, BF16), shp, shp],
        compiler_params=_params(),
    )(qkv_view, qkv_view, qkv_view, do, lse, dd)


def _merge_weights(lses):
    m = jnp.maximum(jnp.maximum(lses[0], lses[1]), lses[2])
    ex = [jnp.exp(v - m) for v in lses]
    tot = ex[0] + ex[1] + ex[2]
    return [v / tot for v in ex]


def _merge_fwd(os_, lses, *, name, tm=512):
    s = os_[0].shape[0]

    def body(o0, o1, o2, l0, l1, l2, ob_ref):
        w = _merge_weights([l0[...], l1[...], l2[...]])
        ob = w[0] * o0[...].astype(F32) + w[1] * o1[...].astype(F32) + w[2] * o2[...].astype(F32)
        ob_ref[...] = ob.astype(BF16)

    blk = pl.BlockSpec((tm, B_OUT_WIDTH), lambda i: (i, 0))
    return pl.pallas_call(
        body, name=name, grid=(s // tm,), in_specs=[blk] * 6, out_specs=blk,
        out_shape=jax.ShapeDtypeStruct((s, B_OUT_WIDTH), BF16),
        compiler_params=_params(),
    )(*os_, *lses)


def _merge_bwd(os_, lses, dob, *, name, tm=512):
    s = os_[0].shape[0]

    def body(o0, o1, o2, l0, l1, l2, dob_ref, d0, d1, d2, t0, t1, t2):
        w = _merge_weights([l0[...], l1[...], l2[...]])
        dv = dob_ref[...]
        ob = w[0] * o0[...].astype(F32) + w[1] * o1[...].astype(F32) + w[2] * o2[...].astype(F32)
        tot = _head_sum(dv * ob)
        for wg, d_ref, t_ref in zip(w, (d0, d1, d2), (t0, t1, t2)):
            d_ref[...] = (wg * dv).astype(BF16)
            t_ref[...] = wg * tot

    blk = pl.BlockSpec((tm, B_OUT_WIDTH), lambda i: (i, 0))
    return pl.pallas_call(
        body, name=name, grid=(s // tm,), in_specs=[blk] * 7, out_specs=[blk] * 6,
        out_shape=[jax.ShapeDtypeStruct((s, B_OUT_WIDTH), BF16)] * 3 + [jax.ShapeDtypeStruct((s, B_OUT_WIDTH), F32)] * 3,
        compiler_params=_params(),
    )(*os_, *lses, dob)


HALO = 16
POOL_BLOCK0 = QKV_WIDTH // LANES


def _window_sum(ext, lo, hi, tm):
    rows = ext.shape[0]
    acc = None
    for j in range(lo, hi + 1):
        r = ext if j == 0 else pltpu.roll(ext, (-j) % rows, axis=0)
        acc = r if acc is None else acc + r
    return acc[HALO:HALO + tm]


def _pool_counts(t, half, s):
    return (jnp.minimum(t + half, s) - jnp.maximum(t - half, 0)).astype(F32)


def _halo_specs(tm, s, col0):
    per = tm // HALO
    last = s // HALO - 1
    prev = pl.BlockSpec((HALO, LANES), lambda g, i: (jnp.maximum(i * per - 1, 0), col0 + g))
    cur = pl.BlockSpec((tm, LANES), lambda g, i: (i, col0 + g))
    nxt = pl.BlockSpec((HALO, LANES), lambda g, i: (jnp.minimum((i + 1) * per, last), col0 + g))
    return prev, cur, nxt


def _extended(prev_ref, cur_ref, next_ref, i, n_tiles):
    prev = jnp.where(i > 0, prev_ref[...].astype(F32), 0.0)
    nxt = jnp.where(i < n_tiles - 1, next_ref[...].astype(F32), 0.0)
    return jnp.concatenate([prev, cur_ref[...].astype(F32), nxt], axis=0)


def _pool_fwd(z, lin, scale, *, name, tm=512):
    s = z.shape[0]
    tm = min(tm, s)
    n_tiles = s // tm

    def body(prev_ref, cur_ref, next_ref, lin_ref, sc_ref, pooled_ref, mixed_ref):
        g = pl.program_id(0)
        i = pl.program_id(1)
        ext = _extended(prev_ref, cur_ref, next_ref, i, n_tiles)
        t = i * tm + lax.broadcasted_iota(jnp.int32, (tm, 1), 0)
        for gi, half in enumerate(POOL_HALF):
            @pl.when(g == gi)
            def _(half=half):
                mean = _window_sum(ext, -half, half - 1, tm) / _pool_counts(t, half, s)
                pooled = (mean - cur_ref[...]).astype(BF16)
                pooled_ref[...] = pooled
                mixed = lax.dot_general(pooled, lin_ref[...].astype(BF16), NN, preferred_element_type=F32)
                mixed_ref[...] = (mixed * sc_ref[...]).astype(BF16)

    prev, cur, nxt = _halo_specs(tm, s, POOL_BLOCK0)
    out = pl.BlockSpec((tm, LANES), lambda g, i: (i, g))
    return pl.pallas_call(
        body, name=name, grid=(len(POOL_HALF), n_tiles),
        in_specs=[prev, cur, nxt, pl.BlockSpec((None, LANES, LANES), lambda g, i: (g, 0, 0)),
                  pl.BlockSpec((1, LANES), lambda g, i: (0, g))],
        out_specs=[out, out],
        out_shape=[jax.ShapeDtypeStruct((s, POOL_WIDTH), BF16)] * 2,
        compiler_params=_params(),
    )(z, z, z, lin, scale)


def _pool_bwd(dmixed, pooled, lin, scale, *, name, tm=512):
    s = dmixed.shape[0]
    tm = min(tm, s)
    n_tiles = s // tm

    def body(prev_ref, cur_ref, next_ref, pooled_ref, lin_ref, sc_ref, du_ref, dlin_ref, dsc_ref):
        g = pl.program_id(0)
        i = pl.program_id(1)

        @pl.when(i == 0)
        def _():
            dlin_ref[...] = jnp.zeros_like(dlin_ref)
            dsc_ref[...] = jnp.zeros_like(dsc_ref)

        linb = lin_ref[...].astype(BF16)
        ext = _extended(prev_ref, cur_ref, next_ref, i, n_tiles)
        dpl_ext = (ext * sc_ref[...]).astype(BF16)
        dpl_cur = (cur_ref[...] * sc_ref[...]).astype(BF16)
        dpooled_ext = lax.dot_general(dpl_ext, linb, NT, preferred_element_type=F32)
        t_ext = i * tm - HALO + lax.broadcasted_iota(jnp.int32, (tm + 2 * HALO, 1), 0)
        pooled = pooled_ref[...]
        mixed = lax.dot_general(pooled, linb, NN, preferred_element_type=F32)
        dsc_ref[...] += jnp.sum(cur_ref[...] * mixed, axis=0, keepdims=True)
        dlin_ref[...] += lax.dot_general(pooled, dpl_cur, TN, preferred_element_type=F32)
        for gi, half in enumerate(POOL_HALF):
            @pl.when(g == gi)
            def _(half=half):
                share = dpooled_ext / jnp.maximum(_pool_counts(t_ext, half, s), 1.0)
                du = _window_sum(share, -(half - 1), half, tm) - dpooled_ext[HALO:HALO + tm]
                du_ref[...] = du.astype(BF16)

    prev, cur, nxt = _halo_specs(tm, s, 0)
    out = pl.BlockSpec((tm, LANES), lambda g, i: (i, g))
    lin_spec = pl.BlockSpec((None, LANES, LANES), lambda g, i: (g, 0, 0))
    vec = pl.BlockSpec((1, LANES), lambda g, i: (0, g))
    return pl.pallas_call(
        body, name=name, grid=(len(POOL_HALF), n_tiles),
        in_specs=[prev, cur, nxt, out, lin_spec, vec],
        out_specs=[out, lin_spec, vec],
        out_shape=[jax.ShapeDtypeStruct((s, POOL_WIDTH), BF16),
                   jax.ShapeDtypeStruct((len(POOL_HALF), LANES, LANES), F32),
                   jax.ShapeDtypeStruct((1, POOL_WIDTH), F32)],
        compiler_params=_params(),
    )(dmixed, dmixed, dmixed, pooled, lin, scale)


GATE_TILE = 512
GATE_BLOCK0 = (QKV_WIDTH + POOL_WIDTH) // GATE_TILE
GATE_BLOCKS_PER_BRANCH = D_MODEL // GATE_TILE


def _sigmoid(v):
    return 1.0 / (1.0 + jnp.exp(-v))


def _gate_specs(tm):
    def zspec(br):
        return pl.BlockSpec((tm, GATE_TILE), lambda jj, i: (i, GATE_BLOCK0 + GATE_BLOCKS_PER_BRANCH * br + jj))

    def bspec(br):
        return pl.BlockSpec((1, GATE_TILE), lambda jj, i: (0, GATE_BLOCKS_PER_BRANCH * br + jj))

    row = pl.BlockSpec((tm, GATE_TILE), lambda jj, i: (i, jj))
    vec = pl.BlockSpec((1, GATE_TILE), lambda jj, i: (0, jj))
    return [zspec(0), zspec(1), zspec(2)], [bspec(0), bspec(1), bspec(2)], row, vec


def _gate_fwd(z, b_gate, ya, yb, yc, *, name, tm=512):
    s = z.shape[0]

    def body(z0, z1, z2, b0, b1, b2, ya_ref, yb_ref, yc_ref, out_ref):
        acc = _sigmoid(z0[...] + b0[...]) * ya_ref[...]
        acc = acc + _sigmoid(z1[...] + b1[...]) * yb_ref[...]
        acc = acc + _sigmoid(z2[...] + b2[...]) * yc_ref[...]
        out_ref[...] = acc.astype(BF16)

    zs, bs, row, _ = _gate_specs(tm)
    return pl.pallas_call(
        body, name=name, grid=(GATE_BLOCKS_PER_BRANCH, s // tm),
        in_specs=zs + bs + [row] * 3, out_specs=row,
        out_shape=jax.ShapeDtypeStruct((s, D_MODEL), BF16),
        compiler_params=_params(),
    )(z, z, z, b_gate, b_gate, b_gate, ya, yb, yc)


def _gate_bwd(z, b_gate, ya, yb, yc, dmerged, *, name, tm=512):
    s = z.shape[0]

    def body(z0, z1, z2, b0, b1, b2, ya_ref, yb_ref, yc_ref, dm_ref,
             dya_ref, dyb_ref, dyc_ref, dg0, dg1, dg2, db0, db1, db2):
        i = pl.program_id(1)
        dm = dm_ref[...]
        for z_ref, b_ref, y_ref, dy_ref, dg_ref, db_ref in (
                (z0, b0, ya_ref, dya_ref, dg0, db0), (z1, b1, yb_ref, dyb_ref, dg1, db1),
                (z2, b2, yc_ref, dyc_ref, dg2, db2)):
            gate = _sigmoid(z_ref[...] + b_ref[...])
            dy_ref[...] = (gate * dm).astype(BF16)
            dpre = dm * y_ref[...] * gate * (1.0 - gate)
            dg_ref[...] = dpre.astype(BF16)
            part = jnp.sum(dpre, axis=0, keepdims=True)

            @pl.when(i == 0)
            def _(db_ref=db_ref, part=part):
                db_ref[...] = part

            @pl.when(i > 0)
            def _(db_ref=db_ref, part=part):
                db_ref[...] += part

    zs, bs, row, vec = _gate_specs(tm)
    big = jax.ShapeDtypeStruct((s, D_MODEL), BF16)
    small = jax.ShapeDtypeStruct((1, D_MODEL), F32)
    return pl.pallas_call(
        body, name=name, grid=(GATE_BLOCKS_PER_BRANCH, s // tm),
        in_specs=zs + bs + [row] * 4, out_specs=[row] * 6 + [vec] * 3,
        out_shape=[big] * 6 + [small] * 3,
        compiler_params=_params(),
    )(z, z, z, b_gate, b_gate, b_gate, ya, yb, yc, dmerged)


def _swiglu_fwd(a, b, *, name, tm=512, tn=1408):
    s, f = a.shape

    def body(a_ref, b_ref, o_ref):
        av = a_ref[...].astype(F32)
        o_ref[...] = (av * _sigmoid(av) * b_ref[...].astype(F32)).astype(BF16)

    blk = pl.BlockSpec((tm, tn), lambda i, j: (i, j))
    return pl.pallas_call(
        body, name=name, grid=(s // tm, f // tn), in_specs=[blk, blk], out_specs=blk,
        out_shape=jax.ShapeDtypeStruct((s, f), BF16), compiler_params=_params(),
    )(a, b)


def _swiglu_bwd(a, b, df, *, name, tm=512, tn=1408):
    s, f = a.shape

    def body(a_ref, b_ref, df_ref, da_ref, db_ref):
        av = a_ref[...].astype(F32)
        dfv = df_ref[...].astype(F32)
        sg = _sigmoid(av)
        silu = av * sg
        da_ref[...] = (dfv * b_ref[...].astype(F32) * (sg + silu * (1.0 - sg))).astype(BF16)
        db_ref[...] = (dfv * silu).astype(BF16)

    blk = pl.BlockSpec((tm, tn), lambda i, j: (i, j))
    out = jax.ShapeDtypeStruct((s, f), BF16)
    return pl.pallas_call(
        body, name=name, grid=(s // tm, f // tn), in_specs=[blk] * 3, out_specs=[blk] * 2,
        out_shape=[out, out], compiler_params=_params(),
    )(a, b, df)


def _loss_head(y, target, *, name, tm=512):
    s, d = y.shape

    def body(y_ref, t_ref, part_ref, dy_ref):
        i = pl.program_id(0)
        err = y_ref[...] - t_ref[...]
        dy_ref[...] = err * (1.0 / d)
        part = jnp.sum(err * err, axis=0, keepdims=True) * (0.5 / d)

        @pl.when(i == 0)
        def _():
            part_ref[...] = part

        @pl.when(i > 0)
        def _():
            part_ref[...] += part

    row = pl.BlockSpec((tm, d), lambda i: (i, 0))
    vec = pl.BlockSpec((1, d), lambda i: (0, 0))
    return pl.pallas_call(
        body, name=name, grid=(s // tm,), in_specs=[row, row], out_specs=[vec, row],
        out_shape=[jax.ShapeDtypeStruct((1, d), F32), jax.ShapeDtypeStruct((s, d), F32)],
        compiler_params=_params(),
    )(y, target)


def _mesh_place():
    x, y, c = lax.axis_index('x'), lax.axis_index('y'), lax.axis_index('c')
    return x, y, c, 4 * x + 2 * y + c


def _peer(x, y, c, k):
    return (x ^ ((k >> 2) & 1), y ^ ((k >> 1) & 1), c ^ (k & 1))


def _exchange(bufs, *, gather, name):
    n = len(bufs)

    def body(*refs):
        start, wait = _exchange_plan(refs[:n], refs[n:2 * n], refs[2 * n:], gather)
        start()
        wait()

    extra = _Extra((bufs, gather))
    return pl.pallas_call(
        body, name=name, in_specs=extra.in_specs, out_specs=extra.out_specs,
        out_shape=extra.out_shape, scratch_shapes=extra.scratch,
    )(*bufs)


_EXCHANGE_SPEC = pl.BlockSpec(memory_space=pl.ANY)
SEMS_PER_BUFFER = 3


class _Extra:
    def __init__(self, ride):
        bufs = [] if ride is None else list(ride[0])
        self.in_specs = [_EXCHANGE_SPEC] * len(bufs)
        self.out_specs = [_EXCHANGE_SPEC] * len(bufs)
        self.out_shape = [jax.ShapeDtypeStruct((N_DEV,) + b.shape[-2:], b.dtype) for b in bufs]
        self.scratch = [pltpu.SemaphoreType.DMA((N_DEV - 1,)), pltpu.SemaphoreType.DMA((N_DEV - 1,)),
                        pltpu.SemaphoreType.DMA] * len(bufs)
        self.args = tuple(bufs)


def _carry_exchange(body, n_in, n_out, grid, ride):
    extra = _Extra(ride)
    if ride is None:
        return body, extra
    n, gather = len(ride[0]), ride[1]

    def carrying(*refs):
        ins, ride_ins = refs[:n_in], refs[n_in:n_in + n]
        outs, ride_outs = refs[n_in + n:n_in + n + n_out], refs[n_in + n + n_out:n_in + 2 * n + n_out]
        sems = refs[n_in + 2 * n + n_out:]
        ids = [pl.program_id(a) for a in range(len(grid))]
        first = functools.reduce(jnp.logical_and, [i == 0 for i in ids])
        last = functools.reduce(jnp.logical_and, [i == size - 1 for i, size in zip(ids, grid)])

        @pl.when(first)
        def _():
            _exchange_plan(ride_ins, ride_outs, sems, gather)[0]()

        body(*ins, *outs)

        @pl.when(last)
        def _():
            _exchange_plan(ride_ins, ride_outs, sems, gather)[1]()

    return carrying, extra


def _exchange_plan(in_refs, out_refs, sems, gather):
    x, y, c, me = _mesh_place()

    def src(b, slot):
        return in_refs[b] if gather else in_refs[b].at[slot]

    def copy(b, k, dst_slot):
        send_sems, recv_sems = sems[SEMS_PER_BUFFER * b], sems[SEMS_PER_BUFFER * b + 1]
        return pltpu.make_async_remote_copy(
            src_ref=src(b, me ^ k), dst_ref=out_refs[b].at[dst_slot],
            send_sem=send_sems.at[k - 1], recv_sem=recv_sems.at[k - 1],
            device_id=_peer(x, y, c, k), device_id_type=pl.DeviceIdType.MESH)

    def mine(b):
        return pltpu.make_async_copy(src(b, me), out_refs[b].at[me], sems[SEMS_PER_BUFFER * b + 2])

    buffers = range(len(in_refs))

    def start():
        for b in buffers:
            mine(b).start()
            for k in range(1, N_DEV):
                copy(b, k, me).start()

    def wait():
        for b in buffers:
            for k in range(1, N_DEV):
                copy(b, k, me ^ k).wait_recv()
        for b in buffers:
            for k in range(1, N_DEV):
                copy(b, k, me).wait_send()
            mine(b).wait()

    return start, wait


def _adamw(parts, w, m, v, *, name, tr, ride=None):
    n_layers = len(parts)
    rows, cols = parts[0].shape[1:]
    assert rows % tr == 0 and w.shape == (n_layers * rows, cols)
    tiles = rows // tr
    bias1 = 1.0 - ADAM_B1 ** ADAM_STEP
    bias2 = 1.0 - ADAM_B2 ** ADAM_STEP

    def body(*refs):
        p_refs = refs[:n_layers]
        w_ref, m_ref, v_ref, g_ref, d_ref, nm_ref, nv_ref = refs[n_layers:]

        def update(p_ref):
            g = p_ref[0].astype(F32)
            for j in range(1, N_DEV):
                g = g + p_ref[j].astype(F32)
            nm = ADAM_B1 * m_ref[...] + (1.0 - ADAM_B1) * g
            nv = ADAM_B2 * v_ref[...] + (1.0 - ADAM_B2) * (g * g)
            g_ref[...] = g
            nm_ref[...] = nm
            nv_ref[...] = nv
            d_ref[...] = -ADAM_LR * ((nm / bias1) / (jnp.sqrt(nv / bias2) + ADAM_EPS) + ADAM_WD * w_ref[...])

        for layer, p_ref in enumerate(p_refs):
            @pl.when(pl.program_id(0) == layer)
            def _(p_ref=p_ref):
                update(p_ref)

    def part_spec(layer):
        def index(l, i):
            return 0, jnp.where(l < layer, 0, jnp.where(l > layer, tiles - 1, i)), 0
        return pl.BlockSpec((N_DEV, tr, cols), index)

    blk = pl.BlockSpec((tr, cols), lambda l, i: (l * tiles + i, 0))
    out = jax.ShapeDtypeStruct((n_layers * rows, cols), F32)
    grid = (n_layers, tiles)
    body, extra = _carry_exchange(body, n_layers + 3, 4, grid, ride)
    res = pl.pallas_call(
        body, name=name, grid=grid,
        in_specs=[part_spec(layer) for layer in range(n_layers)] + [blk, blk, blk] + extra.in_specs,
        out_specs=[blk] * 4 + extra.out_specs, out_shape=[out] * 4 + extra.out_shape,
        scratch_shapes=extra.scratch, compiler_params=_params(),
    )(*parts, w, m, v, *extra.args)
    return res[:4], res[4:]


def _col_blocks(full):
    r, n = full.shape
    return full.reshape(r, N_DEV, n // N_DEV).transpose(1, 0, 2)


def _row_blocks(full):
    r, n = full.shape
    return full.reshape(N_DEV, r // N_DEV, n)


def _from_col_blocks(blocks):
    j, r, c = blocks.shape
    return blocks.transpose(1, 0, 2).reshape(r, j * c)


def _from_row_blocks(blocks):
    j, r, c = blocks.shape
    return blocks.reshape(j * r, c)


SHARD_ROWWISE = {'w_out', 'w_ffn_down'}
EARLY = ('w_ffn_down', 'w_ffn_gate', 'w_ffn_up', 'w_out', 'w_branch_a')
LATE = ('w_branch_b', 'w_branch_c')


def _blocks(name, full):
    return _row_blocks(full) if name in SHARD_ROWWISE else _col_blocks(full)


def _from_blocks(name, blocks):
    return _from_row_blocks(blocks) if name in SHARD_ROWWISE else _from_col_blocks(blocks)


def _pack(shards, names):
    lead = shards[names[0]].shape[:-2]
    return jnp.concatenate([shards[n].reshape(lead + (-1, PACK_COLS)) for n in names], axis=-2)


def _unpack(packed, shapes, names):
    out, off = {}, 0
    lead = packed.shape[:-2]
    for n in names:
        size = shapes[n][0] * shapes[n][1] // PACK_COLS
        out[n] = packed[..., off:off + size, :].reshape(lead + tuple(shapes[n]))
        off += size
    return out


def _pack_small(vals):
    flat = jnp.concatenate([vals[n].reshape(-1) for n in SMALL])
    return flat.reshape(-1, PACK_COLS)


def _unpack_small(packed, shapes):
    flat = packed.reshape(-1)
    out, off = {}, 0
    for n in SMALL:
        size = 1
        for dim in shapes[n]:
            size *= dim
        out[n] = flat[off:off + size].reshape(shapes[n])
        off += size
    return out


def _band_views(qkv, s):
    views = []
    for gi, d in enumerate(B_DILATIONS):
        starts = [A_Q_WIDTH + 2 * A_KV_WIDTH + part * B_WIDTH + gi * B_OUT_WIDTH for part in range(3)]
        group = jnp.concatenate([qkv[:, c0:c0 + B_OUT_WIDTH] for c0 in starts], axis=1)
        views.append(group.reshape(s // d, d * GROUP_QKV))
    return views


def _layer_fwd(x, p, tables, ride=None, after_ride=None):
    s = x.shape[0]
    cos, sin = tables
    h, h_t = _rms_fwd(x, p['norm_mix'], name='rms_mix_fwd')
    z = _mm(h, p['w_in'], dims='nn', out_dtype=BF16, tm=1024, tn=1664, tk=1024, name='mm_in_fwd')
    gains = _qk_gains(p['qn_a'], p['kn_a'], p['qn_b'], p['kn_b'])
    qkv = _qkrope_fwd(z, gains, cos, sin, name='qkrope_fwd')

    kdup = _dup_halves(qkv[:, A_Q_WIDTH:A_Q_WIDTH + A_KV_WIDTH])
    va = qkv[:, A_Q_WIDTH + A_KV_WIDTH:A_Q_WIDTH + 2 * A_KV_WIDTH]
    vdup = _dup_halves(va)
    oa, lse_a, *gathered = _attn_a_fwd(qkv, kdup, _with_ones(va), ride=ride, name='attn_a_fwd')
    if after_ride is not None:
        p = {**p, **after_ride(gathered)}
    ya = _mm(oa, p['w_branch_a'], dims='nn', out_dtype=BF16, tm=1024, tn=1024, tk=512, name='mm_branch_a_fwd')

    views = _band_views(qkv, s)
    o_g, lse_g = [], []
    for gi, d in enumerate(B_DILATIONS):
        o, lse = _band_fwd(views[gi], gi, d, name=f'band_fwd_d{d}')
        o_g.append(o.reshape(s, B_OUT_WIDTH))
        lse_g.append(lse.reshape(s, B_OUT_WIDTH))
    ob = _merge_fwd(o_g, lse_g, name='merge_fwd')
    yb = _mm(ob, p['w_branch_b'], dims='nn', out_dtype=BF16, tm=1024, tn=1024, tk=256, name='mm_branch_b_fwd')

    pooled, mixed = _pool_fwd(z, p['pool_lin'], p['pool_scale'], name='pool_fwd')
    yc = _mm(mixed, p['w_branch_c'], dims='nn', out_dtype=BF16, tm=1024, tn=1024, tk=512, name='mm_branch_c_fwd')

    merged = _gate_fwd(z, p['b_gate'], ya, yb, yc, name='gate_fwd')
    x_mid = _mm(merged, p['w_out'], dims='nn', out_dtype=F32, tm=1024, tn=1024, tk=1024, res=x, name='mm_out_fwd')

    h2, h2_t = _rms_fwd(x_mid, p['norm_ffn'], name='rms_ffn_fwd')
    fa = _mm(h2, p['w_ffn_gate'], dims='nn', out_dtype=BF16, tm=1024, tn=1408, tk=1024, name='mm_ffn_gate_fwd')
    fb = _mm(h2, p['w_ffn_up'], dims='nn', out_dtype=BF16, tm=1024, tn=1408, tk=1024, name='mm_ffn_up_fwd')
    f = _swiglu_fwd(fa, fb, name='swiglu_fwd')
    x_out = _mm(f, p['w_ffn_down'], dims='nn', out_dtype=F32, tm=1024, tn=512, tk=2816, res=x_mid, name='mm_ffn_down_fwd')

    saved = dict(x=x, h=h, z=z, gains=gains, qkv=qkv, kdup=kdup, vdup=vdup, oa=oa, lse_a=lse_a, o_g=o_g, lse_g=lse_g,
                 ob=ob, pooled=pooled, mixed=mixed, ya=ya, yb=yb, yc=yc, merged=merged, x_mid=x_mid, h2=h2,
                 fa=fa, fb=fb, f=f, views=views, h_t=h_t, h2_t=h2_t)
    return x_out, saved, p


def _fold_heads(v, heads):
    return v.reshape(heads, HEAD_DIM).sum(axis=0)


def _layer_bwd(dx, p, sv, tables, make_ride=None):
    s = dx.shape[0]
    cos, sin = tables
    g = {}

    df = _mm(dx, p['w_ffn_down'], dims='nt', out_dtype=BF16, tm=1024, tn=1408, tk=1024, name='mm_ffn_down_dx')
    g['w_ffn_down'] = _mm(sv['f'], dx, dims='tn', out_dtype=BF16, tm=1408, tn=1024, tk=512, name='mm_ffn_down_dw')
    da, db = _swiglu_bwd(sv['fa'], sv['fb'], df, name='swiglu_bwd')
    dh2 = _mm(da, p['w_ffn_gate'], dims='nt', out_dtype=F32, tm=1024, tn=512, tk=2816, name='mm_ffn_gate_dx')
    dh2 = _mm(db, p['w_ffn_up'], dims='nt', out_dtype=F32, tm=1024, tn=512, tk=2816, res=dh2, name='mm_ffn_up_dx')
    g['w_ffn_gate'] = _mm(sv['h2_t'], da, dims='nn', out_dtype=BF16, tm=1024, tn=2816, tk=1024, name='mm_ffn_gate_dw')
    g['w_ffn_up'] = _mm(sv['h2_t'], db, dims='nn', out_dtype=BF16, tm=1024, tn=2816, tk=1024, name='mm_ffn_up_dw')
    dx_mid, g['norm_ffn'] = _rms_bwd(sv['x_mid'], p['norm_ffn'], dh2, dx, name='rms_ffn_bwd')

    dmerged = _mm(dx_mid, p['w_out'], dims='nt', out_dtype=F32, tm=1024, tn=1024, tk=1024, name='mm_out_dx')
    g['w_out'] = _mm(sv['merged'], dx_mid, dims='tn', out_dtype=BF16, tm=1024, tn=1024, tk=512, name='mm_out_dw')
    dya, dyb, dyc, dg0, dg1, dg2, db0, db1, db2 = _gate_bwd(
        sv['z'], p['b_gate'], sv['ya'], sv['yb'], sv['yc'], dmerged, name='gate_bwd')
    g['b_gate'] = jnp.concatenate([db0, db1, db2], axis=1)

    doa = _mm(dya, p['w_branch_a'], dims='nt', out_dtype=BF16, tm=1024, tn=512, tk=1024, name='mm_branch_a_dx')
    g['w_branch_a'] = _mm(sv['oa'], dya, dims='tn', out_dtype=BF16, tm=512, tn=1024, tk=512, name='mm_branch_a_dw')
    ride = None if make_ride is None else make_ride(g)
    dqa, dkdup, dvdup, *arrived = _attn_a_bwd(sv['qkv'], sv['kdup'], sv['vdup'], sv['oa'], sv['lse_a'], doa,
                                              ride=ride, name='attn_a_bwd')

    def fold(dup):
        return jnp.concatenate([dup[:, 0:64] + dup[:, 64:128], dup[:, 128:192] + dup[:, 192:256]], axis=1)

    dka, dva = fold(dkdup), fold(dvdup)

    dob = _mm(dyb, p['w_branch_b'], dims='nt', out_dtype=F32, tm=1024, tn=256, tk=1024, name='mm_branch_b_dx')
    g['w_branch_b'] = _mm(sv['ob'], dyb, dims='tn', out_dtype=BF16, tm=256, tn=1024, tk=512, name='mm_branch_b_dw')
    merged_b = _merge_bwd(sv['o_g'], sv['lse_g'], dob, name='merge_bwd')
    do_g, dd_g = merged_b[:3], merged_b[3:]
    views = sv['views']
    dq_parts, dk_parts, dv_parts = [], [], []
    for gi, d in enumerate(B_DILATIONS):
        ln = s // d
        dq, dk, dv = _band_bwd(views[gi], do_g[gi].reshape(ln, d * B_OUT_WIDTH),
                               sv['lse_g'][gi].reshape(ln, d * B_OUT_WIDTH), dd_g[gi].reshape(ln, d * B_OUT_WIDTH),
                               gi, d, name=f'band_bwd_d{d}')
        dq_parts.append(dq.reshape(s, B_OUT_WIDTH))
        dk_parts.append(dk.reshape(s, B_OUT_WIDTH))
        dv_parts.append(dv.reshape(s, B_OUT_WIDTH))

    dmixed = _mm(dyc, p['w_branch_c'], dims='nt', out_dtype=F32, tm=1024, tn=512, tk=1024, name='mm_branch_c_dx')
    g['w_branch_c'] = _mm(sv['mixed'], dyc, dims='tn', out_dtype=BF16, tm=512, tn=1024, tk=512, name='mm_branch_c_dw')
    du, g['pool_lin'], g['pool_scale'] = _pool_bwd(dmixed, sv['pooled'], p['pool_lin'], p['pool_scale'], name='pool_bwd')

    dz_qkv, dgains = _qkrope_bwd(sv['z'], sv['gains'], cos, sin, [dqa, dka, dva] + dq_parts + dk_parts + dv_parts,
                                 name='qkrope_bwd')
    dgains = jnp.sum(dgains, axis=0)
    g['qn_a'] = _fold_heads(dgains[0:512], 8)
    g['kn_a'] = _fold_heads(dgains[512:640], 2)
    g['qn_b'] = _fold_heads(dgains[768:1536], 12)
    g['kn_b'] = _fold_heads(dgains[1536:2304], 12)

    dz = jnp.concatenate([dz_qkv, du, dg0, dg1, dg2], axis=1)
    dh = _mm(dz, p['w_in'], dims='nt', out_dtype=F32, tm=1024, tn=1024, tk=1664, name='mm_in_dx')
    g['w_in'] = _mm(sv['h_t'], dz, dims='nn', out_dtype=BF16, tm=1024, tn=3328, tk=1024, name='mm_in_dw')
    dx_in, g['norm_mix'] = _rms_bwd(sv['x'], p['norm_mix'], dh, dx_mid, name='rms_mix_bwd')
    return dx_in, g, arrived


def _small_views(vals, l):
    return {
        'norm_mix': vals['norm_mix'][l][None, :], 'b_gate': vals['b_gate'][l][None, :],
        'qn_a': vals['qn_a'][l], 'kn_a': vals['kn_a'][l], 'qn_b': vals['qn_b'][l], 'kn_b': vals['kn_b'][l],
        'pool_lin': vals['pool_lin'][l], 'pool_scale': vals['pool_scale'][l][None, :],
        'norm_ffn': vals['norm_ffn'][l][None, :],
    }


def kernel(x, norm_mix, w_in, b_gate, qn_a, kn_a, qn_b, kn_b, pool_lin, pool_scale, w_branch_a, w_branch_b, w_branch_c, w_out, norm_ffn, w_ffn_gate, w_ffn_up, w_ffn_down, loss_target, m_norm_mix, m_w_in, m_b_gate, m_qn_a, m_kn_a, m_qn_b, m_kn_b, m_pool_lin, m_pool_scale, m_w_branch_a, m_w_branch_b, m_w_branch_c, m_w_out, m_norm_ffn, m_w_ffn_gate, m_w_ffn_up, m_w_ffn_down, v_norm_mix, v_w_in, v_b_gate, v_qn_a, v_kn_a, v_qn_b, v_kn_b, v_pool_lin, v_pool_scale, v_w_branch_a, v_w_branch_b, v_w_branch_c, v_w_out, v_norm_ffn, v_w_ffn_gate, v_w_ffn_up, v_w_ffn_down):
    w = dict(norm_mix=norm_mix, w_in=w_in, b_gate=b_gate, qn_a=qn_a, kn_a=kn_a, qn_b=qn_b, kn_b=kn_b,
             pool_lin=pool_lin, pool_scale=pool_scale, w_branch_a=w_branch_a, w_branch_b=w_branch_b,
             w_branch_c=w_branch_c, w_out=w_out, norm_ffn=norm_ffn, w_ffn_gate=w_ffn_gate, w_ffn_up=w_ffn_up,
             w_ffn_down=w_ffn_down)
    m = dict(norm_mix=m_norm_mix, w_in=m_w_in, b_gate=m_b_gate, qn_a=m_qn_a, kn_a=m_kn_a, qn_b=m_qn_b, kn_b=m_kn_b,
             pool_lin=m_pool_lin, pool_scale=m_pool_scale, w_branch_a=m_w_branch_a, w_branch_b=m_w_branch_b,
             w_branch_c=m_w_branch_c, w_out=m_w_out, norm_ffn=m_norm_ffn, w_ffn_gate=m_w_ffn_gate,
             w_ffn_up=m_w_ffn_up, w_ffn_down=m_w_ffn_down)
    v = dict(norm_mix=v_norm_mix, w_in=v_w_in, b_gate=v_b_gate, qn_a=v_qn_a, kn_a=v_kn_a, qn_b=v_qn_b, kn_b=v_kn_b,
             pool_lin=v_pool_lin, pool_scale=v_pool_scale, w_branch_a=v_w_branch_a, w_branch_b=v_w_branch_b,
             w_branch_c=v_w_branch_c, w_out=v_w_out, norm_ffn=v_norm_ffn, w_ffn_gate=v_w_ffn_gate,
             w_ffn_up=v_w_ffn_up, w_ffn_down=v_w_ffn_down)
    depth = w_in.shape[0]
    shard_shapes = {n: w[n].shape[1:] for n in SHARDED}
    small_shapes = {n: w[n].shape for n in SMALL}

    def my_shards(l, names):
        return _pack({n: w[n][l].astype(BF16) for n in names}, names)

    def full_weights(gathered, names):
        blocks = _unpack(gathered, shard_shapes, names)
        return {n: _from_blocks(n, blocks[n]) for n in names}

    def blocks_to_send(grad, names):
        return _pack({n: _blocks(n, grad[n]) for n in names}, names)

    tables = _rope_tables(x.shape[1])
    (w_in_blocks,) = _exchange([w['w_in'][0].astype(BF16)], gather=True, name='gather_weights')
    layers, saved = [], []
    act = x[0]
    for l in range(depth):
        p = {'w_in': _from_col_blocks(w_in_blocks), **_small_views(w, l)}
        bufs = [my_shards(l, EARLY), my_shards(l, LATE)]
        if l + 1 < depth:
            bufs.append(w['w_in'][l + 1].astype(BF16))
        stash = {}

        def after_ride(gathered, stash=stash):
            stash['next_w_in'] = gathered[2] if len(gathered) > 2 else None
            return {**full_weights(gathered[0], EARLY), **full_weights(gathered[1], LATE)}

        act, sv, p = _layer_fwd(act, p, tables, ride=(bufs, True), after_ride=after_ride)
        w_in_blocks = stash['next_w_in']
        layers.append(p)
        saved.append(sv)
    part, dx = _loss_head(act, loss_target[0], name='loss_head')
    loss = lax.psum(jnp.sum(part), ('x', 'y', 'c'))

    early_parts, late_parts, w_in_parts = ([None] * depth for _ in range(3))
    grads = [None] * depth
    pending = []
    for l in reversed(range(depth)):
        def make_ride(g, pending=pending):
            return [blocks_to_send(g, EARLY)] + pending, False

        dx, grads[l], arrived = _layer_bwd(dx, layers[l], saved[l], tables, make_ride=make_ride)
        early_parts[l] = arrived[0]
        if pending:
            w_in_parts[l + 1], late_parts[l + 1] = arrived[1], arrived[2]
        pending = [_col_blocks(grads[l]['w_in']), blocks_to_send(grads[l], LATE)]
    grad_x = dx

    new = {}

    def update_packed(names, parts, label, ride=None):
        packed = [_pack({n: t[n] for n in names}, names).reshape(-1, PACK_COLS) for t in (w, m, v)]
        res, arrived = _adamw(parts, *packed, name=label, tr=96, ride=ride)
        unpacked = [_unpack(r.reshape(depth, -1, PACK_COLS), shard_shapes, names) for r in res]
        for n in names:
            new[n] = tuple(u[n] for u in unpacked)
        return arrived

    w_in_parts[0], late_parts[0] = update_packed(EARLY, early_parts, 'adamw_early', ride=(pending, False))
    small_grad = {n: jnp.stack([grads[l][n].reshape(small_shapes[n][1:]) for l in range(depth)]) for n in SMALL}
    w_in_shape = w['w_in'].shape
    res, (small_parts,) = _adamw(w_in_parts, *(t['w_in'].reshape(-1, w_in_shape[-1]) for t in (w, m, v)),
                                 name='adamw_w_in', tr=128, ride=([_pack_small(small_grad)], True))
    new['w_in'] = tuple(r.reshape(w_in_shape) for r in res)
    update_packed(LATE, late_parts, 'adamw_late')

    res, _ = _adamw([small_parts], _pack_small({n: w[n] for n in SMALL}), _pack_small({n: m[n] for n in SMALL}),
                    _pack_small({n: v[n] for n in SMALL}), name='adamw_small', tr=small_parts.shape[1])
    unpacked = [_unpack_small(r, small_shapes) for r in res]
    for n in SMALL:
        new[n] = tuple(u[n] for u in unpacked)

    outs = [loss, grad_x[None]]
    for idx in range(4):
        outs.extend(new[n][idx] for n in WEIGHTS)
    return tuple(outs)
```

```python
import functools

import jax
import jax.numpy as jnp
from jax import lax
from jax.experimental import pallas as pl
from jax.experimental.pallas import tpu as pltpu

F32 = jnp.float32
BF16 = jnp.bfloat16

N_DEV = 8
D_MODEL = 1024
DEPTH = 4
HEAD_DIM = 64
LANES = 128
A_Q_WIDTH = 512
A_KV_WIDTH = 128
B_WIDTH = 768
B_GROUPS = 3
B_DILATIONS = (1, 4, 16)
B_HALF_SPAN = 64
B_OUT_WIDTH = 256
POOL_WIDTH = 512
POOL_HALF = (1, 2, 4, 8)
GATE_WIDTH = 3072
QKV_WIDTH = A_Q_WIDTH + 2 * A_KV_WIDTH + 3 * B_WIDTH
IN_WIDTH = QKV_WIDTH + POOL_WIDTH + GATE_WIDTH
D_FF = 2816
GRID_W = 64
ROPE_THETA = 10000.0
EPS = 1e-6
NEG_INF = -1e30
ATTN_SCALE = HEAD_DIM ** -0.5

ADAM_LR = 0.001
ADAM_B1 = 0.9
ADAM_B2 = 0.999
ADAM_EPS = 1e-08
ADAM_WD = 0.01
ADAM_STEP = 10

PACK_COLS = 1024
VMEM_LIMIT = 56 * 1024 * 1024

SHARDED = ('w_in', 'w_branch_a', 'w_branch_b', 'w_branch_c', 'w_out', 'w_ffn_gate', 'w_ffn_up', 'w_ffn_down')
SMALL = ('norm_mix', 'b_gate', 'qn_a', 'kn_a', 'qn_b', 'kn_b', 'pool_lin', 'pool_scale', 'norm_ffn')
WEIGHTS = ('norm_mix', 'w_in', 'b_gate', 'qn_a', 'kn_a', 'qn_b', 'kn_b', 'pool_lin', 'pool_scale',
           'w_branch_a', 'w_branch_b', 'w_branch_c', 'w_out', 'norm_ffn', 'w_ffn_gate', 'w_ffn_up', 'w_ffn_down')

NN = (((1,), (0,)), ((), ()))
NT = (((1,), (1,)), ((), ()))
TN = (((0,), (0,)), ((), ()))


def _params(vmem=None):
    return pltpu.CompilerParams(vmem_limit_bytes=VMEM_LIMIT if vmem is None else vmem)


def _lane_iota(n=LANES):
    return lax.broadcasted_iota(jnp.int32, (1, n), 1)


def _swap(x, sh, lane):
    n = x.shape[-1]
    down = pltpu.roll(x, sh, axis=1)
    up = pltpu.roll(x, n - sh, axis=1)
    return jnp.where((lane & sh) == 0, up, down)


def _head_sum(v):
    w = v.shape[-1]
    r = lax.broadcasted_iota(jnp.int32, (w, w), 0) // HEAD_DIM
    c = lax.broadcasted_iota(jnp.int32, (w, w), 1) // HEAD_DIM
    ones = (r == c).astype(BF16)
    hi = v.astype(BF16)
    lo = (v - hi.astype(F32)).astype(BF16)
    return (lax.dot_general(hi, ones, NN, preferred_element_type=F32)
            + lax.dot_general(lo, ones, NN, preferred_element_type=F32))


def _mm(a, b, *, dims, out_dtype, tm, tn, tk, name, res=None):
    if dims == 'nn':
        (m, k), n = a.shape, b.shape[1]
    elif dims == 'nt':
        (m, k), n = a.shape, b.shape[0]
    else:
        (k, m), n = a.shape, b.shape[1]
    tm, tn, tk = min(tm, m), min(tn, n), min(tk, k)
    assert m % tm == 0 and n % tn == 0 and k % tk == 0, (name, m, n, k, tm, tn, tk)
    nk = k // tk
    if dims == 'tn':
        a_spec = pl.BlockSpec((tk, tm), lambda i, j, kk: (kk, i))
    else:
        a_spec = pl.BlockSpec((tm, tk), lambda i, j, kk: (i, kk))
    if dims == 'nt':
        b_spec = pl.BlockSpec((tn, tk), lambda i, j, kk: (j, kk))
    else:
        b_spec = pl.BlockSpec((tk, tn), lambda i, j, kk: (kk, j))
    o_spec = pl.BlockSpec((tm, tn), lambda i, j, kk: (i, j))
    dn = {'nn': NN, 'nt': NT, 'tn': TN}[dims]
    has_res = res is not None

    def body(*refs):
        if has_res:
            a_ref, b_ref, r_ref, o_ref, acc_ref = refs
        else:
            a_ref, b_ref, o_ref, acc_ref = refs
        prod = lax.dot_general(a_ref[...].astype(BF16), b_ref[...].astype(BF16), dn,
                               preferred_element_type=F32)

        def finish(total):
            if has_res:
                total = total + r_ref[...]
            o_ref[...] = total.astype(out_dtype)

        if nk == 1:
            finish(prod)
        else:
            kk = pl.program_id(2)

            @pl.when(kk == 0)
            def _():
                acc_ref[...] = prod

            @pl.when(kk > 0)
            def _():
                acc_ref[...] += prod

            @pl.when(kk == nk - 1)
            def _():
                finish(acc_ref[...])

    in_specs = [a_spec, b_spec] + ([o_spec] if has_res else [])
    args = (a, b) + ((res,) if has_res else ())
    acc_shape = (tm, tn) if nk > 1 else (8, LANES)
    return pl.pallas_call(
        body, name=name, grid=(m // tm, n // tn, nk),
        in_specs=in_specs, out_specs=o_spec,
        out_shape=jax.ShapeDtypeStruct((m, n), out_dtype),
        scratch_shapes=[pltpu.VMEM(acc_shape, F32)],
        compiler_params=_params(),
    )(*args)


def _rms_fwd(x, g, *, name, tm=512):
    s, d = x.shape

    def body(x_ref, g_ref, h_ref, ht_ref):
        xv = x_ref[...]
        rstd = lax.rsqrt(jnp.mean(xv * xv, axis=-1, keepdims=True) + EPS)
        h = xv * rstd * g_ref[...]
        h_ref[...] = h.astype(BF16)
        ht_ref[...] = h.T.astype(BF16)

    return pl.pallas_call(
        body, name=name, grid=(s // tm,),
        in_specs=[pl.BlockSpec((tm, d), lambda i: (i, 0)), pl.BlockSpec((1, d), lambda i: (0, 0))],
        out_specs=[pl.BlockSpec((tm, d), lambda i: (i, 0)), pl.BlockSpec((d, tm), lambda i: (0, i))],
        out_shape=[jax.ShapeDtypeStruct((s, d), BF16), jax.ShapeDtypeStruct((d, s), BF16)],
        compiler_params=_params(),
    )(x, g)


def _rms_bwd(x, g, dh, dres, *, name, tm=512):
    s, d = x.shape

    def body(x_ref, g_ref, dh_ref, dres_ref, dx_ref, dg_ref):
        i = pl.program_id(0)
        xv = x_ref[...]
        rstd = lax.rsqrt(jnp.mean(xv * xv, axis=-1, keepdims=True) + EPS)
        xhat = xv * rstd
        dhv = dh_ref[...]
        dxhat = dhv * g_ref[...]
        proj = jnp.mean(dxhat * xhat, axis=-1, keepdims=True)
        dx_ref[...] = dres_ref[...] + rstd * (dxhat - xhat * proj)
        part = jnp.sum(dhv * xhat, axis=0, keepdims=True)

        @pl.when(i == 0)
        def _():
            dg_ref[...] = part

        @pl.when(i > 0)
        def _():
            dg_ref[...] += part

    row = pl.BlockSpec((tm, d), lambda i: (i, 0))
    vec = pl.BlockSpec((1, d), lambda i: (0, 0))
    return pl.pallas_call(
        body, name=name, grid=(s // tm,),
        in_specs=[row, vec, row, row], out_specs=[row, vec],
        out_shape=[jax.ShapeDtypeStruct((s, d), F32), jax.ShapeDtypeStruct((1, d), F32)],
        compiler_params=_params(),
    )(x, g, dh, dres)


N_QKV_BLOCKS = QKV_WIDTH // LANES
A_BLOCKS = (A_Q_WIDTH + 2 * A_KV_WIDTH) // LANES
V_A_BLOCK = A_BLOCKS - 1
V_B_FIRST = A_BLOCKS + 2 * (B_WIDTH // LANES)


def _qk_kind(j):
    return jnp.where(j < A_BLOCKS, 0, 1)


def _is_v_block(j):
    return (j == V_A_BLOCK) | (j >= V_B_FIRST)


def _rope_tables(s):
    def ang(pos, dim):
        inv = ROPE_THETA ** (-jnp.arange(0, dim, 2, dtype=F32) / dim)
        return pos.astype(F32)[:, None] * inv[None, :]
    t = jnp.arange(s)
    a_row = ang(t // GRID_W, HEAD_DIM // 2)
    a_col = ang(t % GRID_W, HEAD_DIM // 2)
    a_seq = ang(t, HEAD_DIM)
    cos_a = jnp.concatenate([jnp.cos(a_row)] * 2 + [jnp.cos(a_col)] * 2, axis=-1)
    sin_a = jnp.concatenate([-jnp.sin(a_row), jnp.sin(a_row), -jnp.sin(a_col), jnp.sin(a_col)], axis=-1)
    cos_b = jnp.concatenate([jnp.cos(a_seq)] * 2, axis=-1)
    sin_b = jnp.concatenate([-jnp.sin(a_seq), jnp.sin(a_seq)], axis=-1)
    cos = jnp.stack([jnp.tile(cos_a, (1, 2)), jnp.tile(cos_b, (1, 2))])
    sin = jnp.stack([jnp.tile(sin_a, (1, 2)), jnp.tile(sin_b, (1, 2))])
    return cos, sin


def _qk_gains(qn_a, kn_a, qn_b, kn_b):
    one = jnp.ones((HEAD_DIM,), F32)
    parts = [jnp.tile(qn_a, 8), jnp.tile(kn_a, 2), jnp.tile(one, 2),
             jnp.tile(qn_b, 12), jnp.tile(kn_b, 12), jnp.tile(one, 12)]
    return jnp.concatenate(parts)[None, :]


def _qkrope_fwd(z, gains, cos, sin, *, name, tm=2048):
    s = z.shape[0]
    tm = min(tm, s)

    def body(z_ref, g_ref, c_ref, s_ref, o_ref):
        j = pl.program_id(1)
        lane = _lane_iota()
        xv = z_ref[...].astype(F32)

        def normed_rope(pair):
            ms = _head_sum(xv * xv) * (1.0 / HEAD_DIM)
            n = xv * lax.rsqrt(ms + EPS) * g_ref[...]
            return n * c_ref[...] + _swap(n, pair, lane) * s_ref[...]

        @pl.when(_is_v_block(j))
        def _():
            o_ref[...] = xv.astype(BF16)

        @pl.when(jnp.logical_not(_is_v_block(j)) & (j < A_BLOCKS))
        def _():
            o_ref[...] = normed_rope(HEAD_DIM // 4).astype(BF16)

        @pl.when(jnp.logical_not(_is_v_block(j)) & (j >= A_BLOCKS))
        def _():
            o_ref[...] = normed_rope(HEAD_DIM // 2).astype(BF16)

    tab = pl.BlockSpec((None, tm, LANES), lambda i, j: (_qk_kind(j), i, 0))
    blk = pl.BlockSpec((tm, LANES), lambda i, j: (i, j))
    return pl.pallas_call(
        body, name=name, grid=(s // tm, N_QKV_BLOCKS),
        in_specs=[blk, pl.BlockSpec((1, LANES), lambda i, j: (0, j)), tab, tab],
        out_specs=blk,
        out_shape=jax.ShapeDtypeStruct((s, QKV_WIDTH), BF16),
        compiler_params=_params(),
    )(z, gains, cos, sin)


SUBLANES = 8


def _qkrope_bwd(z, gains, cos, sin, sources, *, name, tm=2048):
    s = z.shape[0]
    tm = min(tm, s)
    n_src = len(sources)
    widths = [a.shape[1] // LANES for a in sources]
    firsts = [sum(widths[:k]) for k in range(n_src)]
    assert sum(widths) == N_QKV_BLOCKS

    def body(z_ref, g_ref, c_ref, s_ref, *rest):
        src_refs, (dz_ref, dg_ref, dy_ref) = rest[:n_src], rest[n_src:]
        j = pl.program_id(1)
        lane = _lane_iota()
        xv = z_ref[...].astype(F32)
        for src_ref, first, width in zip(src_refs, firsts, widths):
            @pl.when((j >= first) & (j < first + width))
            def _(src_ref=src_ref):
                dy_ref[...] = src_ref[...].astype(F32)
        dy = dy_ref[...]

        def back(pair):
            ms = _head_sum(xv * xv) * (1.0 / HEAD_DIM)
            rstd = lax.rsqrt(ms + EPS)
            xhat = xv * rstd
            dn = dy * c_ref[...] + _swap(dy * s_ref[...], pair, lane)
            dg_ref[...] = jnp.sum((dn * xhat).reshape(tm // SUBLANES, SUBLANES, LANES), axis=0)
            dxhat = dn * g_ref[...]
            proj = _head_sum(dxhat * xhat) * (1.0 / HEAD_DIM)
            dz_ref[...] = (rstd * (dxhat - xhat * proj)).astype(BF16)

        @pl.when(_is_v_block(j))
        def _():
            dz_ref[...] = dy.astype(BF16)
            dg_ref[...] = jnp.zeros_like(dg_ref)

        @pl.when(jnp.logical_not(_is_v_block(j)) & (j < A_BLOCKS))
        def _():
            back(HEAD_DIM // 4)

        @pl.when(jnp.logical_not(_is_v_block(j)) & (j >= A_BLOCKS))
        def _():
            back(HEAD_DIM // 2)

    def src_spec(first, width):
        return pl.BlockSpec((tm, LANES), lambda i, j: (i, jnp.clip(j - first, 0, width - 1)))

    tab = pl.BlockSpec((None, tm, LANES), lambda i, j: (_qk_kind(j), i, 0))
    blk = pl.BlockSpec((tm, LANES), lambda i, j: (i, j))
    return pl.pallas_call(
        body, name=name, grid=(s // tm, N_QKV_BLOCKS),
        in_specs=[blk, pl.BlockSpec((1, LANES), lambda i, j: (0, j)), tab, tab]
        + [src_spec(first, width) for first, width in zip(firsts, widths)],
        out_specs=[blk, pl.BlockSpec((SUBLANES, LANES), lambda i, j: (i, j))],
        out_shape=[jax.ShapeDtypeStruct((s, QKV_WIDTH), BF16),
                   jax.ShapeDtypeStruct((s // tm * SUBLANES, QKV_WIDTH), F32)],
        scratch_shapes=[pltpu.VMEM((tm, LANES), F32)],
        compiler_params=_params(),
    )(z, gains, cos, sin, *sources)


def _dup_halves(kv):
    h0, h1 = kv[:, :HEAD_DIM], kv[:, HEAD_DIM:]
    return jnp.concatenate([h0, h0, h1, h1], axis=1)


def _with_ones(kv):
    h0, h1 = kv[:, :HEAD_DIM], kv[:, HEAD_DIM:]
    one = jnp.ones_like(h0)
    return jnp.concatenate([h0, one, one, h0, h1, one, one, h1], axis=1)


def _attn_a_fwd(qkv, kdup, vones, *, name, ride=None, tq=256, tk=512, unroll=8):
    s = qkv.shape[0]
    tq, tk = min(tq, s), min(tk, s)
    n_chunks = s // tk
    unroll = min(unroll, n_chunks)
    assert n_chunks % unroll == 0

    def body(q_ref, k_ref, v_ref, o_ref, lse_ref):
        lane = _lane_iota()
        low = lane < HEAD_DIM
        q = q_ref[...]
        zero = jnp.zeros_like(q)
        qm = [jnp.where(low, q, zero) * ATTN_SCALE, jnp.where(low, zero, q) * ATTN_SCALE]

        def chunks(c, carry):
            state = list(carry)
            scs = []
            for u in range(unroll):
                off = pl.multiple_of((c * unroll + u) * tk, tk)
                kc = k_ref[pl.ds(off, tk), :]
                scs.append([lax.dot_general(qm[e], kc, NT, preferred_element_type=F32) for e in range(2)])
            for u in range(unroll):
                off = pl.multiple_of((c * unroll + u) * tk, tk)
                for e in range(2):
                    m, acc = state[2 * e], state[2 * e + 1]
                    ve = v_ref[pl.ds(off, tk), e * LANES:(e + 1) * LANES]
                    m_new = jnp.maximum(m, jnp.max(scs[u][e], axis=1, keepdims=True))
                    alpha = jnp.exp(m - m_new)
                    p = jnp.exp(scs[u][e] - m_new).astype(BF16)
                    state[2 * e] = m_new
                    state[2 * e + 1] = alpha * acc + lax.dot_general(p, ve, NN, preferred_element_type=F32)
            return tuple(state)

        m_init = jnp.full((tq, 1), NEG_INF, F32)
        a_init = jnp.zeros((tq, LANES), F32)
        m0, a0, m1, a1 = lax.fori_loop(0, n_chunks // unroll, chunks, (m_init, a_init, m_init, a_init))
        l0 = pltpu.roll(a0, HEAD_DIM, axis=1)
        l1 = pltpu.roll(a1, HEAD_DIM, axis=1)
        o_ref[...] = jnp.where(low, a0 / l0, a1 / l1).astype(BF16)
        lse_ref[...] = jnp.where(low, m0 + jnp.log(l0), m1 + jnp.log(l1))

    q_spec = pl.BlockSpec((tq, LANES), lambda hb, qi: (qi, hb))
    k_spec = pl.BlockSpec((s, LANES), lambda hb, qi: (0, hb // 2))
    v_spec = pl.BlockSpec((s, 2 * LANES), lambda hb, qi: (0, hb // 2))
    grid = (A_Q_WIDTH // LANES, s // tq)
    body, extra = _carry_exchange(body, 3, 2, grid, ride)
    return pl.pallas_call(
        body, name=name, grid=grid,
        in_specs=[q_spec, k_spec, v_spec] + extra.in_specs, out_specs=[q_spec, q_spec] + extra.out_specs,
        out_shape=[jax.ShapeDtypeStruct((s, A_Q_WIDTH), BF16), jax.ShapeDtypeStruct((s, A_Q_WIDTH), F32)]
        + extra.out_shape,
        scratch_shapes=extra.scratch, compiler_params=_params(),
    )(qkv, kdup, vones, *extra.args)


def _attn_a_bwd(qkv, kdup, vdup, o, lse, do, *, name, ride=None, tq=512, tk=512, unroll=2):
    s = qkv.shape[0]
    tq, tk = min(tq, s), min(tk, s)
    n_chunks = s // tk
    unroll = min(unroll, n_chunks)
    assert n_chunks % unroll == 0

    def body(q_ref, k_ref, v_ref, o_ref, lse_ref, do_ref, dq_ref, dk_ref, dv_ref):
        first = (pl.program_id(1) == 0) & (pl.program_id(2) == 0)

        @pl.when(first)
        def _():
            dk_ref[...] = jnp.zeros_like(dk_ref)
            dv_ref[...] = jnp.zeros_like(dv_ref)

        lane = _lane_iota()
        low = lane < HEAD_DIM
        q = q_ref[...]
        dov = do_ref[...]
        zero = jnp.zeros_like(q)
        prod = dov.astype(F32) * o_ref[...].astype(F32)
        lsev = lse_ref[...]
        qs = [jnp.where(low, q, zero) * ATTN_SCALE, jnp.where(low, zero, q) * ATTN_SCALE]
        dom = [jnp.where(low, dov, zero), jnp.where(low, zero, dov)]
        delta = [jnp.sum(jnp.where(low, prod, 0.0), axis=1, keepdims=True),
                 jnp.sum(jnp.where(low, 0.0, prod), axis=1, keepdims=True)]
        lse = [lsev[:, 0:1], lsev[:, HEAD_DIM:HEAD_DIM + 1]]
        qs_both = jnp.concatenate(qs, axis=0)
        dom_both = jnp.concatenate(dom, axis=0)

        def chunks(c, carry):
            dqs = list(carry)
            for u in range(unroll):
                off = pl.multiple_of((c * unroll + u) * tk, tk)
                kc = k_ref[pl.ds(off, tk), :]
                vc = v_ref[pl.ds(off, tk), :]
                ps, dss = [], []
                for e in range(2):
                    sc = lax.dot_general(qs[e], kc, NT, preferred_element_type=F32)
                    p = jnp.exp(sc - lse[e])
                    dp = lax.dot_general(dom[e], vc, NT, preferred_element_type=F32)
                    ds = (p * (dp - delta[e])).astype(BF16)
                    ps.append(p.astype(BF16))
                    dss.append(ds)
                    dqs[e] = dqs[e] + lax.dot_general(ds, kc, NN, preferred_element_type=F32)
                dv_ref[pl.ds(off, tk), :] += lax.dot_general(jnp.concatenate(ps, axis=0), dom_both, TN,
                                                             preferred_element_type=F32)
                dk_ref[pl.ds(off, tk), :] += lax.dot_general(jnp.concatenate(dss, axis=0), qs_both, TN,
                                                             preferred_element_type=F32)
            return tuple(dqs)

        dq_init = jnp.zeros((tq, LANES), F32)
        dq0, dq1 = lax.fori_loop(0, n_chunks // unroll, chunks, (dq_init, dq_init))
        dq_ref[...] = (jnp.where(low, dq0, dq1) * ATTN_SCALE).astype(BF16)

    q_spec = pl.BlockSpec((tq, LANES), lambda kvh, hb, qi: (qi, kvh * 2 + hb))
    kv_spec = pl.BlockSpec((s, LANES), lambda kvh, hb, qi: (0, kvh))
    grid = (2, 2, s // tq)
    body, extra = _carry_exchange(body, 6, 3, grid, ride)
    return pl.pallas_call(
        body, name=name, grid=grid,
        in_specs=[q_spec, kv_spec, kv_spec, q_spec, q_spec, q_spec] + extra.in_specs,
        out_specs=[q_spec, kv_spec, kv_spec] + extra.out_specs,
        out_shape=[jax.ShapeDtypeStruct((s, A_Q_WIDTH), BF16),
                   jax.ShapeDtypeStruct((s, 2 * LANES), F32), jax.ShapeDtypeStruct((s, 2 * LANES), F32)]
        + extra.out_shape,
        scratch_shapes=extra.scratch, compiler_params=_params(),
    )(qkv, kdup, vdup, o, lse, do, *extra.args)


BAND_Q = 128
GROUP_QKV = 3 * B_OUT_WIDTH
TOKEN_BLOCKS = GROUP_QKV // LANES
B_Q_BLOCK0 = 0
B_K_BLOCK0 = B_OUT_WIDTH // LANES
B_V_BLOCK0 = 2 * (B_OUT_WIDTH // LANES)


BAND_UNROLL = 4


def _unrolled_loop(n, step):
    unroll = BAND_UNROLL
    while n % unroll:
        unroll //= 2

    def body(it, carry):
        for u in range(unroll):
            step(it * unroll + u)
        return carry

    lax.fori_loop(0, n // unroll, body, 0)


def _band_geometry(length):
    seg = min(length, 2048)
    win = min(2 * BAND_Q, length)
    return seg, win


def _band_window(qs, length, win):
    st = jnp.clip(qs - B_HALF_SPAN, 0, length - win)
    st = pl.multiple_of(st, B_HALF_SPAN)
    qpos = qs + lax.broadcasted_iota(jnp.int32, (BAND_Q, 1), 0)
    kpos = st + lax.broadcasted_iota(jnp.int32, (1, win), 1)
    return st, jnp.abs(qpos - kpos) <= B_HALF_SPAN


def _band_fwd(qkv_view, gi, dil, *, name):
    length = qkv_view.shape[0]
    seg, win = _band_geometry(length)
    n_sub = seg // BAND_Q

    def body(q_ref, k_ref, v_ref, o_ref, lse_ref):
        seg_i = pl.program_id(2)
        lane = _lane_iota()
        low = lane < HEAD_DIM

        def one(i):
            ql = pl.multiple_of(i * BAND_Q, BAND_Q)
            st, valid = _band_window(seg_i * seg + ql, length, win)
            q = q_ref[pl.ds(ql, BAND_Q), :]
            kw = k_ref[pl.ds(st, win), :]
            vw = v_ref[pl.ds(st, win), :]
            outs, lses = [], []
            for e in range(2):
                mine = (lane >= HEAD_DIM) if e else (lane < HEAD_DIM)
                qm = jnp.where(mine, q, jnp.zeros_like(q)) * ATTN_SCALE
                sc = lax.dot_general(qm, kw, NT, preferred_element_type=F32)
                sc = jnp.where(valid, sc, NEG_INF)
                m = jnp.max(sc, axis=1, keepdims=True)
                p = jnp.exp(sc - m)
                l = jnp.sum(p, axis=1, keepdims=True)
                outs.append(lax.dot_general(p.astype(BF16), vw, NN, preferred_element_type=F32) / l)
                lses.append(m + jnp.log(l))
            o_ref[pl.ds(ql, BAND_Q), :] = jnp.where(low, outs[0], outs[1]).astype(BF16)
            lse_ref[pl.ds(ql, BAND_Q), :] = jnp.where(low, lses[0], lses[1])

        _unrolled_loop(n_sub, one)

    def col(base):
        return lambda r, hp, sg: (0, r * TOKEN_BLOCKS + base + hp)

    q_spec = pl.BlockSpec((seg, LANES), lambda r, hp, sg: (sg, r * TOKEN_BLOCKS + B_Q_BLOCK0 + hp))
    out_spec = pl.BlockSpec((seg, LANES), lambda r, hp, sg: (sg, r * 2 + hp))
    return pl.pallas_call(
        body, name=name, grid=(dil, 2, length // seg),
        in_specs=[q_spec, pl.BlockSpec((length, LANES), col(B_K_BLOCK0)), pl.BlockSpec((length, LANES), col(B_V_BLOCK0))],
        out_specs=[out_spec, out_spec],
        out_shape=[jax.ShapeDtypeStruct((length, dil * B_OUT_WIDTH), BF16),
                   jax.ShapeDtypeStruct((length, dil * B_OUT_WIDTH), F32)],
        compiler_params=_params(),
    )(qkv_view, qkv_view, qkv_view)


def _band_bwd(qkv_view, do, lse, dd, gi, dil, *, name):
    length = qkv_view.shape[0]
    seg, win = _band_geometry(length)
    n_sub = seg // BAND_Q

    def body(q_ref, k_ref, v_ref, do_ref, lse_ref, dd_ref, dq_ref, dk_ref, dv_ref):
        seg_i = pl.program_id(2)
        lane = _lane_iota()

        @pl.when(seg_i == 0)
        def _():
            dk_ref[...] = jnp.zeros_like(dk_ref)
            dv_ref[...] = jnp.zeros_like(dv_ref)

        def one(i):
            ql = pl.multiple_of(i * BAND_Q, BAND_Q)
            st, valid = _band_window(seg_i * seg + ql, length, win)
            q = q_ref[pl.ds(ql, BAND_Q), :]
            dov = do_ref[pl.ds(ql, BAND_Q), :]
            lsev = lse_ref[pl.ds(ql, BAND_Q), :]
            ddv = dd_ref[pl.ds(ql, BAND_Q), :]
            kw = k_ref[pl.ds(st, win), :]
            vw = v_ref[pl.ds(st, win), :]
            dq = jnp.zeros((BAND_Q, LANES), F32)
            for e in range(2):
                mine = (lane >= HEAD_DIM) if e else (lane < HEAD_DIM)
                qs = jnp.where(mine, q, jnp.zeros_like(q)) * ATTN_SCALE
                dom = jnp.where(mine, dov, jnp.zeros_like(dov))
                lse_e = lsev[:, e * HEAD_DIM:e * HEAD_DIM + 1]
                dd_e = ddv[:, e * HEAD_DIM:e * HEAD_DIM + 1]
                sc = lax.dot_general(qs, kw, NT, preferred_element_type=F32)
                p = jnp.exp(jnp.where(valid, sc, NEG_INF) - lse_e)
                dp = lax.dot_general(dom, vw, NT, preferred_element_type=F32)
                ds = (p * (dp - dd_e)).astype(BF16)
                dv_ref[pl.ds(st, win), :] += lax.dot_general(p.astype(BF16), dom, TN, preferred_element_type=F32)
                dk_ref[pl.ds(st, win), :] += lax.dot_general(ds, qs, TN, preferred_element_type=F32)
                dq_e = lax.dot_general(ds, kw, NN, preferred_element_type=F32) * ATTN_SCALE
                dq = dq + jnp.where(mine, dq_e, 0.0)
            dq_ref[pl.ds(ql, BAND_Q), :] = dq.astype(BF16)

        _unrolled_loop(n_sub, one)

    def col(base):
        return lambda r, hp, sg: (0, r * TOKEN_BLOCKS + base + hp)

    q_spec = pl.BlockSpec((seg, LANES), lambda r, hp, sg: (sg, r * TOKEN_BLOCKS + B_Q_BLOCK0 + hp))
    seg_spec = pl.BlockSpec((seg, LANES), lambda r, hp, sg: (sg, r * 2 + hp))
    full_spec = pl.BlockSpec((length, LANES), lambda r, hp, sg: (0, r * 2 + hp))
    shp = jax.ShapeDtypeStruct((length, dil * B_OUT_WIDTH), F32)
    return pl.pallas_call(
        body, name=name, grid=(dil, 2, length // seg),
        in_specs=[q_spec, pl.BlockSpec((length, LANES), col(B_K_BLOCK0)), pl.BlockSpec((length, LANES), col(B_V_BLOCK0)),
                  seg_spec, seg_spec, seg_spec],
        out_specs=[seg_spec, full_spec, full_spec],
        out_shape=[jax.ShapeDtypeStruct(shp.shape, BF16), shp, shp],
        compiler_params=_params(),
    )(qkv_view, qkv_view, qkv_view, do, lse, dd)


def _merge_weights(lses):
    m = jnp.maximum(jnp.maximum(lses[0], lses[1]), lses[2])
    ex = [jnp.exp(v - m) for v in lses]
    tot = ex[0] + ex[1] + ex[2]
    return [v / tot for v in ex]


def _merge_fwd(os_, lses, *, name, tm=512):
    s = os_[0].shape[0]

    def body(o0, o1, o2, l0, l1, l2, ob_ref):
        w = _merge_weights([l0[...], l1[...], l2[...]])
        ob = w[0] * o0[...].astype(F32) + w[1] * o1[...].astype(F32) + w[2] * o2[...].astype(F32)
        ob_ref[...] = ob.astype(BF16)

    blk = pl.BlockSpec((tm, B_OUT_WIDTH), lambda i: (i, 0))
    return pl.pallas_call(
        body, name=name, grid=(s // tm,), in_specs=[blk] * 6, out_specs=blk,
        out_shape=jax.ShapeDtypeStruct((s, B_OUT_WIDTH), BF16),
        compiler_params=_params(),
    )(*os_, *lses)


def _merge_bwd(os_, lses, dob, *, name, tm=512):
    s = os_[0].shape[0]

    def body(o0, o1, o2, l0, l1, l2, dob_ref, d0, d1, d2, t0, t1, t2):
        w = _merge_weights([l0[...], l1[...], l2[...]])
        dv = dob_ref[...]
        ob = w[0] * o0[...].astype(F32) + w[1] * o1[...].astype(F32) + w[2] * o2[...].astype(F32)
        tot = _head_sum(dv * ob)
        for wg, d_ref, t_ref in zip(w, (d0, d1, d2), (t0, t1, t2)):
            d_ref[...] = (wg * dv).astype(BF16)
            t_ref[...] = wg * tot

    blk = pl.BlockSpec((tm, B_OUT_WIDTH), lambda i: (i, 0))
    return pl.pallas_call(
        body, name=name, grid=(s // tm,), in_specs=[blk] * 7, out_specs=[blk] * 6,
        out_shape=[jax.ShapeDtypeStruct((s, B_OUT_WIDTH), BF16)] * 3 + [jax.ShapeDtypeStruct((s, B_OUT_WIDTH), F32)] * 3,
        compiler_params=_params(),
    )(*os_, *lses, dob)


HALO = 16
POOL_BLOCK0 = QKV_WIDTH // LANES


def _window_sum(ext, lo, hi, tm):
    rows = ext.shape[0]
    acc = None
    for j in range(lo, hi + 1):
        r = ext if j == 0 else pltpu.roll(ext, (-j) % rows, axis=0)
        acc = r if acc is None else acc + r
    return acc[HALO:HALO + tm]


def _pool_counts(t, half, s):
    return (jnp.minimum(t + half, s) - jnp.maximum(t - half, 0)).astype(F32)


def _halo_specs(tm, s, col0):
    per = tm // HALO
    last = s // HALO - 1
    prev = pl.BlockSpec((HALO, LANES), lambda g, i: (jnp.maximum(i * per - 1, 0), col0 + g))
    cur = pl.BlockSpec((tm, LANES), lambda g, i: (i, col0 + g))
    nxt = pl.BlockSpec((HALO, LANES), lambda g, i: (jnp.minimum((i + 1) * per, last), col0 + g))
    return prev, cur, nxt


def _extended(prev_ref, cur_ref, next_ref, i, n_tiles):
    prev = jnp.where(i > 0, prev_ref[...].astype(F32), 0.0)
    nxt = jnp.where(i < n_tiles - 1, next_ref[...].astype(F32), 0.0)
    return jnp.concatenate([prev, cur_ref[...].astype(F32), nxt], axis=0)


def _pool_fwd(z, lin, scale, *, name, tm=512):
    s = z.shape[0]
    tm = min(tm, s)
    n_tiles = s // tm

    def body(prev_ref, cur_ref, next_ref, lin_ref, sc_ref, pooled_ref, mixed_ref):
        g = pl.program_id(0)
        i = pl.program_id(1)
        ext = _extended(prev_ref, cur_ref, next_ref, i, n_tiles)
        t = i * tm + lax.broadcasted_iota(jnp.int32, (tm, 1), 0)
        for gi, half in enumerate(POOL_HALF):
            @pl.when(g == gi)
            def _(half=half):
                mean = _window_sum(ext, -half, half - 1, tm) / _pool_counts(t, half, s)
                pooled = (mean - cur_ref[...]).astype(BF16)
                pooled_ref[...] = pooled
                mixed = lax.dot_general(pooled, lin_ref[...].astype(BF16), NN, preferred_element_type=F32)
                mixed_ref[...] = (mixed * sc_ref[...]).astype(BF16)

    prev, cur, nxt = _halo_specs(tm, s, POOL_BLOCK0)
    out = pl.BlockSpec((tm, LANES), lambda g, i: (i, g))
    return pl.pallas_call(
        body, name=name, grid=(len(POOL_HALF), n_tiles),
        in_specs=[prev, cur, nxt, pl.BlockSpec((None, LANES, LANES), lambda g, i: (g, 0, 0)),
                  pl.BlockSpec((1, LANES), lambda g, i: (0, g))],
        out_specs=[out, out],
        out_shape=[jax.ShapeDtypeStruct((s, POOL_WIDTH), BF16)] * 2,
        compiler_params=_params(),
    )(z, z, z, lin, scale)


def _pool_bwd(dmixed, pooled, lin, scale, *, name, tm=512):
    s = dmixed.shape[0]
    tm = min(tm, s)
    n_tiles = s // tm

    def body(prev_ref, cur_ref, next_ref, pooled_ref, lin_ref, sc_ref, du_ref, dlin_ref, dsc_ref):
        g = pl.program_id(0)
        i = pl.program_id(1)

        @pl.when(i == 0)
        def _():
            dlin_ref[...] = jnp.zeros_like(dlin_ref)
            dsc_ref[...] = jnp.zeros_like(dsc_ref)

        linb = lin_ref[...].astype(BF16)
        ext = _extended(prev_ref, cur_ref, next_ref, i, n_tiles)
        dpl_ext = (ext * sc_ref[...]).astype(BF16)
        dpl_cur = (cur_ref[...] * sc_ref[...]).astype(BF16)
        dpooled_ext = lax.dot_general(dpl_ext, linb, NT, preferred_element_type=F32)
        t_ext = i * tm - HALO + lax.broadcasted_iota(jnp.int32, (tm + 2 * HALO, 1), 0)
        pooled = pooled_ref[...]
        mixed = lax.dot_general(pooled, linb, NN, preferred_element_type=F32)
        dsc_ref[...] += jnp.sum(cur_ref[...] * mixed, axis=0, keepdims=True)
        dlin_ref[...] += lax.dot_general(pooled, dpl_cur, TN, preferred_element_type=F32)
        for gi, half in enumerate(POOL_HALF):
            @pl.when(g == gi)
            def _(half=half):
                share = dpooled_ext / jnp.maximum(_pool_counts(t_ext, half, s), 1.0)
                du = _window_sum(share, -(half - 1), half, tm) - dpooled_ext[HALO:HALO + tm]
                du_ref[...] = du.astype(BF16)

    prev, cur, nxt = _halo_specs(tm, s, 0)
    out = pl.BlockSpec((tm, LANES), lambda g, i: (i, g))
    lin_spec = pl.BlockSpec((None, LANES, LANES), lambda g, i: (g, 0, 0))
    vec = pl.BlockSpec((1, LANES), lambda g, i: (0, g))
    return pl.pallas_call(
        body, name=name, grid=(len(POOL_HALF), n_tiles),
        in_specs=[prev, cur, nxt, out, lin_spec, vec],
        out_specs=[out, lin_spec, vec],
        out_shape=[jax.ShapeDtypeStruct((s, POOL_WIDTH), BF16),
                   jax.ShapeDtypeStruct((len(POOL_HALF), LANES, LANES), F32),
                   jax.ShapeDtypeStruct((1, POOL_WIDTH), F32)],
        compiler_params=_params(),
    )(dmixed, dmixed, dmixed, pooled, lin, scale)


GATE_TILE = 512
GATE_BLOCK0 = (QKV_WIDTH + POOL_WIDTH) // GATE_TILE
GATE_BLOCKS_PER_BRANCH = D_MODEL // GATE_TILE


def _sigmoid(v):
    return 1.0 / (1.0 + jnp.exp(-v))


def _gate_specs(tm):
    def zspec(br):
        return pl.BlockSpec((tm, GATE_TILE), lambda jj, i: (i, GATE_BLOCK0 + GATE_BLOCKS_PER_BRANCH * br + jj))

    def bspec(br):
        return pl.BlockSpec((1, GATE_TILE), lambda jj, i: (0, GATE_BLOCKS_PER_BRANCH * br + jj))

    row = pl.BlockSpec((tm, GATE_TILE), lambda jj, i: (i, jj))
    vec = pl.BlockSpec((1, GATE_TILE), lambda jj, i: (0, jj))
    return [zspec(0), zspec(1), zspec(2)], [bspec(0), bspec(1), bspec(2)], row, vec


def _gate_fwd(z, b_gate, ya, yb, yc, *, name, tm=512):
    s = z.shape[0]

    def body(z0, z1, z2, b0, b1, b2, ya_ref, yb_ref, yc_ref, out_ref):
        acc = _sigmoid(z0[...] + b0[...]) * ya_ref[...]
        acc = acc + _sigmoid(z1[...] + b1[...]) * yb_ref[...]
        acc = acc + _sigmoid(z2[...] + b2[...]) * yc_ref[...]
        out_ref[...] = acc.astype(BF16)

    zs, bs, row, _ = _gate_specs(tm)
    return pl.pallas_call(
        body, name=name, grid=(GATE_BLOCKS_PER_BRANCH, s // tm),
        in_specs=zs + bs + [row] * 3, out_specs=row,
        out_shape=jax.ShapeDtypeStruct((s, D_MODEL), BF16),
        compiler_params=_params(),
    )(z, z, z, b_gate, b_gate, b_gate, ya, yb, yc)


def _gate_bwd(z, b_gate, ya, yb, yc, dmerged, *, name, tm=512):
    s = z.shape[0]

    def body(z0, z1, z2, b0, b1, b2, ya_ref, yb_ref, yc_ref, dm_ref,
             dya_ref, dyb_ref, dyc_ref, dg0, dg1, dg2, db0, db1, db2):
        i = pl.program_id(1)
        dm = dm_ref[...]
        for z_ref, b_ref, y_ref, dy_ref, dg_ref, db_ref in (
                (z0, b0, ya_ref, dya_ref, dg0, db0), (z1, b1, yb_ref, dyb_ref, dg1, db1),
                (z2, b2, yc_ref, dyc_ref, dg2, db2)):
            gate = _sigmoid(z_ref[...] + b_ref[...])
            dy_ref[...] = (gate * dm).astype(BF16)
            dpre = dm * y_ref[...] * gate * (1.0 - gate)
            dg_ref[...] = dpre.astype(BF16)
            part = jnp.sum(dpre, axis=0, keepdims=True)

            @pl.when(i == 0)
            def _(db_ref=db_ref, part=part):
                db_ref[...] = part

            @pl.when(i > 0)
            def _(db_ref=db_ref, part=part):
                db_ref[...] += part

    zs, bs, row, vec = _gate_specs(tm)
    big = jax.ShapeDtypeStruct((s, D_MODEL), BF16)
    small = jax.ShapeDtypeStruct((1, D_MODEL), F32)
    return pl.pallas_call(
        body, name=name, grid=(GATE_BLOCKS_PER_BRANCH, s // tm),
        in_specs=zs + bs + [row] * 4, out_specs=[row] * 6 + [vec] * 3,
        out_shape=[big] * 6 + [small] * 3,
        compiler_params=_params(),
    )(z, z, z, b_gate, b_gate, b_gate, ya, yb, yc, dmerged)


def _swiglu_fwd(a, b, *, name, tm=512, tn=1408):
    s, f = a.shape

    def body(a_ref, b_ref, o_ref):
        av = a_ref[...].astype(F32)
        o_ref[...] = (av * _sigmoid(av) * b_ref[...].astype(F32)).astype(BF16)

    blk = pl.BlockSpec((tm, tn), lambda i, j: (i, j))
    return pl.pallas_call(
        body, name=name, grid=(s // tm, f // tn), in_specs=[blk, blk], out_specs=blk,
        out_shape=jax.ShapeDtypeStruct((s, f), BF16), compiler_params=_params(),
    )(a, b)


def _swiglu_bwd(a, b, df, *, name, tm=512, tn=1408):
    s, f = a.shape

    def body(a_ref, b_ref, df_ref, da_ref, db_ref):
        av = a_ref[...].astype(F32)
        dfv = df_ref[...].astype(F32)
        sg = _sigmoid(av)
        silu = av * sg
        da_ref[...] = (dfv * b_ref[...].astype(F32) * (sg + silu * (1.0 - sg))).astype(BF16)
        db_ref[...] = (dfv * silu).astype(BF16)

    blk = pl.BlockSpec((tm, tn), lambda i, j: (i, j))
    out = jax.ShapeDtypeStruct((s, f), BF16)
    return pl.pallas_call(
        body, name=name, grid=(s // tm, f // tn), in_specs=[blk] * 3, out_specs=[blk] * 2,
        out_shape=[out, out], compiler_params=_params(),
    )(a, b, df)


def _loss_head(y, target, *, name, tm=512):
    s, d = y.shape

    def body(y_ref, t_ref, part_ref, dy_ref):
        i = pl.program_id(0)
        err = y_ref[...] - t_ref[...]
        dy_ref[...] = err * (1.0 / d)
        part = jnp.sum(err * err, axis=0, keepdims=True) * (0.5 / d)

        @pl.when(i == 0)
        def _():
            part_ref[...] = part

        @pl.when(i > 0)
        def _():
            part_ref[...] += part

    row = pl.BlockSpec((tm, d), lambda i: (i, 0))
    vec = pl.BlockSpec((1, d), lambda i: (0, 0))
    return pl.pallas_call(
        body, name=name, grid=(s // tm,), in_specs=[row, row], out_specs=[vec, row],
        out_shape=[jax.ShapeDtypeStruct((1, d), F32), jax.ShapeDtypeStruct((s, d), F32)],
        compiler_params=_params(),
    )(y, target)


def _mesh_place():
    x, y, c = lax.axis_index('x'), lax.axis_index('y'), lax.axis_index('c')
    return x, y, c, 4 * x + 2 * y + c


def _peer(x, y, c, k):
    return (x ^ ((k >> 2) & 1), y ^ ((k >> 1) & 1), c ^ (k & 1))


def _exchange(bufs, *, gather, name):
    n = len(bufs)

    def body(*refs):
        start, wait = _exchange_plan(refs[:n], refs[n:2 * n], refs[2 * n:], gather)
        start()
        wait()

    extra = _Extra((bufs, gather))
    return pl.pallas_call(
        body, name=name, in_specs=extra.in_specs, out_specs=extra.out_specs,
        out_shape=extra.out_shape, scratch_shapes=extra.scratch,
    )(*bufs)


_EXCHANGE_SPEC = pl.BlockSpec(memory_space=pl.ANY)
SEMS_PER_BUFFER = 3


class _Extra:
    def __init__(self, ride):
        bufs = [] if ride is None else list(ride[0])
        self.in_specs = [_EXCHANGE_SPEC] * len(bufs)
        self.out_specs = [_EXCHANGE_SPEC] * len(bufs)
        self.out_shape = [jax.ShapeDtypeStruct((N_DEV,) + b.shape[-2:], b.dtype) for b in bufs]
        self.scratch = [pltpu.SemaphoreType.DMA((N_DEV - 1,)), pltpu.SemaphoreType.DMA((N_DEV - 1,)),
                        pltpu.SemaphoreType.DMA] * len(bufs)
        self.args = tuple(bufs)


def _carry_exchange(body, n_in, n_out, grid, ride):
    extra = _Extra(ride)
    if ride is None:
        return body, extra
    n, gather = len(ride[0]), ride[1]

    def carrying(*refs):
        ins, ride_ins = refs[:n_in], refs[n_in:n_in + n]
        outs, ride_outs = refs[n_in + n:n_in + n + n_out], refs[n_in + n + n_out:n_in + 2 * n + n_out]
        sems = refs[n_in + 2 * n + n_out:]
        ids = [pl.program_id(a) for a in range(len(grid))]
        first = functools.reduce(jnp.logical_and, [i == 0 for i in ids])
        last = functools.reduce(jnp.logical_and, [i == size - 1 for i, size in zip(ids, grid)])

        @pl.when(first)
        def _():
            _exchange_plan(ride_ins, ride_outs, sems, gather)[0]()

        body(*ins, *outs)

        @pl.when(last)
        def _():
            _exchange_plan(ride_ins, ride_outs, sems, gather)[1]()

    return carrying, extra


def _exchange_plan(in_refs, out_refs, sems, gather):
    x, y, c, me = _mesh_place()

    def src(b, slot):
        return in_refs[b] if gather else in_refs[b].at[slot]

    def copy(b, k, dst_slot):
        send_sems, recv_sems = sems[SEMS_PER_BUFFER * b], sems[SEMS_PER_BUFFER * b + 1]
        return pltpu.make_async_remote_copy(
            src_ref=src(b, me ^ k), dst_ref=out_refs[b].at[dst_slot],
            send_sem=send_sems.at[k - 1], recv_sem=recv_sems.at[k - 1],
            device_id=_peer(x, y, c, k), device_id_type=pl.DeviceIdType.MESH)

    def mine(b):
        return pltpu.make_async_copy(src(b, me), out_refs[b].at[me], sems[SEMS_PER_BUFFER * b + 2])

    buffers = range(len(in_refs))

    def start():
        for b in buffers:
            mine(b).start()
            for k in range(1, N_DEV):
                copy(b, k, me).start()

    def wait():
        for b in buffers:
            for k in range(1, N_DEV):
                copy(b, k, me ^ k).wait_recv()
        for b in buffers:
            for k in range(1, N_DEV):
                copy(b, k, me).wait_send()
            mine(b).wait()

    return start, wait


def _adamw(parts, w, m, v, *, name, tr, ride=None):
    n_layers = len(parts)
    rows, cols = parts[0].shape[1:]
    assert rows % tr == 0 and w.shape == (n_layers * rows, cols)
    tiles = rows // tr
    bias1 = 1.0 - ADAM_B1 ** ADAM_STEP
    bias2 = 1.0 - ADAM_B2 ** ADAM_STEP

    def body(*refs):
        p_refs = refs[:n_layers]
        w_ref, m_ref, v_ref, g_ref, d_ref, nm_ref, nv_ref = refs[n_layers:]

        def update(p_ref):
            g = p_ref[0].astype(F32)
            for j in range(1, N_DEV):
                g = g + p_ref[j].astype(F32)
            nm = ADAM_B1 * m_ref[...] + (1.0 - ADAM_B1) * g
            nv = ADAM_B2 * v_ref[...] + (1.0 - ADAM_B2) * (g * g)
            g_ref[...] = g
            nm_ref[...] = nm
            nv_ref[...] = nv
            d_ref[...] = -ADAM_LR * ((nm / bias1) / (jnp.sqrt(nv / bias2) + ADAM_EPS) + ADAM_WD * w_ref[...])

        for layer, p_ref in enumerate(p_refs):
            @pl.when(pl.program_id(0) == layer)
            def _(p_ref=p_ref):
                update(p_ref)

    def part_spec(layer):
        def index(l, i):
            return 0, jnp.where(l < layer, 0, jnp.where(l > layer, tiles - 1, i)), 0
        return pl.BlockSpec((N_DEV, tr, cols), index)

    blk = pl.BlockSpec((tr, cols), lambda l, i: (l * tiles + i, 0))
    out = jax.ShapeDtypeStruct((n_layers * rows, cols), F32)
    grid = (n_layers, tiles)
    body, extra = _carry_exchange(body, n_layers + 3, 4, grid, ride)
    res = pl.pallas_call(
        body, name=name, grid=grid,
        in_specs=[part_spec(layer) for layer in range(n_layers)] + [blk, blk, blk] + extra.in_specs,
        out_specs=[blk] * 4 + extra.out_specs, out_shape=[out] * 4 + extra.out_shape,
        scratch_shapes=extra.scratch, compiler_params=_params(),
    )(*parts, w, m, v, *extra.args)
    return res[:4], res[4:]


def _col_blocks(full):
    r, n = full.shape
    return full.reshape(r, N_DEV, n // N_DEV).transpose(1, 0, 2)


def _row_blocks(full):
    r, n = full.shape
    return full.reshape(N_DEV, r // N_DEV, n)


def _from_col_blocks(blocks):
    j, r, c = blocks.shape
    return blocks.transpose(1, 0, 2).reshape(r, j * c)


def _from_row_blocks(blocks):
    j, r, c = blocks.shape
    return blocks.reshape(j * r, c)


SHARD_ROWWISE = {'w_out', 'w_ffn_down'}
EARLY = ('w_ffn_down', 'w_ffn_gate', 'w_ffn_up', 'w_out', 'w_branch_a')
LATE = ('w_branch_b', 'w_branch_c')


def _blocks(name, full):
    return _row_blocks(full) if name in SHARD_ROWWISE else _col_blocks(full)


def _from_blocks(name, blocks):
    return _from_row_blocks(blocks) if name in SHARD_ROWWISE else _from_col_blocks(blocks)


def _pack(shards, names):
    lead = shards[names[0]].shape[:-2]
    return jnp.concatenate([shards[n].reshape(lead + (-1, PACK_COLS)) for n in names], axis=-2)


def _unpack(packed, shapes, names):
    out, off = {}, 0
    lead = packed.shape[:-2]
    for n in names:
        size = shapes[n][0] * shapes[n][1] // PACK_COLS
        out[n] = packed[..., off:off + size, :].reshape(lead + tuple(shapes[n]))
        off += size
    return out


def _pack_small(vals):
    flat = jnp.concatenate([vals[n].reshape(-1) for n in SMALL])
    return flat.reshape(-1, PACK_COLS)


def _unpack_small(packed, shapes):
    flat = packed.reshape(-1)
    out, off = {}, 0
    for n in SMALL:
        size = 1
        for dim in shapes[n]:
            size *= dim
        out[n] = flat[off:off + size].reshape(shapes[n])
        off += size
    return out


def _band_views(qkv, s):
    views = []
    for gi, d in enumerate(B_DILATIONS):
        starts = [A_Q_WIDTH + 2 * A_KV_WIDTH + part * B_WIDTH + gi * B_OUT_WIDTH for part in range(3)]
        group = jnp.concatenate([qkv[:, c0:c0 + B_OUT_WIDTH] for c0 in starts], axis=1)
        views.append(group.reshape(s // d, d * GROUP_QKV))
    return views


def _layer_fwd(x, p, tables, ride=None, after_ride=None):
    s = x.shape[0]
    cos, sin = tables
    h, h_t = _rms_fwd(x, p['norm_mix'], name='rms_mix_fwd')
    z = _mm(h, p['w_in'], dims='nn', out_dtype=BF16, tm=1024, tn=1664, tk=1024, name='mm_in_fwd')
    gains = _qk_gains(p['qn_a'], p['kn_a'], p['qn_b'], p['kn_b'])
    qkv = _qkrope_fwd(z, gains, cos, sin, name='qkrope_fwd')

    kdup = _dup_halves(qkv[:, A_Q_WIDTH:A_Q_WIDTH + A_KV_WIDTH])
    va = qkv[:, A_Q_WIDTH + A_KV_WIDTH:A_Q_WIDTH + 2 * A_KV_WIDTH]
    vdup = _dup_halves(va)
    oa, lse_a, *gathered = _attn_a_fwd(qkv, kdup, _with_ones(va), ride=ride, name='attn_a_fwd')
    if after_ride is not None:
        p = {**p, **after_ride(gathered)}
    ya = _mm(oa, p['w_branch_a'], dims='nn', out_dtype=BF16, tm=1024, tn=1024, tk=512, name='mm_branch_a_fwd')

    views = _band_views(qkv, s)
    o_g, lse_g = [], []
    for gi, d in enumerate(B_DILATIONS):
        o, lse = _band_fwd(views[gi], gi, d, name=f'band_fwd_d{d}')
        o_g.append(o.reshape(s, B_OUT_WIDTH))
        lse_g.append(lse.reshape(s, B_OUT_WIDTH))
    ob = _merge_fwd(o_g, lse_g, name='merge_fwd')
    yb = _mm(ob, p['w_branch_b'], dims='nn', out_dtype=BF16, tm=1024, tn=1024, tk=256, name='mm_branch_b_fwd')

    pooled, mixed = _pool_fwd(z, p['pool_lin'], p['pool_scale'], name='pool_fwd')
    yc = _mm(mixed, p['w_branch_c'], dims='nn', out_dtype=BF16, tm=1024, tn=1024, tk=512, name='mm_branch_c_fwd')

    merged = _gate_fwd(z, p['b_gate'], ya, yb, yc, name='gate_fwd')
    x_mid = _mm(merged, p['w_out'], dims='nn', out_dtype=F32, tm=1024, tn=1024, tk=1024, res=x, name='mm_out_fwd')

    h2, h2_t = _rms_fwd(x_mid, p['norm_ffn'], name='rms_ffn_fwd')
    fa = _mm(h2, p['w_ffn_gate'], dims='nn', out_dtype=BF16, tm=1024, tn=1408, tk=1024, name='mm_ffn_gate_fwd')
    fb = _mm(h2, p['w_ffn_up'], dims='nn', out_dtype=BF16, tm=1024, tn=1408, tk=1024, name='mm_ffn_up_fwd')
    f = _swiglu_fwd(fa, fb, name='swiglu_fwd')
    x_out = _mm(f, p['w_ffn_down'], dims='nn', out_dtype=F32, tm=1024, tn=512, tk=2816, res=x_mid, name='mm_ffn_down_fwd')

    saved = dict(x=x, h=h, z=z, gains=gains, qkv=qkv, kdup=kdup, vdup=vdup, oa=oa, lse_a=lse_a, o_g=o_g, lse_g=lse_g,
                 ob=ob, pooled=pooled, mixed=mixed, ya=ya, yb=yb, yc=yc, merged=merged, x_mid=x_mid, h2=h2,
                 fa=fa, fb=fb, f=f, views=views, h_t=h_t, h2_t=h2_t)
    return x_out, saved, p


def _fold_heads(v, heads):
    return v.reshape(heads, HEAD_DIM).sum(axis=0)


def _layer_bwd(dx, p, sv, tables, make_ride=None):
    s = dx.shape[0]
    cos, sin = tables
    g = {}

    df = _mm(dx, p['w_ffn_down'], dims='nt', out_dtype=BF16, tm=1024, tn=1408, tk=1024, name='mm_ffn_down_dx')
    g['w_ffn_down'] = _mm(sv['f'], dx, dims='tn', out_dtype=BF16, tm=1408, tn=1024, tk=1024, name='mm_ffn_down_dw')
    da, db = _swiglu_bwd(sv['fa'], sv['fb'], df, name='swiglu_bwd')
    dh2 = _mm(da, p['w_ffn_gate'], dims='nt', out_dtype=F32, tm=1024, tn=512, tk=2816, name='mm_ffn_gate_dx')
    dh2 = _mm(db, p['w_ffn_up'], dims='nt', out_dtype=F32, tm=1024, tn=512, tk=2816, res=dh2, name='mm_ffn_up_dx')
    g['w_ffn_gate'] = _mm(sv['h2_t'], da, dims='nn', out_dtype=BF16, tm=1024, tn=2816, tk=1024, name='mm_ffn_gate_dw')
    g['w_ffn_up'] = _mm(sv['h2_t'], db, dims='nn', out_dtype=BF16, tm=1024, tn=2816, tk=1024, name='mm_ffn_up_dw')
    dx_mid, g['norm_ffn'] = _rms_bwd(sv['x_mid'], p['norm_ffn'], dh2, dx, name='rms_ffn_bwd')

    dmerged = _mm(dx_mid, p['w_out'], dims='nt', out_dtype=F32, tm=1024, tn=1024, tk=1024, name='mm_out_dx')
    g['w_out'] = _mm(sv['merged'], dx_mid, dims='tn', out_dtype=BF16, tm=1024, tn=1024, tk=1024, name='mm_out_dw')
    dya, dyb, dyc, dg0, dg1, dg2, db0, db1, db2 = _gate_bwd(
        sv['z'], p['b_gate'], sv['ya'], sv['yb'], sv['yc'], dmerged, name='gate_bwd')
    g['b_gate'] = jnp.concatenate([db0, db1, db2], axis=1)

    doa = _mm(dya, p['w_branch_a'], dims='nt', out_dtype=BF16, tm=1024, tn=512, tk=1024, name='mm_branch_a_dx')
    g['w_branch_a'] = _mm(sv['oa'], dya, dims='tn', out_dtype=BF16, tm=512, tn=1024, tk=512, name='mm_branch_a_dw')
    ride = None if make_ride is None else make_ride(g)
    dqa, dkdup, dvdup, *arrived = _attn_a_bwd(sv['qkv'], sv['kdup'], sv['vdup'], sv['oa'], sv['lse_a'], doa,
                                              ride=ride, name='attn_a_bwd')

    def fold(dup):
        return jnp.concatenate([dup[:, 0:64] + dup[:, 64:128], dup[:, 128:192] + dup[:, 192:256]], axis=1)

    dka, dva = fold(dkdup), fold(dvdup)

    dob = _mm(dyb, p['w_branch_b'], dims='nt', out_dtype=F32, tm=1024, tn=256, tk=1024, name='mm_branch_b_dx')
    g['w_branch_b'] = _mm(sv['ob'], dyb, dims='tn', out_dtype=BF16, tm=256, tn=1024, tk=512, name='mm_branch_b_dw')
    merged_b = _merge_bwd(sv['o_g'], sv['lse_g'], dob, name='merge_bwd')
    do_g, dd_g = merged_b[:3], merged_b[3:]
    views = sv['views']
    dq_parts, dk_parts, dv_parts = [], [], []
    for gi, d in enumerate(B_DILATIONS):
        ln = s // d
        dq, dk, dv = _band_bwd(views[gi], do_g[gi].reshape(ln, d * B_OUT_WIDTH),
                               sv['lse_g'][gi].reshape(ln, d * B_OUT_WIDTH), dd_g[gi].reshape(ln, d * B_OUT_WIDTH),
                               gi, d, name=f'band_bwd_d{d}')
        dq_parts.append(dq.reshape(s, B_OUT_WIDTH))
        dk_parts.append(dk.reshape(s, B_OUT_WIDTH))
        dv_parts.append(dv.reshape(s, B_OUT_WIDTH))

    dmixed = _mm(dyc, p['w_branch_c'], dims='nt', out_dtype=F32, tm=1024, tn=512, tk=1024, name='mm_branch_c_dx')
    g['w_branch_c'] = _mm(sv['mixed'], dyc, dims='tn', out_dtype=BF16, tm=512, tn=1024, tk=512, name='mm_branch_c_dw')
    du, g['pool_lin'], g['pool_scale'] = _pool_bwd(dmixed, sv['pooled'], p['pool_lin'], p['pool_scale'], name='pool_bwd')

    dz_qkv, dgains = _qkrope_bwd(sv['z'], sv['gains'], cos, sin, [dqa, dka, dva] + dq_parts + dk_parts + dv_parts,
                                 name='qkrope_bwd')
    dgains = jnp.sum(dgains, axis=0)
    g['qn_a'] = _fold_heads(dgains[0:512], 8)
    g['kn_a'] = _fold_heads(dgains[512:640], 2)
    g['qn_b'] = _fold_heads(dgains[768:1536], 12)
    g['kn_b'] = _fold_heads(dgains[1536:2304], 12)

    dz = jnp.concatenate([dz_qkv, du, dg0, dg1, dg2], axis=1)
    dh = _mm(dz, p['w_in'], dims='nt', out_dtype=F32, tm=1024, tn=1024, tk=1664, name='mm_in_dx')
    g['w_in'] = _mm(sv['h_t'], dz, dims='nn', out_dtype=BF16, tm=1024, tn=3328, tk=1024, name='mm_in_dw')
    dx_in, g['norm_mix'] = _rms_bwd(sv['x'], p['norm_mix'], dh, dx_mid, name='rms_mix_bwd')
    return dx_in, g, arrived


def _small_views(vals, l):
    return {
        'norm_mix': vals['norm_mix'][l][None, :], 'b_gate': vals['b_gate'][l][None, :],
        'qn_a': vals['qn_a'][l], 'kn_a': vals['kn_a'][l], 'qn_b': vals['qn_b'][l], 'kn_b': vals['kn_b'][l],
        'pool_lin': vals['pool_lin'][l], 'pool_scale': vals['pool_scale'][l][None, :],
        'norm_ffn': vals['norm_ffn'][l][None, :],
    }


def kernel(x, norm_mix, w_in, b_gate, qn_a, kn_a, qn_b, kn_b, pool_lin, pool_scale, w_branch_a, w_branch_b, w_branch_c, w_out, norm_ffn, w_ffn_gate, w_ffn_up, w_ffn_down, loss_target, m_norm_mix, m_w_in, m_b_gate, m_qn_a, m_kn_a, m_qn_b, m_kn_b, m_pool_lin, m_pool_scale, m_w_branch_a, m_w_branch_b, m_w_branch_c, m_w_out, m_norm_ffn, m_w_ffn_gate, m_w_ffn_up, m_w_ffn_down, v_norm_mix, v_w_in, v_b_gate, v_qn_a, v_kn_a, v_qn_b, v_kn_b, v_pool_lin, v_pool_scale, v_w_branch_a, v_w_branch_b, v_w_branch_c, v_w_out, v_norm_ffn, v_w_ffn_gate, v_w_ffn_up, v_w_ffn_down):
    w = dict(norm_mix=norm_mix, w_in=w_in, b_gate=b_gate, qn_a=qn_a, kn_a=kn_a, qn_b=qn_b, kn_b=kn_b,
             pool_lin=pool_lin, pool_scale=pool_scale, w_branch_a=w_branch_a, w_branch_b=w_branch_b,
             w_branch_c=w_branch_c, w_out=w_out, norm_ffn=norm_ffn, w_ffn_gate=w_ffn_gate, w_ffn_up=w_ffn_up,
             w_ffn_down=w_ffn_down)
    m = dict(norm_mix=m_norm_mix, w_in=m_w_in, b_gate=m_b_gate, qn_a=m_qn_a, kn_a=m_kn_a, qn_b=m_qn_b, kn_b=m_kn_b,
             pool_lin=m_pool_lin, pool_scale=m_pool_scale, w_branch_a=m_w_branch_a, w_branch_b=m_w_branch_b,
             w_branch_c=m_w_branch_c, w_out=m_w_out, norm_ffn=m_norm_ffn, w_ffn_gate=m_w_ffn_gate,
             w_ffn_up=m_w_ffn_up, w_ffn_down=m_w_ffn_down)
    v = dict(norm_mix=v_norm_mix, w_in=v_w_in, b_gate=v_b_gate, qn_a=v_qn_a, kn_a=v_kn_a, qn_b=v_qn_b, kn_b=v_kn_b,
             pool_lin=v_pool_lin, pool_scale=v_pool_scale, w_branch_a=v_w_branch_a, w_branch_b=v_w_branch_b,
             w_branch_c=v_w_branch_c, w_out=v_w_out, norm_ffn=v_norm_ffn, w_ffn_gate=v_w_ffn_gate,
             w_ffn_up=v_w_ffn_up, w_ffn_down=v_w_ffn_down)
    depth = w_in.shape[0]
    shard_shapes = {n: w[n].shape[1:] for n in SHARDED}
    small_shapes = {n: w[n].shape for n in SMALL}

    def my_shards(l, names):
        return _pack({n: w[n][l].astype(BF16) for n in names}, names)

    def full_weights(gathered, names):
        blocks = _unpack(gathered, shard_shapes, names)
        return {n: _from_blocks(n, blocks[n]) for n in names}

    def blocks_to_send(grad, names):
        return _pack({n: _blocks(n, grad[n]) for n in names}, names)

    tables = _rope_tables(x.shape[1])
    (w_in_blocks,) = _exchange([w['w_in'][0].astype(BF16)], gather=True, name='gather_weights')
    layers, saved = [], []
    act = x[0]
    for l in range(depth):
        p = {'w_in': _from_col_blocks(w_in_blocks), **_small_views(w, l)}
        bufs = [my_shards(l, EARLY), my_shards(l, LATE)]
        if l + 1 < depth:
            bufs.append(w['w_in'][l + 1].astype(BF16))
        stash = {}

        def after_ride(gathered, stash=stash):
            stash['next_w_in'] = gathered[2] if len(gathered) > 2 else None
            return {**full_weights(gathered[0], EARLY), **full_weights(gathered[1], LATE)}

        act, sv, p = _layer_fwd(act, p, tables, ride=(bufs, True), after_ride=after_ride)
        w_in_blocks = stash['next_w_in']
        layers.append(p)
        saved.append(sv)
    part, dx = _loss_head(act, loss_target[0], name='loss_head')
    loss = lax.psum(jnp.sum(part), ('x', 'y', 'c'))

    early_parts, late_parts, w_in_parts = ([None] * depth for _ in range(3))
    grads = [None] * depth
    pending = []
    for l in reversed(range(depth)):
        def make_ride(g, pending=pending):
            return [blocks_to_send(g, EARLY)] + pending, False

        dx, grads[l], arrived = _layer_bwd(dx, layers[l], saved[l], tables, make_ride=make_ride)
        early_parts[l] = arrived[0]
        if pending:
            w_in_parts[l + 1], late_parts[l + 1] = arrived[1], arrived[2]
        pending = [_col_blocks(grads[l]['w_in']), blocks_to_send(grads[l], LATE)]
    grad_x = dx

    new = {}

    def update_packed(names, parts, label, ride=None):
        packed = [_pack({n: t[n] for n in names}, names).reshape(-1, PACK_COLS) for t in (w, m, v)]
        res, arrived = _adamw(parts, *packed, name=label, tr=96, ride=ride)
        unpacked = [_unpack(r.reshape(depth, -1, PACK_COLS), shard_shapes, names) for r in res]
        for n in names:
            new[n] = tuple(u[n] for u in unpacked)
        return arrived

    w_in_parts[0], late_parts[0] = update_packed(EARLY, early_parts, 'adamw_early', ride=(pending, False))
    small_grad = {n: jnp.stack([grads[l][n].reshape(small_shapes[n][1:]) for l in range(depth)]) for n in SMALL}
    w_in_shape = w['w_in'].shape
    res, (small_parts,) = _adamw(w_in_parts, *(t['w_in'].reshape(-1, w_in_shape[-1]) for t in (w, m, v)),
                                 name='adamw_w_in', tr=128, ride=([_pack_small(small_grad)], True))
    new['w_in'] = tuple(r.reshape(w_in_shape) for r in res)
    update_packed(LATE, late_parts, 'adamw_late')

    res, _ = _adamw([small_parts], _pack_small({n: w[n] for n in SMALL}), _pack_small({n: m[n] for n in SMALL}),
                    _pack_small({n: v[n] for n in SMALL}), name='adamw_small', tr=small_parts.shape[1])
    unpacked = [_unpack_small(r, small_shapes) for r in res]
    for n in SMALL:
        new[n] = tuple(u[n] for u in unpacked)

    outs = [loss, grad_x[None]]
    for idx in range(4):
        outs.extend(new[n][idx] for n in WEIGHTS)
    return tuple(outs)
```
